```python
import math
import jax
import jax.numpy as jnp
from jax import lax
import numpy as np

D_MODEL = 1024
BATCH = 8
SEQ = 4096
DEPTH = 2

GRID_W = 64
CTX_LEN = 256
NORM_EPS = 1e-6

NA_HEADS = 8
NA_HEAD_DIM = 64
NA_WIDTH = NA_HEADS * NA_HEAD_DIM
NA_WIN_ROWS = 8
NA_WIN_COLS = 16
POOL_WINDOWS = (2, 4, 8, 16)
POOL_GROUPS = 4
POOL_GROUP_DIM = 64
POOL_WIDTH = POOL_GROUPS * POOL_GROUP_DIM
GLA_HEADS = 4
GLA_DK = 32
GLA_DV = 64
GLA_WIDTH = GLA_HEADS * GLA_DV
GLA_GATE_RANK = 16
GLA_TAU = 16.0
GLA_CHUNK = 64
ROPE_BASE = 10000.0

MIX_WIDTH = NA_WIDTH + POOL_WIDTH + GLA_WIDTH
SPLIT_SIZES = (NA_WIDTH, NA_WIDTH, NA_WIDTH, POOL_WIDTH, GLA_HEADS * GLA_DK, GLA_HEADS * GLA_DK,
               GLA_WIDTH, GLA_WIDTH, GLA_GATE_RANK, GLA_GATE_RANK)
IN_WIDTH = sum(SPLIT_SIZES)
SPLIT_POINTS = tuple(int(s) for s in np.cumsum(SPLIT_SIZES)[:-1])

N_EXPERTS = 256
TOP_K = 8
N_GROUPS = 8
TOPK_GROUPS = 4
D_EXPERT = 256
ROUTED_SCALE = 2.5
MOE_BLOCK = 128

kernel_name = "hymba_na_pool_gla_moe_prefix_dit"


def rmsnorm(x, g):
    xf = x.astype(jnp.float32)
    y = xf * lax.rsqrt(jnp.mean(xf * xf, axis=-1, keepdims=True) + NORM_EPS)
    return (y * g.astype(jnp.float32)).astype(x.dtype)


def modulate(h, shift, scale):
    return h * (1 + scale) + shift


def to_blhd(t, n_heads):
    b, l, _ = t.shape
    return t.reshape(b, l, n_heads, -1)


def to_bhld(t, n_heads):
    return to_blhd(t, n_heads).transpose(0, 2, 1, 3)


def neighborhood_attention(q, k, v, k_ctx, v_ctx, rpb):
    bsz, L, H, dh = q.shape
    rows = L // GRID_W
    wr = min(NA_WIN_ROWS, rows)
    r = jnp.arange(rows)
    key_rows = jnp.clip(r - wr // 2, 0, rows - wr)[:, None] + jnp.arange(wr)[None, :]
    col = jnp.arange(GRID_W)
    col_start = jnp.clip(col - NA_WIN_COLS // 2, 0, GRID_W - NA_WIN_COLS)
    col_mask = (col[None, :] >= col_start[:, None]) & (col[None, :] < col_start[:, None] + NA_WIN_COLS)
    dr = key_rows - r[:, None] + NA_WIN_ROWS - 1
    dc = jnp.clip(col[None, :] - col[:, None], -(NA_WIN_COLS - 1), NA_WIN_COLS - 1) + NA_WIN_COLS - 1
    bias = rpb[:, dr][:, :, :, dc]
    bias = bias.transpose(0, 1, 3, 2, 4).astype(jnp.float32)

    qg = q.reshape(bsz, rows, GRID_W, H, dh)
    kg = jnp.take(k.reshape(bsz, rows, GRID_W, H, dh), key_rows, axis=1)
    vg = jnp.take(v.reshape(bsz, rows, GRID_W, H, dh), key_rows, axis=1)
    scale = dh ** -0.5
    s_loc = jnp.einsum('brqhd,brwkhd->bhrqwk', qg, kg, preferred_element_type=jnp.float32) * scale + bias
    s_loc = jnp.where(col_mask[:, None, :], s_loc, -jnp.inf)
    s_ctx = jnp.einsum('brqhd,bchd->bhrqc', qg, k_ctx, preferred_element_type=jnp.float32) * scale
    n_loc = wr * GRID_W
    s = jnp.concatenate([s_loc.reshape(bsz, H, rows, GRID_W, n_loc), s_ctx], axis=-1)
    p = jax.nn.softmax(s, axis=-1).astype(v.dtype)
    p_loc = p[..., :n_loc].reshape(bsz, H, rows, GRID_W, wr, GRID_W)
    p_ctx = p[..., n_loc:]
    o = (jnp.einsum('bhrqwk,brwkhd->brqhd', p_loc, vg)
         + jnp.einsum('bhrqc,bchd->brqhd', p_ctx, v_ctx))
    return o.reshape(bsz, L, H * dh)


def context_attention(q, k, v):
    bsz, n, H, dh = q.shape
    s = jnp.einsum('bqhd,bkhd->bhqk', q, k, preferred_element_type=jnp.float32) * dh ** -0.5
    p = jax.nn.softmax(s, axis=-1).astype(v.dtype)
    return jnp.einsum('bhqk,bkhd->bqhd', p, v).reshape(bsz, n, H * dh)


def multiscale_pool(u, w_pool, pool_scale):
    bsz, L, C = u.shape
    uf = u.astype(jnp.float32)
    cs = jnp.concatenate([jnp.zeros((bsz, 1, C), jnp.float32), jnp.cumsum(uf, axis=1)], axis=1)
    t = jnp.arange(L)
    outs = []
    for g, w in enumerate(POOL_WINDOWS):
        lo = jnp.clip(t - w // 2, 0, L - 1)
        hi = jnp.clip(t + w - w // 2 - 1, 0, L - 1)
        sl = slice(g * POOL_GROUP_DIM, (g + 1) * POOL_GROUP_DIM)
        csg = cs[..., sl]
        mean = (csg[:, hi + 1] - csg[:, lo]) / (hi - lo + 1).astype(jnp.float32)[:, None]
        outs.append((mean - uf[..., sl]) @ w_pool[g].astype(jnp.float32))
    return (jnp.concatenate(outs, axis=-1) * pool_scale.astype(jnp.float32)).astype(u.dtype)


def gla_log_decay(a_low, w2, b2):
    logits = a_low.astype(jnp.float32) @ w2.astype(jnp.float32) + b2.astype(jnp.float32)
    return to_bhld(jax.nn.log_sigmoid(logits) / GLA_TAU, GLA_HEADS)


def axial_rope(x, row, col):
    half = x.shape[-1] // 2
    quarter = half // 2
    inv_freq = ROPE_BASE ** (-jnp.arange(quarter, dtype=jnp.float32) / quarter)

    def rotate(xp, pos):
        ang = pos.astype(jnp.float32)[:, None] * inv_freq[None, :]
        cos, sin = jnp.cos(ang), jnp.sin(ang)
        x1, x2 = xp[..., :quarter], xp[..., quarter:]
        return jnp.concatenate([x1 * cos - x2 * sin, x2 * cos + x1 * sin], axis=-1)

    xf = x.astype(jnp.float32)
    return jnp.concatenate([rotate(xf[..., :half], row), rotate(xf[..., half:], col)], axis=-1).astype(x.dtype)


def gla_chunk_scan(q, k, v, log_a, s0):
    bsz, H, L, dk = q.shape
    dv = v.shape[-1]
    n = L // GLA_CHUNK
    shp = (bsz, H, n, GLA_CHUNK)
    qf = q.astype(jnp.float32).reshape(*shp, dk)
    kf = k.astype(jnp.float32).reshape(*shp, dk)
    vf = v.astype(jnp.float32).reshape(*shp, dv)
    b = jnp.cumsum(log_a.reshape(*shp, dk), axis=3)
    b_last = b[:, :, :, -1:, :]
    q_in = qf * jnp.exp(b)
    k_in = kf * jnp.exp(-b)
    k_st = kf * jnp.exp(b_last - b)
    lower = jnp.tril(jnp.ones((GLA_CHUNK, GLA_CHUNK), bool))
    att = jnp.where(lower, jnp.einsum('bhncd,bhnsd->bhncs', q_in, k_in), 0.0)
    o = jnp.einsum('bhncs,bhnsv->bhncv', att, vf)
    kv = jnp.einsum('bhncd,bhncv->bhndv', k_st, vf)
    decay = jnp.exp(b_last[:, :, :, 0, :])

    def step(S, inp):
        dec, kv_n = inp
        return dec[..., None] * S + kv_n, S

    s_final, s_prev = lax.scan(step, s0.astype(jnp.float32),
                               (jnp.moveaxis(decay, 2, 0), jnp.moveaxis(kv, 2, 0)))
    o = o + jnp.einsum('bhncd,nbhdv->bhncv', q_in, s_prev)
    return o.reshape(bsz, H, L, dv), s_final


def gla_bidir(q, k, v, la_f, la_b, s_f0, s_b0):
    o_f, s_f = gla_chunk_scan(q, k, v, la_f, s_f0)
    flip = lambda t: jnp.flip(t, axis=2)
    o_b, s_b = gla_chunk_scan(flip(q), flip(k), flip(v), flip(la_b), s_b0)
    return o_f + flip(o_b), s_f, s_b


def gla_inputs(p, w_gf, b_gf, w_gb, b_gb):
    q = to_bhld(p[4], GLA_HEADS) * GLA_DK ** -0.5
    k = to_bhld(p[5], GLA_HEADS)
    v = to_bhld(p[6], GLA_HEADS)
    return q, k, v, gla_log_decay(p[8], w_gf, b_gf), gla_log_decay(p[9], w_gb, b_gb)


def gla_output(o, r, g):
    n = o * lax.rsqrt(jnp.mean(o * o, axis=-1, keepdims=True) + NORM_EPS) * g.astype(jnp.float32)
    bsz, H, L, dv = n.shape
    n = n.transpose(0, 2, 1, 3).reshape(bsz, L, H * dv)
    return (n * jax.nn.silu(r.astype(jnp.float32))).astype(r.dtype)


def swiglu(h, w_in, w_down):
    g, u = jnp.split(h @ w_in, 2, axis=-1)
    return (jax.nn.silu(g) * u) @ w_down


def moe_dispatch(h, top_e, w, w_e_in, w_e_down):
    T, D = h.shape
    A = T * TOP_K
    flat_e = top_e.reshape(A)
    order = jnp.argsort(flat_e)
    sorted_e = flat_e[order]
    counts = jnp.bincount(flat_e, length=N_EXPERTS)
    padded = (counts + MOE_BLOCK - 1) // MOE_BLOCK * MOE_BLOCK
    start = jnp.cumsum(counts) - counts
    pad_end = jnp.cumsum(padded)
    pad_start = pad_end - padded
    dest = pad_start[sorted_e] + jnp.arange(A) - start[sorted_e]
    n_blocks = -(-A // MOE_BLOCK) + N_EXPERTS
    slots = n_blocks * MOE_BLOCK
    slot_tok = jnp.zeros((slots,), jnp.int32).at[dest].set((order // TOP_K).astype(jnp.int32))
    slot_w = jnp.zeros((slots,), jnp.float32).at[dest].set(w.reshape(A)[order])
    block_e = jnp.minimum(jnp.searchsorted(pad_end, jnp.arange(n_blocks) * MOE_BLOCK, side='right'),
                          N_EXPERTS - 1)

    def step(out, blk):
        tok, wt, e = blk
        y = swiglu(h[tok], w_e_in[e], w_e_down[e])
        return out.at[tok].add(y.astype(jnp.float32) * wt[:, None]), None

    out, _ = lax.scan(step, jnp.zeros((T, D), jnp.float32),
                      (slot_tok.reshape(n_blocks, MOE_BLOCK), slot_w.reshape(n_blocks, MOE_BLOCK), block_e))
    return out


def moe_ffn(h, w_router, b_router, w_e_in, w_e_down, w_sh_in, w_sh_down):
    T = h.shape[0]
    scores = jax.nn.sigmoid(jnp.dot(h, w_router, preferred_element_type=jnp.float32))
    biased = scores + b_router.astype(jnp.float32)
    grp_score = lax.top_k(biased.reshape(T, N_GROUPS, N_EXPERTS // N_GROUPS), 2)[0].sum(-1)
    _, top_g = lax.top_k(grp_score, TOPK_GROUPS)
    gmask = jax.nn.one_hot(top_g, N_GROUPS, dtype=jnp.float32).sum(1) > 0
    masked = jnp.where(jnp.repeat(gmask, N_EXPERTS // N_GROUPS, axis=1), biased, -jnp.inf)
    _, top_e = lax.top_k(masked, TOP_K)
    w = jnp.take_along_axis(scores, top_e, axis=1)
    w = w / jnp.sum(w, axis=-1, keepdims=True) * ROUTED_SCALE
    routed = moe_dispatch(h, top_e, w, w_e_in, w_e_down)
    shared = swiglu(h, w_sh_in, w_sh_down).astype(jnp.float32)
    return (routed + shared).astype(h.dtype)


def setup_inputs(seed: int = 0) -> dict:
    key = jax.random.key(seed)
    ks = jax.random.split(key, 26)
    f32 = jnp.float32
    D, E, F = D_MODEL, N_EXPERTS, D_EXPERT

    def nrm(k, shape, scale):
        return jax.random.normal(k, shape, f32) * scale

    return {
        "x": nrm(ks[0], (BATCH, SEQ, D), 1.0),
        "c": nrm(ks[1], (BATCH, D), 1.0),
        "ctx": nrm(ks[2], (BATCH, CTX_LEN, D), 1.0),
        "c_ctx": nrm(ks[3], (D,), 1.0),
        "w_mod": nrm(ks[4], (DEPTH, D, 6 * D), 0.3 * D ** -0.5),
        "b_mod": nrm(ks[5], (DEPTH, 6 * D), 0.02),
        "g_mix": 1.0 + nrm(ks[6], (DEPTH, D), 0.02),
        "w_in": nrm(ks[7], (DEPTH, D, IN_WIDTH), D ** -0.5),
        "rpb": nrm(ks[8], (DEPTH, NA_HEADS, 2 * NA_WIN_ROWS - 1, 2 * NA_WIN_COLS - 1), 0.1),
        "w_pool": nrm(ks[9], (DEPTH, POOL_GROUPS, POOL_GROUP_DIM, POOL_GROUP_DIM), POOL_GROUP_DIM ** -0.5),
        "pool_scale": 1.0 + nrm(ks[10], (DEPTH, POOL_WIDTH), 0.1),
        "w_gate_f": nrm(ks[11], (DEPTH, GLA_GATE_RANK, GLA_HEADS * GLA_DK), GLA_GATE_RANK ** -0.5),
        "b_gate_f": nrm(ks[12], (DEPTH, GLA_HEADS * GLA_DK), 0.1),
        "w_gate_b": nrm(ks[13], (DEPTH, GLA_GATE_RANK, GLA_HEADS * GLA_DK), GLA_GATE_RANK ** -0.5),
        "b_gate_b": nrm(ks[14], (DEPTH, GLA_HEADS * GLA_DK), 0.1),
        "g_gla": 1.0 + nrm(ks[15], (DEPTH, GLA_DV), 0.02),
        "w_out": nrm(ks[16], (DEPTH, MIX_WIDTH, D), MIX_WIDTH ** -0.5),
        "g_ffn": 1.0 + nrm(ks[17], (DEPTH, D), 0.02),
        "w_router": nrm(ks[18], (DEPTH, D, E), D ** -0.5),
        "b_router": nrm(ks[19], (DEPTH, E), 0.01),
        "w_e_in": nrm(ks[20], (DEPTH, E, D, 2 * F), D ** -0.5),
        "w_e_down": nrm(ks[21], (DEPTH, E, F, D), F ** -0.5),
        "w_sh_in": nrm(ks[22], (DEPTH, D, 2 * F), D ** -0.5),
        "w_sh_down": nrm(ks[23], (DEPTH, F, D), F ** -0.5),
        "g_final": 1.0 + nrm(ks[24], (D,), 0.02),
    }


def reference(x, c, ctx, c_ctx, w_mod, b_mod, g_mix, w_in, rpb, w_pool, pool_scale,
              w_gate_f, b_gate_f, w_gate_b, b_gate_b, g_gla, w_out, g_ffn,
              w_router, b_router, w_e_in, w_e_down, w_sh_in, w_sh_down, g_final):
    bsz, seq, d = x.shape
    n_ctx = ctx.shape[1]
    t = jnp.arange(seq)
    row, col = t // GRID_W, t % GRID_W
    xc = ctx
    for layer in range(DEPTH):
        last = layer == DEPTH - 1
        mx = jnp.split((jax.nn.silu(c) @ w_mod[layer] + b_mod[layer])[:, None, :], 6, axis=-1)
        mc = jnp.split(jax.nn.silu(c_ctx) @ w_mod[layer] + b_mod[layer], 6, axis=-1)

        hx = modulate(rmsnorm(x, g_mix[layer]), mx[0], mx[1])
        hc = modulate(rmsnorm(xc, g_mix[layer]), mc[0], mc[1])
        px = jnp.split(hx @ w_in[layer], SPLIT_POINTS, axis=-1)
        pc = jnp.split(hc @ w_in[layer], SPLIT_POINTS, axis=-1)

        k_ctx = to_blhd(pc[1], NA_HEADS)
        v_ctx = to_blhd(pc[2], NA_HEADS)
        oa = neighborhood_attention(to_blhd(px[0], NA_HEADS), to_blhd(px[1], NA_HEADS),
                                    to_blhd(px[2], NA_HEADS), k_ctx, v_ctx, rpb[layer])
        ob = multiscale_pool(px[3], w_pool[layer], pool_scale[layer])
        qc_, kc_, vc_, lafc, labc = gla_inputs(pc, w_gate_f[layer], b_gate_f[layer], w_gate_b[layer], b_gate_b[layer])
        zero_state = jnp.zeros((bsz, GLA_HEADS, GLA_DK, GLA_DV), jnp.float32)
        o_gla_c, s_f, s_b = gla_bidir(qc_, kc_, vc_, lafc, labc, zero_state, zero_state)
        qx_, kx_, vx_, lafx, labx = gla_inputs(px, w_gate_f[layer], b_gate_f[layer], w_gate_b[layer], b_gate_b[layer])
        o_gla_x, _, _ = gla_bidir(axial_rope(qx_, row, col), axial_rope(kx_, row, col), vx_, lafx, labx, s_f, s_b)
        oc = gla_output(o_gla_x, px[7], g_gla[layer])

        x = x + mx[2] * (jnp.concatenate([oa, ob, oc], axis=-1) @ w_out[layer])
        if not last:
            oa_c = context_attention(to_blhd(pc[0], NA_HEADS), k_ctx, v_ctx)
            ob_c = multiscale_pool(pc[3], w_pool[layer], pool_scale[layer])
            oc_c = gla_output(o_gla_c, pc[7], g_gla[layer])
            xc = xc + mc[2] * (jnp.concatenate([oa_c, ob_c, oc_c], axis=-1) @ w_out[layer])

        hx2 = modulate(rmsnorm(x, g_ffn[layer]), mx[3], mx[4]).reshape(bsz * seq, d)
        moe_args = (w_router[layer], b_router[layer], w_e_in[layer], w_e_down[layer],
                    w_sh_in[layer], w_sh_down[layer])
        if not last:
            hc2 = modulate(rmsnorm(xc, g_ffn[layer]), mc[3], mc[4]).reshape(bsz * n_ctx, d)
            y = moe_ffn(jnp.concatenate([hc2, hx2], axis=0), *moe_args)
            xc = xc + mc[5] * y[:bsz * n_ctx].reshape(bsz, n_ctx, d)
            x = x + mx[5] * y[bsz * n_ctx:].reshape(bsz, seq, d)
        else:
            x = x + mx[5] * moe_ffn(hx2, *moe_args).reshape(bsz, seq, d)
    return rmsnorm(x, g_final)
```

```python
import functools

import jax
import jax.numpy as jnp
import numpy as np
from jax import lax
from jax.experimental import pallas as pl
from jax.experimental.pallas import tpu as pltpu

GRID_W = 64
NORM_EPS = 1e-6
NA_HEADS = 8
NA_HEAD_DIM = 64
NA_WIDTH = NA_HEADS * NA_HEAD_DIM
NA_WIN_ROWS = 8
NA_WIN_COLS = 16
POOL_WINDOWS = (2, 4, 8, 16)
POOL_GROUP_DIM = 64
POOL_WIDTH = len(POOL_WINDOWS) * POOL_GROUP_DIM
POOL_REACH = max(POOL_WINDOWS) // 2
GLA_HEADS = 4
GLA_DK = 32
GLA_DV = 64
GLA_QK = GLA_HEADS * GLA_DK
GLA_WIDTH = GLA_HEADS * GLA_DV
GLA_GATE_RANK = 16
GLA_TAU = 16.0
GLA_CHUNK = 64
ROPE_BASE = 10000.0
N_EXPERTS = 256
TOP_K = 8
N_GROUPS = 8
TOPK_GROUPS = 4
ROUTED_SCALE = 2.5
MOE_BLOCK = 128

TM = 256
LANES = 128
MXU_DIM = 256
MASK_VALUE = -1e30
VMEM_LIMIT = 48 * 1024 * 1024

QKV_W = 3 * NA_WIDTH
G_W = 2 * GLA_QK + 2 * GLA_WIDTH + 2 * GLA_QK
MAIN_W = QKV_W + POOL_WIDTH + 2 * GLA_QK + 2 * GLA_WIDTH

BF16 = jnp.bfloat16
F32 = jnp.float32


def _params(*sem):
    return pltpu.CompilerParams(dimension_semantics=sem, vmem_limit_bytes=VMEM_LIMIT)


def _sigmoid(x):
    return 1.0 / (1.0 + jnp.exp(-x))


def _silu(x):
    return x * _sigmoid(x)


def _rmsnorm(x, g):
    return x * lax.rsqrt(jnp.mean(x * x, axis=-1, keepdims=True) + NORM_EPS) * g


def _dot(a, b):
    return jnp.dot(a, b, preferred_element_type=F32)


def _dot_nt(a, b):
    return lax.dot_general(a, b, (((1,), (1,)), ((), ())), preferred_element_type=F32)


def _dot_tn(a, b):
    return lax.dot_general(a, b, (((0,), (0,)), ((), ())), preferred_element_type=F32)


def _mod_kernel(c_ref, w_ref, b_ref, o_ref):
    a = _silu(c_ref[...]).astype(BF16)
    o_ref[0] = _dot(a, w_ref[0].astype(BF16)) + b_ref[0]


def _modulation(cc, w_mod, b_mod):
    depth, d, n = w_mod.shape
    r = cc.shape[0]
    tn = 1024
    return pl.pallas_call(
        _mod_kernel,
        grid=(depth, n // tn),
        in_specs=[
            pl.BlockSpec((r, d), lambda l, j: (0, 0)),
            pl.BlockSpec((1, d, tn), lambda l, j: (l, 0, j)),
            pl.BlockSpec((1, 1, tn), lambda l, j: (l, 0, j)),
        ],
        out_specs=pl.BlockSpec((1, r, tn), lambda l, j: (l, 0, j)),
        out_shape=jax.ShapeDtypeStruct((depth, r, n), F32),
        compiler_params=_params("parallel", "parallel"),
        name="modulation",
    )(cc, w_mod, b_mod.reshape(depth, 1, n))


def _inproj_kernel(x_ref, mod_ref, g_ref, wm_ref, wl_ref, w2_ref, b2_ref, qkv_ref, u_ref, gg_ref):
    h = _rmsnorm(x_ref[0], g_ref[...]) * (1.0 + mod_ref[0, 0, 1:2, :]) + mod_ref[0, 0, 0:1, :]
    hb = h.astype(BF16)
    q = _dot(hb, wm_ref[:, 0:NA_WIDTH]) * (NA_HEAD_DIM ** -0.5)
    qkv_ref[0, :, 0:NA_WIDTH] = q.astype(BF16)
    qkv_ref[0, :, NA_WIDTH:2 * NA_WIDTH] = _dot(hb, wm_ref[:, NA_WIDTH:2 * NA_WIDTH]).astype(BF16)
    qkv_ref[0, :, 2 * NA_WIDTH:QKV_W] = _dot(hb, wm_ref[:, 2 * NA_WIDTH:QKV_W]).astype(BF16)
    u_ref[0] = _dot(hb, wm_ref[:, QKV_W:QKV_W + POOL_WIDTH])
    c0 = QKV_W + POOL_WIDTH
    gg_ref[0, :, 0:GLA_QK] = _dot(hb, wm_ref[:, c0:c0 + GLA_QK]) * (GLA_DK ** -0.5)
    gg_ref[0, :, GLA_QK:2 * GLA_QK + 2 * GLA_WIDTH] = _dot(hb, wm_ref[:, c0 + GLA_QK:MAIN_W])
    a_low = _dot(hb, wl_ref[...]).astype(BF16)
    lg = _dot(a_low, w2_ref[...]) + b2_ref[...]
    log_sig = jnp.minimum(lg, 0.0) - jnp.log1p(jnp.exp(-jnp.abs(lg)))
    gg_ref[0, :, 2 * GLA_QK + 2 * GLA_WIDTH:G_W] = log_sig / GLA_TAU


def _inproj(xa, mod, g, w_main, w_low, w2, b2, n_lat_tiles):
    bsz, s, d = xa.shape
    nt = s // TM
    const = lambda b, i: (0, 0)
    return pl.pallas_call(
        _inproj_kernel,
        grid=(bsz, nt),
        in_specs=[
            pl.BlockSpec((1, TM, d), lambda b, i: (b, i, 0)),
            pl.BlockSpec((1, 1, 8, d), lambda b, i: (b, jnp.where(i >= n_lat_tiles, 0, 1), 0, 0)),
            pl.BlockSpec((1, d), const),
            pl.BlockSpec(w_main.shape, const),
            pl.BlockSpec(w_low.shape, const),
            pl.BlockSpec(w2.shape, const),
            pl.BlockSpec(b2.shape, const),
        ],
        out_specs=[
            pl.BlockSpec((1, TM, QKV_W), lambda b, i: (b, i, 0)),
            pl.BlockSpec((1, TM, POOL_WIDTH), lambda b, i: (b, i, 0)),
            pl.BlockSpec((1, TM, G_W), lambda b, i: (b, i, 0)),
        ],
        out_shape=[
            jax.ShapeDtypeStruct((bsz, s, QKV_W), BF16),
            jax.ShapeDtypeStruct((bsz, s, POOL_WIDTH), F32),
            jax.ShapeDtypeStruct((bsz, s, G_W), F32),
        ],
        compiler_params=_params("parallel", "parallel"),
        name="inproj",
    )(xa, mod, g, w_main, w_low, w2, b2)


HEADS_PER_GROUP = MXU_DIM // NA_HEAD_DIM
NA_GROUPS = NA_HEADS // HEADS_PER_GROUP
NA_LOCAL_KEYS = NA_WIN_ROWS * GRID_W


def _stack_heads(x, width):
    lane = lax.broadcasted_iota(jnp.int32, x.shape, 1) // width
    n_heads = x.shape[1] // width
    return jnp.concatenate([jnp.where(lane == h, x, jnp.zeros_like(x)) for h in range(n_heads)], axis=0)


def _unstack_heads(o, width):
    n_heads = o.shape[1] // width
    r = o.shape[0] // n_heads
    lane = lax.broadcasted_iota(jnp.int32, (r, o.shape[1]), 1) // width
    acc = jnp.zeros((r, o.shape[1]), o.dtype)
    for h in range(n_heads):
        acc = jnp.where(lane == h, o[h * r:(h + 1) * r, :], acc)
    return acc


def _na_kernel(q_ref, k_ref, v_ref, kc_ref, vc_ref, bias_ref, o_ref, *, n_rows):
    r = pl.program_id(1)

    def attend(g, local):
        cols = slice(g * MXU_DIM, (g + 1) * MXU_DIM)
        qs = _stack_heads(q_ref[0, :, cols], NA_HEAD_DIM)
        kc = kc_ref[0, :, cols]
        vc = vc_ref[0, :, cols]
        s_ctx = _dot_nt(qs, kc)
        m = jnp.max(s_ctx, axis=-1, keepdims=True)
        if local:
            start = pl.multiple_of(jnp.clip(r - NA_WIN_ROWS // 2, 0, n_rows - NA_WIN_ROWS) * GRID_W, GRID_W)
            kw = k_ref[0, pl.ds(start, NA_LOCAL_KEYS), cols]
            vw = v_ref[0, pl.ds(start, NA_LOCAL_KEYS), cols]
            s_loc = _dot_nt(qs, kw) + bias_ref[0, g]
            m = jnp.maximum(m, jnp.max(s_loc, axis=-1, keepdims=True))
            p_loc = jnp.exp(s_loc - m)
        p_ctx = jnp.exp(s_ctx - m)
        den = jnp.sum(p_ctx, axis=-1, keepdims=True)
        o = _dot(p_ctx.astype(BF16), vc)
        if local:
            den = den + jnp.sum(p_loc, axis=-1, keepdims=True)
            o = o + _dot(p_loc.astype(BF16), vw)
        o_ref[0, :, cols] = _unstack_heads(o / den, NA_HEAD_DIM).astype(o_ref.dtype)

    @pl.when(r < n_rows)
    def _():
        for g in range(NA_GROUPS):
            attend(g, True)

    @pl.when(r >= n_rows)
    def _():
        for g in range(NA_GROUPS):
            attend(g, False)


def _na_bias_table(rpb):
    o = jnp.arange(NA_WIN_ROWS)
    dr = o[None, :] - o[:, None] + NA_WIN_ROWS - 1
    col = jnp.arange(GRID_W)
    dc = jnp.clip(col[None, :] - col[:, None], -(NA_WIN_COLS - 1), NA_WIN_COLS - 1) + NA_WIN_COLS - 1
    col_start = jnp.clip(col - NA_WIN_COLS // 2, 0, GRID_W - NA_WIN_COLS)
    col_mask = (col[None, :] >= col_start[:, None]) & (col[None, :] < col_start[:, None] + NA_WIN_COLS)
    t = rpb[:, dr][..., dc].astype(F32)
    t = jnp.where(col_mask[None, None, None], t, MASK_VALUE)
    t = t.transpose(1, 0, 3, 2, 4)
    return t.reshape(NA_WIN_ROWS, NA_GROUPS, HEADS_PER_GROUP * GRID_W, NA_LOCAL_KEYS)


def _neighborhood_attention(qkv, bias_tab, n_lat_tiles, with_ctx_queries):
    bsz, s, _ = qkv.shape
    seq = n_lat_tiles * TM
    n_rows = seq // GRID_W
    assert n_rows >= NA_WIN_ROWS
    n_ctx_rows = (s - seq) // GRID_W
    nq = n_rows + (n_ctx_rows if with_ctx_queries else 0)
    ctx_blk = seq // (s - seq)

    def bias_idx(b, r):
        rr = jnp.minimum(r, n_rows - 1)
        return (rr - jnp.clip(rr - NA_WIN_ROWS // 2, 0, n_rows - NA_WIN_ROWS), 0, 0, 0)

    return pl.pallas_call(
        functools.partial(_na_kernel, n_rows=n_rows),
        grid=(bsz, nq),
        in_specs=[
            pl.BlockSpec((1, GRID_W, NA_WIDTH), lambda b, r: (b, r, 0)),
            pl.BlockSpec((1, seq, NA_WIDTH), lambda b, r: (b, 0, 1)),
            pl.BlockSpec((1, seq, NA_WIDTH), lambda b, r: (b, 0, 2)),
            pl.BlockSpec((1, s - seq, NA_WIDTH), lambda b, r: (b, ctx_blk, 1)),
            pl.BlockSpec((1, s - seq, NA_WIDTH), lambda b, r: (b, ctx_blk, 2)),
            pl.BlockSpec((1,) + bias_tab.shape[1:], bias_idx),
        ],
        out_specs=pl.BlockSpec((1, GRID_W, NA_WIDTH), lambda b, r: (b, r, 0)),
        out_shape=jax.ShapeDtypeStruct((bsz, nq * GRID_W, NA_WIDTH), BF16),
        compiler_params=_params("parallel", "arbitrary"),
        name="neighborhood_attention",
    )(qkv, qkv, qkv, qkv, qkv, bias_tab)


def _pool_kernel(up_ref, u_ref, un_ref, w_ref, sc_ref, o_ref, buf, *, n_lat_tiles, seq, ctx_len):
    i = pl.program_id(1)
    hw = POOL_REACH
    buf[0:hw] = up_ref[0]
    buf[hw:hw + TM] = u_ref[0]
    buf[hw + TM:2 * hw + TM] = un_ref[0]
    is_ctx = i >= n_lat_tiles
    base = jnp.where(is_ctx, 0, i * TM)
    lseq = jnp.where(is_ctx, ctx_len, seq)
    shape = (TM, POOL_WIDTH)
    t = base + lax.broadcasted_iota(jnp.int32, shape, 0)
    grp = lax.broadcasted_iota(jnp.int32, shape, 1) // POOL_GROUP_DIM
    win = jnp.left_shift(2, grp)
    back = win // 2
    fwd = win - back - 1
    acc = jnp.zeros(shape, F32)
    for d in range(-hw, hw):
        ok = (t + d >= 0) & (t + d < lseq) & (back >= -d) & (fwd >= d)
        acc = acc + jnp.where(ok, buf[hw + d:hw + d + TM, :], 0.0)
    lo = jnp.clip(t - back, 0, lseq - 1)
    hi = jnp.clip(t + fwd, 0, lseq - 1)
    mean = acc / (hi - lo + 1).astype(F32)
    diff = (mean - u_ref[0]).astype(BF16)
    o_ref[0] = (_dot(diff, w_ref[...]) * sc_ref[...]).astype(o_ref.dtype)


def _multiscale_pool(u, w_bd, scale, n_lat_tiles, n_tiles):
    bsz, s, c = u.shape
    hw = POOL_REACH
    per = TM // hw
    last = s // hw - 1
    return pl.pallas_call(
        functools.partial(_pool_kernel, n_lat_tiles=n_lat_tiles, seq=n_lat_tiles * TM, ctx_len=s - n_lat_tiles * TM),
        grid=(bsz, n_tiles),
        in_specs=[
            pl.BlockSpec((1, hw, c), lambda b, i: (b, jnp.maximum(i * per - 1, 0), 0)),
            pl.BlockSpec((1, TM, c), lambda b, i: (b, i, 0)),
            pl.BlockSpec((1, hw, c), lambda b, i: (b, jnp.minimum((i + 1) * per, last), 0)),
            pl.BlockSpec((c, c), lambda b, i: (0, 0)),
            pl.BlockSpec((1, c), lambda b, i: (0, 0)),
        ],
        out_specs=pl.BlockSpec((1, TM, c), lambda b, i: (b, i, 0)),
        out_shape=jax.ShapeDtypeStruct((bsz, n_tiles * TM, c), BF16),
        scratch_shapes=[pltpu.VMEM((TM + 2 * hw, c), F32)],
        compiler_params=_params("parallel", "parallel"),
        name="multiscale_pool",
    )(u, u, u, w_bd, scale)


def _gla_tile(qk_ref, v_ref, la_ref, cs_ref, sn_ref, st_ref, o_ref, reverse):
    ch = GLA_CHUNK
    lane = lax.broadcasted_iota(jnp.int32, (ch, GLA_QK), 1)
    row = lax.broadcasted_iota(jnp.int32, (ch, GLA_QK), 0)
    rr = lax.broadcasted_iota(jnp.int32, (GLA_HEADS * ch, ch), 0) % ch
    cc = lax.broadcasted_iota(jnp.int32, (GLA_HEADS * ch, ch), 1)
    causal = (rr <= cc) if reverse else (rr >= cc)
    st_shape = (GLA_WIDTH, GLA_QK)
    head_blk = (lax.broadcasted_iota(jnp.int32, st_shape, 0) // GLA_DV
                == lax.broadcasted_iota(jnp.int32, st_shape, 1) // GLA_DK)
    quarter = GLA_DK // 4
    first_half = (lane % (2 * quarter)) < quarter

    def rope(x, cs, sn):
        partner = jnp.where(first_half, pltpu.roll(x, GLA_QK - quarter, 1), pltpu.roll(x, quarter, 1))
        return x * cs + partner * sn

    chunks = range(TM // ch)
    for c in (reversed(chunks) if reverse else chunks):
        sl = slice(c * ch, (c + 1) * ch)
        cs = cs_ref[sl, :]
        sn = sn_ref[sl, :]
        q = rope(qk_ref[0, sl, 0:GLA_QK], cs, sn)
        k = rope(qk_ref[0, sl, GLA_QK:2 * GLA_QK], cs, sn)
        b = la_ref[0, sl, :]
        d = 1
        while d < ch:
            if reverse:
                b = b + jnp.where(row < ch - d, pltpu.roll(b, ch - d, 0), 0.0)
            else:
                b = b + jnp.where(row >= d, pltpu.roll(b, d, 0), 0.0)
            d *= 2
        b_last = b[0:1, :] if reverse else b[ch - 1:ch, :]
        q_in = q * jnp.exp(b)
        k_in = (k * jnp.exp(-b)).astype(BF16)
        k_st = (k * jnp.exp(b_last - b)).astype(BF16)
        att = _dot_nt(_stack_heads(q_in, GLA_DK).astype(BF16), k_in)
        att = jnp.where(causal, att, 0.0).astype(BF16)
        vb = v_ref[0, sl, :].astype(BF16)
        o = _unstack_heads(_dot(att, vb), GLA_DV)
        st = st_ref[...]
        o_ref[0, sl, :] = o + _dot_nt(q_in.astype(BF16), st.astype(BF16))
        st_ref[...] = st * jnp.exp(b_last) + jnp.where(head_blk, _dot_tn(vb, k_st), 0.0)


def _gla_kernel(qkf, vf, laf, csf, snf, qkb, vb, lab, csb, snb, of_ref, ob_ref, st_f, st_b):
    @pl.when(pl.program_id(1) == 0)
    def _():
        st_f[...] = jnp.zeros_like(st_f)
        st_b[...] = jnp.zeros_like(st_b)

    _gla_tile(qkf, vf, laf, csf, snf, st_f, of_ref, False)
    _gla_tile(qkb, vb, lab, csb, snb, st_b, ob_ref, True)


def _rope_tables(seq, s):
    quarter = GLA_DK // 4
    t = np.arange(s)
    pos_row = np.where(t < seq, t // GRID_W, 0).astype(np.float32)
    pos_col = np.where(t < seq, t % GRID_W, 0).astype(np.float32)
    d = np.arange(GLA_QK) % GLA_DK
    inv_freq = ROPE_BASE ** (-jnp.arange(quarter, dtype=F32) / quarter)
    freq = inv_freq[d % quarter]
    pos = jnp.where((d < GLA_DK // 2)[None, :], pos_row[:, None], pos_col[:, None])
    ang = pos * freq[None, :]
    sign = np.where((d % (2 * quarter)) < quarter, -1.0, 1.0).astype(np.float32)
    return jnp.cos(ang), jnp.sin(ang) * sign[None, :]


def _gla(gg, cos, sin, n_lat_tiles):
    bsz, s, _ = gg.shape
    nt = s // TM
    assert nt == n_lat_tiles + 1
    fwd = lambda i: (i + n_lat_tiles) % nt
    bwd = lambda i: jnp.where(i == 0, n_lat_tiles, n_lat_tiles - i)
    la_f_blk = (2 * GLA_QK + 2 * GLA_WIDTH) // GLA_QK
    la_b_blk = la_f_blk + 1
    v_blk = 2 * GLA_QK // GLA_WIDTH

    def specs(order, la_blk):
        return [
            pl.BlockSpec((1, TM, 2 * GLA_QK), lambda b, i: (b, order(i), 0)),
            pl.BlockSpec((1, TM, GLA_WIDTH), lambda b, i: (b, order(i), v_blk)),
            pl.BlockSpec((1, TM, GLA_QK), lambda b, i: (b, order(i), la_blk)),
            pl.BlockSpec((TM, GLA_QK), lambda b, i: (order(i), 0)),
            pl.BlockSpec((TM, GLA_QK), lambda b, i: (order(i), 0)),
        ]

    return pl.pallas_call(
        _gla_kernel,
        grid=(bsz, nt),
        in_specs=specs(fwd, la_f_blk) + specs(bwd, la_b_blk),
        out_specs=[
            pl.BlockSpec((1, TM, GLA_WIDTH), lambda b, i: (b, fwd(i), 0)),
            pl.BlockSpec((1, TM, GLA_WIDTH), lambda b, i: (b, bwd(i), 0)),
        ],
        out_shape=[jax.ShapeDtypeStruct((bsz, s, GLA_WIDTH), F32)] * 2,
        scratch_shapes=[pltpu.VMEM((GLA_WIDTH, GLA_QK), F32)] * 2,
        compiler_params=_params("arbitrary", "arbitrary"),
        name="gla_bidir",
    )(gg, gg, gg, cos, sin, gg, gg, gg, cos, sin)


def _outproj_kernel(x_ref, oa_ref, ob_ref, of_ref, obk_ref, r_ref, gg_ref, wo_ref, mod_ref, gf_ref, wr_ref,
                    xo_ref, h_ref, sc_ref):
    o = of_ref[0] + obk_ref[0]
    head = lax.broadcasted_iota(jnp.int32, o.shape, 1) // GLA_DV
    o2 = o * o
    rs = jnp.zeros_like(o)
    for h in range(GLA_HEADS):
        ssq = jnp.sum(jnp.where(head == h, o2, 0.0), axis=-1, keepdims=True)
        rs = jnp.where(head == h, lax.rsqrt(ssq / GLA_DV + NORM_EPS), rs)
    oc = (o * rs * gg_ref[...] * _silu(r_ref[0])).astype(BF16)
    c1 = NA_WIDTH + POOL_WIDTH
    acc = _dot(oa_ref[0], wo_ref[0:NA_WIDTH]) + _dot(ob_ref[0], wo_ref[NA_WIDTH:c1]) + _dot(oc, wo_ref[c1:])
    x = x_ref[0] + mod_ref[0, 0, 2:3, :] * acc
    xo_ref[0] = x
    h = _rmsnorm(x, gf_ref[...]) * (1.0 + mod_ref[0, 0, 4:5, :]) + mod_ref[0, 0, 3:4, :]
    h_ref[0] = h
    sc_ref[0] = _sigmoid(_dot(h.astype(BF16), wr_ref[...]))


def _outproj(xa, oa, ob, o_f, o_b, gg, g_gla4, w_out, mod, g_ffn, w_router, n_lat_tiles, n_tiles):
    bsz, _, d = xa.shape
    n_exp = w_router.shape[1]
    r_blk = (2 * GLA_QK + GLA_WIDTH) // GLA_WIDTH
    tile = lambda w: pl.BlockSpec((1, TM, w), lambda b, i: (b, i, 0))
    const = lambda b, i: (0, 0)
    return pl.pallas_call(
        _outproj_kernel,
        grid=(bsz, n_tiles),
        in_specs=[
            tile(d), tile(NA_WIDTH), tile(POOL_WIDTH), tile(GLA_WIDTH), tile(GLA_WIDTH),
            pl.BlockSpec((1, TM, GLA_WIDTH), lambda b, i: (b, i, r_blk)),
            pl.BlockSpec((1, GLA_WIDTH), const),
            pl.BlockSpec(w_out.shape, const),
            pl.BlockSpec((1, 1, 8, d), lambda b, i: (b, jnp.where(i >= n_lat_tiles, 0, 1), 0, 0)),
            pl.BlockSpec((1, d), const),
            pl.BlockSpec(w_router.shape, const),
        ],
        out_specs=[tile(d), tile(d), tile(n_exp)],
        out_shape=[
            jax.ShapeDtypeStruct((bsz, n_tiles * TM, d), F32),
            jax.ShapeDtypeStruct((bsz, n_tiles * TM, d), F32),
            jax.ShapeDtypeStruct((bsz, n_tiles * TM, n_exp), F32),
        ],
        compiler_params=_params("parallel", "parallel"),
        name="outproj_router",
    )(xa, oa, ob, o_f, o_b, gg, g_gla4, w_out, mod, g_ffn, w_router)


def _route(scores, b_router):
    t_tok, n_exp = scores.shape
    per_grp = n_exp // N_GROUPS
    biased = scores + b_router.astype(F32)
    grp_score = lax.top_k(biased.reshape(t_tok, N_GROUPS, per_grp), 2)[0].sum(-1)
    _, top_g = lax.top_k(grp_score, TOPK_GROUPS)
    gmask = jax.nn.one_hot(top_g, N_GROUPS, dtype=F32).sum(1) > 0
    masked = jnp.where(jnp.repeat(gmask, per_grp, axis=1), biased, -jnp.inf)
    _, top_e = lax.top_k(masked, TOP_K)
    w = jnp.take_along_axis(scores, top_e, axis=1)
    w = w / jnp.sum(w, axis=-1, keepdims=True) * ROUTED_SCALE

    n_asg = t_tok * TOP_K
    flat_e = top_e.reshape(n_asg)
    order = jnp.argsort(flat_e)
    sorted_e = flat_e[order]
    counts = jnp.bincount(flat_e, length=n_exp)
    padded = (counts + MOE_BLOCK - 1) // MOE_BLOCK * MOE_BLOCK
    start = jnp.cumsum(counts) - counts
    pad_end = jnp.cumsum(padded)
    pad_start = pad_end - padded
    dest = pad_start[sorted_e] + jnp.arange(n_asg) - start[sorted_e]
    n_blocks = -(-n_asg // MOE_BLOCK) + n_exp
    slots = n_blocks * MOE_BLOCK
    order = order.astype(jnp.int32)
    trash = TOP_K * t_tok + jnp.arange(slots, dtype=jnp.int32) % (2 * MOE_BLOCK)
    slot_tok = jnp.zeros((slots,), jnp.int32).at[dest].set(order // TOP_K)
    slot_dst = trash.at[dest].set((order % TOP_K) * t_tok + order // TOP_K)
    slot_w = jnp.zeros((slots,), F32).at[dest].set(w.reshape(n_asg)[order])
    block_e = jnp.minimum(jnp.searchsorted(pad_end, jnp.arange(n_blocks) * MOE_BLOCK, side='right'),
                          n_exp - 1).astype(jnp.int32)
    n_used = (pad_end[-1] // MOE_BLOCK).astype(jnp.int32).reshape(1)
    return (slot_tok.reshape(n_blocks, 1, MOE_BLOCK), slot_dst.reshape(n_blocks, 1, MOE_BLOCK),
            slot_w.reshape(n_blocks, MOE_BLOCK, 1), block_e, n_used)


def _moe_kernel(be_ref, nu_ref, tok0_ref, tokn_ref, dst_ref, w_ref, h_hbm, wi_ref, wd_ref, y_hbm,
                xbuf, ybuf, wi_bf, wd_bf, gsem, ssem, *, n_blocks):
    i = pl.program_id(0)
    n_used = nu_ref[0]
    slot = i % 2
    d_exp = wd_ref.shape[1]

    def gather(tok_ref, s):
        def body(k, c):
            pltpu.make_async_copy(h_hbm.at[pl.ds(tok_ref[0, 0, k], 1), :], xbuf.at[s, pl.ds(k, 1), :],
                                  gsem.at[s]).start()
            return c
        lax.fori_loop(0, MOE_BLOCK, body, 0, unroll=8)

    def wait_gather(s):
        pltpu.make_async_copy(h_hbm.at[pl.ds(0, MOE_BLOCK), :], xbuf.at[s], gsem.at[s]).wait()

    def wait_scatter(s):
        pltpu.make_async_copy(ybuf.at[s], y_hbm.at[pl.ds(0, MOE_BLOCK), :], ssem.at[s]).wait()

    @pl.when(i == 0)
    def _():
        trash0 = y_hbm.shape[0] - 2 * MOE_BLOCK
        ybuf[...] = jnp.zeros_like(ybuf)
        fills = [pltpu.make_async_copy(ybuf.at[s], y_hbm.at[pl.ds(trash0 + s * MOE_BLOCK, MOE_BLOCK), :], ssem.at[s])
                 for s in range(2)]
        for f in fills:
            f.start()
        for f in fills:
            f.wait()
        gather(tok0_ref, 0)

    @pl.when(i + 1 < n_used)
    def _():
        gather(tokn_ref, 1 - slot)

    @pl.when(i < n_used)
    def _():
        @pl.when((i == 0) | (be_ref[i] != be_ref[jnp.maximum(i - 1, 0)]))
        def _():
            wi_bf[...] = wi_ref[0].astype(BF16)
            wd_bf[...] = wd_ref[0].astype(BF16)

        wait_gather(slot)

        @pl.when(i >= 2)
        def _():
            wait_scatter(slot)

        hh = _dot(xbuf[slot].astype(BF16), wi_bf[...])
        a = (_silu(hh[:, :d_exp]) * hh[:, d_exp:]).astype(BF16)
        ybuf[slot] = _dot(a, wd_bf[...]) * w_ref[0]

        def body(k, c):
            pltpu.make_async_copy(ybuf.at[slot, pl.ds(k, 1), :], y_hbm.at[pl.ds(dst_ref[0, 0, k], 1), :],
                                  ssem.at[slot]).start()
            return c
        lax.fori_loop(0, MOE_BLOCK, body, 0, unroll=8)

    @pl.when(i == n_blocks - 1)
    def _():
        wait_scatter((n_used - 1) % 2)

        @pl.when(n_used >= 2)
        def _():
            wait_scatter(n_used % 2)


def _moe_routed(h2d, slot_tok, slot_dst, slot_w, block_e, n_used, w_e_in, w_e_down):
    t_tok, d = h2d.shape
    n_blocks = block_e.shape[0]
    _, _, f2 = w_e_in.shape
    smem_blk = lambda f: pl.BlockSpec((1, 1, MOE_BLOCK), f, memory_space=pltpu.SMEM)
    grid_spec = pltpu.PrefetchScalarGridSpec(
        num_scalar_prefetch=2,
        grid=(n_blocks,),
        in_specs=[
            smem_blk(lambda i, be, nu: (0, 0, 0)),
            smem_blk(lambda i, be, nu: (jnp.minimum(i + 1, n_blocks - 1), 0, 0)),
            smem_blk(lambda i, be, nu: (i, 0, 0)),
            pl.BlockSpec((1, MOE_BLOCK, 1), lambda i, be, nu: (i, 0, 0)),
            pl.BlockSpec(memory_space=pl.ANY),
            pl.BlockSpec((1, d, f2), lambda i, be, nu: (be[i], 0, 0)),
            pl.BlockSpec((1, f2 // 2, d), lambda i, be, nu: (be[i], 0, 0)),
        ],
        out_specs=pl.BlockSpec(memory_space=pl.ANY),
        scratch_shapes=[
            pltpu.VMEM((2, MOE_BLOCK, d), F32),
            pltpu.VMEM((2, MOE_BLOCK, d), F32),
            pltpu.VMEM((d, f2), BF16),
            pltpu.VMEM((f2 // 2, d), BF16),
            pltpu.SemaphoreType.DMA((2,)),
            pltpu.SemaphoreType.DMA((2,)),
        ],
    )
    return pl.pallas_call(
        functools.partial(_moe_kernel, n_blocks=n_blocks),
        grid_spec=grid_spec,
        out_shape=jax.ShapeDtypeStruct((TOP_K * t_tok + 2 * MOE_BLOCK, d), F32),
        compiler_params=_params("arbitrary"),
        name="moe_routed",
    )(block_e, n_used, slot_tok, slot_tok, slot_dst, slot_w, h2d, w_e_in, w_e_down)


def _combine_kernel(x_ref, h_ref, *rest, final):
    y_refs = rest[:TOP_K]
    wsi_ref, wsd_ref, mod_ref, gfin_ref, o_ref = rest[TOP_K:]
    d_exp = wsd_ref.shape[0]
    hh = _dot(h_ref[0].astype(BF16), wsi_ref[...])
    a = (_silu(hh[:, :d_exp]) * hh[:, d_exp:]).astype(BF16)
    y = _dot(a, wsd_ref[...])
    for y_ref in y_refs:
        y = y + y_ref[...]
    x = x_ref[0] + mod_ref[0, 0, 5:6, :] * y
    if final:
        x = _rmsnorm(x, gfin_ref[...])
    o_ref[0] = x


def _combine(x_mid, h2, y, w_sh_in, w_sh_down, mod, g_final, n_lat_tiles, final):
    bsz, s, d = x_mid.shape
    nt = s // TM
    t_blocks = bsz * nt
    const = lambda b, i: (0, 0)
    tile = pl.BlockSpec((1, TM, d), lambda b, i: (b, i, 0))
    y_specs = [pl.BlockSpec((TM, d), functools.partial(lambda b, i, k: (k * t_blocks + b * nt + i, 0), k=k))
               for k in range(TOP_K)]
    return pl.pallas_call(
        functools.partial(_combine_kernel, final=final),
        grid=(bsz, nt),
        in_specs=[tile, tile] + y_specs + [
            pl.BlockSpec(w_sh_in.shape, const),
            pl.BlockSpec(w_sh_down.shape, const),
            pl.BlockSpec((1, 1, 8, d), lambda b, i: (b, jnp.where(i >= n_lat_tiles, 0, 1), 0, 0)),
            pl.BlockSpec((1, d), const),
        ],
        out_specs=tile,
        out_shape=jax.ShapeDtypeStruct((bsz, s, d), F32),
        compiler_params=_params("parallel", "parallel"),
        name="moe_combine",
    )(x_mid, h2, *([y] * TOP_K), w_sh_in, w_sh_down, mod, g_final)


def _block_diag(w):
    g, a, b = w.shape
    out = jnp.zeros((g * a, g * b), w.dtype)
    for j in range(g):
        out = out.at[j * a:(j + 1) * a, j * b:(j + 1) * b].set(w[j])
    return out


def kernel(x, c, ctx, c_ctx, w_mod, b_mod, g_mix, w_in, rpb, w_pool, pool_scale, w_gate_f, b_gate_f, w_gate_b, b_gate_b, g_gla, w_out, g_ffn, w_router, b_router, w_e_in, w_e_down, w_sh_in, w_sh_down, g_final):
    bsz, seq, d = x.shape
    n_ctx = ctx.shape[1]
    depth = w_mod.shape[0]
    assert n_ctx == TM and seq % TM == 0 and seq % GRID_W == 0
    n_lat = seq // TM
    s = seq + n_ctx

    n_rows = -(-(bsz + 1) // 8) * 8
    cc = jnp.zeros((n_rows, d), F32).at[:bsz].set(c).at[bsz].set(c_ctx)
    mod_all = _modulation(cc, w_mod, b_mod).reshape(depth, n_rows, 6, d)
    cos, sin = _rope_tables(seq, s)

    xa = jnp.concatenate([x, ctx], axis=1)
    for layer in range(depth):
        last = layer == depth - 1
        n_tiles = n_lat if last else n_lat + 1
        m = mod_all[layer]
        mod = jnp.stack([jnp.broadcast_to(m[bsz], (bsz, 6, d)), m[:bsz]], axis=1)
        mod = jnp.pad(mod, ((0, 0), (0, 0), (0, 2), (0, 0)))

        wl = w_in[layer]
        w_main = wl[:, :MAIN_W].astype(BF16)
        w_low = jnp.pad(wl[:, MAIN_W:], ((0, 0), (0, LANES - 2 * GLA_GATE_RANK))).astype(BF16)
        w2 = jnp.zeros((LANES, 2 * GLA_QK), F32)
        w2 = w2.at[:GLA_GATE_RANK, :GLA_QK].set(w_gate_f[layer])
        w2 = w2.at[GLA_GATE_RANK:2 * GLA_GATE_RANK, GLA_QK:].set(w_gate_b[layer]).astype(BF16)
        b2 = jnp.concatenate([b_gate_f[layer], b_gate_b[layer]])[None, :]
        qkv, u, gg = _inproj(xa, mod, g_mix[layer][None, :], w_main, w_low, w2, b2, n_lat)

        oa = _neighborhood_attention(qkv, _na_bias_table(rpb[layer]), n_lat, not last)
        ob = _multiscale_pool(u, _block_diag(w_pool[layer]).astype(BF16), pool_scale[layer][None, :], n_lat, n_tiles)
        o_f, o_b = _gla(gg, cos, sin, n_lat)

        x_mid, h2, scores = _outproj(xa, oa, ob, o_f, o_b, gg, jnp.tile(g_gla[layer], GLA_HEADS)[None, :],
                                     w_out[layer].astype(BF16), mod, g_ffn[layer][None, :],
                                     w_router[layer].astype(BF16), n_lat, n_tiles)
        t_tok = bsz * n_tiles * TM
        slot_tok, slot_dst, slot_w, block_e, n_used = _route(scores.reshape(t_tok, -1), b_router[layer])
        y = _moe_routed(h2.reshape(t_tok, d), slot_tok, slot_dst, slot_w, block_e, n_used,
                        w_e_in[layer], w_e_down[layer])
        xa = _combine(x_mid, h2, y, w_sh_in[layer].astype(BF16), w_sh_down[layer].astype(BF16), mod,
                      g_final[None, :], n_lat, last)
    return xa
```

```python
import functools

import jax
import jax.numpy as jnp
import numpy as np
from jax import lax
from jax.experimental import pallas as pl
from jax.experimental.pallas import tpu as pltpu

GRID_W = 64
NORM_EPS = 1e-6
NA_HEADS = 8
NA_HEAD_DIM = 64
NA_WIDTH = NA_HEADS * NA_HEAD_DIM
NA_WIN_ROWS = 8
NA_WIN_COLS = 16
POOL_WINDOWS = (2, 4, 8, 16)
POOL_GROUP_DIM = 64
POOL_WIDTH = len(POOL_WINDOWS) * POOL_GROUP_DIM
POOL_REACH = max(POOL_WINDOWS) // 2
assert POOL_WINDOWS == tuple(2 ** (g + 1) for g in range(len(POOL_WINDOWS)))
GLA_HEADS = 4
GLA_DK = 32
GLA_DV = 64
GLA_QK = GLA_HEADS * GLA_DK
GLA_WIDTH = GLA_HEADS * GLA_DV
GLA_GATE_RANK = 16
GLA_TAU = 16.0
GLA_CHUNK = 64
ROPE_BASE = 10000.0
N_EXPERTS = 256
TOP_K = 8
N_GROUPS = 8
TOPK_GROUPS = 4
ROUTED_SCALE = 2.5

TM = 256
TMC = 128
MOE_BLOCK = 256
LANES = 128
MXU_DIM = 256
MASK_VALUE = -1e30
VMEM_LIMIT = 48 * 1024 * 1024

QKV_W = 3 * NA_WIDTH
G_W = 2 * GLA_QK + 2 * GLA_WIDTH + 2 * GLA_QK
MAIN_W = QKV_W + POOL_WIDTH + 2 * GLA_QK + 2 * GLA_WIDTH

BF16 = jnp.bfloat16
F32 = jnp.float32


def _params(*sem):
    return pltpu.CompilerParams(dimension_semantics=sem, vmem_limit_bytes=VMEM_LIMIT)


def _sigmoid(x):
    return 1.0 / (1.0 + jnp.exp(-x))


def _silu(x):
    return x * _sigmoid(x)


def _rmsnorm(x, g):
    return x * lax.rsqrt(jnp.mean(x * x, axis=-1, keepdims=True) + NORM_EPS) * g


def _dot(a, b):
    return jnp.dot(a, b, preferred_element_type=F32)


def _dot_nt(a, b):
    return lax.dot_general(a, b, (((1,), (1,)), ((), ())), preferred_element_type=F32)


def _dot_tn(a, b):
    return lax.dot_general(a, b, (((0,), (0,)), ((), ())), preferred_element_type=F32)


def _mod_kernel(c_ref, w_ref, b_ref, o_ref):
    a = _silu(c_ref[...]).astype(BF16)
    o_ref[0] = _dot(a, w_ref[0].astype(BF16)) + b_ref[0]


def _modulation(cc, w_mod, b_mod):
    depth, d, n = w_mod.shape
    r = cc.shape[0]
    tn = 1024
    return pl.pallas_call(
        _mod_kernel,
        grid=(depth, n // tn),
        in_specs=[
            pl.BlockSpec((r, d), lambda l, j: (0, 0)),
            pl.BlockSpec((1, d, tn), lambda l, j: (l, 0, j)),
            pl.BlockSpec((1, 1, tn), lambda l, j: (l, 0, j)),
        ],
        out_specs=pl.BlockSpec((1, r, tn), lambda l, j: (l, 0, j)),
        out_shape=jax.ShapeDtypeStruct((depth, r, n), F32),
        compiler_params=_params("parallel", "parallel"),
        name="modulation",
    )(cc, w_mod, b_mod.reshape(depth, 1, n))


def _inproj_kernel(x_ref, mod_ref, g_ref, wm_ref, wl_ref, w2_ref, b2_ref, qkv_ref, u_ref, gg_ref):
    h = _rmsnorm(x_ref[0], g_ref[...]) * (1.0 + mod_ref[0, 0, 1:2, :]) + mod_ref[0, 0, 0:1, :]
    hb = h.astype(BF16)
    q = _dot(hb, wm_ref[:, 0:NA_WIDTH]) * (NA_HEAD_DIM ** -0.5)
    qkv_ref[0, :, 0:NA_WIDTH] = q.astype(BF16)
    qkv_ref[0, :, NA_WIDTH:2 * NA_WIDTH] = _dot(hb, wm_ref[:, NA_WIDTH:2 * NA_WIDTH]).astype(BF16)
    qkv_ref[0, :, 2 * NA_WIDTH:QKV_W] = _dot(hb, wm_ref[:, 2 * NA_WIDTH:QKV_W]).astype(BF16)
    u_ref[0] = _dot(hb, wm_ref[:, QKV_W:QKV_W + POOL_WIDTH])
    c0 = QKV_W + POOL_WIDTH
    gg_ref[0, :, 0:GLA_QK] = _dot(hb, wm_ref[:, c0:c0 + GLA_QK]) * (GLA_DK ** -0.5)
    gg_ref[0, :, GLA_QK:2 * GLA_QK + 2 * GLA_WIDTH] = _dot(hb, wm_ref[:, c0 + GLA_QK:MAIN_W])
    a_low = _dot(hb, wl_ref[...]).astype(BF16)
    lg = _dot(a_low, w2_ref[...]) + b2_ref[...]
    log_sig = jnp.minimum(lg, 0.0) - jnp.log1p(jnp.exp(-jnp.abs(lg)))
    gg_ref[0, :, 2 * GLA_QK + 2 * GLA_WIDTH:G_W] = log_sig / GLA_TAU


def _inproj(xa, mod, g, w_main, w_low, w2, b2, n_lat_tiles):
    bsz, s, d = xa.shape
    nt = s // TM
    const = lambda b, i: (0, 0)
    return pl.pallas_call(
        _inproj_kernel,
        grid=(bsz, nt),
        in_specs=[
            pl.BlockSpec((1, TM, d), lambda b, i: (b, i, 0)),
            pl.BlockSpec((1, 1, 8, d), lambda b, i: (b, jnp.where(i >= n_lat_tiles, 0, 1), 0, 0)),
            pl.BlockSpec((1, d), const),
            pl.BlockSpec(w_main.shape, const),
            pl.BlockSpec(w_low.shape, const),
            pl.BlockSpec(w2.shape, const),
            pl.BlockSpec(b2.shape, const),
        ],
        out_specs=[
            pl.BlockSpec((1, TM, QKV_W), lambda b, i: (b, i, 0)),
            pl.BlockSpec((1, TM, POOL_WIDTH), lambda b, i: (b, i, 0)),
            pl.BlockSpec((1, TM, G_W), lambda b, i: (b, i, 0)),
        ],
        out_shape=[
            jax.ShapeDtypeStruct((bsz, s, QKV_W), BF16),
            jax.ShapeDtypeStruct((bsz, s, POOL_WIDTH), F32),
            jax.ShapeDtypeStruct((bsz, s, G_W), F32),
        ],
        compiler_params=_params("parallel", "parallel"),
        name="inproj",
    )(xa, mod, g, w_main, w_low, w2, b2)


HEADS_PER_GROUP = MXU_DIM // NA_HEAD_DIM
NA_GROUPS = NA_HEADS // HEADS_PER_GROUP
NA_LOCAL_KEYS = NA_WIN_ROWS * GRID_W


def _stack_heads(x, width):
    lane = lax.broadcasted_iota(jnp.int32, x.shape, 1) // width
    n_heads = x.shape[1] // width
    return jnp.concatenate([jnp.where(lane == h, x, jnp.zeros_like(x)) for h in range(n_heads)], axis=0)


def _unstack_heads(o, width):
    n_heads = o.shape[1] // width
    r = o.shape[0] // n_heads
    lane = lax.broadcasted_iota(jnp.int32, (r, o.shape[1]), 1) // width
    acc = jnp.zeros((r, o.shape[1]), o.dtype)
    for h in range(n_heads):
        acc = jnp.where(lane == h, o[h * r:(h + 1) * r, :], acc)
    return acc


def _na_kernel(q_ref, k_ref, v_ref, kc_ref, vc_ref, bias_ref, o_ref, *, n_rows):
    r = pl.program_id(1)

    def attend(g, local):
        cols = slice(g * MXU_DIM, (g + 1) * MXU_DIM)
        qs = _stack_heads(q_ref[0, :, cols], NA_HEAD_DIM)
        kc = kc_ref[0, :, cols]
        vc = vc_ref[0, :, cols]
        s_ctx = _dot_nt(qs, kc)
        m = jnp.max(s_ctx, axis=-1, keepdims=True)
        if local:
            start = pl.multiple_of(jnp.clip(r - NA_WIN_ROWS // 2, 0, n_rows - NA_WIN_ROWS) * GRID_W, GRID_W)
            kw = k_ref[0, pl.ds(start, NA_LOCAL_KEYS), cols]
            vw = v_ref[0, pl.ds(start, NA_LOCAL_KEYS), cols]
            s_loc = _dot_nt(qs, kw) + bias_ref[0, g]
            m = jnp.maximum(m, jnp.max(s_loc, axis=-1, keepdims=True))
            p_loc = jnp.exp(s_loc - m)
        p_ctx = jnp.exp(s_ctx - m)
        den = jnp.sum(p_ctx, axis=-1, keepdims=True)
        o = _dot(p_ctx.astype(BF16), vc)
        if local:
            den = den + jnp.sum(p_loc, axis=-1, keepdims=True)
            o = o + _dot(p_loc.astype(BF16), vw)
        o_ref[0, :, cols] = _unstack_heads(o / den, NA_HEAD_DIM).astype(o_ref.dtype)

    @pl.when(r < n_rows)
    def _():
        for g in range(NA_GROUPS):
            attend(g, True)

    @pl.when(r >= n_rows)
    def _():
        for g in range(NA_GROUPS):
            attend(g, False)


def _na_bias_table(rpb):
    o = jnp.arange(NA_WIN_ROWS)
    dr = o[None, :] - o[:, None] + NA_WIN_ROWS - 1
    col = jnp.arange(GRID_W)
    dc = jnp.clip(col[None, :] - col[:, None], -(NA_WIN_COLS - 1), NA_WIN_COLS - 1) + NA_WIN_COLS - 1
    col_start = jnp.clip(col - NA_WIN_COLS // 2, 0, GRID_W - NA_WIN_COLS)
    col_mask = (col[None, :] >= col_start[:, None]) & (col[None, :] < col_start[:, None] + NA_WIN_COLS)
    t = rpb[:, dr][..., dc].astype(F32)
    t = jnp.where(col_mask[None, None, None], t, MASK_VALUE)
    t = t.transpose(1, 0, 3, 2, 4)
    return t.reshape(NA_WIN_ROWS, NA_GROUPS, HEADS_PER_GROUP * GRID_W, NA_LOCAL_KEYS)


def _neighborhood_attention(qkv, bias_tab, n_lat_tiles, with_ctx_queries):
    bsz, s, _ = qkv.shape
    seq = n_lat_tiles * TM
    n_rows = seq // GRID_W
    assert n_rows >= NA_WIN_ROWS
    n_ctx_rows = (s - seq) // GRID_W
    nq = n_rows + (n_ctx_rows if with_ctx_queries else 0)
    ctx_blk = seq // (s - seq)

    def bias_idx(b, r):
        rr = jnp.minimum(r, n_rows - 1)
        return (rr - jnp.clip(rr - NA_WIN_ROWS // 2, 0, n_rows - NA_WIN_ROWS), 0, 0, 0)

    return pl.pallas_call(
        functools.partial(_na_kernel, n_rows=n_rows),
        grid=(bsz, nq),
        in_specs=[
            pl.BlockSpec((1, GRID_W, NA_WIDTH), lambda b, r: (b, r, 0)),
            pl.BlockSpec((1, seq, NA_WIDTH), lambda b, r: (b, 0, 1)),
            pl.BlockSpec((1, seq, NA_WIDTH), lambda b, r: (b, 0, 2)),
            pl.BlockSpec((1, s - seq, NA_WIDTH), lambda b, r: (b, ctx_blk, 1)),
            pl.BlockSpec((1, s - seq, NA_WIDTH), lambda b, r: (b, ctx_blk, 2)),
            pl.BlockSpec((1,) + bias_tab.shape[1:], bias_idx),
        ],
        out_specs=pl.BlockSpec((1, GRID_W, NA_WIDTH), lambda b, r: (b, r, 0)),
        out_shape=jax.ShapeDtypeStruct((bsz, nq * GRID_W, NA_WIDTH), BF16),
        compiler_params=_params("parallel", "arbitrary"),
        name="neighborhood_attention",
    )(qkv, qkv, qkv, qkv, qkv, bias_tab)


def _pool_kernel(up_ref, u_ref, un_ref, w_ref, sc_ref, o_ref, buf, *, n_lat_tiles, seq, ctx_len):
    i = pl.program_id(1)
    hw = POOL_REACH
    buf[0:hw] = up_ref[0]
    buf[hw:hw + TM] = u_ref[0]
    buf[hw + TM:2 * hw + TM] = un_ref[0]
    is_ctx = i >= n_lat_tiles
    base = jnp.where(is_ctx, 0, i * TM)
    lseq = jnp.where(is_ctx, ctx_len, seq)
    shape = (TM, POOL_WIDTH)
    t = base + lax.broadcasted_iota(jnp.int32, shape, 0)
    grp = lax.broadcasted_iota(jnp.int32, shape, 1) // POOL_GROUP_DIM
    win = jnp.left_shift(2, grp)
    back = win // 2
    fwd = win - back - 1
    acc = jnp.zeros(shape, F32)
    for d in range(-hw, hw):
        ok = (t + d >= 0) & (t + d < lseq) & (back >= -d) & (fwd >= d)
        acc = acc + jnp.where(ok, buf[hw + d:hw + d + TM, :], 0.0)
    lo = jnp.clip(t - back, 0, lseq - 1)
    hi = jnp.clip(t + fwd, 0, lseq - 1)
    mean = acc / (hi - lo + 1).astype(F32)
    diff = (mean - u_ref[0]).astype(BF16)
    o_ref[0] = (_dot(diff, w_ref[...]) * sc_ref[...]).astype(o_ref.dtype)


def _multiscale_pool(u, w_bd, scale, n_lat_tiles, n_tiles):
    bsz, s, c = u.shape
    hw = POOL_REACH
    per = TM // hw
    last = s // hw - 1
    return pl.pallas_call(
        functools.partial(_pool_kernel, n_lat_tiles=n_lat_tiles, seq=n_lat_tiles * TM, ctx_len=s - n_lat_tiles * TM),
        grid=(bsz, n_tiles),
        in_specs=[
            pl.BlockSpec((1, hw, c), lambda b, i: (b, jnp.maximum(i * per - 1, 0), 0)),
            pl.BlockSpec((1, TM, c), lambda b, i: (b, i, 0)),
            pl.BlockSpec((1, hw, c), lambda b, i: (b, jnp.minimum((i + 1) * per, last), 0)),
            pl.BlockSpec((c, c), lambda b, i: (0, 0)),
            pl.BlockSpec((1, c), lambda b, i: (0, 0)),
        ],
        out_specs=pl.BlockSpec((1, TM, c), lambda b, i: (b, i, 0)),
        out_shape=jax.ShapeDtypeStruct((bsz, n_tiles * TM, c), BF16),
        scratch_shapes=[pltpu.VMEM((TM + 2 * hw, c), F32)],
        compiler_params=_params("parallel", "parallel"),
        name="multiscale_pool",
    )(u, u, u, w_bd, scale)


def _gla_tile(qk_ref, v_ref, la_ref, cs_ref, sn_ref, st_ref, o_ref, reverse):
    ch = GLA_CHUNK
    lane = lax.broadcasted_iota(jnp.int32, (ch, GLA_QK), 1)
    row = lax.broadcasted_iota(jnp.int32, (ch, GLA_QK), 0)
    rr = lax.broadcasted_iota(jnp.int32, (GLA_HEADS * ch, ch), 0) % ch
    cc = lax.broadcasted_iota(jnp.int32, (GLA_HEADS * ch, ch), 1)
    causal = (rr <= cc) if reverse else (rr >= cc)
    st_shape = (GLA_WIDTH, GLA_QK)
    head_blk = (lax.broadcasted_iota(jnp.int32, st_shape, 0) // GLA_DV
                == lax.broadcasted_iota(jnp.int32, st_shape, 1) // GLA_DK)
    quarter = GLA_DK // 4
    first_half = (lane % (2 * quarter)) < quarter

    def rope(x, cs, sn):
        partner = jnp.where(first_half, pltpu.roll(x, GLA_QK - quarter, 1), pltpu.roll(x, quarter, 1))
        return x * cs + partner * sn

    chunks = range(TM // ch)
    for c in (reversed(chunks) if reverse else chunks):
        sl = slice(c * ch, (c + 1) * ch)
        cs = cs_ref[sl, :]
        sn = sn_ref[sl, :]
        q = rope(qk_ref[0, sl, 0:GLA_QK], cs, sn)
        k = rope(qk_ref[0, sl, GLA_QK:2 * GLA_QK], cs, sn)
        b = la_ref[0, sl, :]
        d = 1
        while d < ch:
            if reverse:
                b = b + jnp.where(row < ch - d, pltpu.roll(b, ch - d, 0), 0.0)
            else:
                b = b + jnp.where(row >= d, pltpu.roll(b, d, 0), 0.0)
            d *= 2
        b_last = b[0:1, :] if reverse else b[ch - 1:ch, :]
        q_in = q * jnp.exp(b)
        k_in = (k * jnp.exp(-b)).astype(BF16)
        k_st = (k * jnp.exp(b_last - b)).astype(BF16)
        att = _dot_nt(_stack_heads(q_in, GLA_DK).astype(BF16), k_in)
        att = jnp.where(causal, att, 0.0).astype(BF16)
        vb = v_ref[0, sl, :].astype(BF16)
        o = _unstack_heads(_dot(att, vb), GLA_DV)
        st = st_ref[...]
        o_ref[0, sl, :] = o + _dot_nt(q_in.astype(BF16), st.astype(BF16))
        st_ref[...] = st * jnp.exp(b_last) + jnp.where(head_blk, _dot_tn(vb, k_st), 0.0)


def _gla_kernel(qkf, vf, laf, csf, snf, qkb, vb, lab, csb, snb, of_ref, ob_ref, st_f, st_b):
    @pl.when(pl.program_id(1) == 0)
    def _():
        st_f[...] = jnp.zeros_like(st_f)
        st_b[...] = jnp.zeros_like(st_b)

    _gla_tile(qkf, vf, laf, csf, snf, st_f, of_ref, False)
    _gla_tile(qkb, vb, lab, csb, snb, st_b, ob_ref, True)


def _rope_tables(seq, s):
    quarter = GLA_DK // 4
    t = np.arange(s)
    pos_row = np.where(t < seq, t // GRID_W, 0).astype(np.float32)
    pos_col = np.where(t < seq, t % GRID_W, 0).astype(np.float32)
    d = np.arange(GLA_QK) % GLA_DK
    inv_freq = ROPE_BASE ** (-jnp.arange(quarter, dtype=F32) / quarter)
    freq = inv_freq[d % quarter]
    pos = jnp.where((d < GLA_DK // 2)[None, :], pos_row[:, None], pos_col[:, None])
    ang = pos * freq[None, :]
    sign = np.where((d % (2 * quarter)) < quarter, -1.0, 1.0).astype(np.float32)
    return jnp.cos(ang), jnp.sin(ang) * sign[None, :]


def _gla(gg, cos, sin, n_lat_tiles):
    bsz, s, _ = gg.shape
    nt = s // TM
    assert nt == n_lat_tiles + 1
    fwd = lambda i: (i + n_lat_tiles) % nt
    bwd = lambda i: jnp.where(i == 0, n_lat_tiles, n_lat_tiles - i)
    la_f_blk = (2 * GLA_QK + 2 * GLA_WIDTH) // GLA_QK
    la_b_blk = la_f_blk + 1
    v_blk = 2 * GLA_QK // GLA_WIDTH

    def specs(order, la_blk):
        return [
            pl.BlockSpec((1, TM, 2 * GLA_QK), lambda b, i: (b, order(i), 0)),
            pl.BlockSpec((1, TM, GLA_WIDTH), lambda b, i: (b, order(i), v_blk)),
            pl.BlockSpec((1, TM, GLA_QK), lambda b, i: (b, order(i), la_blk)),
            pl.BlockSpec((TM, GLA_QK), lambda b, i: (order(i), 0)),
            pl.BlockSpec((TM, GLA_QK), lambda b, i: (order(i), 0)),
        ]

    return pl.pallas_call(
        _gla_kernel,
        grid=(bsz, nt),
        in_specs=specs(fwd, la_f_blk) + specs(bwd, la_b_blk),
        out_specs=[
            pl.BlockSpec((1, TM, GLA_WIDTH), lambda b, i: (b, fwd(i), 0)),
            pl.BlockSpec((1, TM, GLA_WIDTH), lambda b, i: (b, bwd(i), 0)),
        ],
        out_shape=[jax.ShapeDtypeStruct((bsz, s, GLA_WIDTH), F32)] * 2,
        scratch_shapes=[pltpu.VMEM((GLA_WIDTH, GLA_QK), F32)] * 2,
        compiler_params=_params("arbitrary", "arbitrary"),
        name="gla_bidir",
    )(gg, gg, gg, cos, sin, gg, gg, gg, cos, sin)


def _row_index(shape):
    return lax.broadcasted_iota(jnp.int32, shape, 0).astype(F32)


def _stack_rows(rows):
    shape = (len(rows), rows[0].shape[1])
    rid = lax.broadcasted_iota(jnp.int32, shape, 0)
    out = jnp.zeros(shape, rows[0].dtype)
    for k, r in enumerate(rows):
        out = jnp.where(rid == k, r, out)
    return out


def _select_experts(scores, biased):
    n_exp, n_tok = biased.shape
    per = n_exp // N_GROUPS
    neg = -jnp.inf
    sub = _row_index((per, n_tok))
    grp_rows = []
    for g in range(N_GROUPS):
        blk = biased[g * per:(g + 1) * per, :]
        m1 = jnp.max(blk, axis=0, keepdims=True)
        i1 = jnp.min(jnp.where(blk == m1, sub, float(per)), axis=0, keepdims=True)
        m2 = jnp.max(jnp.where(sub == i1, neg, blk), axis=0, keepdims=True)
        grp_rows.append(m1 + m2)
    cur = _stack_rows(grp_rows)
    gid = _row_index(cur.shape)
    picked = jnp.zeros(cur.shape, F32)
    for _ in range(TOPK_GROUPS):
        gm = jnp.max(cur, axis=0, keepdims=True)
        gi = jnp.min(jnp.where(cur == gm, gid, float(N_GROUPS)), axis=0, keepdims=True)
        hit = gid == gi
        picked = jnp.where(hit, 1.0, picked)
        cur = jnp.where(hit, neg, cur)
    cur = jnp.concatenate(
        [jnp.where(picked[g:g + 1, :] > 0.0, biased[g * per:(g + 1) * per, :], neg) for g in range(N_GROUPS)], axis=0)
    eid = _row_index(cur.shape)
    ids, vals = [], []
    chosen = jnp.zeros(cur.shape, F32)
    for _ in range(TOP_K):
        m = jnp.max(cur, axis=0, keepdims=True)
        idx = jnp.min(jnp.where(cur == m, eid, float(n_exp)), axis=0, keepdims=True)
        hit = eid == idx
        ids.append(idx)
        vals.append(jnp.sum(jnp.where(hit, scores, 0.0), axis=0, keepdims=True))
        chosen = jnp.where(hit, 1.0, chosen)
        cur = jnp.where(hit, neg, cur)
    return ids, vals, chosen


def _outproj_kernel(x_ref, oa_ref, ob_ref, of_ref, obk_ref, r_ref, gg_ref, wo_ref, mod_ref, gf_ref, wr_ref, br_ref,
                    tri_ref, xo_ref, h_ref, e_ref, w_ref, rk_ref, hist_ref):
    o = of_ref[0] + obk_ref[0]
    head = lax.broadcasted_iota(jnp.int32, o.shape, 1) // GLA_DV
    o2 = o * o
    rs = jnp.zeros_like(o)
    for h in range(GLA_HEADS):
        ssq = jnp.sum(jnp.where(head == h, o2, 0.0), axis=-1, keepdims=True)
        rs = jnp.where(head == h, lax.rsqrt(ssq / GLA_DV + NORM_EPS), rs)
    oc = (o * rs * gg_ref[...] * _silu(r_ref[0])).astype(BF16)
    c1 = NA_WIDTH + POOL_WIDTH
    acc = _dot(oa_ref[0], wo_ref[0:NA_WIDTH]) + _dot(ob_ref[0], wo_ref[NA_WIDTH:c1]) + _dot(oc, wo_ref[c1:])
    x = x_ref[0] + mod_ref[0, 0, 2:3, :] * acc
    xo_ref[0] = x
    h = _rmsnorm(x, gf_ref[...]) * (1.0 + mod_ref[0, 0, 4:5, :]) + mod_ref[0, 0, 3:4, :]
    h_ref[0] = h

    scores = _sigmoid(_dot_nt(wr_ref[...], h.astype(BF16)))
    ids, vals, chosen = _select_experts(scores, scores + br_ref[...])
    total = vals[0]
    for v in vals[1:]:
        total = total + v
    w_ref[...] = _stack_rows([v / total * ROUTED_SCALE for v in vals])
    e_ref[...] = _stack_rows(ids).astype(jnp.int32)
    chosen_b = chosen.astype(BF16)
    before = _dot(chosen_b, tri_ref[...])
    eid = _row_index(before.shape)
    rk_ref[...] = _stack_rows([jnp.sum(jnp.where(eid == idx, before, 0.0), axis=0, keepdims=True) for idx in ids])
    hist_ref[0] = _dot_nt(jnp.ones((8, chosen.shape[1]), BF16), chosen_b)


def _outproj(xa, oa, ob, o_f, o_b, gg, g_gla4, w_out, mod, g_ffn, w_router_t, b_router, n_lat_tiles, n_tiles):
    bsz, _, d = xa.shape
    n_exp = w_router_t.shape[0]
    r_blk = (2 * GLA_QK + GLA_WIDTH) // GLA_WIDTH
    tile = lambda w: pl.BlockSpec((1, TM, w), lambda b, i: (b, i, 0))
    const = lambda b, i: (0, 0)
    per_tok = pl.BlockSpec((TOP_K, TM), lambda b, i: (0, b * n_tiles + i))
    t_tok = bsz * n_tiles * TM
    tri = (jnp.arange(TM)[:, None] < jnp.arange(TM)[None, :]).astype(BF16)
    return pl.pallas_call(
        _outproj_kernel,
        grid=(bsz, n_tiles),
        in_specs=[
            tile(d), tile(NA_WIDTH), tile(POOL_WIDTH), tile(GLA_WIDTH), tile(GLA_WIDTH),
            pl.BlockSpec((1, TM, GLA_WIDTH), lambda b, i: (b, i, r_blk)),
            pl.BlockSpec((1, GLA_WIDTH), const),
            pl.BlockSpec(w_out.shape, const),
            pl.BlockSpec((1, 1, 8, d), lambda b, i: (b, jnp.where(i >= n_lat_tiles, 0, 1), 0, 0)),
            pl.BlockSpec((1, d), const),
            pl.BlockSpec(w_router_t.shape, const),
            pl.BlockSpec((n_exp, 1), const),
            pl.BlockSpec((TM, TM), const),
        ],
        out_specs=[tile(d), tile(d), per_tok, per_tok, per_tok,
                   pl.BlockSpec((1, 8, n_exp), lambda b, i: (b * n_tiles + i, 0, 0))],
        out_shape=[
            jax.ShapeDtypeStruct((bsz, n_tiles * TM, d), F32),
            jax.ShapeDtypeStruct((bsz, n_tiles * TM, d), F32),
            jax.ShapeDtypeStruct((TOP_K, t_tok), jnp.int32),
            jax.ShapeDtypeStruct((TOP_K, t_tok), F32),
            jax.ShapeDtypeStruct((TOP_K, t_tok), F32),
            jax.ShapeDtypeStruct((bsz * n_tiles, 8, n_exp), F32),
        ],
        compiler_params=_params("parallel", "parallel"),
        name="outproj_router",
    )(xa, oa, ob, o_f, o_b, gg, g_gla4, w_out, mod, g_ffn, w_router_t, b_router, tri)


def _slot_layout(hist):
    n_tiles, n_exp = hist.shape
    counts = hist.sum(0)
    tile_base = jnp.cumsum(hist, axis=0) - hist
    padded = (counts + MOE_BLOCK - 1) // MOE_BLOCK * MOE_BLOCK
    pad_end = jnp.cumsum(padded)
    pad_start = pad_end - padded
    base = (pad_start[None, :] + tile_base).astype(F32).reshape(n_tiles, n_exp, 1)
    n_blocks = -(-(n_tiles * TM * TOP_K) // MOE_BLOCK) + n_exp
    blk_start = jnp.arange(n_blocks, dtype=jnp.int32) * MOE_BLOCK
    block_e = jnp.minimum(jnp.searchsorted(pad_end, blk_start, side='right'), n_exp - 1).astype(jnp.int32)
    n_used = (pad_end[-1] // MOE_BLOCK).astype(jnp.int32)
    fill = jnp.concatenate([jnp.maximum(pad_end // MOE_BLOCK - 1, 0).astype(jnp.int32),
                            jnp.minimum(n_used + jnp.arange(n_exp, dtype=jnp.int32), n_blocks - 1)])
    return base, block_e, fill, n_used.reshape(1)


def _pos_kernel(e_ref, rk_ref, base_ref, pos_ref):
    eid = lax.broadcasted_iota(jnp.int32, (base_ref.shape[1], e_ref.shape[1]), 0)
    base = base_ref[0]
    rows = [jnp.sum(jnp.where(eid == e_ref[k:k + 1, :], base, 0.0), axis=0, keepdims=True) for k in range(TOP_K)]
    pos_ref[...] = (_stack_rows(rows) + rk_ref[...]).astype(jnp.int32)


def _slot_positions(e_t, rank_t, base):
    n_tiles, n_exp, _ = base.shape
    per_tok = pl.BlockSpec((TOP_K, TM), lambda i: (0, i))
    return pl.pallas_call(
        _pos_kernel,
        grid=(n_tiles,),
        in_specs=[per_tok, per_tok, pl.BlockSpec((1, n_exp, 1), lambda i: (i, 0, 0))],
        out_specs=per_tok,
        out_shape=jax.ShapeDtypeStruct(e_t.shape, jnp.int32),
        compiler_params=_params("parallel"),
        name="moe_positions",
    )(e_t, rank_t, base)


def _dispatch_kernel(fill_ref, pos_ref, h_ref, xs_hbm, zbuf, sem, zsem):
    @pl.when(pl.program_id(0) == 0)
    def _():
        zbuf[...] = jnp.zeros_like(zbuf)
        n_fill = fill_ref.shape[0]

        def is_new(j):
            return (j == 0) | (fill_ref[j] != fill_ref[jnp.maximum(j - 1, 0)])

        def start(j, c):
            @pl.when(is_new(j))
            def _():
                row0 = pl.multiple_of(fill_ref[j] * MOE_BLOCK, MOE_BLOCK)
                pltpu.make_async_copy(zbuf, xs_hbm.at[pl.ds(row0, MOE_BLOCK), :], zsem).start()
            return c
        lax.fori_loop(0, n_fill, start, 0)

        def wait(j, c):
            @pl.when(is_new(j))
            def _():
                pltpu.make_async_copy(zbuf, xs_hbm.at[pl.ds(0, MOE_BLOCK), :], zsem).wait()
            return c
        lax.fori_loop(0, n_fill, wait, 0)

    def body(t, c):
        for k in range(TOP_K):
            pltpu.make_async_copy(h_ref.at[pl.ds(t, 1), :], xs_hbm.at[pl.ds(pos_ref[k, t], 1), :], sem).start()
        return c
    lax.fori_loop(0, TM, body, 0)
    for _ in range(TOP_K):
        pltpu.make_async_copy(h_ref, xs_hbm.at[pl.ds(0, TM), :], sem).wait()


def _dispatch(h2d, pos_t, fill, n_slots):
    t_tok, d = h2d.shape
    grid_spec = pltpu.PrefetchScalarGridSpec(
        num_scalar_prefetch=1,
        grid=(t_tok // TM,),
        in_specs=[
            pl.BlockSpec((TOP_K, TM), lambda i, fill: (0, i), memory_space=pltpu.SMEM),
            pl.BlockSpec((TM, d), lambda i, fill: (i, 0)),
        ],
        out_specs=pl.BlockSpec(memory_space=pl.ANY),
        scratch_shapes=[pltpu.VMEM((MOE_BLOCK, d), F32), pltpu.SemaphoreType.DMA, pltpu.SemaphoreType.DMA],
    )
    return pl.pallas_call(
        _dispatch_kernel,
        grid_spec=grid_spec,
        out_shape=jax.ShapeDtypeStruct((n_slots, d), F32),
        compiler_params=_params("arbitrary"),
        name="moe_dispatch",
    )(fill, pos_t, h2d)


def _expert_kernel(be_ref, nu_ref, x_ref, wi_ref, wd_ref, y_ref, wi_bf, wd_bf):
    i = pl.program_id(0)
    d_exp = wd_ref.shape[1]

    @pl.when(i < nu_ref[0])
    def _():
        @pl.when((i == 0) | (be_ref[i] != be_ref[jnp.maximum(i - 1, 0)]))
        def _():
            wi_bf[...] = wi_ref[0].astype(BF16)
            wd_bf[...] = wd_ref[0].astype(BF16)

        hh = _dot(x_ref[...].astype(BF16), wi_bf[...])
        a = (_silu(hh[:, :d_exp]) * hh[:, d_exp:]).astype(BF16)
        y_ref[...] = _dot(a, wd_bf[...])

    @pl.when(i >= nu_ref[0])
    def _():
        y_ref[...] = jnp.zeros_like(y_ref)


def _moe_experts(xs, block_e, n_used, w_e_in, w_e_down):
    n_slots, d = xs.shape
    n_blocks = block_e.shape[0]
    _, _, f2 = w_e_in.shape
    grid_spec = pltpu.PrefetchScalarGridSpec(
        num_scalar_prefetch=2,
        grid=(n_blocks,),
        in_specs=[
            pl.BlockSpec((MOE_BLOCK, d), lambda i, be, nu: (jnp.minimum(i, nu[0] - 1), 0)),
            pl.BlockSpec((1, d, f2), lambda i, be, nu: (be[i], 0, 0)),
            pl.BlockSpec((1, f2 // 2, d), lambda i, be, nu: (be[i], 0, 0)),
        ],
        out_specs=pl.BlockSpec((MOE_BLOCK, d), lambda i, be, nu: (i, 0)),
        scratch_shapes=[pltpu.VMEM((d, f2), BF16), pltpu.VMEM((f2 // 2, d), BF16)],
    )
    return pl.pallas_call(
        _expert_kernel,
        grid_spec=grid_spec,
        out_shape=jax.ShapeDtypeStruct((n_slots, d), F32),
        compiler_params=_params("arbitrary"),
        name="moe_experts",
    )(block_e, n_used, xs, w_e_in, w_e_down)


def _combine_kernel(pos0_ref, posn_ref, x_ref, h_ref, w_ref, wsi_ref, wsd_ref, mod_ref, gfin_ref, ys_hbm, o_ref,
                    ybuf, sem, *, final, n_steps):
    step = pl.program_id(0) * pl.num_programs(1) + pl.program_id(1)
    slot = step % 2
    d_exp = wsd_ref.shape[0]

    def gather(pos_ref, s):
        def body(t, c):
            for k in range(TOP_K):
                pltpu.make_async_copy(ys_hbm.at[pl.ds(pos_ref[k, t], 1), :], ybuf.at[s, k, pl.ds(t, 1), :],
                                      sem.at[s]).start()
            return c
        lax.fori_loop(0, TMC, body, 0)

    @pl.when(step == 0)
    def _():
        gather(pos0_ref, 0)

    @pl.when(step + 1 < n_steps)
    def _():
        gather(posn_ref, 1 - slot)

    hh = _dot(h_ref[0].astype(BF16), wsi_ref[...])
    a = (_silu(hh[:, :d_exp]) * hh[:, d_exp:]).astype(BF16)
    y = _dot(a, wsd_ref[...])
    for k in range(TOP_K):
        pltpu.make_async_copy(ys_hbm.at[pl.ds(0, TMC), :], ybuf.at[slot, k], sem.at[slot]).wait()
    w = w_ref[...]
    for k in range(TOP_K):
        y = y + ybuf[slot, k] * w[:, k:k + 1]
    x = x_ref[0] + mod_ref[0, 0, 5:6, :] * y
    if final:
        x = _rmsnorm(x, gfin_ref[...])
    o_ref[0] = x


def _combine(x_mid, h2, ys, pos_t, w_tok, w_sh_in, w_sh_down, mod, g_final, n_lat_tiles, final):
    bsz, s, d = x_mid.shape
    nt = s // TMC
    n_steps = bsz * nt
    lat_steps = n_lat_tiles * (TM // TMC)
    const = lambda b, i: (0, 0)
    tile = pl.BlockSpec((1, TMC, d), lambda b, i: (b, i, 0))
    smem_blk = lambda f: pl.BlockSpec((TOP_K, TMC), f, memory_space=pltpu.SMEM)
    return pl.pallas_call(
        functools.partial(_combine_kernel, final=final, n_steps=n_steps),
        grid=(bsz, nt),
        in_specs=[
            smem_blk(lambda b, i: (0, 0)),
            smem_blk(lambda b, i: (0, jnp.minimum(b * nt + i + 1, n_steps - 1))),
            tile, tile,
            pl.BlockSpec((TMC, TOP_K), lambda b, i: (b * nt + i, 0)),
            pl.BlockSpec(w_sh_in.shape, const),
            pl.BlockSpec(w_sh_down.shape, const),
            pl.BlockSpec((1, 1, 8, d), lambda b, i: (b, jnp.where(i >= lat_steps, 0, 1), 0, 0)),
            pl.BlockSpec((1, d), const),
            pl.BlockSpec(memory_space=pl.ANY),
        ],
        out_specs=tile,
        out_shape=jax.ShapeDtypeStruct((bsz, s, d), F32),
        scratch_shapes=[pltpu.VMEM((2, TOP_K, TMC, d), F32), pltpu.SemaphoreType.DMA((2,))],
        compiler_params=_params("arbitrary", "arbitrary"),
        name="moe_combine",
    )(pos_t, pos_t, x_mid, h2, w_tok, w_sh_in, w_sh_down, mod, g_final, ys)


def _block_diag(w):
    g, a, b = w.shape
    out = jnp.zeros((g * a, g * b), w.dtype)
    for j in range(g):
        out = out.at[j * a:(j + 1) * a, j * b:(j + 1) * b].set(w[j])
    return out


def kernel(x, c, ctx, c_ctx, w_mod, b_mod, g_mix, w_in, rpb, w_pool, pool_scale, w_gate_f, b_gate_f, w_gate_b, b_gate_b, g_gla, w_out, g_ffn, w_router, b_router, w_e_in, w_e_down, w_sh_in, w_sh_down, g_final):
    bsz, seq, d = x.shape
    n_ctx = ctx.shape[1]
    depth = w_mod.shape[0]
    n_exp = w_router.shape[2]
    assert n_ctx == TM and seq % TM == 0 and seq % GRID_W == 0
    n_lat = seq // TM
    s = seq + n_ctx

    n_rows = -(-(bsz + 1) // 8) * 8
    cc = jnp.zeros((n_rows, d), F32).at[:bsz].set(c).at[bsz].set(c_ctx)
    mod_all = _modulation(cc, w_mod, b_mod).reshape(depth, n_rows, 6, d)
    cos, sin = _rope_tables(seq, s)

    xa = jnp.concatenate([x, ctx], axis=1)
    for layer in range(depth):
        last = layer == depth - 1
        n_tiles = n_lat if last else n_lat + 1
        m = mod_all[layer]
        mod = jnp.stack([jnp.broadcast_to(m[bsz], (bsz, 6, d)), m[:bsz]], axis=1)
        mod = jnp.pad(mod, ((0, 0), (0, 0), (0, 2), (0, 0)))

        wl = w_in[layer]
        w_main = wl[:, :MAIN_W].astype(BF16)
        w_low = jnp.pad(wl[:, MAIN_W:], ((0, 0), (0, LANES - 2 * GLA_GATE_RANK))).astype(BF16)
        w2 = jnp.zeros((LANES, 2 * GLA_QK), F32)
        w2 = w2.at[:GLA_GATE_RANK, :GLA_QK].set(w_gate_f[layer])
        w2 = w2.at[GLA_GATE_RANK:2 * GLA_GATE_RANK, GLA_QK:].set(w_gate_b[layer]).astype(BF16)
        b2 = jnp.concatenate([b_gate_f[layer], b_gate_b[layer]])[None, :]
        qkv, u, gg = _inproj(xa, mod, g_mix[layer][None, :], w_main, w_low, w2, b2, n_lat)

        oa = _neighborhood_attention(qkv, _na_bias_table(rpb[layer]), n_lat, not last)
        ob = _multiscale_pool(u, _block_diag(w_pool[layer]).astype(BF16), pool_scale[layer][None, :], n_lat, n_tiles)
        o_f, o_b = _gla(gg, cos, sin, n_lat)

        x_mid, h2, e_t, w_t, rank_t, hist = _outproj(
            xa, oa, ob, o_f, o_b, gg, jnp.tile(g_gla[layer], GLA_HEADS)[None, :], w_out[layer].astype(BF16), mod,
            g_ffn[layer][None, :], w_router[layer].T.astype(BF16), b_router[layer].reshape(n_exp, 1), n_lat, n_tiles)
        t_tok = bsz * n_tiles * TM
        base, block_e, fill, n_used = _slot_layout(hist[:, 0, :].astype(jnp.int32))
        pos_t = _slot_positions(e_t, rank_t, base)
        xs = _dispatch(h2.reshape(t_tok, d), pos_t, fill, block_e.shape[0] * MOE_BLOCK)
        ys = _moe_experts(xs, block_e, n_used, w_e_in[layer], w_e_down[layer])
        xa = _combine(x_mid, h2, ys, pos_t, w_t.T, w_sh_in[layer].astype(BF16), w_sh_down[layer].astype(BF16), mod,
                      g_final[None, :], n_lat, last)
    return xa
```

```python
import functools

import jax
import jax.numpy as jnp
import numpy as np
from jax import lax
from jax.experimental import pallas as pl
from jax.experimental.pallas import tpu as pltpu

GRID_W = 64
NORM_EPS = 1e-6
NA_HEADS = 8
NA_HEAD_DIM = 64
NA_WIDTH = NA_HEADS * NA_HEAD_DIM
NA_WIN_ROWS = 8
NA_WIN_COLS = 16
POOL_WINDOWS = (2, 4, 8, 16)
POOL_GROUP_DIM = 64
POOL_WIDTH = len(POOL_WINDOWS) * POOL_GROUP_DIM
POOL_REACH = max(POOL_WINDOWS) // 2
assert POOL_WINDOWS == tuple(2 ** (g + 1) for g in range(len(POOL_WINDOWS)))
GLA_HEADS = 4
GLA_DK = 32
GLA_DV = 64
GLA_QK = GLA_HEADS * GLA_DK
GLA_WIDTH = GLA_HEADS * GLA_DV
GLA_GATE_RANK = 16
GLA_TAU = 16.0
GLA_CHUNK = 64
ROPE_BASE = 10000.0
N_EXPERTS = 256
TOP_K = 8
N_GROUPS = 8
TOPK_GROUPS = 4
ROUTED_SCALE = 2.5

TM = 256
TMC = 128
MOE_BLOCK = 256
LANES = 128
MXU_DIM = 256
MASK_VALUE = -1e30
VMEM_LIMIT = 48 * 1024 * 1024

QKV_W = 3 * NA_WIDTH
G_W = 2 * GLA_QK + 2 * GLA_WIDTH + 2 * GLA_QK
MAIN_W = QKV_W + POOL_WIDTH + 2 * GLA_QK + 2 * GLA_WIDTH

BF16 = jnp.bfloat16
F32 = jnp.float32


def _params(*sem):
    return pltpu.CompilerParams(dimension_semantics=sem, vmem_limit_bytes=VMEM_LIMIT)


def _sigmoid(x):
    return 1.0 / (1.0 + jnp.exp(-x))


def _silu(x):
    return x * _sigmoid(x)


def _rmsnorm(x, g):
    return x * lax.rsqrt(jnp.mean(x * x, axis=-1, keepdims=True) + NORM_EPS) * g


def _dot(a, b):
    return jnp.dot(a, b, preferred_element_type=F32)


def _dot_nt(a, b):
    return lax.dot_general(a, b, (((1,), (1,)), ((), ())), preferred_element_type=F32)


def _dot_tn(a, b):
    return lax.dot_general(a, b, (((0,), (0,)), ((), ())), preferred_element_type=F32)


def _mod_kernel(c_ref, w_ref, b_ref, o_ref):
    a = _silu(c_ref[...]).astype(BF16)
    o_ref[0] = _dot(a, w_ref[0].astype(BF16)) + b_ref[0]


def _modulation(cc, w_mod, b_mod):
    depth, d, n = w_mod.shape
    r = cc.shape[0]
    tn = 1024
    return pl.pallas_call(
        _mod_kernel,
        grid=(depth, n // tn),
        in_specs=[
            pl.BlockSpec((r, d), lambda l, j: (0, 0)),
            pl.BlockSpec((1, d, tn), lambda l, j: (l, 0, j)),
            pl.BlockSpec((1, 1, tn), lambda l, j: (l, 0, j)),
        ],
        out_specs=pl.BlockSpec((1, r, tn), lambda l, j: (l, 0, j)),
        out_shape=jax.ShapeDtypeStruct((depth, r, n), F32),
        compiler_params=_params("parallel", "parallel"),
        name="modulation",
    )(cc, w_mod, b_mod.reshape(depth, 1, n))


def _inproj_kernel(x_ref, mod_ref, g_ref, wm_ref, wl_ref, w2_ref, b2_ref, qkv_ref, u_ref, gg_ref):
    h = _rmsnorm(x_ref[0], g_ref[...]) * (1.0 + mod_ref[0, 0, 1:2, :]) + mod_ref[0, 0, 0:1, :]
    hb = h.astype(BF16)
    q = _dot(hb, wm_ref[:, 0:NA_WIDTH]) * (NA_HEAD_DIM ** -0.5)
    qkv_ref[0, :, 0:NA_WIDTH] = q.astype(BF16)
    qkv_ref[0, :, NA_WIDTH:2 * NA_WIDTH] = _dot(hb, wm_ref[:, NA_WIDTH:2 * NA_WIDTH]).astype(BF16)
    qkv_ref[0, :, 2 * NA_WIDTH:QKV_W] = _dot(hb, wm_ref[:, 2 * NA_WIDTH:QKV_W]).astype(BF16)
    u_ref[0] = _dot(hb, wm_ref[:, QKV_W:QKV_W + POOL_WIDTH])
    c0 = QKV_W + POOL_WIDTH
    gg_ref[0, :, 0:GLA_QK] = _dot(hb, wm_ref[:, c0:c0 + GLA_QK]) * (GLA_DK ** -0.5)
    gg_ref[0, :, GLA_QK:2 * GLA_QK + 2 * GLA_WIDTH] = _dot(hb, wm_ref[:, c0 + GLA_QK:MAIN_W])
    a_low = _dot(hb, wl_ref[...]).astype(BF16)
    lg = _dot(a_low, w2_ref[...]) + b2_ref[...]
    log_sig = jnp.minimum(lg, 0.0) - jnp.log1p(jnp.exp(-jnp.abs(lg)))
    gg_ref[0, :, 2 * GLA_QK + 2 * GLA_WIDTH:G_W] = log_sig / GLA_TAU


def _inproj(xa, mod, g, w_main, w_low, w2, b2, n_lat_tiles):
    bsz, s, d = xa.shape
    nt = s // TM
    const = lambda b, i: (0, 0)
    return pl.pallas_call(
        _inproj_kernel,
        grid=(bsz, nt),
        in_specs=[
            pl.BlockSpec((1, TM, d), lambda b, i: (b, i, 0)),
            pl.BlockSpec((1, 1, 8, d), lambda b, i: (b, jnp.where(i >= n_lat_tiles, 0, 1), 0, 0)),
            pl.BlockSpec((1, d), const),
            pl.BlockSpec(w_main.shape, const),
            pl.BlockSpec(w_low.shape, const),
            pl.BlockSpec(w2.shape, const),
            pl.BlockSpec(b2.shape, const),
        ],
        out_specs=[
            pl.BlockSpec((1, TM, QKV_W), lambda b, i: (b, i, 0)),
            pl.BlockSpec((1, TM, POOL_WIDTH), lambda b, i: (b, i, 0)),
            pl.BlockSpec((1, TM, G_W), lambda b, i: (b, i, 0)),
        ],
        out_shape=[
            jax.ShapeDtypeStruct((bsz, s, QKV_W), BF16),
            jax.ShapeDtypeStruct((bsz, s, POOL_WIDTH), F32),
            jax.ShapeDtypeStruct((bsz, s, G_W), F32),
        ],
        compiler_params=_params("parallel", "parallel"),
        name="inproj",
    )(xa, mod, g, w_main, w_low, w2, b2)


HEADS_PER_GROUP = MXU_DIM // NA_HEAD_DIM
NA_GROUPS = NA_HEADS // HEADS_PER_GROUP
NA_LOCAL_KEYS = NA_WIN_ROWS * GRID_W


def _stack_heads(x, width):
    lane = lax.broadcasted_iota(jnp.int32, x.shape, 1) // width
    n_heads = x.shape[1] // width
    return jnp.concatenate([jnp.where(lane == h, x, jnp.zeros_like(x)) for h in range(n_heads)], axis=0)


def _unstack_heads(o, width):
    n_heads = o.shape[1] // width
    r = o.shape[0] // n_heads
    lane = lax.broadcasted_iota(jnp.int32, (r, o.shape[1]), 1) // width
    acc = jnp.zeros((r, o.shape[1]), o.dtype)
    for h in range(n_heads):
        acc = jnp.where(lane == h, o[h * r:(h + 1) * r, :], acc)
    return acc


def _na_kernel(q_ref, k_ref, v_ref, kc_ref, vc_ref, bias_ref, o_ref, *, n_rows):
    r = pl.program_id(1)

    def attend(g, local):
        cols = slice(g * MXU_DIM, (g + 1) * MXU_DIM)
        qs = _stack_heads(q_ref[0, :, cols], NA_HEAD_DIM)
        kc = kc_ref[0, :, cols]
        vc = vc_ref[0, :, cols]
        s_ctx = _dot_nt(qs, kc)
        m = jnp.max(s_ctx, axis=-1, keepdims=True)
        if local:
            start = pl.multiple_of(jnp.clip(r - NA_WIN_ROWS // 2, 0, n_rows - NA_WIN_ROWS) * GRID_W, GRID_W)
            kw = k_ref[0, pl.ds(start, NA_LOCAL_KEYS), cols]
            vw = v_ref[0, pl.ds(start, NA_LOCAL_KEYS), cols]
            s_loc = _dot_nt(qs, kw) + bias_ref[0, g]
            m = jnp.maximum(m, jnp.max(s_loc, axis=-1, keepdims=True))
            p_loc = jnp.exp(s_loc - m)
        p_ctx = jnp.exp(s_ctx - m)
        den = jnp.sum(p_ctx, axis=-1, keepdims=True)
        o = _dot(p_ctx.astype(BF16), vc)
        if local:
            den = den + jnp.sum(p_loc, axis=-1, keepdims=True)
            o = o + _dot(p_loc.astype(BF16), vw)
        o_ref[0, :, cols] = _unstack_heads(o / den, NA_HEAD_DIM).astype(o_ref.dtype)

    @pl.when(r < n_rows)
    def _():
        for g in range(NA_GROUPS):
            attend(g, True)

    @pl.when(r >= n_rows)
    def _():
        for g in range(NA_GROUPS):
            attend(g, False)


def _na_bias_table(rpb):
    o = jnp.arange(NA_WIN_ROWS)
    dr = o[None, :] - o[:, None] + NA_WIN_ROWS - 1
    col = jnp.arange(GRID_W)
    dc = jnp.clip(col[None, :] - col[:, None], -(NA_WIN_COLS - 1), NA_WIN_COLS - 1) + NA_WIN_COLS - 1
    col_start = jnp.clip(col - NA_WIN_COLS // 2, 0, GRID_W - NA_WIN_COLS)
    col_mask = (col[None, :] >= col_start[:, None]) & (col[None, :] < col_start[:, None] + NA_WIN_COLS)
    t = rpb[:, dr][..., dc].astype(F32)
    t = jnp.where(col_mask[None, None, None], t, MASK_VALUE)
    t = t.transpose(1, 0, 3, 2, 4)
    return t.reshape(NA_WIN_ROWS, NA_GROUPS, HEADS_PER_GROUP * GRID_W, NA_LOCAL_KEYS)


def _neighborhood_attention(qkv, bias_tab, n_lat_tiles, with_ctx_queries):
    bsz, s, _ = qkv.shape
    seq = n_lat_tiles * TM
    n_rows = seq // GRID_W
    assert n_rows >= NA_WIN_ROWS
    n_ctx_rows = (s - seq) // GRID_W
    nq = n_rows + (n_ctx_rows if with_ctx_queries else 0)
    ctx_blk = seq // (s - seq)

    def bias_idx(b, r):
        rr = jnp.minimum(r, n_rows - 1)
        return (rr - jnp.clip(rr - NA_WIN_ROWS // 2, 0, n_rows - NA_WIN_ROWS), 0, 0, 0)

    return pl.pallas_call(
        functools.partial(_na_kernel, n_rows=n_rows),
        grid=(bsz, nq),
        in_specs=[
            pl.BlockSpec((1, GRID_W, NA_WIDTH), lambda b, r: (b, r, 0)),
            pl.BlockSpec((1, seq, NA_WIDTH), lambda b, r: (b, 0, 1)),
            pl.BlockSpec((1, seq, NA_WIDTH), lambda b, r: (b, 0, 2)),
            pl.BlockSpec((1, s - seq, NA_WIDTH), lambda b, r: (b, ctx_blk, 1)),
            pl.BlockSpec((1, s - seq, NA_WIDTH), lambda b, r: (b, ctx_blk, 2)),
            pl.BlockSpec((1,) + bias_tab.shape[1:], bias_idx),
        ],
        out_specs=pl.BlockSpec((1, GRID_W, NA_WIDTH), lambda b, r: (b, r, 0)),
        out_shape=jax.ShapeDtypeStruct((bsz, nq * GRID_W, NA_WIDTH), BF16),
        compiler_params=_params("parallel", "arbitrary"),
        name="neighborhood_attention",
    )(qkv, qkv, qkv, qkv, qkv, bias_tab)


def _pool_kernel(up_ref, u_ref, un_ref, w_ref, sc_ref, o_ref, buf, *, n_lat_tiles, seq, ctx_len):
    i = pl.program_id(1)
    hw = POOL_REACH
    buf[0:hw] = up_ref[0]
    buf[hw:hw + TM] = u_ref[0]
    buf[hw + TM:2 * hw + TM] = un_ref[0]
    is_ctx = i >= n_lat_tiles
    base = jnp.where(is_ctx, 0, i * TM)
    lseq = jnp.where(is_ctx, ctx_len, seq)
    shape = (TM, POOL_WIDTH)
    t = base + lax.broadcasted_iota(jnp.int32, shape, 0)
    grp = lax.broadcasted_iota(jnp.int32, shape, 1) // POOL_GROUP_DIM
    win = jnp.left_shift(2, grp)
    back = win // 2
    fwd = win - back - 1
    acc = jnp.zeros(shape, F32)
    for d in range(-hw, hw):
        ok = (t + d >= 0) & (t + d < lseq) & (back >= -d) & (fwd >= d)
        acc = acc + jnp.where(ok, buf[hw + d:hw + d + TM, :], 0.0)
    lo = jnp.clip(t - back, 0, lseq - 1)
    hi = jnp.clip(t + fwd, 0, lseq - 1)
    mean = acc / (hi - lo + 1).astype(F32)
    diff = (mean - u_ref[0]).astype(BF16)
    o_ref[0] = (_dot(diff, w_ref[...]) * sc_ref[...]).astype(o_ref.dtype)


def _multiscale_pool(u, w_bd, scale, n_lat_tiles, n_tiles):
    bsz, s, c = u.shape
    hw = POOL_REACH
    per = TM // hw
    last = s // hw - 1
    return pl.pallas_call(
        functools.partial(_pool_kernel, n_lat_tiles=n_lat_tiles, seq=n_lat_tiles * TM, ctx_len=s - n_lat_tiles * TM),
        grid=(bsz, n_tiles),
        in_specs=[
            pl.BlockSpec((1, hw, c), lambda b, i: (b, jnp.maximum(i * per - 1, 0), 0)),
            pl.BlockSpec((1, TM, c), lambda b, i: (b, i, 0)),
            pl.BlockSpec((1, hw, c), lambda b, i: (b, jnp.minimum((i + 1) * per, last), 0)),
            pl.BlockSpec((c, c), lambda b, i: (0, 0)),
            pl.BlockSpec((1, c), lambda b, i: (0, 0)),
        ],
        out_specs=pl.BlockSpec((1, TM, c), lambda b, i: (b, i, 0)),
        out_shape=jax.ShapeDtypeStruct((bsz, n_tiles * TM, c), BF16),
        scratch_shapes=[pltpu.VMEM((TM + 2 * hw, c), F32)],
        compiler_params=_params("parallel", "parallel"),
        name="multiscale_pool",
    )(u, u, u, w_bd, scale)


def _gla_tile(qk_ref, v_ref, la_ref, cs_ref, sn_ref, st_ref, o_ref, reverse):
    ch = GLA_CHUNK
    lane = lax.broadcasted_iota(jnp.int32, (ch, GLA_QK), 1)
    row = lax.broadcasted_iota(jnp.int32, (ch, GLA_QK), 0)
    rr = lax.broadcasted_iota(jnp.int32, (GLA_HEADS * ch, ch), 0) % ch
    cc = lax.broadcasted_iota(jnp.int32, (GLA_HEADS * ch, ch), 1)
    causal = (rr <= cc) if reverse else (rr >= cc)
    st_shape = (GLA_WIDTH, GLA_QK)
    head_blk = (lax.broadcasted_iota(jnp.int32, st_shape, 0) // GLA_DV
                == lax.broadcasted_iota(jnp.int32, st_shape, 1) // GLA_DK)
    quarter = GLA_DK // 4
    first_half = (lane % (2 * quarter)) < quarter

    def rope(x, cs, sn):
        partner = jnp.where(first_half, pltpu.roll(x, GLA_QK - quarter, 1), pltpu.roll(x, quarter, 1))
        return x * cs + partner * sn

    chunks = range(TM // ch)
    for c in (reversed(chunks) if reverse else chunks):
        sl = slice(c * ch, (c + 1) * ch)
        cs = cs_ref[sl, :]
        sn = sn_ref[sl, :]
        q = rope(qk_ref[0, sl, 0:GLA_QK], cs, sn)
        k = rope(qk_ref[0, sl, GLA_QK:2 * GLA_QK], cs, sn)
        b = la_ref[0, sl, :]
        d = 1
        while d < ch:
            if reverse:
                b = b + jnp.where(row < ch - d, pltpu.roll(b, ch - d, 0), 0.0)
            else:
                b = b + jnp.where(row >= d, pltpu.roll(b, d, 0), 0.0)
            d *= 2
        b_last = b[0:1, :] if reverse else b[ch - 1:ch, :]
        q_in = q * jnp.exp(b)
        k_in = (k * jnp.exp(-b)).astype(BF16)
        k_st = (k * jnp.exp(b_last - b)).astype(BF16)
        att = _dot_nt(_stack_heads(q_in, GLA_DK).astype(BF16), k_in)
        att = jnp.where(causal, att, 0.0).astype(BF16)
        vb = v_ref[0, sl, :].astype(BF16)
        o = _unstack_heads(_dot(att, vb), GLA_DV)
        st = st_ref[...]
        o_ref[0, sl, :] = o + _dot_nt(q_in.astype(BF16), st.astype(BF16))
        st_ref[...] = st * jnp.exp(b_last) + jnp.where(head_blk, _dot_tn(vb, k_st), 0.0)


def _gla_kernel(qkf, vf, laf, csf, snf, qkb, vb, lab, csb, snb, of_ref, ob_ref, st_f, st_b):
    @pl.when(pl.program_id(1) == 0)
    def _():
        st_f[...] = jnp.zeros_like(st_f)
        st_b[...] = jnp.zeros_like(st_b)

    _gla_tile(qkf, vf, laf, csf, snf, st_f, of_ref, False)
    _gla_tile(qkb, vb, lab, csb, snb, st_b, ob_ref, True)


def _rope_tables(seq, s):
    quarter = GLA_DK // 4
    t = np.arange(s)
    pos_row = np.where(t < seq, t // GRID_W, 0).astype(np.float32)
    pos_col = np.where(t < seq, t % GRID_W, 0).astype(np.float32)
    d = np.arange(GLA_QK) % GLA_DK
    inv_freq = ROPE_BASE ** (-jnp.arange(quarter, dtype=F32) / quarter)
    freq = inv_freq[d % quarter]
    pos = jnp.where((d < GLA_DK // 2)[None, :], pos_row[:, None], pos_col[:, None])
    ang = pos * freq[None, :]
    sign = np.where((d % (2 * quarter)) < quarter, -1.0, 1.0).astype(np.float32)
    return jnp.cos(ang), jnp.sin(ang) * sign[None, :]


def _gla(gg, cos, sin, n_lat_tiles):
    bsz, s, _ = gg.shape
    nt = s // TM
    assert nt == n_lat_tiles + 1
    fwd = lambda i: (i + n_lat_tiles) % nt
    bwd = lambda i: jnp.where(i == 0, n_lat_tiles, n_lat_tiles - i)
    la_f_blk = (2 * GLA_QK + 2 * GLA_WIDTH) // GLA_QK
    la_b_blk = la_f_blk + 1
    v_blk = 2 * GLA_QK // GLA_WIDTH

    def specs(order, la_blk):
        return [
            pl.BlockSpec((1, TM, 2 * GLA_QK), lambda b, i: (b, order(i), 0)),
            pl.BlockSpec((1, TM, GLA_WIDTH), lambda b, i: (b, order(i), v_blk)),
            pl.BlockSpec((1, TM, GLA_QK), lambda b, i: (b, order(i), la_blk)),
            pl.BlockSpec((TM, GLA_QK), lambda b, i: (order(i), 0)),
            pl.BlockSpec((TM, GLA_QK), lambda b, i: (order(i), 0)),
        ]

    return pl.pallas_call(
        _gla_kernel,
        grid=(bsz, nt),
        in_specs=specs(fwd, la_f_blk) + specs(bwd, la_b_blk),
        out_specs=[
            pl.BlockSpec((1, TM, GLA_WIDTH), lambda b, i: (b, fwd(i), 0)),
            pl.BlockSpec((1, TM, GLA_WIDTH), lambda b, i: (b, bwd(i), 0)),
        ],
        out_shape=[jax.ShapeDtypeStruct((bsz, s, GLA_WIDTH), F32)] * 2,
        scratch_shapes=[pltpu.VMEM((GLA_WIDTH, GLA_QK), F32)] * 2,
        compiler_params=_params("arbitrary", "arbitrary"),
        name="gla_bidir",
    )(gg, gg, gg, cos, sin, gg, gg, gg, cos, sin)


def _row_index(shape):
    return lax.broadcasted_iota(jnp.int32, shape, 0).astype(F32)


def _stack_rows(rows):
    shape = (len(rows), rows[0].shape[1])
    rid = lax.broadcasted_iota(jnp.int32, shape, 0)
    out = jnp.zeros(shape, rows[0].dtype)
    for k, r in enumerate(rows):
        out = jnp.where(rid == k, r, out)
    return out


def _select_experts(scores, biased):
    n_exp, n_tok = biased.shape
    per = n_exp // N_GROUPS
    neg = -jnp.inf
    sub = _row_index((per, n_tok))
    grp_rows = []
    for g in range(N_GROUPS):
        blk = biased[g * per:(g + 1) * per, :]
        m1 = jnp.max(blk, axis=0, keepdims=True)
        i1 = jnp.min(jnp.where(blk == m1, sub, float(per)), axis=0, keepdims=True)
        m2 = jnp.max(jnp.where(sub == i1, neg, blk), axis=0, keepdims=True)
        grp_rows.append(m1 + m2)
    cur = _stack_rows(grp_rows)
    gid = _row_index(cur.shape)
    picked = jnp.zeros(cur.shape, F32)
    for _ in range(TOPK_GROUPS):
        gm = jnp.max(cur, axis=0, keepdims=True)
        gi = jnp.min(jnp.where(cur == gm, gid, float(N_GROUPS)), axis=0, keepdims=True)
        hit = gid == gi
        picked = jnp.where(hit, 1.0, picked)
        cur = jnp.where(hit, neg, cur)
    cur = jnp.concatenate(
        [jnp.where(picked[g:g + 1, :] > 0.0, biased[g * per:(g + 1) * per, :], neg) for g in range(N_GROUPS)], axis=0)
    eid = _row_index(cur.shape)
    ids, vals = [], []
    chosen = jnp.zeros(cur.shape, F32)
    for _ in range(TOP_K):
        m = jnp.max(cur, axis=0, keepdims=True)
        idx = jnp.min(jnp.where(cur == m, eid, float(n_exp)), axis=0, keepdims=True)
        hit = eid == idx
        ids.append(idx)
        vals.append(jnp.sum(jnp.where(hit, scores, 0.0), axis=0, keepdims=True))
        chosen = jnp.where(hit, 1.0, chosen)
        cur = jnp.where(hit, neg, cur)
    return ids, vals, chosen


def _outproj_kernel(x_ref, oa_ref, ob_ref, of_ref, obk_ref, r_ref, gg_ref, wo_ref, mod_ref, gf_ref, wr_ref, br_ref,
                    tri_ref, xo_ref, h_ref, e_ref, w_ref, rk_ref, hist_ref):
    o = of_ref[0] + obk_ref[0]
    head = lax.broadcasted_iota(jnp.int32, o.shape, 1) // GLA_DV
    o2 = o * o
    rs = jnp.zeros_like(o)
    for h in range(GLA_HEADS):
        ssq = jnp.sum(jnp.where(head == h, o2, 0.0), axis=-1, keepdims=True)
        rs = jnp.where(head == h, lax.rsqrt(ssq / GLA_DV + NORM_EPS), rs)
    oc = (o * rs * gg_ref[...] * _silu(r_ref[0])).astype(BF16)
    c1 = NA_WIDTH + POOL_WIDTH
    acc = _dot(oa_ref[0], wo_ref[0:NA_WIDTH]) + _dot(ob_ref[0], wo_ref[NA_WIDTH:c1]) + _dot(oc, wo_ref[c1:])
    x = x_ref[0] + mod_ref[0, 0, 2:3, :] * acc
    xo_ref[0] = x
    h = _rmsnorm(x, gf_ref[...]) * (1.0 + mod_ref[0, 0, 4:5, :]) + mod_ref[0, 0, 3:4, :]
    h_ref[0] = h

    scores = _sigmoid(_dot_nt(wr_ref[...], h.astype(BF16)))
    ids, vals, chosen = _select_experts(scores, scores + br_ref[...])
    total = vals[0]
    for v in vals[1:]:
        total = total + v
    w_ref[...] = _stack_rows([v / total * ROUTED_SCALE for v in vals])
    e_ref[...] = _stack_rows(ids).astype(jnp.int32)
    chosen_b = chosen.astype(BF16)
    before = _dot(chosen_b, tri_ref[...])
    eid = _row_index(before.shape)
    rk_ref[...] = _stack_rows([jnp.sum(jnp.where(eid == idx, before, 0.0), axis=0, keepdims=True) for idx in ids])
    hist_ref[0] = _dot_nt(jnp.ones((8, chosen.shape[1]), BF16), chosen_b)


def _outproj(xa, oa, ob, o_f, o_b, gg, g_gla4, w_out, mod, g_ffn, w_router_t, b_router, n_lat_tiles, n_tiles):
    bsz, _, d = xa.shape
    n_exp = w_router_t.shape[0]
    r_blk = (2 * GLA_QK + GLA_WIDTH) // GLA_WIDTH
    tile = lambda w: pl.BlockSpec((1, TM, w), lambda b, i: (b, i, 0))
    const = lambda b, i: (0, 0)
    per_tok = pl.BlockSpec((TOP_K, TM), lambda b, i: (0, b * n_tiles + i))
    t_tok = bsz * n_tiles * TM
    tri = (jnp.arange(TM)[:, None] < jnp.arange(TM)[None, :]).astype(BF16)
    return pl.pallas_call(
        _outproj_kernel,
        grid=(bsz, n_tiles),
        in_specs=[
            tile(d), tile(NA_WIDTH), tile(POOL_WIDTH), tile(GLA_WIDTH), tile(GLA_WIDTH),
            pl.BlockSpec((1, TM, GLA_WIDTH), lambda b, i: (b, i, r_blk)),
            pl.BlockSpec((1, GLA_WIDTH), const),
            pl.BlockSpec(w_out.shape, const),
            pl.BlockSpec((1, 1, 8, d), lambda b, i: (b, jnp.where(i >= n_lat_tiles, 0, 1), 0, 0)),
            pl.BlockSpec((1, d), const),
            pl.BlockSpec(w_router_t.shape, const),
            pl.BlockSpec((n_exp, 1), const),
            pl.BlockSpec((TM, TM), const),
        ],
        out_specs=[tile(d), tile(d), per_tok, per_tok, per_tok,
                   pl.BlockSpec((1, 8, n_exp), lambda b, i: (b * n_tiles + i, 0, 0))],
        out_shape=[
            jax.ShapeDtypeStruct((bsz, n_tiles * TM, d), F32),
            jax.ShapeDtypeStruct((bsz, n_tiles * TM, d), F32),
            jax.ShapeDtypeStruct((TOP_K, t_tok), jnp.int32),
            jax.ShapeDtypeStruct((TOP_K, t_tok), F32),
            jax.ShapeDtypeStruct((TOP_K, t_tok), F32),
            jax.ShapeDtypeStruct((bsz * n_tiles, 8, n_exp), F32),
        ],
        compiler_params=_params("parallel", "parallel"),
        name="outproj_router",
    )(xa, oa, ob, o_f, o_b, gg, g_gla4, w_out, mod, g_ffn, w_router_t, b_router, tri)


def _slot_layout(hist):
    n_tiles, n_exp = hist.shape
    counts = hist.sum(0)
    tile_base = jnp.cumsum(hist, axis=0) - hist
    padded = (counts + MOE_BLOCK - 1) // MOE_BLOCK * MOE_BLOCK
    pad_end = jnp.cumsum(padded)
    pad_start = pad_end - padded
    base = (pad_start[None, :] + tile_base).astype(F32).reshape(n_tiles, n_exp, 1)
    n_blocks = -(-(n_tiles * TM * TOP_K) // MOE_BLOCK) + n_exp
    blk_start = jnp.arange(n_blocks, dtype=jnp.int32) * MOE_BLOCK
    block_e = jnp.minimum(jnp.searchsorted(pad_end, blk_start, side='right'), n_exp - 1).astype(jnp.int32)
    n_used = (pad_end[-1] // MOE_BLOCK).astype(jnp.int32)
    fill = jnp.concatenate([jnp.maximum(pad_end // MOE_BLOCK - 1, 0).astype(jnp.int32),
                            jnp.minimum(n_used + jnp.arange(n_exp, dtype=jnp.int32), n_blocks - 1)])
    return base, block_e, fill, n_used.reshape(1)


def _pos_kernel(e_ref, rk_ref, base_ref, pos_ref):
    eid = lax.broadcasted_iota(jnp.int32, (base_ref.shape[1], e_ref.shape[1]), 0)
    base = base_ref[0]
    rows = [jnp.sum(jnp.where(eid == e_ref[k:k + 1, :], base, 0.0), axis=0, keepdims=True) for k in range(TOP_K)]
    pos_ref[...] = (_stack_rows(rows) + rk_ref[...]).astype(jnp.int32)


def _slot_positions(e_t, rank_t, base):
    n_tiles, n_exp, _ = base.shape
    per_tok = pl.BlockSpec((TOP_K, TM), lambda i: (0, i))
    return pl.pallas_call(
        _pos_kernel,
        grid=(n_tiles,),
        in_specs=[per_tok, per_tok, pl.BlockSpec((1, n_exp, 1), lambda i: (i, 0, 0))],
        out_specs=per_tok,
        out_shape=jax.ShapeDtypeStruct(e_t.shape, jnp.int32),
        compiler_params=_params("parallel"),
        name="moe_positions",
    )(e_t, rank_t, base)


def _dispatch_kernel(fill_ref, pos_ref, h_ref, xs_hbm, zbuf, sem, zsem):
    @pl.when(pl.program_id(0) == 0)
    def _():
        zbuf[...] = jnp.zeros_like(zbuf)
        n_fill = fill_ref.shape[0]

        def is_new(j):
            return (j == 0) | (fill_ref[j] != fill_ref[jnp.maximum(j - 1, 0)])

        def start(j, c):
            @pl.when(is_new(j))
            def _():
                row0 = pl.multiple_of(fill_ref[j] * MOE_BLOCK, MOE_BLOCK)
                pltpu.make_async_copy(zbuf, xs_hbm.at[pl.ds(row0, MOE_BLOCK), :], zsem).start()
            return c
        lax.fori_loop(0, n_fill, start, 0)

        def wait(j, c):
            @pl.when(is_new(j))
            def _():
                pltpu.make_async_copy(zbuf, xs_hbm.at[pl.ds(0, MOE_BLOCK), :], zsem).wait()
            return c
        lax.fori_loop(0, n_fill, wait, 0)

    def body(t, c):
        for k in range(TOP_K):
            pltpu.make_async_copy(h_ref.at[pl.ds(t, 1), :], xs_hbm.at[pl.ds(pos_ref[k, t], 1), :],
                                  sem).start(priority=k % 2)
        return c
    lax.fori_loop(0, TM, body, 0, unroll=2)
    for _ in range(TOP_K):
        pltpu.make_async_copy(h_ref, xs_hbm.at[pl.ds(0, TM), :], sem).wait()


def _dispatch(h2d, pos_t, fill, n_slots):
    t_tok, d = h2d.shape
    grid_spec = pltpu.PrefetchScalarGridSpec(
        num_scalar_prefetch=1,
        grid=(t_tok // TM,),
        in_specs=[
            pl.BlockSpec((TOP_K, TM), lambda i, fill: (0, i), memory_space=pltpu.SMEM),
            pl.BlockSpec((TM, d), lambda i, fill: (i, 0)),
        ],
        out_specs=pl.BlockSpec(memory_space=pl.ANY),
        scratch_shapes=[pltpu.VMEM((MOE_BLOCK, d), F32), pltpu.SemaphoreType.DMA, pltpu.SemaphoreType.DMA],
    )
    return pl.pallas_call(
        _dispatch_kernel,
        grid_spec=grid_spec,
        out_shape=jax.ShapeDtypeStruct((n_slots, d), F32),
        compiler_params=_params("arbitrary"),
        name="moe_dispatch",
    )(fill, pos_t, h2d)


def _expert_kernel(be_ref, nu_ref, x_ref, wi_ref, wd_ref, y_ref, wi_bf, wd_bf):
    i = pl.program_id(0)
    d_exp = wd_ref.shape[2]

    @pl.when(i < nu_ref[0])
    def _():
        @pl.when((i == 0) | (be_ref[i] != be_ref[jnp.maximum(i - 1, 0)]))
        def _():
            wi_bf[...] = wi_ref[0, 0].astype(BF16)
            wd_bf[...] = wd_ref[0, 0].astype(BF16)

        hh = _dot(x_ref[...].astype(BF16), wi_bf[...])
        a = (_silu(hh[:, :d_exp]) * hh[:, d_exp:]).astype(BF16)
        y_ref[...] = _dot(a, wd_bf[...])

    @pl.when(i >= nu_ref[0])
    def _():
        y_ref[...] = jnp.zeros_like(y_ref)


def _moe_experts(xs, block_e, n_used, w_e_in, w_e_down, layer):
    n_slots, d = xs.shape
    n_blocks = block_e.shape[0]
    f2 = w_e_in.shape[3]
    grid_spec = pltpu.PrefetchScalarGridSpec(
        num_scalar_prefetch=2,
        grid=(n_blocks,),
        in_specs=[
            pl.BlockSpec((MOE_BLOCK, d), lambda i, be, nu: (jnp.minimum(i, nu[0] - 1), 0)),
            pl.BlockSpec((1, 1, d, f2), lambda i, be, nu: (layer, be[i], 0, 0)),
            pl.BlockSpec((1, 1, f2 // 2, d), lambda i, be, nu: (layer, be[i], 0, 0)),
        ],
        out_specs=pl.BlockSpec((MOE_BLOCK, d), lambda i, be, nu: (i, 0)),
        scratch_shapes=[pltpu.VMEM((d, f2), BF16), pltpu.VMEM((f2 // 2, d), BF16)],
    )
    return pl.pallas_call(
        _expert_kernel,
        grid_spec=grid_spec,
        out_shape=jax.ShapeDtypeStruct((n_slots, d), F32),
        compiler_params=_params("arbitrary"),
        name="moe_experts",
    )(block_e, n_used, xs, w_e_in, w_e_down)


def _combine_kernel(pos0_ref, posn_ref, x_ref, h_ref, w_ref, wsi_ref, wsd_ref, mod_ref, gfin_ref, ys_hbm, o_ref,
                    ybuf, sem, *, final, n_steps):
    step = pl.program_id(0) * pl.num_programs(1) + pl.program_id(1)
    slot = step % 2
    d_exp = wsd_ref.shape[0]

    def gather(pos_ref, s):
        def body(t, c):
            for k in range(TOP_K):
                pltpu.make_async_copy(ys_hbm.at[pl.ds(pos_ref[k, t], 1), :], ybuf.at[s, k, pl.ds(t, 1), :],
                                      sem.at[s]).start(priority=k % 2)
            return c
        lax.fori_loop(0, TMC, body, 0, unroll=2)

    @pl.when(step == 0)
    def _():
        gather(pos0_ref, 0)

    @pl.when(step + 1 < n_steps)
    def _():
        gather(posn_ref, 1 - slot)

    hh = _dot(h_ref[0].astype(BF16), wsi_ref[...])
    a = (_silu(hh[:, :d_exp]) * hh[:, d_exp:]).astype(BF16)
    y = _dot(a, wsd_ref[...])
    for k in range(TOP_K):
        pltpu.make_async_copy(ys_hbm.at[pl.ds(0, TMC), :], ybuf.at[slot, k], sem.at[slot]).wait()
    w = w_ref[...]
    for k in range(TOP_K):
        y = y + ybuf[slot, k] * w[:, k:k + 1]
    x = x_ref[0] + mod_ref[0, 0, 5:6, :] * y
    if final:
        x = _rmsnorm(x, gfin_ref[...])
    o_ref[0] = x


def _combine(x_mid, h2, ys, pos_t, w_tok, w_sh_in, w_sh_down, mod, g_final, n_lat_tiles, final):
    bsz, s, d = x_mid.shape
    nt = s // TMC
    n_steps = bsz * nt
    lat_steps = n_lat_tiles * (TM // TMC)
    const = lambda b, i: (0, 0)
    tile = pl.BlockSpec((1, TMC, d), lambda b, i: (b, i, 0))
    smem_blk = lambda f: pl.BlockSpec((TOP_K, TMC), f, memory_space=pltpu.SMEM)
    return pl.pallas_call(
        functools.partial(_combine_kernel, final=final, n_steps=n_steps),
        grid=(bsz, nt),
        in_specs=[
            smem_blk(lambda b, i: (0, 0)),
            smem_blk(lambda b, i: (0, jnp.minimum(b * nt + i + 1, n_steps - 1))),
            tile, tile,
            pl.BlockSpec((TMC, TOP_K), lambda b, i: (b * nt + i, 0)),
            pl.BlockSpec(w_sh_in.shape, const),
            pl.BlockSpec(w_sh_down.shape, const),
            pl.BlockSpec((1, 1, 8, d), lambda b, i: (b, jnp.where(i >= lat_steps, 0, 1), 0, 0)),
            pl.BlockSpec((1, d), const),
            pl.BlockSpec(memory_space=pl.ANY),
        ],
        out_specs=tile,
        out_shape=jax.ShapeDtypeStruct((bsz, s, d), F32),
        scratch_shapes=[pltpu.VMEM((2, TOP_K, TMC, d), F32), pltpu.SemaphoreType.DMA((2,))],
        compiler_params=_params("arbitrary", "arbitrary"),
        name="moe_combine",
    )(pos_t, pos_t, x_mid, h2, w_tok, w_sh_in, w_sh_down, mod, g_final, ys)


def _block_diag(w):
    g, a, b = w.shape
    out = jnp.zeros((g * a, g * b), w.dtype)
    for j in range(g):
        out = out.at[j * a:(j + 1) * a, j * b:(j + 1) * b].set(w[j])
    return out


def kernel(x, c, ctx, c_ctx, w_mod, b_mod, g_mix, w_in, rpb, w_pool, pool_scale, w_gate_f, b_gate_f, w_gate_b, b_gate_b, g_gla, w_out, g_ffn, w_router, b_router, w_e_in, w_e_down, w_sh_in, w_sh_down, g_final):
    bsz, seq, d = x.shape
    n_ctx = ctx.shape[1]
    depth = w_mod.shape[0]
    n_exp = w_router.shape[2]
    assert n_ctx == TM and seq % TM == 0 and seq % GRID_W == 0
    n_lat = seq // TM
    s = seq + n_ctx

    n_rows = -(-(bsz + 1) // 8) * 8
    cc = jnp.zeros((n_rows, d), F32).at[:bsz].set(c).at[bsz].set(c_ctx)
    mod_all = _modulation(cc, w_mod, b_mod).reshape(depth, n_rows, 6, d)
    cos, sin = _rope_tables(seq, s)

    xa = jnp.concatenate([x, ctx], axis=1)
    for layer in range(depth):
        last = layer == depth - 1
        n_tiles = n_lat if last else n_lat + 1
        m = mod_all[layer]
        mod = jnp.stack([jnp.broadcast_to(m[bsz], (bsz, 6, d)), m[:bsz]], axis=1)
        mod = jnp.pad(mod, ((0, 0), (0, 0), (0, 2), (0, 0)))

        wl = w_in[layer]
        w_main = wl[:, :MAIN_W].astype(BF16)
        w_low = jnp.pad(wl[:, MAIN_W:], ((0, 0), (0, LANES - 2 * GLA_GATE_RANK))).astype(BF16)
        w2 = jnp.zeros((LANES, 2 * GLA_QK), F32)
        w2 = w2.at[:GLA_GATE_RANK, :GLA_QK].set(w_gate_f[layer])
        w2 = w2.at[GLA_GATE_RANK:2 * GLA_GATE_RANK, GLA_QK:].set(w_gate_b[layer]).astype(BF16)
        b2 = jnp.concatenate([b_gate_f[layer], b_gate_b[layer]])[None, :]
        qkv, u, gg = _inproj(xa, mod, g_mix[layer][None, :], w_main, w_low, w2, b2, n_lat)

        oa = _neighborhood_attention(qkv, _na_bias_table(rpb[layer]), n_lat, not last)
        ob = _multiscale_pool(u, _block_diag(w_pool[layer]).astype(BF16), pool_scale[layer][None, :], n_lat, n_tiles)
        o_f, o_b = _gla(gg, cos, sin, n_lat)

        x_mid, h2, e_t, w_t, rank_t, hist = _outproj(
            xa, oa, ob, o_f, o_b, gg, jnp.tile(g_gla[layer], GLA_HEADS)[None, :], w_out[layer].astype(BF16), mod,
            g_ffn[layer][None, :], w_router[layer].T.astype(BF16), b_router[layer].reshape(n_exp, 1), n_lat, n_tiles)
        t_tok = bsz * n_tiles * TM
        base, block_e, fill, n_used = _slot_layout(hist[:, 0, :].astype(jnp.int32))
        pos_t = _slot_positions(e_t, rank_t, base)
        xs = _dispatch(h2.reshape(t_tok, d), pos_t, fill, block_e.shape[0] * MOE_BLOCK)
        ys = _moe_experts(xs, block_e, n_used, w_e_in, w_e_down, layer)
        xa = _combine(x_mid, h2, ys, pos_t, w_t.T, w_sh_in[layer].astype(BF16), w_sh_down[layer].astype(BF16), mod,
                      g_final[None, :], n_lat, last)
    return xa
```

```python
import functools

import jax
import jax.numpy as jnp
import numpy as np
from jax import lax
from jax.experimental import pallas as pl
from jax.experimental.pallas import tpu as pltpu

GRID_W = 64
NORM_EPS = 1e-6
NA_HEADS = 8
NA_HEAD_DIM = 64
NA_WIDTH = NA_HEADS * NA_HEAD_DIM
NA_WIN_ROWS = 8
NA_WIN_COLS = 16
POOL_WINDOWS = (2, 4, 8, 16)
POOL_GROUP_DIM = 64
POOL_WIDTH = len(POOL_WINDOWS) * POOL_GROUP_DIM
POOL_REACH = max(POOL_WINDOWS) // 2
assert POOL_WINDOWS == tuple(2 ** (g + 1) for g in range(len(POOL_WINDOWS)))
GLA_HEADS = 4
GLA_DK = 32
GLA_DV = 64
GLA_QK = GLA_HEADS * GLA_DK
GLA_WIDTH = GLA_HEADS * GLA_DV
GLA_GATE_RANK = 16
GLA_TAU = 16.0
GLA_CHUNK = 64
ROPE_BASE = 10000.0
N_EXPERTS = 256
TOP_K = 8
N_GROUPS = 8
TOPK_GROUPS = 4
ROUTED_SCALE = 2.5

TM = 256
TMC = 128
MOE_BLOCK = 256
LANES = 128
MXU_DIM = 256
MASK_VALUE = -1e30
VMEM_LIMIT = 48 * 1024 * 1024

QKV_W = 3 * NA_WIDTH
G_W = 2 * GLA_QK + 2 * GLA_WIDTH + 2 * GLA_QK
MAIN_W = QKV_W + POOL_WIDTH + 2 * GLA_QK + 2 * GLA_WIDTH

BF16 = jnp.bfloat16
F32 = jnp.float32


def _params(*sem):
    return pltpu.CompilerParams(dimension_semantics=sem, vmem_limit_bytes=VMEM_LIMIT)


def _sigmoid(x):
    return 1.0 / (1.0 + jnp.exp(-x))


def _silu(x):
    return x * _sigmoid(x)


def _rmsnorm(x, g):
    return x * lax.rsqrt(jnp.mean(x * x, axis=-1, keepdims=True) + NORM_EPS) * g


def _dot(a, b):
    return jnp.dot(a, b, preferred_element_type=F32)


def _dot_nt(a, b):
    return lax.dot_general(a, b, (((1,), (1,)), ((), ())), preferred_element_type=F32)


def _dot_tn(a, b):
    return lax.dot_general(a, b, (((0,), (0,)), ((), ())), preferred_element_type=F32)


def _mod_kernel(c_ref, w_ref, b_ref, o_ref):
    a = _silu(c_ref[...]).astype(BF16)
    o_ref[0] = _dot(a, w_ref[0].astype(BF16)) + b_ref[0]


def _modulation(cc, w_mod, b_mod):
    depth, d, n = w_mod.shape
    r = cc.shape[0]
    tn = 1024
    return pl.pallas_call(
        _mod_kernel,
        grid=(depth, n // tn),
        in_specs=[
            pl.BlockSpec((r, d), lambda l, j: (0, 0)),
            pl.BlockSpec((1, d, tn), lambda l, j: (l, 0, j)),
            pl.BlockSpec((1, 1, tn), lambda l, j: (l, 0, j)),
        ],
        out_specs=pl.BlockSpec((1, r, tn), lambda l, j: (l, 0, j)),
        out_shape=jax.ShapeDtypeStruct((depth, r, n), F32),
        compiler_params=_params("parallel", "parallel"),
        name="modulation",
    )(cc, w_mod, b_mod.reshape(depth, 1, n))


def _inproj_kernel(x_ref, mod_ref, g_ref, wm_ref, wl_ref, w2_ref, b2_ref, qkv_ref, u_ref, gg_ref):
    h = _rmsnorm(x_ref[0], g_ref[...]) * (1.0 + mod_ref[0, 0, 1:2, :]) + mod_ref[0, 0, 0:1, :]
    hb = h.astype(BF16)
    q = _dot(hb, wm_ref[:, 0:NA_WIDTH]) * (NA_HEAD_DIM ** -0.5)
    qkv_ref[0, :, 0:NA_WIDTH] = q.astype(BF16)
    qkv_ref[0, :, NA_WIDTH:2 * NA_WIDTH] = _dot(hb, wm_ref[:, NA_WIDTH:2 * NA_WIDTH]).astype(BF16)
    qkv_ref[0, :, 2 * NA_WIDTH:QKV_W] = _dot(hb, wm_ref[:, 2 * NA_WIDTH:QKV_W]).astype(BF16)
    u_ref[0] = _dot(hb, wm_ref[:, QKV_W:QKV_W + POOL_WIDTH])
    c0 = QKV_W + POOL_WIDTH
    gg_ref[0, :, 0:GLA_QK] = _dot(hb, wm_ref[:, c0:c0 + GLA_QK]) * (GLA_DK ** -0.5)
    gg_ref[0, :, GLA_QK:2 * GLA_QK + 2 * GLA_WIDTH] = _dot(hb, wm_ref[:, c0 + GLA_QK:MAIN_W])
    a_low = _dot(hb, wl_ref[...]).astype(BF16)
    lg = _dot(a_low, w2_ref[...]) + b2_ref[...]
    log_sig = jnp.minimum(lg, 0.0) - jnp.log1p(jnp.exp(-jnp.abs(lg)))
    gg_ref[0, :, 2 * GLA_QK + 2 * GLA_WIDTH:G_W] = log_sig / GLA_TAU


def _inproj(xa, mod, g, w_main, w_low, w2, b2, n_lat_tiles):
    bsz, s, d = xa.shape
    nt = s // TM
    const = lambda b, i: (0, 0)
    return pl.pallas_call(
        _inproj_kernel,
        grid=(bsz, nt),
        in_specs=[
            pl.BlockSpec((1, TM, d), lambda b, i: (b, i, 0)),
            pl.BlockSpec((1, 1, 8, d), lambda b, i: (b, jnp.where(i >= n_lat_tiles, 0, 1), 0, 0)),
            pl.BlockSpec((1, d), const),
            pl.BlockSpec(w_main.shape, const),
            pl.BlockSpec(w_low.shape, const),
            pl.BlockSpec(w2.shape, const),
            pl.BlockSpec(b2.shape, const),
        ],
        out_specs=[
            pl.BlockSpec((1, TM, QKV_W), lambda b, i: (b, i, 0)),
            pl.BlockSpec((1, TM, POOL_WIDTH), lambda b, i: (b, i, 0)),
            pl.BlockSpec((1, TM, G_W), lambda b, i: (b, i, 0)),
        ],
        out_shape=[
            jax.ShapeDtypeStruct((bsz, s, QKV_W), BF16),
            jax.ShapeDtypeStruct((bsz, s, POOL_WIDTH), F32),
            jax.ShapeDtypeStruct((bsz, s, G_W), F32),
        ],
        compiler_params=_params("parallel", "parallel"),
        name="inproj",
    )(xa, mod, g, w_main, w_low, w2, b2)


HEADS_PER_GROUP = MXU_DIM // NA_HEAD_DIM
NA_GROUPS = NA_HEADS // HEADS_PER_GROUP
NA_LOCAL_KEYS = NA_WIN_ROWS * GRID_W


def _stack_heads(x, width):
    lane = lax.broadcasted_iota(jnp.int32, x.shape, 1) // width
    n_heads = x.shape[1] // width
    return jnp.concatenate([jnp.where(lane == h, x, jnp.zeros_like(x)) for h in range(n_heads)], axis=0)


def _unstack_heads(o, width):
    n_heads = o.shape[1] // width
    r = o.shape[0] // n_heads
    lane = lax.broadcasted_iota(jnp.int32, (r, o.shape[1]), 1) // width
    acc = jnp.zeros((r, o.shape[1]), o.dtype)
    for h in range(n_heads):
        acc = jnp.where(lane == h, o[h * r:(h + 1) * r, :], acc)
    return acc


def _na_kernel(q_ref, k_ref, v_ref, kc_ref, vc_ref, bias_ref, o_ref, *, n_rows):
    r = pl.program_id(1)

    def attend(g, local):
        cols = slice(g * MXU_DIM, (g + 1) * MXU_DIM)
        qs = _stack_heads(q_ref[0, :, cols], NA_HEAD_DIM)
        kc = kc_ref[0, :, cols]
        vc = vc_ref[0, :, cols]
        s_ctx = _dot_nt(qs, kc)
        m = jnp.max(s_ctx, axis=-1, keepdims=True)
        if local:
            start = pl.multiple_of(jnp.clip(r - NA_WIN_ROWS // 2, 0, n_rows - NA_WIN_ROWS) * GRID_W, GRID_W)
            kw = k_ref[0, pl.ds(start, NA_LOCAL_KEYS), cols]
            vw = v_ref[0, pl.ds(start, NA_LOCAL_KEYS), cols]
            s_loc = _dot_nt(qs, kw) + bias_ref[0, g]
            m = jnp.maximum(m, jnp.max(s_loc, axis=-1, keepdims=True))
            p_loc = jnp.exp(s_loc - m)
        p_ctx = jnp.exp(s_ctx - m)
        den = jnp.sum(p_ctx, axis=-1, keepdims=True)
        o = _dot(p_ctx.astype(BF16), vc)
        if local:
            den = den + jnp.sum(p_loc, axis=-1, keepdims=True)
            o = o + _dot(p_loc.astype(BF16), vw)
        o_ref[0, :, cols] = _unstack_heads(o / den, NA_HEAD_DIM).astype(o_ref.dtype)

    @pl.when(r < n_rows)
    def _():
        for g in range(NA_GROUPS):
            attend(g, True)

    @pl.when(r >= n_rows)
    def _():
        for g in range(NA_GROUPS):
            attend(g, False)


def _na_bias_table(rpb):
    o = jnp.arange(NA_WIN_ROWS)
    dr = o[None, :] - o[:, None] + NA_WIN_ROWS - 1
    col = jnp.arange(GRID_W)
    dc = jnp.clip(col[None, :] - col[:, None], -(NA_WIN_COLS - 1), NA_WIN_COLS - 1) + NA_WIN_COLS - 1
    col_start = jnp.clip(col - NA_WIN_COLS // 2, 0, GRID_W - NA_WIN_COLS)
    col_mask = (col[None, :] >= col_start[:, None]) & (col[None, :] < col_start[:, None] + NA_WIN_COLS)
    t = rpb[:, dr][..., dc].astype(F32)
    t = jnp.where(col_mask[None, None, None], t, MASK_VALUE)
    t = t.transpose(1, 0, 3, 2, 4)
    return t.reshape(NA_WIN_ROWS, NA_GROUPS, HEADS_PER_GROUP * GRID_W, NA_LOCAL_KEYS)


def _neighborhood_attention(qkv, bias_tab, n_lat_tiles, with_ctx_queries):
    bsz, s, _ = qkv.shape
    seq = n_lat_tiles * TM
    n_rows = seq // GRID_W
    assert n_rows >= NA_WIN_ROWS
    n_ctx_rows = (s - seq) // GRID_W
    nq = n_rows + (n_ctx_rows if with_ctx_queries else 0)
    ctx_blk = seq // (s - seq)

    def bias_idx(b, r):
        rr = jnp.minimum(r, n_rows - 1)
        return (rr - jnp.clip(rr - NA_WIN_ROWS // 2, 0, n_rows - NA_WIN_ROWS), 0, 0, 0)

    return pl.pallas_call(
        functools.partial(_na_kernel, n_rows=n_rows),
        grid=(bsz, nq),
        in_specs=[
            pl.BlockSpec((1, GRID_W, NA_WIDTH), lambda b, r: (b, r, 0)),
            pl.BlockSpec((1, seq, NA_WIDTH), lambda b, r: (b, 0, 1)),
            pl.BlockSpec((1, seq, NA_WIDTH), lambda b, r: (b, 0, 2)),
            pl.BlockSpec((1, s - seq, NA_WIDTH), lambda b, r: (b, ctx_blk, 1)),
            pl.BlockSpec((1, s - seq, NA_WIDTH), lambda b, r: (b, ctx_blk, 2)),
            pl.BlockSpec((1,) + bias_tab.shape[1:], bias_idx),
        ],
        out_specs=pl.BlockSpec((1, GRID_W, NA_WIDTH), lambda b, r: (b, r, 0)),
        out_shape=jax.ShapeDtypeStruct((bsz, nq * GRID_W, NA_WIDTH), BF16),
        compiler_params=_params("parallel", "arbitrary"),
        name="neighborhood_attention",
    )(qkv, qkv, qkv, qkv, qkv, bias_tab)


def _pool_kernel(up_ref, u_ref, un_ref, w_ref, sc_ref, o_ref, buf, *, n_lat_tiles, seq, ctx_len):
    i = pl.program_id(1)
    hw = POOL_REACH
    buf[0:hw] = up_ref[0]
    buf[hw:hw + TM] = u_ref[0]
    buf[hw + TM:2 * hw + TM] = un_ref[0]
    is_ctx = i >= n_lat_tiles
    base = jnp.where(is_ctx, 0, i * TM)
    lseq = jnp.where(is_ctx, ctx_len, seq)
    shape = (TM, POOL_WIDTH)
    t = base + lax.broadcasted_iota(jnp.int32, shape, 0)
    grp = lax.broadcasted_iota(jnp.int32, shape, 1) // POOL_GROUP_DIM
    win = jnp.left_shift(2, grp)
    back = win // 2
    fwd = win - back - 1
    acc = jnp.zeros(shape, F32)
    for d in range(-hw, hw):
        ok = (t + d >= 0) & (t + d < lseq) & (back >= -d) & (fwd >= d)
        acc = acc + jnp.where(ok, buf[hw + d:hw + d + TM, :], 0.0)
    lo = jnp.clip(t - back, 0, lseq - 1)
    hi = jnp.clip(t + fwd, 0, lseq - 1)
    mean = acc / (hi - lo + 1).astype(F32)
    diff = (mean - u_ref[0]).astype(BF16)
    o_ref[0] = (_dot(diff, w_ref[...]) * sc_ref[...]).astype(o_ref.dtype)


def _multiscale_pool(u, w_bd, scale, n_lat_tiles, n_tiles):
    bsz, s, c = u.shape
    hw = POOL_REACH
    per = TM // hw
    last = s // hw - 1
    return pl.pallas_call(
        functools.partial(_pool_kernel, n_lat_tiles=n_lat_tiles, seq=n_lat_tiles * TM, ctx_len=s - n_lat_tiles * TM),
        grid=(bsz, n_tiles),
        in_specs=[
            pl.BlockSpec((1, hw, c), lambda b, i: (b, jnp.maximum(i * per - 1, 0), 0)),
            pl.BlockSpec((1, TM, c), lambda b, i: (b, i, 0)),
            pl.BlockSpec((1, hw, c), lambda b, i: (b, jnp.minimum((i + 1) * per, last), 0)),
            pl.BlockSpec((c, c), lambda b, i: (0, 0)),
            pl.BlockSpec((1, c), lambda b, i: (0, 0)),
        ],
        out_specs=pl.BlockSpec((1, TM, c), lambda b, i: (b, i, 0)),
        out_shape=jax.ShapeDtypeStruct((bsz, n_tiles * TM, c), BF16),
        scratch_shapes=[pltpu.VMEM((TM + 2 * hw, c), F32)],
        compiler_params=_params("parallel", "parallel"),
        name="multiscale_pool",
    )(u, u, u, w_bd, scale)


def _gla_tile(qk_ref, v_ref, la_ref, cs_ref, sn_ref, st_ref, o_ref, reverse):
    ch = GLA_CHUNK
    lane = lax.broadcasted_iota(jnp.int32, (ch, GLA_QK), 1)
    row = lax.broadcasted_iota(jnp.int32, (ch, GLA_QK), 0)
    rr = lax.broadcasted_iota(jnp.int32, (GLA_HEADS * ch, ch), 0) % ch
    cc = lax.broadcasted_iota(jnp.int32, (GLA_HEADS * ch, ch), 1)
    causal = (rr <= cc) if reverse else (rr >= cc)
    st_shape = (GLA_WIDTH, GLA_QK)
    head_blk = (lax.broadcasted_iota(jnp.int32, st_shape, 0) // GLA_DV
                == lax.broadcasted_iota(jnp.int32, st_shape, 1) // GLA_DK)
    quarter = GLA_DK // 4
    first_half = (lane % (2 * quarter)) < quarter

    def rope(x, cs, sn):
        partner = jnp.where(first_half, pltpu.roll(x, GLA_QK - quarter, 1), pltpu.roll(x, quarter, 1))
        return x * cs + partner * sn

    chunks = range(TM // ch)
    for c in (reversed(chunks) if reverse else chunks):
        sl = slice(c * ch, (c + 1) * ch)
        cs = cs_ref[sl, :]
        sn = sn_ref[sl, :]
        q = rope(qk_ref[0, sl, 0:GLA_QK], cs, sn)
        k = rope(qk_ref[0, sl, GLA_QK:2 * GLA_QK], cs, sn)
        b = la_ref[0, sl, :]
        d = 1
        while d < ch:
            if reverse:
                b = b + jnp.where(row < ch - d, pltpu.roll(b, ch - d, 0), 0.0)
            else:
                b = b + jnp.where(row >= d, pltpu.roll(b, d, 0), 0.0)
            d *= 2
        b_last = b[0:1, :] if reverse else b[ch - 1:ch, :]
        q_in = q * jnp.exp(b)
        k_in = (k * jnp.exp(-b)).astype(BF16)
        k_st = (k * jnp.exp(b_last - b)).astype(BF16)
        att = _dot_nt(_stack_heads(q_in, GLA_DK).astype(BF16), k_in)
        att = jnp.where(causal, att, 0.0).astype(BF16)
        vb = v_ref[0, sl, :].astype(BF16)
        o = _unstack_heads(_dot(att, vb), GLA_DV)
        st = st_ref[...]
        o_ref[0, sl, :] = o + _dot_nt(q_in.astype(BF16), st.astype(BF16))
        st_ref[...] = st * jnp.exp(b_last) + jnp.where(head_blk, _dot_tn(vb, k_st), 0.0)


def _gla_kernel(qkf, vf, laf, csf, snf, qkb, vb, lab, csb, snb, of_ref, ob_ref, st_f, st_b):
    @pl.when(pl.program_id(1) == 0)
    def _():
        st_f[...] = jnp.zeros_like(st_f)
        st_b[...] = jnp.zeros_like(st_b)

    _gla_tile(qkf, vf, laf, csf, snf, st_f, of_ref, False)
    _gla_tile(qkb, vb, lab, csb, snb, st_b, ob_ref, True)


def _rope_tables(seq, s):
    quarter = GLA_DK // 4
    t = np.arange(s)
    pos_row = np.where(t < seq, t // GRID_W, 0).astype(np.float32)
    pos_col = np.where(t < seq, t % GRID_W, 0).astype(np.float32)
    d = np.arange(GLA_QK) % GLA_DK
    inv_freq = ROPE_BASE ** (-jnp.arange(quarter, dtype=F32) / quarter)
    freq = inv_freq[d % quarter]
    pos = jnp.where((d < GLA_DK // 2)[None, :], pos_row[:, None], pos_col[:, None])
    ang = pos * freq[None, :]
    sign = np.where((d % (2 * quarter)) < quarter, -1.0, 1.0).astype(np.float32)
    return jnp.cos(ang), jnp.sin(ang) * sign[None, :]


def _gla(gg, cos, sin, n_lat_tiles):
    bsz, s, _ = gg.shape
    nt = s // TM
    assert nt == n_lat_tiles + 1
    fwd = lambda i: (i + n_lat_tiles) % nt
    bwd = lambda i: jnp.where(i == 0, n_lat_tiles, n_lat_tiles - i)
    la_f_blk = (2 * GLA_QK + 2 * GLA_WIDTH) // GLA_QK
    la_b_blk = la_f_blk + 1
    v_blk = 2 * GLA_QK // GLA_WIDTH

    def specs(order, la_blk):
        return [
            pl.BlockSpec((1, TM, 2 * GLA_QK), lambda b, i: (b, order(i), 0)),
            pl.BlockSpec((1, TM, GLA_WIDTH), lambda b, i: (b, order(i), v_blk)),
            pl.BlockSpec((1, TM, GLA_QK), lambda b, i: (b, order(i), la_blk)),
            pl.BlockSpec((TM, GLA_QK), lambda b, i: (order(i), 0)),
            pl.BlockSpec((TM, GLA_QK), lambda b, i: (order(i), 0)),
        ]

    return pl.pallas_call(
        _gla_kernel,
        grid=(bsz, nt),
        in_specs=specs(fwd, la_f_blk) + specs(bwd, la_b_blk),
        out_specs=[
            pl.BlockSpec((1, TM, GLA_WIDTH), lambda b, i: (b, fwd(i), 0)),
            pl.BlockSpec((1, TM, GLA_WIDTH), lambda b, i: (b, bwd(i), 0)),
        ],
        out_shape=[jax.ShapeDtypeStruct((bsz, s, GLA_WIDTH), F32)] * 2,
        scratch_shapes=[pltpu.VMEM((GLA_WIDTH, GLA_QK), F32)] * 2,
        compiler_params=_params("arbitrary", "arbitrary"),
        name="gla_bidir",
    )(gg, gg, gg, cos, sin, gg, gg, gg, cos, sin)


def _store_row_tiles(ref, val):
    rows, width = val.shape
    pieces = width // LANES
    for s in range(pieces):
        ref[pl.ds(s, rows, stride=pieces), :] = val[:, s * LANES:(s + 1) * LANES]


def _load_row_tiles(ref, rows, pieces):
    return jnp.concatenate([ref[pl.ds(s, rows, stride=pieces), :] for s in range(pieces)], axis=1)


def _row_index(shape):
    return lax.broadcasted_iota(jnp.int32, shape, 0).astype(F32)


def _stack_rows(rows):
    shape = (len(rows), rows[0].shape[1])
    rid = lax.broadcasted_iota(jnp.int32, shape, 0)
    out = jnp.zeros(shape, rows[0].dtype)
    for k, r in enumerate(rows):
        out = jnp.where(rid == k, r, out)
    return out


def _select_experts(scores, biased):
    n_exp, n_tok = biased.shape
    per = n_exp // N_GROUPS
    neg = -jnp.inf
    sub = _row_index((per, n_tok))
    grp_rows = []
    for g in range(N_GROUPS):
        blk = biased[g * per:(g + 1) * per, :]
        m1 = jnp.max(blk, axis=0, keepdims=True)
        i1 = jnp.min(jnp.where(blk == m1, sub, float(per)), axis=0, keepdims=True)
        m2 = jnp.max(jnp.where(sub == i1, neg, blk), axis=0, keepdims=True)
        grp_rows.append(m1 + m2)
    cur = _stack_rows(grp_rows)
    gid = _row_index(cur.shape)
    picked = jnp.zeros(cur.shape, F32)
    for _ in range(TOPK_GROUPS):
        gm = jnp.max(cur, axis=0, keepdims=True)
        gi = jnp.min(jnp.where(cur == gm, gid, float(N_GROUPS)), axis=0, keepdims=True)
        hit = gid == gi
        picked = jnp.where(hit, 1.0, picked)
        cur = jnp.where(hit, neg, cur)
    cur = jnp.concatenate(
        [jnp.where(picked[g:g + 1, :] > 0.0, biased[g * per:(g + 1) * per, :], neg) for g in range(N_GROUPS)], axis=0)
    eid = _row_index(cur.shape)
    ids, vals = [], []
    chosen = jnp.zeros(cur.shape, F32)
    for _ in range(TOP_K):
        m = jnp.max(cur, axis=0, keepdims=True)
        idx = jnp.min(jnp.where(cur == m, eid, float(n_exp)), axis=0, keepdims=True)
        hit = eid == idx
        ids.append(idx)
        vals.append(jnp.sum(jnp.where(hit, scores, 0.0), axis=0, keepdims=True))
        chosen = jnp.where(hit, 1.0, chosen)
        cur = jnp.where(hit, neg, cur)
    return ids, vals, chosen


def _outproj_kernel(x_ref, oa_ref, ob_ref, of_ref, obk_ref, r_ref, gg_ref, wo_ref, mod_ref, gf_ref, wr_ref, br_ref,
                    tri_ref, xo_ref, h_ref, e_ref, w_ref, rk_ref, hist_ref):
    o = of_ref[0] + obk_ref[0]
    head = lax.broadcasted_iota(jnp.int32, o.shape, 1) // GLA_DV
    o2 = o * o
    rs = jnp.zeros_like(o)
    for h in range(GLA_HEADS):
        ssq = jnp.sum(jnp.where(head == h, o2, 0.0), axis=-1, keepdims=True)
        rs = jnp.where(head == h, lax.rsqrt(ssq / GLA_DV + NORM_EPS), rs)
    oc = (o * rs * gg_ref[...] * _silu(r_ref[0])).astype(BF16)
    c1 = NA_WIDTH + POOL_WIDTH
    acc = _dot(oa_ref[0], wo_ref[0:NA_WIDTH]) + _dot(ob_ref[0], wo_ref[NA_WIDTH:c1]) + _dot(oc, wo_ref[c1:])
    x = x_ref[0] + mod_ref[0, 0, 2:3, :] * acc
    xo_ref[0] = x
    h = _rmsnorm(x, gf_ref[...]) * (1.0 + mod_ref[0, 0, 4:5, :]) + mod_ref[0, 0, 3:4, :]
    _store_row_tiles(h_ref, h)

    scores = _sigmoid(_dot_nt(wr_ref[...], h.astype(BF16)))
    ids, vals, chosen = _select_experts(scores, scores + br_ref[...])
    total = vals[0]
    for v in vals[1:]:
        total = total + v
    w_ref[...] = _stack_rows([v / total * ROUTED_SCALE for v in vals])
    e_ref[...] = _stack_rows(ids).astype(jnp.int32)
    chosen_b = chosen.astype(BF16)
    before = _dot(chosen_b, tri_ref[...])
    eid = _row_index(before.shape)
    rk_ref[...] = _stack_rows([jnp.sum(jnp.where(eid == idx, before, 0.0), axis=0, keepdims=True) for idx in ids])
    hist_ref[0] = _dot_nt(jnp.ones((8, chosen.shape[1]), BF16), chosen_b)


def _outproj(xa, oa, ob, o_f, o_b, gg, g_gla4, w_out, mod, g_ffn, w_router_t, b_router, n_lat_tiles, n_tiles):
    bsz, _, d = xa.shape
    n_exp = w_router_t.shape[0]
    r_blk = (2 * GLA_QK + GLA_WIDTH) // GLA_WIDTH
    tile = lambda w: pl.BlockSpec((1, TM, w), lambda b, i: (b, i, 0))
    const = lambda b, i: (0, 0)
    per_tok = pl.BlockSpec((TOP_K, TM), lambda b, i: (0, b * n_tiles + i))
    t_tok = bsz * n_tiles * TM
    tri = (jnp.arange(TM)[:, None] < jnp.arange(TM)[None, :]).astype(BF16)
    return pl.pallas_call(
        _outproj_kernel,
        grid=(bsz, n_tiles),
        in_specs=[
            tile(d), tile(NA_WIDTH), tile(POOL_WIDTH), tile(GLA_WIDTH), tile(GLA_WIDTH),
            pl.BlockSpec((1, TM, GLA_WIDTH), lambda b, i: (b, i, r_blk)),
            pl.BlockSpec((1, GLA_WIDTH), const),
            pl.BlockSpec(w_out.shape, const),
            pl.BlockSpec((1, 1, 8, d), lambda b, i: (b, jnp.where(i >= n_lat_tiles, 0, 1), 0, 0)),
            pl.BlockSpec((1, d), const),
            pl.BlockSpec(w_router_t.shape, const),
            pl.BlockSpec((n_exp, 1), const),
            pl.BlockSpec((TM, TM), const),
        ],
        out_specs=[tile(d), pl.BlockSpec((TM * d // LANES, LANES), lambda b, i: (b * n_tiles + i, 0)),
                   per_tok, per_tok, per_tok,
                   pl.BlockSpec((1, 8, n_exp), lambda b, i: (b * n_tiles + i, 0, 0))],
        out_shape=[
            jax.ShapeDtypeStruct((bsz, n_tiles * TM, d), F32),
            jax.ShapeDtypeStruct((t_tok * d // LANES, LANES), F32),
            jax.ShapeDtypeStruct((TOP_K, t_tok), jnp.int32),
            jax.ShapeDtypeStruct((TOP_K, t_tok), F32),
            jax.ShapeDtypeStruct((TOP_K, t_tok), F32),
            jax.ShapeDtypeStruct((bsz * n_tiles, 8, n_exp), F32),
        ],
        compiler_params=_params("parallel", "parallel"),
        name="outproj_router",
    )(xa, oa, ob, o_f, o_b, gg, g_gla4, w_out, mod, g_ffn, w_router_t, b_router, tri)


def _slot_layout(hist):
    n_tiles, n_exp = hist.shape
    counts = hist.sum(0)
    tile_base = jnp.cumsum(hist, axis=0) - hist
    padded = (counts + MOE_BLOCK - 1) // MOE_BLOCK * MOE_BLOCK
    pad_end = jnp.cumsum(padded)
    pad_start = pad_end - padded
    base = (pad_start[None, :] + tile_base).astype(F32).reshape(n_tiles, n_exp, 1)
    n_blocks = -(-(n_tiles * TM * TOP_K) // MOE_BLOCK) + n_exp
    blk_start = jnp.arange(n_blocks, dtype=jnp.int32) * MOE_BLOCK
    block_e = jnp.minimum(jnp.searchsorted(pad_end, blk_start, side='right'), n_exp - 1).astype(jnp.int32)
    n_used = (pad_end[-1] // MOE_BLOCK).astype(jnp.int32)
    fill = jnp.concatenate([jnp.maximum(pad_end // MOE_BLOCK - 1, 0).astype(jnp.int32),
                            jnp.minimum(n_used + jnp.arange(n_exp, dtype=jnp.int32), n_blocks - 1)])
    return base, block_e, fill, n_used.reshape(1)


def _pos_kernel(e_ref, rk_ref, base_ref, pos_ref, *, pieces):
    eid = lax.broadcasted_iota(jnp.int32, (base_ref.shape[1], e_ref.shape[1]), 0)
    base = base_ref[0]
    rows = [jnp.sum(jnp.where(eid == e_ref[k:k + 1, :], base, 0.0), axis=0, keepdims=True) for k in range(TOP_K)]
    pos_ref[0] = ((_stack_rows(rows) + rk_ref[...]) * float(pieces)).astype(jnp.int32)


def _slot_positions(e_t, rank_t, base, pieces):
    n_tiles, n_exp, _ = base.shape
    per_tok = pl.BlockSpec((TOP_K, TM), lambda i: (0, i))
    return pl.pallas_call(
        functools.partial(_pos_kernel, pieces=pieces),
        grid=(n_tiles,),
        in_specs=[per_tok, per_tok, pl.BlockSpec((1, n_exp, 1), lambda i: (i, 0, 0))],
        out_specs=pl.BlockSpec((1, TOP_K, TM), lambda i: (i, 0, 0)),
        out_shape=jax.ShapeDtypeStruct((n_tiles, TOP_K, TM), jnp.int32),
        compiler_params=_params("parallel"),
        name="moe_positions",
    )(e_t, rank_t, base).reshape(-1)


def _dispatch_kernel(fill_ref, pos_ref, h_ref, xs_hbm, zbuf, sem, zsem):
    @pl.when(pl.program_id(0) == 0)
    def _():
        zbuf[...] = jnp.zeros_like(zbuf)
        n_fill = fill_ref.shape[0]

        def is_new(j):
            return (j == 0) | (fill_ref[j] != fill_ref[jnp.maximum(j - 1, 0)])

        def start(j, c):
            @pl.when(is_new(j))
            def _():
                row0 = pl.multiple_of(fill_ref[j] * zbuf.shape[0], zbuf.shape[0])
                pltpu.make_async_copy(zbuf, xs_hbm.at[pl.ds(row0, zbuf.shape[0]), :], zsem).start()
            return c
        lax.fori_loop(0, n_fill, start, 0)

        def wait(j, c):
            @pl.when(is_new(j))
            def _():
                pltpu.make_async_copy(zbuf, xs_hbm.at[pl.ds(0, zbuf.shape[0]), :], zsem).wait()
            return c
        lax.fori_loop(0, n_fill, wait, 0)

    pieces = h_ref.shape[0] // TM

    def body(t, c):
        src = h_ref.at[pl.ds(pl.multiple_of(t * pieces, pieces), pieces), :]
        for k in range(TOP_K):
            dst = xs_hbm.at[pl.ds(pl.multiple_of(pos_ref[k * TM + t], pieces), pieces), :]
            pltpu.make_async_copy(src, dst, sem).start(priority=k % 2)
        return c
    lax.fori_loop(0, TM, body, 0, unroll=2)
    for _ in range(TOP_K):
        pltpu.make_async_copy(h_ref, xs_hbm.at[pl.ds(0, h_ref.shape[0]), :], sem).wait()


def _dispatch(h2t, pos, fill, n_slots, pieces):
    n_tiles = h2t.shape[0] // (TM * pieces)
    grid_spec = pltpu.PrefetchScalarGridSpec(
        num_scalar_prefetch=1,
        grid=(n_tiles,),
        in_specs=[
            pl.BlockSpec((TOP_K * TM,), lambda i, fill: (i,), memory_space=pltpu.SMEM),
            pl.BlockSpec((TM * pieces, LANES), lambda i, fill: (i, 0)),
        ],
        out_specs=pl.BlockSpec(memory_space=pl.ANY),
        scratch_shapes=[pltpu.VMEM((MOE_BLOCK * pieces, LANES), F32), pltpu.SemaphoreType.DMA,
                        pltpu.SemaphoreType.DMA],
    )
    return pl.pallas_call(
        _dispatch_kernel,
        grid_spec=grid_spec,
        out_shape=jax.ShapeDtypeStruct((n_slots * pieces, LANES), F32),
        compiler_params=_params("arbitrary"),
        name="moe_dispatch",
    )(fill, pos, h2t)


def _expert_kernel(be_ref, nu_ref, x_ref, wi_ref, wd_ref, y_ref, wi_bf, wd_bf):
    i = pl.program_id(0)
    d_exp = wd_ref.shape[2]

    @pl.when(i < nu_ref[0])
    def _():
        @pl.when((i == 0) | (be_ref[i] != be_ref[jnp.maximum(i - 1, 0)]))
        def _():
            wi_bf[...] = wi_ref[0, 0].astype(BF16)
            wd_bf[...] = wd_ref[0, 0].astype(BF16)

        pieces = x_ref.shape[0] // MOE_BLOCK
        hh = _dot(_load_row_tiles(x_ref, MOE_BLOCK, pieces).astype(BF16), wi_bf[...])
        a = (_silu(hh[:, :d_exp]) * hh[:, d_exp:]).astype(BF16)
        _store_row_tiles(y_ref, _dot(a, wd_bf[...]))

    @pl.when(i >= nu_ref[0])
    def _():
        y_ref[...] = jnp.zeros_like(y_ref)


def _moe_experts(xs, block_e, n_used, w_e_in, w_e_down, layer):
    n_blocks = block_e.shape[0]
    _, _, d, f2 = w_e_in.shape
    blk = (xs.shape[0] // n_blocks, LANES)
    grid_spec = pltpu.PrefetchScalarGridSpec(
        num_scalar_prefetch=2,
        grid=(n_blocks,),
        in_specs=[
            pl.BlockSpec(blk, lambda i, be, nu: (jnp.minimum(i, nu[0] - 1), 0)),
            pl.BlockSpec((1, 1, d, f2), lambda i, be, nu: (layer, be[i], 0, 0)),
            pl.BlockSpec((1, 1, f2 // 2, d), lambda i, be, nu: (layer, be[i], 0, 0)),
        ],
        out_specs=pl.BlockSpec(blk, lambda i, be, nu: (i, 0)),
        scratch_shapes=[pltpu.VMEM((d, f2), BF16), pltpu.VMEM((f2 // 2, d), BF16)],
    )
    return pl.pallas_call(
        _expert_kernel,
        grid_spec=grid_spec,
        out_shape=jax.ShapeDtypeStruct(xs.shape, F32),
        compiler_params=_params("arbitrary"),
        name="moe_experts",
    )(block_e, n_used, xs, w_e_in, w_e_down)


def _combine_kernel(pos0_ref, posn_ref, x_ref, h_ref, w_ref, wsi_ref, wsd_ref, mod_ref, gfin_ref, ys_hbm, o_ref,
                    ybuf, sem, *, final, n_steps):
    step = pl.program_id(0) * pl.num_programs(1) + pl.program_id(1)
    slot = step % 2
    d_exp = wsd_ref.shape[0]
    pieces = h_ref.shape[0] // TMC

    def gather(pos_ref, st, s):
        off = (st % (TM // TMC)) * TMC
        def body(t, c):
            for k in range(TOP_K):
                src = ys_hbm.at[pl.ds(pl.multiple_of(pos_ref[k * TM + off + t], pieces), pieces), :]
                dst = ybuf.at[s, k, pl.ds(pl.multiple_of(t * pieces, pieces), pieces), :]
                pltpu.make_async_copy(src, dst, sem.at[s]).start(priority=k % 2)
            return c
        lax.fori_loop(0, TMC, body, 0, unroll=2)

    @pl.when(step == 0)
    def _():
        gather(pos0_ref, 0, 0)

    @pl.when(step + 1 < n_steps)
    def _():
        gather(posn_ref, step + 1, 1 - slot)

    hh = _dot(_load_row_tiles(h_ref, TMC, pieces).astype(BF16), wsi_ref[...])
    a = (_silu(hh[:, :d_exp]) * hh[:, d_exp:]).astype(BF16)
    y = _dot(a, wsd_ref[...])
    for k in range(TOP_K):
        pltpu.make_async_copy(ys_hbm.at[pl.ds(0, TMC * pieces), :], ybuf.at[slot, k], sem.at[slot]).wait()
    w = w_ref[...]
    for k in range(TOP_K):
        y = y + _load_row_tiles(ybuf.at[slot, k], TMC, pieces) * w[:, k:k + 1]
    x = x_ref[0] + mod_ref[0, 0, 5:6, :] * y
    if final:
        x = _rmsnorm(x, gfin_ref[...])
    o_ref[0] = x


def _combine(x_mid, h2t, ys, pos, w_tok, w_sh_in, w_sh_down, mod, g_final, n_lat_tiles, final):
    bsz, s, d = x_mid.shape
    nt = s // TMC
    n_steps = bsz * nt
    halves = TM // TMC
    lat_steps = n_lat_tiles * halves
    pieces = d // LANES
    const = lambda b, i: (0, 0)
    tile = pl.BlockSpec((1, TMC, d), lambda b, i: (b, i, 0))
    smem_blk = lambda f: pl.BlockSpec((TOP_K * TM,), f, memory_space=pltpu.SMEM)
    return pl.pallas_call(
        functools.partial(_combine_kernel, final=final, n_steps=n_steps),
        grid=(bsz, nt),
        in_specs=[
            smem_blk(lambda b, i: (0,)),
            smem_blk(lambda b, i: (jnp.minimum(b * nt + i + 1, n_steps - 1) // halves,)),
            tile,
            pl.BlockSpec((TMC * pieces, LANES), lambda b, i: (b * nt + i, 0)),
            pl.BlockSpec((TMC, TOP_K), lambda b, i: (b * nt + i, 0)),
            pl.BlockSpec(w_sh_in.shape, const),
            pl.BlockSpec(w_sh_down.shape, const),
            pl.BlockSpec((1, 1, 8, d), lambda b, i: (b, jnp.where(i >= lat_steps, 0, 1), 0, 0)),
            pl.BlockSpec((1, d), const),
            pl.BlockSpec(memory_space=pl.ANY),
        ],
        out_specs=tile,
        out_shape=jax.ShapeDtypeStruct((bsz, s, d), F32),
        scratch_shapes=[pltpu.VMEM((2, TOP_K, TMC * pieces, LANES), F32), pltpu.SemaphoreType.DMA((2,))],
        compiler_params=_params("arbitrary", "arbitrary"),
        name="moe_combine",
    )(pos, pos, x_mid, h2t, w_tok, w_sh_in, w_sh_down, mod, g_final, ys)


def _block_diag(w):
    g, a, b = w.shape
    out = jnp.zeros((g * a, g * b), w.dtype)
    for j in range(g):
        out = out.at[j * a:(j + 1) * a, j * b:(j + 1) * b].set(w[j])
    return out


def kernel(x, c, ctx, c_ctx, w_mod, b_mod, g_mix, w_in, rpb, w_pool, pool_scale, w_gate_f, b_gate_f, w_gate_b, b_gate_b, g_gla, w_out, g_ffn, w_router, b_router, w_e_in, w_e_down, w_sh_in, w_sh_down, g_final):
    bsz, seq, d = x.shape
    n_ctx = ctx.shape[1]
    depth = w_mod.shape[0]
    n_exp = w_router.shape[2]
    assert n_ctx == TM and seq % TM == 0 and seq % GRID_W == 0
    n_lat = seq // TM
    s = seq + n_ctx

    n_rows = -(-(bsz + 1) // 8) * 8
    cc = jnp.zeros((n_rows, d), F32).at[:bsz].set(c).at[bsz].set(c_ctx)
    mod_all = _modulation(cc, w_mod, b_mod).reshape(depth, n_rows, 6, d)
    cos, sin = _rope_tables(seq, s)

    xa = jnp.concatenate([x, ctx], axis=1)
    for layer in range(depth):
        last = layer == depth - 1
        n_tiles = n_lat if last else n_lat + 1
        m = mod_all[layer]
        mod = jnp.stack([jnp.broadcast_to(m[bsz], (bsz, 6, d)), m[:bsz]], axis=1)
        mod = jnp.pad(mod, ((0, 0), (0, 0), (0, 2), (0, 0)))

        wl = w_in[layer]
        w_main = wl[:, :MAIN_W].astype(BF16)
        w_low = jnp.pad(wl[:, MAIN_W:], ((0, 0), (0, LANES - 2 * GLA_GATE_RANK))).astype(BF16)
        w2 = jnp.zeros((LANES, 2 * GLA_QK), F32)
        w2 = w2.at[:GLA_GATE_RANK, :GLA_QK].set(w_gate_f[layer])
        w2 = w2.at[GLA_GATE_RANK:2 * GLA_GATE_RANK, GLA_QK:].set(w_gate_b[layer]).astype(BF16)
        b2 = jnp.concatenate([b_gate_f[layer], b_gate_b[layer]])[None, :]
        qkv, u, gg = _inproj(xa, mod, g_mix[layer][None, :], w_main, w_low, w2, b2, n_lat)

        oa = _neighborhood_attention(qkv, _na_bias_table(rpb[layer]), n_lat, not last)
        ob = _multiscale_pool(u, _block_diag(w_pool[layer]).astype(BF16), pool_scale[layer][None, :], n_lat, n_tiles)
        o_f, o_b = _gla(gg, cos, sin, n_lat)

        x_mid, h2t, e_t, w_t, rank_t, hist = _outproj(
            xa, oa, ob, o_f, o_b, gg, jnp.tile(g_gla[layer], GLA_HEADS)[None, :], w_out[layer].astype(BF16), mod,
            g_ffn[layer][None, :], w_router[layer].T.astype(BF16), b_router[layer].reshape(n_exp, 1), n_lat, n_tiles)
        pieces = d // LANES
        base, block_e, fill, n_used = _slot_layout(hist[:, 0, :].astype(jnp.int32))
        pos = _slot_positions(e_t, rank_t, base, pieces)
        xs = _dispatch(h2t, pos, fill, block_e.shape[0] * MOE_BLOCK, pieces)
        ys = _moe_experts(xs, block_e, n_used, w_e_in, w_e_down, layer)
        xa = _combine(x_mid, h2t, ys, pos, w_t.T, w_sh_in[layer].astype(BF16), w_sh_down[layer].astype(BF16), mod,
                      g_final[None, :], n_lat, last)
    return xa
```

```python
import functools

import jax
import jax.numpy as jnp
import numpy as np
from jax import lax
from jax.experimental import pallas as pl
from jax.experimental.pallas import tpu as pltpu

GRID_W = 64
NORM_EPS = 1e-6
NA_HEADS = 8
NA_HEAD_DIM = 64
NA_WIDTH = NA_HEADS * NA_HEAD_DIM
NA_WIN_ROWS = 8
NA_WIN_COLS = 16
POOL_WINDOWS = (2, 4, 8, 16)
POOL_GROUP_DIM = 64
POOL_WIDTH = len(POOL_WINDOWS) * POOL_GROUP_DIM
POOL_REACH = max(POOL_WINDOWS) // 2
assert POOL_WINDOWS == tuple(2 ** (g + 1) for g in range(len(POOL_WINDOWS)))
GLA_HEADS = 4
GLA_DK = 32
GLA_DV = 64
GLA_QK = GLA_HEADS * GLA_DK
GLA_WIDTH = GLA_HEADS * GLA_DV
GLA_GATE_RANK = 16
GLA_TAU = 16.0
GLA_CHUNK = 64
ROPE_BASE = 10000.0
N_EXPERTS = 256
TOP_K = 8
N_GROUPS = 8
TOPK_GROUPS = 4
ROUTED_SCALE = 2.5

TM = 256
TMC = 128
MOE_BLOCK = 256
LANES = 128
MXU_DIM = 256
MASK_VALUE = -1e30
VMEM_LIMIT = 48 * 1024 * 1024

QKV_W = 3 * NA_WIDTH
G_W = 2 * GLA_QK + 2 * GLA_WIDTH + 2 * GLA_QK
MAIN_W = QKV_W + POOL_WIDTH + 2 * GLA_QK + 2 * GLA_WIDTH

BF16 = jnp.bfloat16
F32 = jnp.float32


def _params(*sem):
    return pltpu.CompilerParams(dimension_semantics=sem, vmem_limit_bytes=VMEM_LIMIT)


def _sigmoid(x):
    return 1.0 / (1.0 + jnp.exp(-x))


def _silu(x):
    return x * _sigmoid(x)


def _rmsnorm(x, g):
    return x * lax.rsqrt(jnp.mean(x * x, axis=-1, keepdims=True) + NORM_EPS) * g


def _dot(a, b):
    return jnp.dot(a, b, preferred_element_type=F32)


def _dot_nt(a, b):
    return lax.dot_general(a, b, (((1,), (1,)), ((), ())), preferred_element_type=F32)


def _dot_tn(a, b):
    return lax.dot_general(a, b, (((0,), (0,)), ((), ())), preferred_element_type=F32)


def _mod_kernel(c_ref, w_ref, b_ref, o_ref):
    a = _silu(c_ref[...]).astype(BF16)
    o_ref[0] = _dot(a, w_ref[0].astype(BF16)) + b_ref[0]


def _modulation(cc, w_mod, b_mod):
    depth, d, n = w_mod.shape
    r = cc.shape[0]
    tn = 1024
    return pl.pallas_call(
        _mod_kernel,
        grid=(depth, n // tn),
        in_specs=[
            pl.BlockSpec((r, d), lambda l, j: (0, 0)),
            pl.BlockSpec((1, d, tn), lambda l, j: (l, 0, j)),
            pl.BlockSpec((1, 1, tn), lambda l, j: (l, 0, j)),
        ],
        out_specs=pl.BlockSpec((1, r, tn), lambda l, j: (l, 0, j)),
        out_shape=jax.ShapeDtypeStruct((depth, r, n), F32),
        compiler_params=_params("parallel", "parallel"),
        name="modulation",
    )(cc, w_mod, b_mod.reshape(depth, 1, n))


def _inproj_kernel(x_ref, mod_ref, g_ref, wm_ref, wl_ref, w2_ref, b2_ref, qkv_ref, u_ref, gg_ref):
    h = _rmsnorm(x_ref[0], g_ref[...]) * (1.0 + mod_ref[0, 0, 1:2, :]) + mod_ref[0, 0, 0:1, :]
    hb = h.astype(BF16)
    q = _dot(hb, wm_ref[:, 0:NA_WIDTH]) * (NA_HEAD_DIM ** -0.5)
    qkv_ref[0, :, 0:NA_WIDTH] = q.astype(BF16)
    qkv_ref[0, :, NA_WIDTH:2 * NA_WIDTH] = _dot(hb, wm_ref[:, NA_WIDTH:2 * NA_WIDTH]).astype(BF16)
    qkv_ref[0, :, 2 * NA_WIDTH:QKV_W] = _dot(hb, wm_ref[:, 2 * NA_WIDTH:QKV_W]).astype(BF16)
    u_ref[0] = _dot(hb, wm_ref[:, QKV_W:QKV_W + POOL_WIDTH])
    c0 = QKV_W + POOL_WIDTH
    gg_ref[0, :, 0:GLA_QK] = _dot(hb, wm_ref[:, c0:c0 + GLA_QK]) * (GLA_DK ** -0.5)
    gg_ref[0, :, GLA_QK:2 * GLA_QK + 2 * GLA_WIDTH] = _dot(hb, wm_ref[:, c0 + GLA_QK:MAIN_W])
    a_low = _dot(hb, wl_ref[...]).astype(BF16)
    lg = _dot(a_low, w2_ref[...]) + b2_ref[...]
    log_sig = jnp.minimum(lg, 0.0) - jnp.log1p(jnp.exp(-jnp.abs(lg)))
    gg_ref[0, :, 2 * GLA_QK + 2 * GLA_WIDTH:G_W] = log_sig / GLA_TAU


def _inproj(xa, mod, g, w_main, w_low, w2, b2, n_lat_tiles):
    bsz, s, d = xa.shape
    nt = s // TM
    const = lambda b, i: (0, 0)
    return pl.pallas_call(
        _inproj_kernel,
        grid=(bsz, nt),
        in_specs=[
            pl.BlockSpec((1, TM, d), lambda b, i: (b, i, 0)),
            pl.BlockSpec((1, 1, 8, d), lambda b, i: (b, jnp.where(i >= n_lat_tiles, 0, 1), 0, 0)),
            pl.BlockSpec((1, d), const),
            pl.BlockSpec(w_main.shape, const),
            pl.BlockSpec(w_low.shape, const),
            pl.BlockSpec(w2.shape, const),
            pl.BlockSpec(b2.shape, const),
        ],
        out_specs=[
            pl.BlockSpec((1, TM, QKV_W), lambda b, i: (b, i, 0)),
            pl.BlockSpec((1, TM, POOL_WIDTH), lambda b, i: (b, i, 0)),
            pl.BlockSpec((1, TM, G_W), lambda b, i: (b, i, 0)),
        ],
        out_shape=[
            jax.ShapeDtypeStruct((bsz, s, QKV_W), BF16),
            jax.ShapeDtypeStruct((bsz, s, POOL_WIDTH), F32),
            jax.ShapeDtypeStruct((bsz, s, G_W), F32),
        ],
        compiler_params=_params("parallel", "parallel"),
        name="inproj",
    )(xa, mod, g, w_main, w_low, w2, b2)


HEADS_PER_GROUP = MXU_DIM // NA_HEAD_DIM
NA_GROUPS = NA_HEADS // HEADS_PER_GROUP
NA_LOCAL_KEYS = NA_WIN_ROWS * GRID_W


def _stack_heads(x, width):
    lane = lax.broadcasted_iota(jnp.int32, x.shape, 1) // width
    n_heads = x.shape[1] // width
    return jnp.concatenate([jnp.where(lane == h, x, jnp.zeros_like(x)) for h in range(n_heads)], axis=0)


def _unstack_heads(o, width):
    n_heads = o.shape[1] // width
    r = o.shape[0] // n_heads
    lane = lax.broadcasted_iota(jnp.int32, (r, o.shape[1]), 1) // width
    acc = jnp.zeros((r, o.shape[1]), o.dtype)
    for h in range(n_heads):
        acc = jnp.where(lane == h, o[h * r:(h + 1) * r, :], acc)
    return acc


def _na_kernel(q_ref, k_ref, v_ref, kc_ref, vc_ref, bias_ref, o_ref, *, n_rows):
    r = pl.program_id(1)

    def attend(g, local):
        cols = slice(g * MXU_DIM, (g + 1) * MXU_DIM)
        qs = _stack_heads(q_ref[0, :, cols], NA_HEAD_DIM)
        kc = kc_ref[0, :, cols]
        vc = vc_ref[0, :, cols]
        s_ctx = _dot_nt(qs, kc)
        m = jnp.max(s_ctx, axis=-1, keepdims=True)
        if local:
            start = pl.multiple_of(jnp.clip(r - NA_WIN_ROWS // 2, 0, n_rows - NA_WIN_ROWS) * GRID_W, GRID_W)
            kw = k_ref[0, pl.ds(start, NA_LOCAL_KEYS), cols]
            vw = v_ref[0, pl.ds(start, NA_LOCAL_KEYS), cols]
            s_loc = _dot_nt(qs, kw) + bias_ref[0, g]
            m = jnp.maximum(m, jnp.max(s_loc, axis=-1, keepdims=True))
            p_loc = jnp.exp(s_loc - m)
        p_ctx = jnp.exp(s_ctx - m)
        den = jnp.sum(p_ctx, axis=-1, keepdims=True)
        o = _dot(p_ctx.astype(BF16), vc)
        if local:
            den = den + jnp.sum(p_loc, axis=-1, keepdims=True)
            o = o + _dot(p_loc.astype(BF16), vw)
        o_ref[0, :, cols] = _unstack_heads(o / den, NA_HEAD_DIM).astype(o_ref.dtype)

    @pl.when(r < n_rows)
    def _():
        for g in range(NA_GROUPS):
            attend(g, True)

    @pl.when(r >= n_rows)
    def _():
        for g in range(NA_GROUPS):
            attend(g, False)


def _na_bias_table(rpb):
    o = jnp.arange(NA_WIN_ROWS)
    dr = o[None, :] - o[:, None] + NA_WIN_ROWS - 1
    col = jnp.arange(GRID_W)
    dc = jnp.clip(col[None, :] - col[:, None], -(NA_WIN_COLS - 1), NA_WIN_COLS - 1) + NA_WIN_COLS - 1
    col_start = jnp.clip(col - NA_WIN_COLS // 2, 0, GRID_W - NA_WIN_COLS)
    col_mask = (col[None, :] >= col_start[:, None]) & (col[None, :] < col_start[:, None] + NA_WIN_COLS)
    t = rpb[:, dr][..., dc].astype(F32)
    t = jnp.where(col_mask[None, None, None], t, MASK_VALUE)
    t = t.transpose(1, 0, 3, 2, 4)
    return t.reshape(NA_WIN_ROWS, NA_GROUPS, HEADS_PER_GROUP * GRID_W, NA_LOCAL_KEYS)


def _neighborhood_attention(qkv, bias_tab, n_lat_tiles, with_ctx_queries):
    bsz, s, _ = qkv.shape
    seq = n_lat_tiles * TM
    n_rows = seq // GRID_W
    assert n_rows >= NA_WIN_ROWS
    n_ctx_rows = (s - seq) // GRID_W
    nq = n_rows + (n_ctx_rows if with_ctx_queries else 0)
    ctx_blk = seq // (s - seq)

    def bias_idx(b, r):
        rr = jnp.minimum(r, n_rows - 1)
        return (rr - jnp.clip(rr - NA_WIN_ROWS // 2, 0, n_rows - NA_WIN_ROWS), 0, 0, 0)

    return pl.pallas_call(
        functools.partial(_na_kernel, n_rows=n_rows),
        grid=(bsz, nq),
        in_specs=[
            pl.BlockSpec((1, GRID_W, NA_WIDTH), lambda b, r: (b, r, 0)),
            pl.BlockSpec((1, seq, NA_WIDTH), lambda b, r: (b, 0, 1)),
            pl.BlockSpec((1, seq, NA_WIDTH), lambda b, r: (b, 0, 2)),
            pl.BlockSpec((1, s - seq, NA_WIDTH), lambda b, r: (b, ctx_blk, 1)),
            pl.BlockSpec((1, s - seq, NA_WIDTH), lambda b, r: (b, ctx_blk, 2)),
            pl.BlockSpec((1,) + bias_tab.shape[1:], bias_idx),
        ],
        out_specs=pl.BlockSpec((1, GRID_W, NA_WIDTH), lambda b, r: (b, r, 0)),
        out_shape=jax.ShapeDtypeStruct((bsz, nq * GRID_W, NA_WIDTH), BF16),
        compiler_params=_params("parallel", "arbitrary"),
        name="neighborhood_attention",
    )(qkv, qkv, qkv, qkv, qkv, bias_tab)


def _pool_kernel(up_ref, u_ref, un_ref, w_ref, sc_ref, o_ref, buf, s0, s1, s2, *, n_lat_tiles, seq, ctx_len):
    i = pl.program_id(1)
    hw = POOL_REACH
    n = TM + 2 * hw
    is_ctx = i >= n_lat_tiles
    base = jnp.where(is_ctx, 0, i * TM)
    lseq = jnp.where(is_ctx, ctx_len, seq)
    buf[0:hw] = up_ref[0]
    buf[hw:hw + TM] = u_ref[0]
    buf[hw + TM:n] = un_ref[0]
    p = base - hw + lax.broadcasted_iota(jnp.int32, (n, POOL_WIDTH), 0)
    buf[...] = jnp.where((p >= 0) & (p < lseq), buf[...], 0.0)
    s0[0:n - 1] = buf[0:n - 1] + buf[1:n]
    s1[0:n - 3] = s0[0:n - 3] + s0[2:n - 1]
    s2[0:n - 7] = s1[0:n - 7] + s1[4:n - 3]
    shape = (TM, POOL_WIDTH)
    t = base + lax.broadcasted_iota(jnp.int32, shape, 0)
    grp = lax.broadcasted_iota(jnp.int32, shape, 1) // POOL_GROUP_DIM
    acc = jnp.where(grp == 0, s0[hw - 1:hw - 1 + TM],
                    jnp.where(grp == 1, s1[hw - 2:hw - 2 + TM],
                              jnp.where(grp == 2, s2[hw - 4:hw - 4 + TM], s2[0:TM] + s2[hw:hw + TM])))
    win = jnp.left_shift(2, grp)
    back = win // 2
    fwd = win - back - 1
    lo = jnp.clip(t - back, 0, lseq - 1)
    hi = jnp.clip(t + fwd, 0, lseq - 1)
    mean = acc / (hi - lo + 1).astype(F32)
    diff = (mean - u_ref[0]).astype(BF16)
    o_ref[0] = (_dot(diff, w_ref[...]) * sc_ref[...]).astype(o_ref.dtype)


def _multiscale_pool(u, w_bd, scale, n_lat_tiles, n_tiles):
    bsz, s, c = u.shape
    hw = POOL_REACH
    per = TM // hw
    last = s // hw - 1
    return pl.pallas_call(
        functools.partial(_pool_kernel, n_lat_tiles=n_lat_tiles, seq=n_lat_tiles * TM, ctx_len=s - n_lat_tiles * TM),
        grid=(bsz, n_tiles),
        in_specs=[
            pl.BlockSpec((1, hw, c), lambda b, i: (b, jnp.maximum(i * per - 1, 0), 0)),
            pl.BlockSpec((1, TM, c), lambda b, i: (b, i, 0)),
            pl.BlockSpec((1, hw, c), lambda b, i: (b, jnp.minimum((i + 1) * per, last), 0)),
            pl.BlockSpec((c, c), lambda b, i: (0, 0)),
            pl.BlockSpec((1, c), lambda b, i: (0, 0)),
        ],
        out_specs=pl.BlockSpec((1, TM, c), lambda b, i: (b, i, 0)),
        out_shape=jax.ShapeDtypeStruct((bsz, n_tiles * TM, c), BF16),
        scratch_shapes=[pltpu.VMEM((TM + 2 * hw, c), F32)] * 4,
        compiler_params=_params("parallel", "parallel"),
        name="multiscale_pool",
    )(u, u, u, w_bd, scale)


def _gla_tile(qk_ref, v_ref, la_ref, cs_ref, sn_ref, st_ref, o_ref, reverse):
    ch = GLA_CHUNK
    lane = lax.broadcasted_iota(jnp.int32, (ch, GLA_QK), 1)
    row = lax.broadcasted_iota(jnp.int32, (ch, GLA_QK), 0)
    rr = lax.broadcasted_iota(jnp.int32, (GLA_HEADS * ch, ch), 0) % ch
    cc = lax.broadcasted_iota(jnp.int32, (GLA_HEADS * ch, ch), 1)
    causal = (rr <= cc) if reverse else (rr >= cc)
    st_shape = (GLA_WIDTH, GLA_QK)
    head_blk = (lax.broadcasted_iota(jnp.int32, st_shape, 0) // GLA_DV
                == lax.broadcasted_iota(jnp.int32, st_shape, 1) // GLA_DK)
    quarter = GLA_DK // 4
    first_half = (lane % (2 * quarter)) < quarter

    def rope(x, cs, sn):
        partner = jnp.where(first_half, pltpu.roll(x, GLA_QK - quarter, 1), pltpu.roll(x, quarter, 1))
        return x * cs + partner * sn

    chunks = range(TM // ch)
    for c in (reversed(chunks) if reverse else chunks):
        sl = slice(c * ch, (c + 1) * ch)
        cs = cs_ref[sl, :]
        sn = sn_ref[sl, :]
        q = rope(qk_ref[0, sl, 0:GLA_QK], cs, sn)
        k = rope(qk_ref[0, sl, GLA_QK:2 * GLA_QK], cs, sn)
        b = la_ref[0, sl, :]
        d = 1
        while d < ch:
            if reverse:
                b = b + jnp.where(row < ch - d, pltpu.roll(b, ch - d, 0), 0.0)
            else:
                b = b + jnp.where(row >= d, pltpu.roll(b, d, 0), 0.0)
            d *= 2
        b_last = b[0:1, :] if reverse else b[ch - 1:ch, :]
        q_in = q * jnp.exp(b)
        k_in = (k * jnp.exp(-b)).astype(BF16)
        k_st = (k * jnp.exp(b_last - b)).astype(BF16)
        att = _dot_nt(_stack_heads(q_in, GLA_DK).astype(BF16), k_in)
        att = jnp.where(causal, att, 0.0).astype(BF16)
        vb = v_ref[0, sl, :].astype(BF16)
        o = _unstack_heads(_dot(att, vb), GLA_DV)
        st = st_ref[...]
        o_ref[0, sl, :] = o + _dot_nt(q_in.astype(BF16), st.astype(BF16))
        st_ref[...] = st * jnp.exp(b_last) + jnp.where(head_blk, _dot_tn(vb, k_st), 0.0)


def _gla_kernel(qkf, vf, laf, csf, snf, qkb, vb, lab, csb, snb, of_ref, ob_ref, st_f, st_b):
    @pl.when(pl.program_id(1) == 0)
    def _():
        st_f[...] = jnp.zeros_like(st_f)
        st_b[...] = jnp.zeros_like(st_b)

    _gla_tile(qkf, vf, laf, csf, snf, st_f, of_ref, False)
    _gla_tile(qkb, vb, lab, csb, snb, st_b, ob_ref, True)


def _rope_tables(seq, s):
    quarter = GLA_DK // 4
    t = np.arange(s)
    pos_row = np.where(t < seq, t // GRID_W, 0).astype(np.float32)
    pos_col = np.where(t < seq, t % GRID_W, 0).astype(np.float32)
    d = np.arange(GLA_QK) % GLA_DK
    inv_freq = ROPE_BASE ** (-jnp.arange(quarter, dtype=F32) / quarter)
    freq = inv_freq[d % quarter]
    pos = jnp.where((d < GLA_DK // 2)[None, :], pos_row[:, None], pos_col[:, None])
    ang = pos * freq[None, :]
    sign = np.where((d % (2 * quarter)) < quarter, -1.0, 1.0).astype(np.float32)
    return jnp.cos(ang), jnp.sin(ang) * sign[None, :]


def _gla(gg, cos, sin, n_lat_tiles):
    bsz, s, _ = gg.shape
    nt = s // TM
    assert nt == n_lat_tiles + 1
    fwd = lambda i: (i + n_lat_tiles) % nt
    bwd = lambda i: jnp.where(i == 0, n_lat_tiles, n_lat_tiles - i)
    la_f_blk = (2 * GLA_QK + 2 * GLA_WIDTH) // GLA_QK
    la_b_blk = la_f_blk + 1
    v_blk = 2 * GLA_QK // GLA_WIDTH

    def specs(order, la_blk):
        return [
            pl.BlockSpec((1, TM, 2 * GLA_QK), lambda b, i: (b, order(i), 0)),
            pl.BlockSpec((1, TM, GLA_WIDTH), lambda b, i: (b, order(i), v_blk)),
            pl.BlockSpec((1, TM, GLA_QK), lambda b, i: (b, order(i), la_blk)),
            pl.BlockSpec((TM, GLA_QK), lambda b, i: (order(i), 0)),
            pl.BlockSpec((TM, GLA_QK), lambda b, i: (order(i), 0)),
        ]

    return pl.pallas_call(
        _gla_kernel,
        grid=(bsz, nt),
        in_specs=specs(fwd, la_f_blk) + specs(bwd, la_b_blk),
        out_specs=[
            pl.BlockSpec((1, TM, GLA_WIDTH), lambda b, i: (b, fwd(i), 0)),
            pl.BlockSpec((1, TM, GLA_WIDTH), lambda b, i: (b, bwd(i), 0)),
        ],
        out_shape=[jax.ShapeDtypeStruct((bsz, s, GLA_WIDTH), F32)] * 2,
        scratch_shapes=[pltpu.VMEM((GLA_WIDTH, GLA_QK), F32)] * 2,
        compiler_params=_params("arbitrary", "arbitrary"),
        name="gla_bidir",
    )(gg, gg, gg, cos, sin, gg, gg, gg, cos, sin)


def _store_row_tiles(ref, val):
    rows, width = val.shape
    pieces = width // LANES
    for s in range(pieces):
        ref[pl.ds(s, rows, stride=pieces), :] = val[:, s * LANES:(s + 1) * LANES]


def _load_row_tiles(ref, rows, pieces):
    return jnp.concatenate([ref[pl.ds(s, rows, stride=pieces), :] for s in range(pieces)], axis=1)


def _row_index(shape):
    return lax.broadcasted_iota(jnp.int32, shape, 0).astype(F32)


def _stack_rows(rows):
    shape = (len(rows), rows[0].shape[1])
    rid = lax.broadcasted_iota(jnp.int32, shape, 0)
    out = jnp.zeros(shape, rows[0].dtype)
    for k, r in enumerate(rows):
        out = jnp.where(rid == k, r, out)
    return out


def _select_experts(scores, biased):
    n_exp, n_tok = biased.shape
    per = n_exp // N_GROUPS
    neg = -jnp.inf
    sub = _row_index((per, n_tok))
    grp_rows = []
    for g in range(N_GROUPS):
        blk = biased[g * per:(g + 1) * per, :]
        m1 = jnp.max(blk, axis=0, keepdims=True)
        i1 = jnp.min(jnp.where(blk == m1, sub, float(per)), axis=0, keepdims=True)
        m2 = jnp.max(jnp.where(sub == i1, neg, blk), axis=0, keepdims=True)
        grp_rows.append(m1 + m2)
    cur = _stack_rows(grp_rows)
    gid = _row_index(cur.shape)
    picked = jnp.zeros(cur.shape, F32)
    for _ in range(TOPK_GROUPS):
        gm = jnp.max(cur, axis=0, keepdims=True)
        gi = jnp.min(jnp.where(cur == gm, gid, float(N_GROUPS)), axis=0, keepdims=True)
        hit = gid == gi
        picked = jnp.where(hit, 1.0, picked)
        cur = jnp.where(hit, neg, cur)
    cur = jnp.concatenate(
        [jnp.where(picked[g:g + 1, :] > 0.0, biased[g * per:(g + 1) * per, :], neg) for g in range(N_GROUPS)], axis=0)
    eid = _row_index(cur.shape)
    ids, vals = [], []
    chosen = jnp.zeros(cur.shape, F32)
    for _ in range(TOP_K):
        m = jnp.max(cur, axis=0, keepdims=True)
        idx = jnp.min(jnp.where(cur == m, eid, float(n_exp)), axis=0, keepdims=True)
        hit = eid == idx
        ids.append(idx)
        vals.append(jnp.sum(jnp.where(hit, scores, 0.0), axis=0, keepdims=True))
        chosen = jnp.where(hit, 1.0, chosen)
        cur = jnp.where(hit, neg, cur)
    return ids, vals, chosen


def _outproj_kernel(x_ref, oa_ref, ob_ref, of_ref, obk_ref, r_ref, gg_ref, wo_ref, mod_ref, gf_ref, wr_ref, br_ref,
                    tri_ref, xo_ref, h_ref, e_ref, w_ref, rk_ref, hist_ref):
    o = of_ref[0] + obk_ref[0]
    head = lax.broadcasted_iota(jnp.int32, o.shape, 1) // GLA_DV
    o2 = o * o
    rs = jnp.zeros_like(o)
    for h in range(GLA_HEADS):
        ssq = jnp.sum(jnp.where(head == h, o2, 0.0), axis=-1, keepdims=True)
        rs = jnp.where(head == h, lax.rsqrt(ssq / GLA_DV + NORM_EPS), rs)
    oc = (o * rs * gg_ref[...] * _silu(r_ref[0])).astype(BF16)
    c1 = NA_WIDTH + POOL_WIDTH
    acc = _dot(oa_ref[0], wo_ref[0:NA_WIDTH]) + _dot(ob_ref[0], wo_ref[NA_WIDTH:c1]) + _dot(oc, wo_ref[c1:])
    x = x_ref[0] + mod_ref[0, 0, 2:3, :] * acc
    xo_ref[0] = x
    h = _rmsnorm(x, gf_ref[...]) * (1.0 + mod_ref[0, 0, 4:5, :]) + mod_ref[0, 0, 3:4, :]
    _store_row_tiles(h_ref, h)

    scores = _sigmoid(_dot_nt(wr_ref[...], h.astype(BF16)))
    ids, vals, chosen = _select_experts(scores, scores + br_ref[...])
    total = vals[0]
    for v in vals[1:]:
        total = total + v
    w_ref[...] = _stack_rows([v / total * ROUTED_SCALE for v in vals])
    e_ref[...] = _stack_rows(ids).astype(jnp.int32)
    chosen_b = chosen.astype(BF16)
    before = _dot(chosen_b, tri_ref[...])
    eid = _row_index(before.shape)
    rk_ref[...] = _stack_rows([jnp.sum(jnp.where(eid == idx, before, 0.0), axis=0, keepdims=True) for idx in ids])
    hist_ref[0] = _dot_nt(jnp.ones((8, chosen.shape[1]), BF16), chosen_b)


def _outproj(xa, oa, ob, o_f, o_b, gg, g_gla4, w_out, mod, g_ffn, w_router_t, b_router, n_lat_tiles, n_tiles):
    bsz, _, d = xa.shape
    n_exp = w_router_t.shape[0]
    r_blk = (2 * GLA_QK + GLA_WIDTH) // GLA_WIDTH
    tile = lambda w: pl.BlockSpec((1, TM, w), lambda b, i: (b, i, 0))
    const = lambda b, i: (0, 0)
    per_tok = pl.BlockSpec((TOP_K, TM), lambda b, i: (0, b * n_tiles + i))
    t_tok = bsz * n_tiles * TM
    tri = (jnp.arange(TM)[:, None] < jnp.arange(TM)[None, :]).astype(BF16)
    return pl.pallas_call(
        _outproj_kernel,
        grid=(bsz, n_tiles),
        in_specs=[
            tile(d), tile(NA_WIDTH), tile(POOL_WIDTH), tile(GLA_WIDTH), tile(GLA_WIDTH),
            pl.BlockSpec((1, TM, GLA_WIDTH), lambda b, i: (b, i, r_blk)),
            pl.BlockSpec((1, GLA_WIDTH), const),
            pl.BlockSpec(w_out.shape, const),
            pl.BlockSpec((1, 1, 8, d), lambda b, i: (b, jnp.where(i >= n_lat_tiles, 0, 1), 0, 0)),
            pl.BlockSpec((1, d), const),
            pl.BlockSpec(w_router_t.shape, const),
            pl.BlockSpec((n_exp, 1), const),
            pl.BlockSpec((TM, TM), const),
        ],
        out_specs=[tile(d), pl.BlockSpec((TM * d // LANES, LANES), lambda b, i: (b * n_tiles + i, 0)),
                   per_tok, per_tok, per_tok,
                   pl.BlockSpec((1, 8, n_exp), lambda b, i: (b * n_tiles + i, 0, 0))],
        out_shape=[
            jax.ShapeDtypeStruct((bsz, n_tiles * TM, d), F32),
            jax.ShapeDtypeStruct((t_tok * d // LANES, LANES), F32),
            jax.ShapeDtypeStruct((TOP_K, t_tok), jnp.int32),
            jax.ShapeDtypeStruct((TOP_K, t_tok), F32),
            jax.ShapeDtypeStruct((TOP_K, t_tok), F32),
            jax.ShapeDtypeStruct((bsz * n_tiles, 8, n_exp), F32),
        ],
        compiler_params=_params("parallel", "parallel"),
        name="outproj_router",
    )(xa, oa, ob, o_f, o_b, gg, g_gla4, w_out, mod, g_ffn, w_router_t, b_router, tri)


def _slot_layout(hist):
    n_tiles, n_exp = hist.shape
    counts = hist.sum(0)
    tile_base = jnp.cumsum(hist, axis=0) - hist
    padded = (counts + MOE_BLOCK - 1) // MOE_BLOCK * MOE_BLOCK
    pad_end = jnp.cumsum(padded)
    pad_start = pad_end - padded
    base = (pad_start[None, :] + tile_base).astype(F32).reshape(n_tiles, n_exp, 1)
    n_blocks = -(-(n_tiles * TM * TOP_K) // MOE_BLOCK) + n_exp
    n_used = (pad_end[-1] // MOE_BLOCK).astype(jnp.int32)
    fill = jnp.concatenate([jnp.maximum(pad_end // MOE_BLOCK - 1, 0).astype(jnp.int32),
                            jnp.minimum(n_used + jnp.arange(n_exp, dtype=jnp.int32), n_blocks - 1)])
    first_blk = (pad_start // MOE_BLOCK).astype(jnp.int32)
    blk_count = (padded // MOE_BLOCK).astype(jnp.int32)
    return base, first_blk, blk_count, fill, n_used.reshape(1), n_blocks


def _pos_kernel(e_ref, rk_ref, base_ref, pos_ref, *, pieces):
    eid = lax.broadcasted_iota(jnp.int32, (base_ref.shape[1], e_ref.shape[1]), 0)
    base = base_ref[0]
    rows = [jnp.sum(jnp.where(eid == e_ref[k:k + 1, :], base, 0.0), axis=0, keepdims=True) for k in range(TOP_K)]
    pos_ref[0] = ((_stack_rows(rows) + rk_ref[...]) * float(pieces)).astype(jnp.int32)


def _slot_positions(e_t, rank_t, base, pieces):
    n_tiles, n_exp, _ = base.shape
    per_tok = pl.BlockSpec((TOP_K, TM), lambda i: (0, i))
    return pl.pallas_call(
        functools.partial(_pos_kernel, pieces=pieces),
        grid=(n_tiles,),
        in_specs=[per_tok, per_tok, pl.BlockSpec((1, n_exp, 1), lambda i: (i, 0, 0))],
        out_specs=pl.BlockSpec((1, TOP_K, TM), lambda i: (i, 0, 0)),
        out_shape=jax.ShapeDtypeStruct((n_tiles, TOP_K, TM), jnp.int32),
        compiler_params=_params("parallel"),
        name="moe_positions",
    )(e_t, rank_t, base).reshape(-1)


def _dispatch_kernel(fill_ref, pos_ref, h_ref, xs_hbm, zbuf, sem, zsem):
    @pl.when(pl.program_id(0) == 0)
    def _():
        zbuf[...] = jnp.zeros_like(zbuf)
        n_fill = fill_ref.shape[0]

        def is_new(j):
            return (j == 0) | (fill_ref[j] != fill_ref[jnp.maximum(j - 1, 0)])

        def start(j, c):
            @pl.when(is_new(j))
            def _():
                row0 = pl.multiple_of(fill_ref[j] * zbuf.shape[0], zbuf.shape[0])
                pltpu.make_async_copy(zbuf, xs_hbm.at[pl.ds(row0, zbuf.shape[0]), :], zsem).start()
            return c
        lax.fori_loop(0, n_fill, start, 0)

        def wait(j, c):
            @pl.when(is_new(j))
            def _():
                pltpu.make_async_copy(zbuf, xs_hbm.at[pl.ds(0, zbuf.shape[0]), :], zsem).wait()
            return c
        lax.fori_loop(0, n_fill, wait, 0)

    pieces = h_ref.shape[0] // TM

    def body(t, c):
        src = h_ref.at[pl.ds(pl.multiple_of(t * pieces, pieces), pieces), :]
        for k in range(TOP_K):
            dst = xs_hbm.at[pl.ds(pl.multiple_of(pos_ref[k * TM + t], pieces), pieces), :]
            pltpu.make_async_copy(src, dst, sem).start(priority=k % 2)
        return c
    lax.fori_loop(0, TM, body, 0, unroll=2)
    for _ in range(TOP_K):
        pltpu.make_async_copy(h_ref, xs_hbm.at[pl.ds(0, h_ref.shape[0]), :], sem).wait()


def _dispatch(h2t, pos, fill, n_slots, pieces):
    n_tiles = h2t.shape[0] // (TM * pieces)
    grid_spec = pltpu.PrefetchScalarGridSpec(
        num_scalar_prefetch=1,
        grid=(n_tiles,),
        in_specs=[
            pl.BlockSpec((TOP_K * TM,), lambda i, fill: (i,), memory_space=pltpu.SMEM),
            pl.BlockSpec((TM * pieces, LANES), lambda i, fill: (i, 0)),
        ],
        out_specs=pl.BlockSpec(memory_space=pl.ANY),
        scratch_shapes=[pltpu.VMEM((MOE_BLOCK * pieces, LANES), F32), pltpu.SemaphoreType.DMA,
                        pltpu.SemaphoreType.DMA],
    )
    return pl.pallas_call(
        _dispatch_kernel,
        grid_spec=grid_spec,
        out_shape=jax.ShapeDtypeStruct((n_slots * pieces, LANES), F32),
        compiler_params=_params("arbitrary"),
        name="moe_dispatch",
    )(fill, pos, h2t)


def _expert_kernel(first_ref, cnt_ref, nu_ref, wi_ref, wd_ref, xs_hbm, ys_hbm, xbuf, ybuf, wi_bf, wd_bf, xsem, ysem,
                   *, n_blocks):
    e = pl.program_id(0)
    n_used = nu_ref[0]
    rows = xbuf.shape[1]
    pieces = rows // MOE_BLOCK
    d_exp = wd_ref.shape[2]

    def x_copy(g, s):
        return pltpu.make_async_copy(xs_hbm.at[pl.ds(pl.multiple_of(g * rows, rows), rows), :], xbuf.at[s], xsem.at[s])

    def y_copy(g, s):
        return pltpu.make_async_copy(ybuf.at[s], ys_hbm.at[pl.ds(pl.multiple_of(g * rows, rows), rows), :], ysem.at[s])

    @pl.when(e == 0)
    def _():
        x_copy(0, 0).start()

    @pl.when(cnt_ref[e] > 0)
    def _():
        wi_bf[...] = wi_ref[0, 0].astype(BF16)
        wd_bf[...] = wd_ref[0, 0].astype(BF16)

    def block(j, c):
        g = first_ref[e] + j
        s = g % 2
        x_copy(g, s).wait()

        @pl.when(g + 1 < n_used)
        def _():
            x_copy(g + 1, 1 - s).start()

        @pl.when(g >= 2)
        def _():
            y_copy(g - 2, s).wait()

        hh = _dot(_load_row_tiles(xbuf.at[s], MOE_BLOCK, pieces).astype(BF16), wi_bf[...])
        a = (_silu(hh[:, :d_exp]) * hh[:, d_exp:]).astype(BF16)
        _store_row_tiles(ybuf.at[s], _dot(a, wd_bf[...]))
        y_copy(g, s).start()
        return c
    lax.fori_loop(0, cnt_ref[e], block, 0)

    @pl.when(e == pl.num_programs(0) - 1)
    def _():
        y_copy(n_used - 1, (n_used - 1) % 2).wait()

        @pl.when(n_used >= 2)
        def _():
            y_copy(n_used - 2, n_used % 2).wait()

        ybuf[0] = jnp.zeros(ybuf.shape[1:], ybuf.dtype)

        def start(g, c):
            y_copy(g, 0).start()
            return c
        lax.fori_loop(n_used, n_blocks, start, 0)

        def wait(g, c):
            y_copy(g, 0).wait()
            return c
        lax.fori_loop(n_used, n_blocks, wait, 0)


def _moe_experts(xs, first_blk, blk_count, n_used, n_blocks, w_e_in, w_e_down, layer):
    _, n_exp, d, f2 = w_e_in.shape
    rows = xs.shape[0] // n_blocks
    grid_spec = pltpu.PrefetchScalarGridSpec(
        num_scalar_prefetch=3,
        grid=(n_exp,),
        in_specs=[
            pl.BlockSpec((1, 1, d, f2), lambda e, first, cnt, nu: (layer, e, 0, 0)),
            pl.BlockSpec((1, 1, f2 // 2, d), lambda e, first, cnt, nu: (layer, e, 0, 0)),
            pl.BlockSpec(memory_space=pl.ANY),
        ],
        out_specs=pl.BlockSpec(memory_space=pl.ANY),
        scratch_shapes=[
            pltpu.VMEM((2, rows, LANES), F32),
            pltpu.VMEM((2, rows, LANES), F32),
            pltpu.VMEM((d, f2), BF16),
            pltpu.VMEM((f2 // 2, d), BF16),
            pltpu.SemaphoreType.DMA((2,)),
            pltpu.SemaphoreType.DMA((2,)),
        ],
    )
    return pl.pallas_call(
        functools.partial(_expert_kernel, n_blocks=n_blocks),
        grid_spec=grid_spec,
        out_shape=jax.ShapeDtypeStruct(xs.shape, F32),
        compiler_params=_params("arbitrary"),
        name="moe_experts",
    )(first_blk, blk_count, n_used, w_e_in, w_e_down, xs)


def _combine_kernel(pos0_ref, posn_ref, x_ref, h_ref, w_ref, wsi_ref, wsd_ref, mod_ref, gfin_ref, ys_hbm, o_ref,
                    ybuf, sem, *, final, n_steps):
    step = pl.program_id(0) * pl.num_programs(1) + pl.program_id(1)
    slot = step % 2
    d_exp = wsd_ref.shape[0]
    pieces = h_ref.shape[0] // TMC

    def gather(pos_ref, st, s):
        off = (st % (TM // TMC)) * TMC
        def body(t, c):
            for k in range(TOP_K):
                src = ys_hbm.at[pl.ds(pl.multiple_of(pos_ref[k * TM + off + t], pieces), pieces), :]
                dst = ybuf.at[s, k, pl.ds(pl.multiple_of(t * pieces, pieces), pieces), :]
                pltpu.make_async_copy(src, dst, sem.at[s]).start(priority=k % 2)
            return c
        lax.fori_loop(0, TMC, body, 0, unroll=2)

    @pl.when(step == 0)
    def _():
        gather(pos0_ref, 0, 0)

    @pl.when(step + 1 < n_steps)
    def _():
        gather(posn_ref, step + 1, 1 - slot)

    hh = _dot(_load_row_tiles(h_ref, TMC, pieces).astype(BF16), wsi_ref[...])
    a = (_silu(hh[:, :d_exp]) * hh[:, d_exp:]).astype(BF16)
    y = _dot(a, wsd_ref[...])
    for k in range(TOP_K):
        pltpu.make_async_copy(ys_hbm.at[pl.ds(0, TMC * pieces), :], ybuf.at[slot, k], sem.at[slot]).wait()
    w = w_ref[...]
    for k in range(TOP_K):
        y = y + _load_row_tiles(ybuf.at[slot, k], TMC, pieces) * w[:, k:k + 1]
    x = x_ref[0] + mod_ref[0, 0, 5:6, :] * y
    if final:
        x = _rmsnorm(x, gfin_ref[...])
    o_ref[0] = x


def _combine(x_mid, h2t, ys, pos, w_tok, w_sh_in, w_sh_down, mod, g_final, n_lat_tiles, final):
    bsz, s, d = x_mid.shape
    nt = s // TMC
    n_steps = bsz * nt
    halves = TM // TMC
    lat_steps = n_lat_tiles * halves
    pieces = d // LANES
    const = lambda b, i: (0, 0)
    tile = pl.BlockSpec((1, TMC, d), lambda b, i: (b, i, 0))
    smem_blk = lambda f: pl.BlockSpec((TOP_K * TM,), f, memory_space=pltpu.SMEM)
    return pl.pallas_call(
        functools.partial(_combine_kernel, final=final, n_steps=n_steps),
        grid=(bsz, nt),
        in_specs=[
            smem_blk(lambda b, i: (0,)),
            smem_blk(lambda b, i: (jnp.minimum(b * nt + i + 1, n_steps - 1) // halves,)),
            tile,
            pl.BlockSpec((TMC * pieces, LANES), lambda b, i: (b * nt + i, 0)),
            pl.BlockSpec((TMC, TOP_K), lambda b, i: (b * nt + i, 0)),
            pl.BlockSpec(w_sh_in.shape, const),
            pl.BlockSpec(w_sh_down.shape, const),
            pl.BlockSpec((1, 1, 8, d), lambda b, i: (b, jnp.where(i >= lat_steps, 0, 1), 0, 0)),
            pl.BlockSpec((1, d), const),
            pl.BlockSpec(memory_space=pl.ANY),
        ],
        out_specs=tile,
        out_shape=jax.ShapeDtypeStruct((bsz, s, d), F32),
        scratch_shapes=[pltpu.VMEM((2, TOP_K, TMC * pieces, LANES), F32), pltpu.SemaphoreType.DMA((2,))],
        compiler_params=_params("arbitrary", "arbitrary"),
        name="moe_combine",
    )(pos, pos, x_mid, h2t, w_tok, w_sh_in, w_sh_down, mod, g_final, ys)


def _block_diag(w):
    g, a, b = w.shape
    out = jnp.zeros((g * a, g * b), w.dtype)
    for j in range(g):
        out = out.at[j * a:(j + 1) * a, j * b:(j + 1) * b].set(w[j])
    return out


def kernel(x, c, ctx, c_ctx, w_mod, b_mod, g_mix, w_in, rpb, w_pool, pool_scale, w_gate_f, b_gate_f, w_gate_b, b_gate_b, g_gla, w_out, g_ffn, w_router, b_router, w_e_in, w_e_down, w_sh_in, w_sh_down, g_final):
    bsz, seq, d = x.shape
    n_ctx = ctx.shape[1]
    depth = w_mod.shape[0]
    n_exp = w_router.shape[2]
    assert n_ctx == TM and seq % TM == 0 and seq % GRID_W == 0
    n_lat = seq // TM
    s = seq + n_ctx

    n_rows = -(-(bsz + 1) // 8) * 8
    cc = jnp.zeros((n_rows, d), F32).at[:bsz].set(c).at[bsz].set(c_ctx)
    mod_all = _modulation(cc, w_mod, b_mod).reshape(depth, n_rows, 6, d)
    cos, sin = _rope_tables(seq, s)

    xa = jnp.concatenate([x, ctx], axis=1)
    for layer in range(depth):
        last = layer == depth - 1
        n_tiles = n_lat if last else n_lat + 1
        m = mod_all[layer]
        mod = jnp.stack([jnp.broadcast_to(m[bsz], (bsz, 6, d)), m[:bsz]], axis=1)
        mod = jnp.pad(mod, ((0, 0), (0, 0), (0, 2), (0, 0)))

        wl = w_in[layer]
        w_main = wl[:, :MAIN_W].astype(BF16)
        w_low = jnp.pad(wl[:, MAIN_W:], ((0, 0), (0, LANES - 2 * GLA_GATE_RANK))).astype(BF16)
        w2 = jnp.zeros((LANES, 2 * GLA_QK), F32)
        w2 = w2.at[:GLA_GATE_RANK, :GLA_QK].set(w_gate_f[layer])
        w2 = w2.at[GLA_GATE_RANK:2 * GLA_GATE_RANK, GLA_QK:].set(w_gate_b[layer]).astype(BF16)
        b2 = jnp.concatenate([b_gate_f[layer], b_gate_b[layer]])[None, :]
        qkv, u, gg = _inproj(xa, mod, g_mix[layer][None, :], w_main, w_low, w2, b2, n_lat)

        oa = _neighborhood_attention(qkv, _na_bias_table(rpb[layer]), n_lat, not last)
        ob = _multiscale_pool(u, _block_diag(w_pool[layer]).astype(BF16), pool_scale[layer][None, :], n_lat, n_tiles)
        o_f, o_b = _gla(gg, cos, sin, n_lat)

        x_mid, h2t, e_t, w_t, rank_t, hist = _outproj(
            xa, oa, ob, o_f, o_b, gg, jnp.tile(g_gla[layer], GLA_HEADS)[None, :], w_out[layer].astype(BF16), mod,
            g_ffn[layer][None, :], w_router[layer].T.astype(BF16), b_router[layer].reshape(n_exp, 1), n_lat, n_tiles)
        pieces = d // LANES
        base, first_blk, blk_count, fill, n_used, n_blocks = _slot_layout(hist[:, 0, :].astype(jnp.int32))
        pos = _slot_positions(e_t, rank_t, base, pieces)
        xs = _dispatch(h2t, pos, fill, n_blocks * MOE_BLOCK, pieces)
        ys = _moe_experts(xs, first_blk, blk_count, n_used, n_blocks, w_e_in, w_e_down, layer)
        xa = _combine(x_mid, h2t, ys, pos, w_t.T, w_sh_in[layer].astype(BF16), w_sh_down[layer].astype(BF16), mod,
                      g_final[None, :], n_lat, last)
    return xa
```

```python
import functools

import jax
import jax.numpy as jnp
import numpy as np
from jax import lax
from jax.experimental import pallas as pl
from jax.experimental.pallas import tpu as pltpu

GRID_W = 64
NORM_EPS = 1e-6
NA_HEADS = 8
NA_HEAD_DIM = 64
NA_WIDTH = NA_HEADS * NA_HEAD_DIM
NA_WIN_ROWS = 8
NA_WIN_COLS = 16
POOL_WINDOWS = (2, 4, 8, 16)
POOL_GROUP_DIM = 64
POOL_WIDTH = len(POOL_WINDOWS) * POOL_GROUP_DIM
POOL_REACH = max(POOL_WINDOWS) // 2
assert POOL_WINDOWS == tuple(2 ** (g + 1) for g in range(len(POOL_WINDOWS)))
GLA_HEADS = 4
GLA_DK = 32
GLA_DV = 64
GLA_QK = GLA_HEADS * GLA_DK
GLA_WIDTH = GLA_HEADS * GLA_DV
GLA_GATE_RANK = 16
GLA_TAU = 16.0
GLA_CHUNK = 64
ROPE_BASE = 10000.0
N_EXPERTS = 256
TOP_K = 8
N_GROUPS = 8
TOPK_GROUPS = 4
ROUTED_SCALE = 2.5

TM = 256
TMC = 128
MOE_BLOCK = 256
LANES = 128
MXU_DIM = 256
MASK_VALUE = -1e30
VMEM_LIMIT = 48 * 1024 * 1024

QKV_W = 3 * NA_WIDTH
G_W = 2 * GLA_QK + 2 * GLA_WIDTH + 2 * GLA_QK
MAIN_W = QKV_W + POOL_WIDTH + 2 * GLA_QK + 2 * GLA_WIDTH

BF16 = jnp.bfloat16
F32 = jnp.float32


def _params(*sem):
    return pltpu.CompilerParams(dimension_semantics=sem, vmem_limit_bytes=VMEM_LIMIT)


def _sigmoid(x):
    return 1.0 / (1.0 + jnp.exp(-x))


def _silu(x):
    return x * _sigmoid(x)


def _rmsnorm(x, g):
    return x * lax.rsqrt(jnp.mean(x * x, axis=-1, keepdims=True) + NORM_EPS) * g


def _dot(a, b):
    return jnp.dot(a, b, preferred_element_type=F32)


def _dot_nt(a, b):
    return lax.dot_general(a, b, (((1,), (1,)), ((), ())), preferred_element_type=F32)


def _dot_tn(a, b):
    return lax.dot_general(a, b, (((0,), (0,)), ((), ())), preferred_element_type=F32)


def _mod_kernel(c_ref, w_ref, b_ref, o_ref):
    a = _silu(c_ref[...]).astype(BF16)
    o_ref[0] = _dot(a, w_ref[0].astype(BF16)) + b_ref[0]


def _modulation(cc, w_mod, b_mod):
    depth, d, n = w_mod.shape
    r = cc.shape[0]
    tn = 1024
    return pl.pallas_call(
        _mod_kernel,
        grid=(depth, n // tn),
        in_specs=[
            pl.BlockSpec((r, d), lambda l, j: (0, 0)),
            pl.BlockSpec((1, d, tn), lambda l, j: (l, 0, j)),
            pl.BlockSpec((1, 1, tn), lambda l, j: (l, 0, j)),
        ],
        out_specs=pl.BlockSpec((1, r, tn), lambda l, j: (l, 0, j)),
        out_shape=jax.ShapeDtypeStruct((depth, r, n), F32),
        compiler_params=_params("parallel", "parallel"),
        name="modulation",
    )(cc, w_mod, b_mod.reshape(depth, 1, n))


def _inproj_kernel(x_ref, mod_ref, g_ref, wm_ref, wl_ref, w2_ref, b2_ref, qkv_ref, u_ref, gg_ref):
    h = _rmsnorm(x_ref[0], g_ref[...]) * (1.0 + mod_ref[0, 0, 1:2, :]) + mod_ref[0, 0, 0:1, :]
    hb = h.astype(BF16)
    q = _dot(hb, wm_ref[:, 0:NA_WIDTH]) * (NA_HEAD_DIM ** -0.5)
    qkv_ref[0, :, 0:NA_WIDTH] = q.astype(BF16)
    qkv_ref[0, :, NA_WIDTH:2 * NA_WIDTH] = _dot(hb, wm_ref[:, NA_WIDTH:2 * NA_WIDTH]).astype(BF16)
    qkv_ref[0, :, 2 * NA_WIDTH:QKV_W] = _dot(hb, wm_ref[:, 2 * NA_WIDTH:QKV_W]).astype(BF16)
    u_ref[0] = _dot(hb, wm_ref[:, QKV_W:QKV_W + POOL_WIDTH])
    c0 = QKV_W + POOL_WIDTH
    gg_ref[0, :, 0:GLA_QK] = _dot(hb, wm_ref[:, c0:c0 + GLA_QK]) * (GLA_DK ** -0.5)
    gg_ref[0, :, GLA_QK:2 * GLA_QK + 2 * GLA_WIDTH] = _dot(hb, wm_ref[:, c0 + GLA_QK:MAIN_W])
    a_low = _dot(hb, wl_ref[...]).astype(BF16)
    lg = _dot(a_low, w2_ref[...]) + b2_ref[...]
    log_sig = jnp.minimum(lg, 0.0) - jnp.log1p(jnp.exp(-jnp.abs(lg)))
    gg_ref[0, :, 2 * GLA_QK + 2 * GLA_WIDTH:G_W] = log_sig / GLA_TAU


def _inproj(xa, mod, g, w_main, w_low, w2, b2, n_lat_tiles):
    bsz, s, d = xa.shape
    nt = s // TM
    const = lambda b, i: (0, 0)
    return pl.pallas_call(
        _inproj_kernel,
        grid=(bsz, nt),
        in_specs=[
            pl.BlockSpec((1, TM, d), lambda b, i: (b, i, 0)),
            pl.BlockSpec((1, 1, 8, d), lambda b, i: (b, jnp.where(i >= n_lat_tiles, 0, 1), 0, 0)),
            pl.BlockSpec((1, d), const),
            pl.BlockSpec(w_main.shape, const),
            pl.BlockSpec(w_low.shape, const),
            pl.BlockSpec(w2.shape, const),
            pl.BlockSpec(b2.shape, const),
        ],
        out_specs=[
            pl.BlockSpec((1, TM, QKV_W), lambda b, i: (b, i, 0)),
            pl.BlockSpec((1, TM, POOL_WIDTH), lambda b, i: (b, i, 0)),
            pl.BlockSpec((1, TM, G_W), lambda b, i: (b, i, 0)),
        ],
        out_shape=[
            jax.ShapeDtypeStruct((bsz, s, QKV_W), BF16),
            jax.ShapeDtypeStruct((bsz, s, POOL_WIDTH), F32),
            jax.ShapeDtypeStruct((bsz, s, G_W), F32),
        ],
        compiler_params=_params("parallel", "parallel"),
        name="inproj",
    )(xa, mod, g, w_main, w_low, w2, b2)


HEADS_PER_GROUP = MXU_DIM // NA_HEAD_DIM
NA_GROUPS = NA_HEADS // HEADS_PER_GROUP
NA_LOCAL_KEYS = NA_WIN_ROWS * GRID_W


def _stack_heads(x, width):
    lane = lax.broadcasted_iota(jnp.int32, x.shape, 1) // width
    n_heads = x.shape[1] // width
    return jnp.concatenate([jnp.where(lane == h, x, jnp.zeros_like(x)) for h in range(n_heads)], axis=0)


def _unstack_heads(o, width):
    n_heads = o.shape[1] // width
    r = o.shape[0] // n_heads
    lane = lax.broadcasted_iota(jnp.int32, (r, o.shape[1]), 1) // width
    acc = jnp.zeros((r, o.shape[1]), o.dtype)
    for h in range(n_heads):
        acc = jnp.where(lane == h, o[h * r:(h + 1) * r, :], acc)
    return acc


def _na_kernel(q_ref, k_ref, v_ref, kc_ref, vc_ref, bias_ref, o_ref, *, n_rows):
    r = pl.program_id(1)

    def attend(g, local):
        cols = slice(g * MXU_DIM, (g + 1) * MXU_DIM)
        qs = _stack_heads(q_ref[0, :, cols], NA_HEAD_DIM)
        kc = kc_ref[0, :, cols]
        vc = vc_ref[0, :, cols]
        s_ctx = _dot_nt(qs, kc)
        m = jnp.max(s_ctx, axis=-1, keepdims=True)
        if local:
            start = pl.multiple_of(jnp.clip(r - NA_WIN_ROWS // 2, 0, n_rows - NA_WIN_ROWS) * GRID_W, GRID_W)
            kw = k_ref[0, pl.ds(start, NA_LOCAL_KEYS), cols]
            vw = v_ref[0, pl.ds(start, NA_LOCAL_KEYS), cols]
            s_loc = _dot_nt(qs, kw) + bias_ref[0, g]
            m = jnp.maximum(m, jnp.max(s_loc, axis=-1, keepdims=True))
            p_loc = jnp.exp(s_loc - m)
        p_ctx = jnp.exp(s_ctx - m)
        den = jnp.sum(p_ctx, axis=-1, keepdims=True)
        o = _dot(p_ctx.astype(BF16), vc)
        if local:
            den = den + jnp.sum(p_loc, axis=-1, keepdims=True)
            o = o + _dot(p_loc.astype(BF16), vw)
        o_ref[0, :, cols] = _unstack_heads(o / den, NA_HEAD_DIM).astype(o_ref.dtype)

    @pl.when(r < n_rows)
    def _():
        for g in range(NA_GROUPS):
            attend(g, True)

    @pl.when(r >= n_rows)
    def _():
        for g in range(NA_GROUPS):
            attend(g, False)


def _na_bias_table(rpb):
    o = jnp.arange(NA_WIN_ROWS)
    dr = o[None, :] - o[:, None] + NA_WIN_ROWS - 1
    col = jnp.arange(GRID_W)
    dc = jnp.clip(col[None, :] - col[:, None], -(NA_WIN_COLS - 1), NA_WIN_COLS - 1) + NA_WIN_COLS - 1
    col_start = jnp.clip(col - NA_WIN_COLS // 2, 0, GRID_W - NA_WIN_COLS)
    col_mask = (col[None, :] >= col_start[:, None]) & (col[None, :] < col_start[:, None] + NA_WIN_COLS)
    t = rpb[:, dr][..., dc].astype(F32)
    t = jnp.where(col_mask[None, None, None], t, MASK_VALUE)
    t = t.transpose(1, 0, 3, 2, 4)
    return t.reshape(NA_WIN_ROWS, NA_GROUPS, HEADS_PER_GROUP * GRID_W, NA_LOCAL_KEYS)


def _neighborhood_attention(qkv, bias_tab, n_lat_tiles, with_ctx_queries):
    bsz, s, _ = qkv.shape
    seq = n_lat_tiles * TM
    n_rows = seq // GRID_W
    assert n_rows >= NA_WIN_ROWS
    n_ctx_rows = (s - seq) // GRID_W
    nq = n_rows + (n_ctx_rows if with_ctx_queries else 0)
    ctx_blk = seq // (s - seq)

    def bias_idx(b, r):
        rr = jnp.minimum(r, n_rows - 1)
        return (rr - jnp.clip(rr - NA_WIN_ROWS // 2, 0, n_rows - NA_WIN_ROWS), 0, 0, 0)

    return pl.pallas_call(
        functools.partial(_na_kernel, n_rows=n_rows),
        grid=(bsz, nq),
        in_specs=[
            pl.BlockSpec((1, GRID_W, NA_WIDTH), lambda b, r: (b, r, 0)),
            pl.BlockSpec((1, seq, NA_WIDTH), lambda b, r: (b, 0, 1)),
            pl.BlockSpec((1, seq, NA_WIDTH), lambda b, r: (b, 0, 2)),
            pl.BlockSpec((1, s - seq, NA_WIDTH), lambda b, r: (b, ctx_blk, 1)),
            pl.BlockSpec((1, s - seq, NA_WIDTH), lambda b, r: (b, ctx_blk, 2)),
            pl.BlockSpec((1,) + bias_tab.shape[1:], bias_idx),
        ],
        out_specs=pl.BlockSpec((1, GRID_W, NA_WIDTH), lambda b, r: (b, r, 0)),
        out_shape=jax.ShapeDtypeStruct((bsz, nq * GRID_W, NA_WIDTH), BF16),
        compiler_params=_params("parallel", "arbitrary"),
        name="neighborhood_attention",
    )(qkv, qkv, qkv, qkv, qkv, bias_tab)


def _pool_kernel(up_ref, u_ref, un_ref, w_ref, sc_ref, o_ref, buf, s0, s1, s2, *, n_lat_tiles, seq, ctx_len):
    i = pl.program_id(1)
    hw = POOL_REACH
    n = TM + 2 * hw
    is_ctx = i >= n_lat_tiles
    base = jnp.where(is_ctx, 0, i * TM)
    lseq = jnp.where(is_ctx, ctx_len, seq)
    buf[0:hw] = up_ref[0]
    buf[hw:hw + TM] = u_ref[0]
    buf[hw + TM:n] = un_ref[0]
    p = base - hw + lax.broadcasted_iota(jnp.int32, (n, POOL_WIDTH), 0)
    buf[...] = jnp.where((p >= 0) & (p < lseq), buf[...], 0.0)
    s0[0:n - 1] = buf[0:n - 1] + buf[1:n]
    s1[0:n - 3] = s0[0:n - 3] + s0[2:n - 1]
    s2[0:n - 7] = s1[0:n - 7] + s1[4:n - 3]
    shape = (TM, POOL_WIDTH)
    t = base + lax.broadcasted_iota(jnp.int32, shape, 0)
    grp = lax.broadcasted_iota(jnp.int32, shape, 1) // POOL_GROUP_DIM
    acc = jnp.where(grp == 0, s0[hw - 1:hw - 1 + TM],
                    jnp.where(grp == 1, s1[hw - 2:hw - 2 + TM],
                              jnp.where(grp == 2, s2[hw - 4:hw - 4 + TM], s2[0:TM] + s2[hw:hw + TM])))
    win = jnp.left_shift(2, grp)
    back = win // 2
    fwd = win - back - 1
    lo = jnp.clip(t - back, 0, lseq - 1)
    hi = jnp.clip(t + fwd, 0, lseq - 1)
    mean = acc / (hi - lo + 1).astype(F32)
    diff = (mean - u_ref[0]).astype(BF16)
    o_ref[0] = (_dot(diff, w_ref[...]) * sc_ref[...]).astype(o_ref.dtype)


def _multiscale_pool(u, w_bd, scale, n_lat_tiles, n_tiles):
    bsz, s, c = u.shape
    hw = POOL_REACH
    per = TM // hw
    last = s // hw - 1
    return pl.pallas_call(
        functools.partial(_pool_kernel, n_lat_tiles=n_lat_tiles, seq=n_lat_tiles * TM, ctx_len=s - n_lat_tiles * TM),
        grid=(bsz, n_tiles),
        in_specs=[
            pl.BlockSpec((1, hw, c), lambda b, i: (b, jnp.maximum(i * per - 1, 0), 0)),
            pl.BlockSpec((1, TM, c), lambda b, i: (b, i, 0)),
            pl.BlockSpec((1, hw, c), lambda b, i: (b, jnp.minimum((i + 1) * per, last), 0)),
            pl.BlockSpec((c, c), lambda b, i: (0, 0)),
            pl.BlockSpec((1, c), lambda b, i: (0, 0)),
        ],
        out_specs=pl.BlockSpec((1, TM, c), lambda b, i: (b, i, 0)),
        out_shape=jax.ShapeDtypeStruct((bsz, n_tiles * TM, c), BF16),
        scratch_shapes=[pltpu.VMEM((TM + 2 * hw, c), F32)] * 4,
        compiler_params=_params("parallel", "parallel"),
        name="multiscale_pool",
    )(u, u, u, w_bd, scale)


def _gla_tile(qk_ref, v_ref, la_ref, cs_ref, sn_ref, st_ref, o_ref, reverse):
    ch = GLA_CHUNK
    lane = lax.broadcasted_iota(jnp.int32, (ch, GLA_QK), 1)
    row = lax.broadcasted_iota(jnp.int32, (ch, GLA_QK), 0)
    rr = lax.broadcasted_iota(jnp.int32, (GLA_HEADS * ch, ch), 0) % ch
    cc = lax.broadcasted_iota(jnp.int32, (GLA_HEADS * ch, ch), 1)
    causal = (rr <= cc) if reverse else (rr >= cc)
    st_shape = (GLA_WIDTH, GLA_QK)
    head_blk = (lax.broadcasted_iota(jnp.int32, st_shape, 0) // GLA_DV
                == lax.broadcasted_iota(jnp.int32, st_shape, 1) // GLA_DK)
    quarter = GLA_DK // 4
    first_half = (lane % (2 * quarter)) < quarter

    def rope(x, cs, sn):
        partner = jnp.where(first_half, pltpu.roll(x, GLA_QK - quarter, 1), pltpu.roll(x, quarter, 1))
        return x * cs + partner * sn

    chunks = range(TM // ch)
    for c in (reversed(chunks) if reverse else chunks):
        sl = slice(c * ch, (c + 1) * ch)
        cs = cs_ref[sl, :]
        sn = sn_ref[sl, :]
        q = rope(qk_ref[0, sl, 0:GLA_QK], cs, sn)
        k = rope(qk_ref[0, sl, GLA_QK:2 * GLA_QK], cs, sn)
        b = la_ref[0, sl, :]
        d = 1
        while d < ch:
            if reverse:
                b = b + jnp.where(row < ch - d, pltpu.roll(b, ch - d, 0), 0.0)
            else:
                b = b + jnp.where(row >= d, pltpu.roll(b, d, 0), 0.0)
            d *= 2
        b_last = b[0:1, :] if reverse else b[ch - 1:ch, :]
        q_in = q * jnp.exp(b)
        k_in = (k * jnp.exp(-b)).astype(BF16)
        k_st = (k * jnp.exp(b_last - b)).astype(BF16)
        att = _dot_nt(_stack_heads(q_in, GLA_DK).astype(BF16), k_in)
        att = jnp.where(causal, att, 0.0).astype(BF16)
        vb = v_ref[0, sl, :].astype(BF16)
        o = _unstack_heads(_dot(att, vb), GLA_DV)
        st = st_ref[...]
        o_ref[0, sl, :] = o + _dot_nt(q_in.astype(BF16), st.astype(BF16))
        st_ref[...] = st * jnp.exp(b_last) + jnp.where(head_blk, _dot_tn(vb, k_st), 0.0)


def _gla_kernel(qkf, vf, laf, csf, snf, qkb, vb, lab, csb, snb, of_ref, ob_ref, st_f, st_b):
    @pl.when(pl.program_id(1) == 0)
    def _():
        st_f[...] = jnp.zeros_like(st_f)
        st_b[...] = jnp.zeros_like(st_b)

    _gla_tile(qkf, vf, laf, csf, snf, st_f, of_ref, False)
    _gla_tile(qkb, vb, lab, csb, snb, st_b, ob_ref, True)


def _rope_tables(seq, s):
    quarter = GLA_DK // 4
    t = np.arange(s)
    pos_row = np.where(t < seq, t // GRID_W, 0).astype(np.float32)
    pos_col = np.where(t < seq, t % GRID_W, 0).astype(np.float32)
    d = np.arange(GLA_QK) % GLA_DK
    inv_freq = ROPE_BASE ** (-jnp.arange(quarter, dtype=F32) / quarter)
    freq = inv_freq[d % quarter]
    pos = jnp.where((d < GLA_DK // 2)[None, :], pos_row[:, None], pos_col[:, None])
    ang = pos * freq[None, :]
    sign = np.where((d % (2 * quarter)) < quarter, -1.0, 1.0).astype(np.float32)
    return jnp.cos(ang), jnp.sin(ang) * sign[None, :]


def _gla(gg, cos, sin, n_lat_tiles):
    bsz, s, _ = gg.shape
    nt = s // TM
    assert nt == n_lat_tiles + 1
    fwd = lambda i: (i + n_lat_tiles) % nt
    bwd = lambda i: jnp.where(i == 0, n_lat_tiles, n_lat_tiles - i)
    la_f_blk = (2 * GLA_QK + 2 * GLA_WIDTH) // GLA_QK
    la_b_blk = la_f_blk + 1
    v_blk = 2 * GLA_QK // GLA_WIDTH

    def specs(order, la_blk):
        return [
            pl.BlockSpec((1, TM, 2 * GLA_QK), lambda b, i: (b, order(i), 0)),
            pl.BlockSpec((1, TM, GLA_WIDTH), lambda b, i: (b, order(i), v_blk)),
            pl.BlockSpec((1, TM, GLA_QK), lambda b, i: (b, order(i), la_blk)),
            pl.BlockSpec((TM, GLA_QK), lambda b, i: (order(i), 0)),
            pl.BlockSpec((TM, GLA_QK), lambda b, i: (order(i), 0)),
        ]

    return pl.pallas_call(
        _gla_kernel,
        grid=(bsz, nt),
        in_specs=specs(fwd, la_f_blk) + specs(bwd, la_b_blk),
        out_specs=[
            pl.BlockSpec((1, TM, GLA_WIDTH), lambda b, i: (b, fwd(i), 0)),
            pl.BlockSpec((1, TM, GLA_WIDTH), lambda b, i: (b, bwd(i), 0)),
        ],
        out_shape=[jax.ShapeDtypeStruct((bsz, s, GLA_WIDTH), F32)] * 2,
        scratch_shapes=[pltpu.VMEM((GLA_WIDTH, GLA_QK), F32)] * 2,
        compiler_params=_params("arbitrary", "arbitrary"),
        name="gla_bidir",
    )(gg, gg, gg, cos, sin, gg, gg, gg, cos, sin)


def _store_row_tiles(ref, val):
    rows, width = val.shape
    pieces = width // LANES
    for s in range(pieces):
        ref[pl.ds(s, rows, stride=pieces), :] = val[:, s * LANES:(s + 1) * LANES]


def _load_row_tiles(ref, rows, pieces):
    return jnp.concatenate([ref[pl.ds(s, rows, stride=pieces), :] for s in range(pieces)], axis=1)


def _row_index(shape):
    return lax.broadcasted_iota(jnp.int32, shape, 0).astype(F32)


def _stack_rows(rows):
    shape = (len(rows), rows[0].shape[1])
    rid = lax.broadcasted_iota(jnp.int32, shape, 0)
    out = jnp.zeros(shape, rows[0].dtype)
    for k, r in enumerate(rows):
        out = jnp.where(rid == k, r, out)
    return out


def _select_experts(scores, biased):
    n_exp, n_tok = biased.shape
    per = n_exp // N_GROUPS
    neg = -jnp.inf
    sub = _row_index((per, n_tok))
    grp_rows = []
    for g in range(N_GROUPS):
        blk = biased[g * per:(g + 1) * per, :]
        m1 = jnp.max(blk, axis=0, keepdims=True)
        i1 = jnp.min(jnp.where(blk == m1, sub, float(per)), axis=0, keepdims=True)
        m2 = jnp.max(jnp.where(sub == i1, neg, blk), axis=0, keepdims=True)
        grp_rows.append(m1 + m2)
    cur = _stack_rows(grp_rows)
    gid = _row_index(cur.shape)
    picked = jnp.zeros(cur.shape, F32)
    for _ in range(TOPK_GROUPS):
        gm = jnp.max(cur, axis=0, keepdims=True)
        gi = jnp.min(jnp.where(cur == gm, gid, float(N_GROUPS)), axis=0, keepdims=True)
        hit = gid == gi
        picked = jnp.where(hit, 1.0, picked)
        cur = jnp.where(hit, neg, cur)
    cur = jnp.concatenate(
        [jnp.where(picked[g:g + 1, :] > 0.0, biased[g * per:(g + 1) * per, :], neg) for g in range(N_GROUPS)], axis=0)
    eid = _row_index(cur.shape)
    ids, vals = [], []
    chosen = jnp.zeros(cur.shape, F32)
    for _ in range(TOP_K):
        m = jnp.max(cur, axis=0, keepdims=True)
        idx = jnp.min(jnp.where(cur == m, eid, float(n_exp)), axis=0, keepdims=True)
        hit = eid == idx
        ids.append(idx)
        vals.append(jnp.sum(jnp.where(hit, scores, 0.0), axis=0, keepdims=True))
        chosen = jnp.where(hit, 1.0, chosen)
        cur = jnp.where(hit, neg, cur)
    return ids, vals, chosen


def _outproj_kernel(x_ref, oa_ref, ob_ref, of_ref, obk_ref, r_ref, gg_ref, wo_ref, mod_ref, gf_ref, wr_ref, br_ref,
                    tri_ref, xo_ref, h_ref, e_ref, w_ref, rk_ref, hist_ref):
    o = of_ref[0] + obk_ref[0]
    head = lax.broadcasted_iota(jnp.int32, o.shape, 1) // GLA_DV
    o2 = o * o
    rs = jnp.zeros_like(o)
    for h in range(GLA_HEADS):
        ssq = jnp.sum(jnp.where(head == h, o2, 0.0), axis=-1, keepdims=True)
        rs = jnp.where(head == h, lax.rsqrt(ssq / GLA_DV + NORM_EPS), rs)
    oc = (o * rs * gg_ref[...] * _silu(r_ref[0])).astype(BF16)
    c1 = NA_WIDTH + POOL_WIDTH
    acc = _dot(oa_ref[0], wo_ref[0:NA_WIDTH]) + _dot(ob_ref[0], wo_ref[NA_WIDTH:c1]) + _dot(oc, wo_ref[c1:])
    x = x_ref[0] + mod_ref[0, 0, 2:3, :] * acc
    xo_ref[0] = x
    h = _rmsnorm(x, gf_ref[...]) * (1.0 + mod_ref[0, 0, 4:5, :]) + mod_ref[0, 0, 3:4, :]
    _store_row_tiles(h_ref, h)

    scores = _sigmoid(_dot_nt(wr_ref[...], h.astype(BF16)))
    ids, vals, chosen = _select_experts(scores, scores + br_ref[...])
    total = vals[0]
    for v in vals[1:]:
        total = total + v
    w_ref[...] = _stack_rows([v / total * ROUTED_SCALE for v in vals])
    e_ref[...] = _stack_rows(ids).astype(jnp.int32)
    chosen_b = chosen.astype(BF16)
    before = _dot(chosen_b, tri_ref[...])
    eid = _row_index(before.shape)
    rk_ref[...] = _stack_rows([jnp.sum(jnp.where(eid == idx, before, 0.0), axis=0, keepdims=True) for idx in ids])
    hist_ref[0] = _dot_nt(jnp.ones((8, chosen.shape[1]), BF16), chosen_b)


def _outproj(xa, oa, ob, o_f, o_b, gg, g_gla4, w_out, mod, g_ffn, w_router_t, b_router, n_lat_tiles, n_tiles):
    bsz, _, d = xa.shape
    n_exp = w_router_t.shape[0]
    r_blk = (2 * GLA_QK + GLA_WIDTH) // GLA_WIDTH
    tile = lambda w: pl.BlockSpec((1, TM, w), lambda b, i: (b, i, 0))
    const = lambda b, i: (0, 0)
    per_tok = pl.BlockSpec((TOP_K, TM), lambda b, i: (0, b * n_tiles + i))
    t_tok = bsz * n_tiles * TM
    tri = (jnp.arange(TM)[:, None] < jnp.arange(TM)[None, :]).astype(BF16)
    return pl.pallas_call(
        _outproj_kernel,
        grid=(bsz, n_tiles),
        in_specs=[
            tile(d), tile(NA_WIDTH), tile(POOL_WIDTH), tile(GLA_WIDTH), tile(GLA_WIDTH),
            pl.BlockSpec((1, TM, GLA_WIDTH), lambda b, i: (b, i, r_blk)),
            pl.BlockSpec((1, GLA_WIDTH), const),
            pl.BlockSpec(w_out.shape, const),
            pl.BlockSpec((1, 1, 8, d), lambda b, i: (b, jnp.where(i >= n_lat_tiles, 0, 1), 0, 0)),
            pl.BlockSpec((1, d), const),
            pl.BlockSpec(w_router_t.shape, const),
            pl.BlockSpec((n_exp, 1), const),
            pl.BlockSpec((TM, TM), const),
        ],
        out_specs=[tile(d), pl.BlockSpec((TM * d // LANES, LANES), lambda b, i: (b * n_tiles + i, 0)),
                   per_tok, per_tok, per_tok,
                   pl.BlockSpec((1, 8, n_exp), lambda b, i: (b * n_tiles + i, 0, 0))],
        out_shape=[
            jax.ShapeDtypeStruct((bsz, n_tiles * TM, d), F32),
            jax.ShapeDtypeStruct((t_tok * d // LANES, LANES), F32),
            jax.ShapeDtypeStruct((TOP_K, t_tok), jnp.int32),
            jax.ShapeDtypeStruct((TOP_K, t_tok), F32),
            jax.ShapeDtypeStruct((TOP_K, t_tok), F32),
            jax.ShapeDtypeStruct((bsz * n_tiles, 8, n_exp), F32),
        ],
        compiler_params=_params("parallel", "parallel"),
        name="outproj_router",
    )(xa, oa, ob, o_f, o_b, gg, g_gla4, w_out, mod, g_ffn, w_router_t, b_router, tri)


def _slot_layout(hist):
    n_tiles, n_exp = hist.shape
    counts = hist.sum(0)
    tile_base = jnp.cumsum(hist, axis=0) - hist
    padded = (counts + MOE_BLOCK - 1) // MOE_BLOCK * MOE_BLOCK
    pad_end = jnp.cumsum(padded)
    pad_start = pad_end - padded
    base = (pad_start[None, :] + tile_base).astype(F32).reshape(n_tiles, n_exp, 1)
    n_blocks = -(-(n_tiles * TM * TOP_K) // MOE_BLOCK) + n_exp
    n_used = (pad_end[-1] // MOE_BLOCK).astype(jnp.int32)
    fill = jnp.concatenate([jnp.maximum(pad_end // MOE_BLOCK - 1, 0).astype(jnp.int32),
                            jnp.minimum(n_used + jnp.arange(n_exp, dtype=jnp.int32), n_blocks - 1)])
    first_blk = (pad_start // MOE_BLOCK).astype(jnp.int32)
    blk_count = (padded // MOE_BLOCK).astype(jnp.int32)
    return base, first_blk, blk_count, fill, n_used.reshape(1), n_blocks


def _pos_kernel(e_ref, rk_ref, base_ref, pos_ref, *, pieces):
    eid = lax.broadcasted_iota(jnp.int32, (base_ref.shape[1], e_ref.shape[1]), 0)
    base = base_ref[0]
    rows = [jnp.sum(jnp.where(eid == e_ref[k:k + 1, :], base, 0.0), axis=0, keepdims=True) for k in range(TOP_K)]
    pos_ref[0] = ((_stack_rows(rows) + rk_ref[...]) * float(pieces)).astype(jnp.int32)


def _slot_positions(e_t, rank_t, base, pieces):
    n_tiles, n_exp, _ = base.shape
    per_tok = pl.BlockSpec((TOP_K, TM), lambda i: (0, i))
    return pl.pallas_call(
        functools.partial(_pos_kernel, pieces=pieces),
        grid=(n_tiles,),
        in_specs=[per_tok, per_tok, pl.BlockSpec((1, n_exp, 1), lambda i: (i, 0, 0))],
        out_specs=pl.BlockSpec((1, TOP_K, TM), lambda i: (i, 0, 0)),
        out_shape=jax.ShapeDtypeStruct((n_tiles, TOP_K, TM), jnp.int32),
        compiler_params=_params("parallel"),
        name="moe_positions",
    )(e_t, rank_t, base).reshape(-1)


def _dispatch_kernel(fill_ref, pos_ref, h_ref, xs_hbm, zbuf, sem, zsem):
    @pl.when(pl.program_id(0) == 0)
    def _():
        zbuf[...] = jnp.zeros_like(zbuf)
        n_fill = fill_ref.shape[0]

        def is_new(j):
            return (j == 0) | (fill_ref[j] != fill_ref[jnp.maximum(j - 1, 0)])

        def start(j, c):
            @pl.when(is_new(j))
            def _():
                row0 = pl.multiple_of(fill_ref[j] * zbuf.shape[0], zbuf.shape[0])
                pltpu.make_async_copy(zbuf, xs_hbm.at[pl.ds(row0, zbuf.shape[0]), :], zsem).start()
            return c
        lax.fori_loop(0, n_fill, start, 0)

        def wait(j, c):
            @pl.when(is_new(j))
            def _():
                pltpu.make_async_copy(zbuf, xs_hbm.at[pl.ds(0, zbuf.shape[0]), :], zsem).wait()
            return c
        lax.fori_loop(0, n_fill, wait, 0)

    pieces = h_ref.shape[0] // TM

    def body(t, c):
        src = h_ref.at[pl.ds(pl.multiple_of(t * pieces, pieces), pieces), :]
        for k in range(TOP_K):
            dst = xs_hbm.at[pl.ds(pl.multiple_of(pos_ref[k * TM + t], pieces), pieces), :]
            pltpu.make_async_copy(src, dst, sem).start(priority=k % 2)
        return c
    lax.fori_loop(0, TM, body, 0, unroll=2)
    for _ in range(TOP_K):
        pltpu.make_async_copy(h_ref, xs_hbm.at[pl.ds(0, h_ref.shape[0]), :], sem).wait()


def _dispatch(h2t, pos, fill, n_slots, pieces):
    n_tiles = h2t.shape[0] // (TM * pieces)
    grid_spec = pltpu.PrefetchScalarGridSpec(
        num_scalar_prefetch=1,
        grid=(n_tiles,),
        in_specs=[
            pl.BlockSpec((TOP_K * TM,), lambda i, fill: (i,), memory_space=pltpu.SMEM),
            pl.BlockSpec((TM * pieces, LANES), lambda i, fill: (i, 0)),
        ],
        out_specs=pl.BlockSpec(memory_space=pl.ANY),
        scratch_shapes=[pltpu.VMEM((MOE_BLOCK * pieces, LANES), F32), pltpu.SemaphoreType.DMA,
                        pltpu.SemaphoreType.DMA],
    )
    return pl.pallas_call(
        _dispatch_kernel,
        grid_spec=grid_spec,
        out_shape=jax.ShapeDtypeStruct((n_slots * pieces, LANES), F32),
        compiler_params=_params("arbitrary"),
        name="moe_dispatch",
    )(fill, pos, h2t)


def _expert_kernel(first_ref, cnt_ref, nu_ref, wi_ref, wd_ref, xs_hbm, ys_hbm, xbuf, ybuf, wi_bf, wd_bf, xsem, ysem,
                   *, n_blocks):
    e = pl.program_id(0)
    n_used = nu_ref[0]
    rows = xbuf.shape[1]
    pieces = rows // MOE_BLOCK
    d_exp = wd_ref.shape[2]

    def x_copy(g, s):
        return pltpu.make_async_copy(xs_hbm.at[pl.ds(pl.multiple_of(g * rows, rows), rows), :], xbuf.at[s], xsem.at[s])

    def y_copy(g, s):
        return pltpu.make_async_copy(ybuf.at[s], ys_hbm.at[pl.ds(pl.multiple_of(g * rows, rows), rows), :], ysem.at[s])

    n_xbuf = xbuf.shape[0]

    @pl.when(e == 0)
    def _():
        for g in range(n_xbuf - 1):
            @pl.when(g < n_used)
            def _():
                x_copy(g, g).start()

    @pl.when(cnt_ref[e] > 0)
    def _():
        wi_bf[...] = wi_ref[0, 0].astype(BF16)
        wd_bf[...] = wd_ref[0, 0].astype(BF16)

    def block(j, c):
        g = first_ref[e] + j
        s = g % 2
        x_copy(g, g % n_xbuf).wait()

        @pl.when(g + n_xbuf - 1 < n_used)
        def _():
            x_copy(g + n_xbuf - 1, (g + n_xbuf - 1) % n_xbuf).start()

        @pl.when(g >= 2)
        def _():
            y_copy(g - 2, s).wait()

        hh = _dot(_load_row_tiles(xbuf.at[g % n_xbuf], MOE_BLOCK, pieces).astype(BF16), wi_bf[...])
        a = (_silu(hh[:, :d_exp]) * hh[:, d_exp:]).astype(BF16)
        _store_row_tiles(ybuf.at[s], _dot(a, wd_bf[...]))
        y_copy(g, s).start(priority=1)
        return c
    lax.fori_loop(0, cnt_ref[e], block, 0)

    @pl.when(e == pl.num_programs(0) - 1)
    def _():
        y_copy(n_used - 1, (n_used - 1) % 2).wait()

        @pl.when(n_used >= 2)
        def _():
            y_copy(n_used - 2, n_used % 2).wait()

        ybuf[0] = jnp.zeros(ybuf.shape[1:], ybuf.dtype)

        def start(g, c):
            y_copy(g, 0).start()
            return c
        lax.fori_loop(n_used, n_blocks, start, 0)

        def wait(g, c):
            y_copy(g, 0).wait()
            return c
        lax.fori_loop(n_used, n_blocks, wait, 0)


def _moe_experts(xs, first_blk, blk_count, n_used, n_blocks, w_e_in, w_e_down, layer):
    _, n_exp, d, f2 = w_e_in.shape
    rows = xs.shape[0] // n_blocks
    grid_spec = pltpu.PrefetchScalarGridSpec(
        num_scalar_prefetch=3,
        grid=(n_exp,),
        in_specs=[
            pl.BlockSpec((1, 1, d, f2), lambda e, first, cnt, nu: (layer, e, 0, 0)),
            pl.BlockSpec((1, 1, f2 // 2, d), lambda e, first, cnt, nu: (layer, e, 0, 0)),
            pl.BlockSpec(memory_space=pl.ANY),
        ],
        out_specs=pl.BlockSpec(memory_space=pl.ANY),
        scratch_shapes=[
            pltpu.VMEM((3, rows, LANES), F32),
            pltpu.VMEM((2, rows, LANES), F32),
            pltpu.VMEM((d, f2), BF16),
            pltpu.VMEM((f2 // 2, d), BF16),
            pltpu.SemaphoreType.DMA((3,)),
            pltpu.SemaphoreType.DMA((2,)),
        ],
    )
    return pl.pallas_call(
        functools.partial(_expert_kernel, n_blocks=n_blocks),
        grid_spec=grid_spec,
        out_shape=jax.ShapeDtypeStruct(xs.shape, F32),
        compiler_params=_params("arbitrary"),
        name="moe_experts",
    )(first_blk, blk_count, n_used, w_e_in, w_e_down, xs)


def _combine_kernel(pos0_ref, posn_ref, x_ref, h_ref, w_ref, wsi_ref, wsd_ref, mod_ref, gfin_ref, ys_hbm, o_ref,
                    ybuf, sem, *, final, n_steps):
    step = pl.program_id(0) * pl.num_programs(1) + pl.program_id(1)
    slot = step % 2
    d_exp = wsd_ref.shape[0]
    pieces = h_ref.shape[0] // TMC

    def gather(pos_ref, st, s):
        off = (st % (TM // TMC)) * TMC
        def body(t, c):
            for k in range(TOP_K):
                src = ys_hbm.at[pl.ds(pl.multiple_of(pos_ref[k * TM + off + t], pieces), pieces), :]
                dst = ybuf.at[s, k, pl.ds(pl.multiple_of(t * pieces, pieces), pieces), :]
                pltpu.make_async_copy(src, dst, sem.at[s]).start(priority=k % 2)
            return c
        lax.fori_loop(0, TMC, body, 0, unroll=2)

    @pl.when(step == 0)
    def _():
        gather(pos0_ref, 0, 0)

    @pl.when(step + 1 < n_steps)
    def _():
        gather(posn_ref, step + 1, 1 - slot)

    hh = _dot(_load_row_tiles(h_ref, TMC, pieces).astype(BF16), wsi_ref[...])
    a = (_silu(hh[:, :d_exp]) * hh[:, d_exp:]).astype(BF16)
    y = _dot(a, wsd_ref[...])
    for k in range(TOP_K):
        pltpu.make_async_copy(ys_hbm.at[pl.ds(0, TMC * pieces), :], ybuf.at[slot, k], sem.at[slot]).wait()
    w = w_ref[...]
    for k in range(TOP_K):
        y = y + _load_row_tiles(ybuf.at[slot, k], TMC, pieces) * w[:, k:k + 1]
    x = x_ref[0] + mod_ref[0, 0, 5:6, :] * y
    if final:
        x = _rmsnorm(x, gfin_ref[...])
    o_ref[0] = x


def _combine(x_mid, h2t, ys, pos, w_tok, w_sh_in, w_sh_down, mod, g_final, n_lat_tiles, final):
    bsz, s, d = x_mid.shape
    nt = s // TMC
    n_steps = bsz * nt
    halves = TM // TMC
    lat_steps = n_lat_tiles * halves
    pieces = d // LANES
    const = lambda b, i: (0, 0)
    tile = pl.BlockSpec((1, TMC, d), lambda b, i: (b, i, 0))
    smem_blk = lambda f: pl.BlockSpec((TOP_K * TM,), f, memory_space=pltpu.SMEM)
    return pl.pallas_call(
        functools.partial(_combine_kernel, final=final, n_steps=n_steps),
        grid=(bsz, nt),
        in_specs=[
            smem_blk(lambda b, i: (0,)),
            smem_blk(lambda b, i: (jnp.minimum(b * nt + i + 1, n_steps - 1) // halves,)),
            tile,
            pl.BlockSpec((TMC * pieces, LANES), lambda b, i: (b * nt + i, 0)),
            pl.BlockSpec((TMC, TOP_K), lambda b, i: (b * nt + i, 0)),
            pl.BlockSpec(w_sh_in.shape, const),
            pl.BlockSpec(w_sh_down.shape, const),
            pl.BlockSpec((1, 1, 8, d), lambda b, i: (b, jnp.where(i >= lat_steps, 0, 1), 0, 0)),
            pl.BlockSpec((1, d), const),
            pl.BlockSpec(memory_space=pl.ANY),
        ],
        out_specs=tile,
        out_shape=jax.ShapeDtypeStruct((bsz, s, d), F32),
        scratch_shapes=[pltpu.VMEM((2, TOP_K, TMC * pieces, LANES), F32), pltpu.SemaphoreType.DMA((2,))],
        compiler_params=_params("arbitrary", "arbitrary"),
        name="moe_combine",
    )(pos, pos, x_mid, h2t, w_tok, w_sh_in, w_sh_down, mod, g_final, ys)


def _block_diag(w):
    g, a, b = w.shape
    out = jnp.zeros((g * a, g * b), w.dtype)
    for j in range(g):
        out = out.at[j * a:(j + 1) * a, j * b:(j + 1) * b].set(w[j])
    return out


def kernel(x, c, ctx, c_ctx, w_mod, b_mod, g_mix, w_in, rpb, w_pool, pool_scale, w_gate_f, b_gate_f, w_gate_b, b_gate_b, g_gla, w_out, g_ffn, w_router, b_router, w_e_in, w_e_down, w_sh_in, w_sh_down, g_final):
    bsz, seq, d = x.shape
    n_ctx = ctx.shape[1]
    depth = w_mod.shape[0]
    n_exp = w_router.shape[2]
    assert n_ctx == TM and seq % TM == 0 and seq % GRID_W == 0
    n_lat = seq // TM
    s = seq + n_ctx

    n_rows = -(-(bsz + 1) // 8) * 8
    cc = jnp.zeros((n_rows, d), F32).at[:bsz].set(c).at[bsz].set(c_ctx)
    mod_all = _modulation(cc, w_mod, b_mod).reshape(depth, n_rows, 6, d)
    cos, sin = _rope_tables(seq, s)

    xa = jnp.concatenate([x, ctx], axis=1)
    for layer in range(depth):
        last = layer == depth - 1
        n_tiles = n_lat if last else n_lat + 1
        m = mod_all[layer]
        mod = jnp.stack([jnp.broadcast_to(m[bsz], (bsz, 6, d)), m[:bsz]], axis=1)
        mod = jnp.pad(mod, ((0, 0), (0, 0), (0, 2), (0, 0)))

        wl = w_in[layer]
        w_main = wl[:, :MAIN_W].astype(BF16)
        w_low = jnp.pad(wl[:, MAIN_W:], ((0, 0), (0, LANES - 2 * GLA_GATE_RANK))).astype(BF16)
        w2 = jnp.zeros((LANES, 2 * GLA_QK), F32)
        w2 = w2.at[:GLA_GATE_RANK, :GLA_QK].set(w_gate_f[layer])
        w2 = w2.at[GLA_GATE_RANK:2 * GLA_GATE_RANK, GLA_QK:].set(w_gate_b[layer]).astype(BF16)
        b2 = jnp.concatenate([b_gate_f[layer], b_gate_b[layer]])[None, :]
        qkv, u, gg = _inproj(xa, mod, g_mix[layer][None, :], w_main, w_low, w2, b2, n_lat)

        oa = _neighborhood_attention(qkv, _na_bias_table(rpb[layer]), n_lat, not last)
        ob = _multiscale_pool(u, _block_diag(w_pool[layer]).astype(BF16), pool_scale[layer][None, :], n_lat, n_tiles)
        o_f, o_b = _gla(gg, cos, sin, n_lat)

        x_mid, h2t, e_t, w_t, rank_t, hist = _outproj(
            xa, oa, ob, o_f, o_b, gg, jnp.tile(g_gla[layer], GLA_HEADS)[None, :], w_out[layer].astype(BF16), mod,
            g_ffn[layer][None, :], w_router[layer].T.astype(BF16), b_router[layer].reshape(n_exp, 1), n_lat, n_tiles)
        pieces = d // LANES
        base, first_blk, blk_count, fill, n_used, n_blocks = _slot_layout(hist[:, 0, :].astype(jnp.int32))
        pos = _slot_positions(e_t, rank_t, base, pieces)
        xs = _dispatch(h2t, pos, fill, n_blocks * MOE_BLOCK, pieces)
        ys = _moe_experts(xs, first_blk, blk_count, n_used, n_blocks, w_e_in, w_e_down, layer)
        xa = _combine(x_mid, h2t, ys, pos, w_t.T, w_sh_in[layer].astype(BF16), w_sh_down[layer].astype(BF16), mod,
                      g_final[None, :], n_lat, last)
    return xa
```

```python
import functools

import jax
import jax.numpy as jnp
import numpy as np
from jax import lax
from jax.experimental import pallas as pl
from jax.experimental.pallas import tpu as pltpu

GRID_W = 64
NORM_EPS = 1e-6
NA_HEADS = 8
NA_HEAD_DIM = 64
NA_WIDTH = NA_HEADS * NA_HEAD_DIM
NA_WIN_ROWS = 8
NA_WIN_COLS = 16
POOL_WINDOWS = (2, 4, 8, 16)
POOL_GROUP_DIM = 64
POOL_WIDTH = len(POOL_WINDOWS) * POOL_GROUP_DIM
POOL_REACH = max(POOL_WINDOWS) // 2
assert POOL_WINDOWS == tuple(2 ** (g + 1) for g in range(len(POOL_WINDOWS)))
GLA_HEADS = 4
GLA_DK = 32
GLA_DV = 64
GLA_QK = GLA_HEADS * GLA_DK
GLA_WIDTH = GLA_HEADS * GLA_DV
GLA_GATE_RANK = 16
GLA_TAU = 16.0
GLA_CHUNK = 64
ROPE_BASE = 10000.0
N_EXPERTS = 256
TOP_K = 8
N_GROUPS = 8
TOPK_GROUPS = 4
ROUTED_SCALE = 2.5

TM = 256
TMC = 128
MOE_BLOCK = 256
LANES = 128
MXU_DIM = 256
MASK_VALUE = -1e30
VMEM_LIMIT = 48 * 1024 * 1024

QKV_W = 3 * NA_WIDTH
G_W = 2 * GLA_QK + 2 * GLA_WIDTH + 2 * GLA_QK
MAIN_W = QKV_W + POOL_WIDTH + 2 * GLA_QK + 2 * GLA_WIDTH

BF16 = jnp.bfloat16
F32 = jnp.float32


def _params(*sem):
    return pltpu.CompilerParams(dimension_semantics=sem, vmem_limit_bytes=VMEM_LIMIT)


def _sigmoid(x):
    return 1.0 / (1.0 + jnp.exp(-x))


def _silu(x):
    return x * _sigmoid(x)


def _rmsnorm(x, g):
    return x * lax.rsqrt(jnp.mean(x * x, axis=-1, keepdims=True) + NORM_EPS) * g


def _dot(a, b):
    return jnp.dot(a, b, preferred_element_type=F32)


def _dot_nt(a, b):
    return lax.dot_general(a, b, (((1,), (1,)), ((), ())), preferred_element_type=F32)


def _dot_tn(a, b):
    return lax.dot_general(a, b, (((0,), (0,)), ((), ())), preferred_element_type=F32)


def _mod_kernel(c_ref, w_ref, b_ref, o_ref):
    a = _silu(c_ref[...]).astype(BF16)
    o_ref[0] = _dot(a, w_ref[0].astype(BF16)) + b_ref[0]


def _modulation(cc, w_mod, b_mod):
    depth, d, n = w_mod.shape
    r = cc.shape[0]
    tn = 1024
    return pl.pallas_call(
        _mod_kernel,
        grid=(depth, n // tn),
        in_specs=[
            pl.BlockSpec((r, d), lambda l, j: (0, 0)),
            pl.BlockSpec((1, d, tn), lambda l, j: (l, 0, j)),
            pl.BlockSpec((1, 1, tn), lambda l, j: (l, 0, j)),
        ],
        out_specs=pl.BlockSpec((1, r, tn), lambda l, j: (l, 0, j)),
        out_shape=jax.ShapeDtypeStruct((depth, r, n), F32),
        compiler_params=_params("parallel", "parallel"),
        name="modulation",
    )(cc, w_mod, b_mod.reshape(depth, 1, n))


def _stream_tile(x_ref, xc_ref, n_lat_tiles):
    return jnp.where(pl.program_id(1) >= n_lat_tiles, xc_ref[0], x_ref[0])


def _stream_specs(d, n_lat_tiles, ctx_blk):
    return [pl.BlockSpec((1, TM, d), lambda b, i: (b, jnp.minimum(i, n_lat_tiles - 1), 0)),
            pl.BlockSpec((1, TM, d), lambda b, i: (b, ctx_blk, 0))]


def _inproj_kernel(x_ref, xc_ref, mod_ref, g_ref, wm_ref, wl_ref, w2_ref, b2_ref, qkv_ref, u_ref, gg_ref, *, n_lat_tiles):
    x = _stream_tile(x_ref, xc_ref, n_lat_tiles)
    h = _rmsnorm(x, g_ref[...]) * (1.0 + mod_ref[0, 0, 1:2, :]) + mod_ref[0, 0, 0:1, :]
    hb = h.astype(BF16)
    q = _dot(hb, wm_ref[:, 0:NA_WIDTH]) * (NA_HEAD_DIM ** -0.5)
    qkv_ref[0, :, 0:NA_WIDTH] = q.astype(BF16)
    qkv_ref[0, :, NA_WIDTH:2 * NA_WIDTH] = _dot(hb, wm_ref[:, NA_WIDTH:2 * NA_WIDTH]).astype(BF16)
    qkv_ref[0, :, 2 * NA_WIDTH:QKV_W] = _dot(hb, wm_ref[:, 2 * NA_WIDTH:QKV_W]).astype(BF16)
    u_ref[0] = _dot(hb, wm_ref[:, QKV_W:QKV_W + POOL_WIDTH])
    c0 = QKV_W + POOL_WIDTH
    gg_ref[0, :, 0:GLA_QK] = _dot(hb, wm_ref[:, c0:c0 + GLA_QK]) * (GLA_DK ** -0.5)
    gg_ref[0, :, GLA_QK:2 * GLA_QK + 2 * GLA_WIDTH] = _dot(hb, wm_ref[:, c0 + GLA_QK:MAIN_W])
    a_low = _dot(hb, wl_ref[...]).astype(BF16)
    lg = _dot(a_low, w2_ref[...]) + b2_ref[...]
    log_sig = jnp.minimum(lg, 0.0) - jnp.log1p(jnp.exp(-jnp.abs(lg)))
    gg_ref[0, :, 2 * GLA_QK + 2 * GLA_WIDTH:G_W] = log_sig / GLA_TAU


def _inproj(x_lat, x_ctx, ctx_blk, mod, g, w_main, w_low, w2, b2, n_lat_tiles):
    bsz, _, d = x_lat.shape
    nt = n_lat_tiles + 1
    s = nt * TM
    const = lambda b, i: (0, 0)
    return pl.pallas_call(
        functools.partial(_inproj_kernel, n_lat_tiles=n_lat_tiles),
        grid=(bsz, nt),
        in_specs=_stream_specs(d, n_lat_tiles, ctx_blk) + [
            pl.BlockSpec((1, 1, 8, d), lambda b, i: (b, jnp.where(i >= n_lat_tiles, 0, 1), 0, 0)),
            pl.BlockSpec((1, d), const),
            pl.BlockSpec(w_main.shape, const),
            pl.BlockSpec(w_low.shape, const),
            pl.BlockSpec(w2.shape, const),
            pl.BlockSpec(b2.shape, const),
        ],
        out_specs=[
            pl.BlockSpec((1, TM, QKV_W), lambda b, i: (b, i, 0)),
            pl.BlockSpec((1, TM, POOL_WIDTH), lambda b, i: (b, i, 0)),
            pl.BlockSpec((1, TM, G_W), lambda b, i: (b, i, 0)),
        ],
        out_shape=[
            jax.ShapeDtypeStruct((bsz, s, QKV_W), BF16),
            jax.ShapeDtypeStruct((bsz, s, POOL_WIDTH), F32),
            jax.ShapeDtypeStruct((bsz, s, G_W), F32),
        ],
        compiler_params=_params("parallel", "parallel"),
        name="inproj",
    )(x_lat, x_ctx, mod, g, w_main, w_low, w2, b2)


HEADS_PER_GROUP = MXU_DIM // NA_HEAD_DIM
NA_GROUPS = NA_HEADS // HEADS_PER_GROUP
NA_LOCAL_KEYS = NA_WIN_ROWS * GRID_W
NA_ROWS_PER_STEP = 2


def _stack_heads(x, width):
    lane = lax.broadcasted_iota(jnp.int32, x.shape, 1) // width
    n_heads = x.shape[1] // width
    return jnp.concatenate([jnp.where(lane == h, x, jnp.zeros_like(x)) for h in range(n_heads)], axis=0)


def _unstack_heads(o, width):
    n_heads = o.shape[1] // width
    r = o.shape[0] // n_heads
    lane = lax.broadcasted_iota(jnp.int32, (r, o.shape[1]), 1) // width
    acc = jnp.zeros((r, o.shape[1]), o.dtype)
    for h in range(n_heads):
        acc = jnp.where(lane == h, o[h * r:(h + 1) * r, :], acc)
    return acc


def _na_kernel(q_ref, k_ref, v_ref, kc_ref, vc_ref, *rest, n_rows):
    bias_refs, o_ref = rest[:-1], rest[-1]
    rows_per_step = len(bias_refs)
    step = pl.program_id(1)

    def attend(g, j, local):
        r = step * rows_per_step + j
        rows = slice(j * GRID_W, (j + 1) * GRID_W)
        cols = slice(g * MXU_DIM, (g + 1) * MXU_DIM)
        qs = _stack_heads(q_ref[0, rows, cols], NA_HEAD_DIM)
        kc = kc_ref[0, :, cols]
        vc = vc_ref[0, :, cols]
        s_ctx = _dot_nt(qs, kc)
        m = jnp.max(s_ctx, axis=-1, keepdims=True)
        if local:
            start = pl.multiple_of(jnp.clip(r - NA_WIN_ROWS // 2, 0, n_rows - NA_WIN_ROWS) * GRID_W, GRID_W)
            kw = k_ref[0, pl.ds(start, NA_LOCAL_KEYS), cols]
            vw = v_ref[0, pl.ds(start, NA_LOCAL_KEYS), cols]
            s_loc = _dot_nt(qs, kw) + bias_refs[j][0, g]
            m = jnp.maximum(m, jnp.max(s_loc, axis=-1, keepdims=True))
            p_loc = jnp.exp(s_loc - m)
        p_ctx = jnp.exp(s_ctx - m)
        den = jnp.sum(p_ctx, axis=-1, keepdims=True)
        o = _dot(p_ctx.astype(BF16), vc)
        if local:
            den = den + jnp.sum(p_loc, axis=-1, keepdims=True)
            o = o + _dot(p_loc.astype(BF16), vw)
        o_ref[0, rows, cols] = _unstack_heads(o / den, NA_HEAD_DIM).astype(o_ref.dtype)

    @pl.when(step * rows_per_step < n_rows)
    def _():
        for j in range(rows_per_step):
            for g in range(NA_GROUPS):
                attend(g, j, True)

    @pl.when(step * rows_per_step >= n_rows)
    def _():
        for j in range(rows_per_step):
            for g in range(NA_GROUPS):
                attend(g, j, False)


def _na_bias_table(rpb):
    o = jnp.arange(NA_WIN_ROWS)
    dr = o[None, :] - o[:, None] + NA_WIN_ROWS - 1
    col = jnp.arange(GRID_W)
    dc = jnp.clip(col[None, :] - col[:, None], -(NA_WIN_COLS - 1), NA_WIN_COLS - 1) + NA_WIN_COLS - 1
    col_start = jnp.clip(col - NA_WIN_COLS // 2, 0, GRID_W - NA_WIN_COLS)
    col_mask = (col[None, :] >= col_start[:, None]) & (col[None, :] < col_start[:, None] + NA_WIN_COLS)
    onehot = (dc[None] == jnp.arange(2 * NA_WIN_COLS - 1)[:, None, None]).astype(F32)
    t = jnp.einsum('howc,cqk->howqk', rpb[:, dr].astype(F32), onehot,
                   precision=lax.Precision.HIGHEST)
    t = jnp.where(col_mask[None, None, None], t, MASK_VALUE)
    t = t.transpose(1, 0, 3, 2, 4)
    return t.reshape(NA_WIN_ROWS, NA_GROUPS, HEADS_PER_GROUP * GRID_W, NA_LOCAL_KEYS)


def _neighborhood_attention(qkv, bias_tab, n_lat_tiles, with_ctx_queries):
    bsz, s, _ = qkv.shape
    seq = n_lat_tiles * TM
    n_rows = seq // GRID_W
    assert n_rows >= NA_WIN_ROWS
    n_ctx_rows = (s - seq) // GRID_W
    rq = NA_ROWS_PER_STEP
    assert n_rows % rq == 0 and n_ctx_rows % rq == 0
    nq = (n_rows + (n_ctx_rows if with_ctx_queries else 0)) // rq
    ctx_blk = seq // (s - seq)

    def bias_spec(j):
        def idx(b, i):
            rr = jnp.minimum(i * rq + j, n_rows - 1)
            return (rr - jnp.clip(rr - NA_WIN_ROWS // 2, 0, n_rows - NA_WIN_ROWS), 0, 0, 0)
        return pl.BlockSpec((1,) + bias_tab.shape[1:], idx)

    return pl.pallas_call(
        functools.partial(_na_kernel, n_rows=n_rows),
        grid=(bsz, nq),
        in_specs=[
            pl.BlockSpec((1, rq * GRID_W, NA_WIDTH), lambda b, i: (b, i, 0)),
            pl.BlockSpec((1, seq, NA_WIDTH), lambda b, i: (b, 0, 1)),
            pl.BlockSpec((1, seq, NA_WIDTH), lambda b, i: (b, 0, 2)),
            pl.BlockSpec((1, s - seq, NA_WIDTH), lambda b, i: (b, ctx_blk, 1)),
            pl.BlockSpec((1, s - seq, NA_WIDTH), lambda b, i: (b, ctx_blk, 2)),
        ] + [bias_spec(j) for j in range(rq)],
        out_specs=pl.BlockSpec((1, rq * GRID_W, NA_WIDTH), lambda b, i: (b, i, 0)),
        out_shape=jax.ShapeDtypeStruct((bsz, nq * rq * GRID_W, NA_WIDTH), BF16),
        compiler_params=_params("parallel", "arbitrary"),
        name="neighborhood_attention",
    )(qkv, qkv, qkv, qkv, qkv, *([bias_tab] * rq))


def _pool_kernel(up_ref, u_ref, un_ref, w_ref, sc_ref, o_ref, buf, s0, s1, s2, *, n_lat_tiles, seq, ctx_len):
    i = pl.program_id(1)
    hw = POOL_REACH
    n = TM + 2 * hw
    is_ctx = i >= n_lat_tiles
    base = jnp.where(is_ctx, 0, i * TM)
    lseq = jnp.where(is_ctx, ctx_len, seq)
    buf[0:hw] = up_ref[0]
    buf[hw:hw + TM] = u_ref[0]
    buf[hw + TM:n] = un_ref[0]
    p = base - hw + lax.broadcasted_iota(jnp.int32, (n, POOL_WIDTH), 0)
    buf[...] = jnp.where((p >= 0) & (p < lseq), buf[...], 0.0)
    s0[0:n - 1] = buf[0:n - 1] + buf[1:n]
    s1[0:n - 3] = s0[0:n - 3] + s0[2:n - 1]
    s2[0:n - 7] = s1[0:n - 7] + s1[4:n - 3]
    shape = (TM, POOL_WIDTH)
    t = base + lax.broadcasted_iota(jnp.int32, shape, 0)
    grp = lax.broadcasted_iota(jnp.int32, shape, 1) // POOL_GROUP_DIM
    acc = jnp.where(grp == 0, s0[hw - 1:hw - 1 + TM],
                    jnp.where(grp == 1, s1[hw - 2:hw - 2 + TM],
                              jnp.where(grp == 2, s2[hw - 4:hw - 4 + TM], s2[0:TM] + s2[hw:hw + TM])))
    win = jnp.left_shift(2, grp)
    back = win // 2
    fwd = win - back - 1
    lo = jnp.clip(t - back, 0, lseq - 1)
    hi = jnp.clip(t + fwd, 0, lseq - 1)
    mean = acc / (hi - lo + 1).astype(F32)
    diff = (mean - u_ref[0]).astype(BF16)
    o_ref[0] = (_dot(diff, w_ref[...]) * sc_ref[...]).astype(o_ref.dtype)


def _multiscale_pool(u, w_bd, scale, n_lat_tiles, n_tiles):
    bsz, s, c = u.shape
    hw = POOL_REACH
    per = TM // hw
    last = s // hw - 1
    return pl.pallas_call(
        functools.partial(_pool_kernel, n_lat_tiles=n_lat_tiles, seq=n_lat_tiles * TM, ctx_len=s - n_lat_tiles * TM),
        grid=(bsz, n_tiles),
        in_specs=[
            pl.BlockSpec((1, hw, c), lambda b, i: (b, jnp.maximum(i * per - 1, 0), 0)),
            pl.BlockSpec((1, TM, c), lambda b, i: (b, i, 0)),
            pl.BlockSpec((1, hw, c), lambda b, i: (b, jnp.minimum((i + 1) * per, last), 0)),
            pl.BlockSpec((c, c), lambda b, i: (0, 0)),
            pl.BlockSpec((1, c), lambda b, i: (0, 0)),
        ],
        out_specs=pl.BlockSpec((1, TM, c), lambda b, i: (b, i, 0)),
        out_shape=jax.ShapeDtypeStruct((bsz, n_tiles * TM, c), BF16),
        scratch_shapes=[pltpu.VMEM((TM + 2 * hw, c), F32)] * 4,
        compiler_params=_params("parallel", "parallel"),
        name="multiscale_pool",
    )(u, u, u, w_bd, scale)


def _gla_tile(qk_ref, v_ref, la_ref, cs_ref, sn_ref, st_ref, o_ref, reverse):
    ch = GLA_CHUNK
    lane = lax.broadcasted_iota(jnp.int32, (ch, GLA_QK), 1)
    row = lax.broadcasted_iota(jnp.int32, (ch, GLA_QK), 0)
    rr = lax.broadcasted_iota(jnp.int32, (GLA_HEADS * ch, ch), 0) % ch
    cc = lax.broadcasted_iota(jnp.int32, (GLA_HEADS * ch, ch), 1)
    causal = (rr <= cc) if reverse else (rr >= cc)
    st_shape = (GLA_WIDTH, GLA_QK)
    head_blk = (lax.broadcasted_iota(jnp.int32, st_shape, 0) // GLA_DV
                == lax.broadcasted_iota(jnp.int32, st_shape, 1) // GLA_DK)
    quarter = GLA_DK // 4
    first_half = (lane % (2 * quarter)) < quarter

    def rope(x, cs, sn):
        partner = jnp.where(first_half, pltpu.roll(x, GLA_QK - quarter, 1), pltpu.roll(x, quarter, 1))
        return x * cs + partner * sn

    chunks = range(TM // ch)
    for c in (reversed(chunks) if reverse else chunks):
        sl = slice(c * ch, (c + 1) * ch)
        cs = cs_ref[sl, :]
        sn = sn_ref[sl, :]
        q = rope(qk_ref[0, sl, 0:GLA_QK], cs, sn)
        k = rope(qk_ref[0, sl, GLA_QK:2 * GLA_QK], cs, sn)
        b = la_ref[0, sl, :]
        d = 1
        while d < ch:
            if reverse:
                b = b + jnp.where(row < ch - d, pltpu.roll(b, ch - d, 0), 0.0)
            else:
                b = b + jnp.where(row >= d, pltpu.roll(b, d, 0), 0.0)
            d *= 2
        b_last = b[0:1, :] if reverse else b[ch - 1:ch, :]
        q_in = q * jnp.exp(b)
        k_in = (k * jnp.exp(-b)).astype(BF16)
        k_st = (k * jnp.exp(b_last - b)).astype(BF16)
        att = _dot_nt(_stack_heads(q_in, GLA_DK).astype(BF16), k_in)
        att = jnp.where(causal, att, 0.0).astype(BF16)
        vb = v_ref[0, sl, :].astype(BF16)
        o = _unstack_heads(_dot(att, vb), GLA_DV)
        st = st_ref[...]
        o_ref[0, sl, :] = o + _dot_nt(q_in.astype(BF16), st.astype(BF16))
        st_ref[...] = st * jnp.exp(b_last) + jnp.where(head_blk, _dot_tn(vb, k_st), 0.0)


def _gla_kernel(qkf, vf, laf, csf, snf, qkb, vb, lab, csb, snb, of_ref, ob_ref, st_f, st_b):
    @pl.when(pl.program_id(1) == 0)
    def _():
        st_f[...] = jnp.zeros_like(st_f)
        st_b[...] = jnp.zeros_like(st_b)

    _gla_tile(qkf, vf, laf, csf, snf, st_f, of_ref, False)
    _gla_tile(qkb, vb, lab, csb, snb, st_b, ob_ref, True)


def _rope_tables(seq, s):
    quarter = GLA_DK // 4
    t = np.arange(s)
    pos_row = np.where(t < seq, t // GRID_W, 0).astype(np.float32)
    pos_col = np.where(t < seq, t % GRID_W, 0).astype(np.float32)
    d = np.arange(GLA_QK) % GLA_DK
    inv_freq = ROPE_BASE ** (-jnp.arange(quarter, dtype=F32) / quarter)
    freq = inv_freq[d % quarter]
    pos = jnp.where((d < GLA_DK // 2)[None, :], pos_row[:, None], pos_col[:, None])
    ang = pos * freq[None, :]
    sign = np.where((d % (2 * quarter)) < quarter, -1.0, 1.0).astype(np.float32)
    return jnp.cos(ang), jnp.sin(ang) * sign[None, :]


def _gla(gg, cos, sin, n_lat_tiles):
    bsz, s, _ = gg.shape
    nt = s // TM
    assert nt == n_lat_tiles + 1
    fwd = lambda i: (i + n_lat_tiles) % nt
    bwd = lambda i: jnp.where(i == 0, n_lat_tiles, n_lat_tiles - i)
    la_f_blk = (2 * GLA_QK + 2 * GLA_WIDTH) // GLA_QK
    la_b_blk = la_f_blk + 1
    v_blk = 2 * GLA_QK // GLA_WIDTH

    def specs(order, la_blk):
        return [
            pl.BlockSpec((1, TM, 2 * GLA_QK), lambda b, i: (b, order(i), 0)),
            pl.BlockSpec((1, TM, GLA_WIDTH), lambda b, i: (b, order(i), v_blk)),
            pl.BlockSpec((1, TM, GLA_QK), lambda b, i: (b, order(i), la_blk)),
            pl.BlockSpec((TM, GLA_QK), lambda b, i: (order(i), 0)),
            pl.BlockSpec((TM, GLA_QK), lambda b, i: (order(i), 0)),
        ]

    return pl.pallas_call(
        _gla_kernel,
        grid=(bsz, nt),
        in_specs=specs(fwd, la_f_blk) + specs(bwd, la_b_blk),
        out_specs=[
            pl.BlockSpec((1, TM, GLA_WIDTH), lambda b, i: (b, fwd(i), 0)),
            pl.BlockSpec((1, TM, GLA_WIDTH), lambda b, i: (b, bwd(i), 0)),
        ],
        out_shape=[jax.ShapeDtypeStruct((bsz, s, GLA_WIDTH), F32)] * 2,
        scratch_shapes=[pltpu.VMEM((GLA_WIDTH, GLA_QK), F32)] * 2,
        compiler_params=_params("arbitrary", "arbitrary"),
        name="gla_bidir",
    )(gg, gg, gg, cos, sin, gg, gg, gg, cos, sin)


def _store_row_tiles(ref, val):
    rows, width = val.shape
    pieces = width // LANES
    for s in range(pieces):
        ref[pl.ds(s, rows, stride=pieces), :] = val[:, s * LANES:(s + 1) * LANES]


def _load_row_tiles(ref, rows, pieces):
    return jnp.concatenate([ref[pl.ds(s, rows, stride=pieces), :] for s in range(pieces)], axis=1)


def _row_index(shape):
    return lax.broadcasted_iota(jnp.int32, shape, 0).astype(F32)


def _stack_rows(rows):
    shape = (len(rows), rows[0].shape[1])
    rid = lax.broadcasted_iota(jnp.int32, shape, 0)
    out = jnp.zeros(shape, rows[0].dtype)
    for k, r in enumerate(rows):
        out = jnp.where(rid == k, r, out)
    return out


def _select_experts(scores, biased):
    n_exp, n_tok = biased.shape
    per = n_exp // N_GROUPS
    neg = -jnp.inf
    sub = _row_index((per, n_tok))
    grp_rows = []
    for g in range(N_GROUPS):
        blk = biased[g * per:(g + 1) * per, :]
        m1 = jnp.max(blk, axis=0, keepdims=True)
        i1 = jnp.min(jnp.where(blk == m1, sub, float(per)), axis=0, keepdims=True)
        m2 = jnp.max(jnp.where(sub == i1, neg, blk), axis=0, keepdims=True)
        grp_rows.append(m1 + m2)
    cur = _stack_rows(grp_rows)
    gid = _row_index(cur.shape)
    picked = jnp.zeros(cur.shape, F32)
    for _ in range(TOPK_GROUPS):
        gm = jnp.max(cur, axis=0, keepdims=True)
        gi = jnp.min(jnp.where(cur == gm, gid, float(N_GROUPS)), axis=0, keepdims=True)
        hit = gid == gi
        picked = jnp.where(hit, 1.0, picked)
        cur = jnp.where(hit, neg, cur)
    cur = jnp.concatenate(
        [jnp.where(picked[g:g + 1, :] > 0.0, biased[g * per:(g + 1) * per, :], neg) for g in range(N_GROUPS)], axis=0)
    eid = _row_index(cur.shape)
    ids, vals = [], []
    chosen = jnp.zeros(cur.shape, F32)
    for _ in range(TOP_K):
        m = jnp.max(cur, axis=0, keepdims=True)
        idx = jnp.min(jnp.where(cur == m, eid, float(n_exp)), axis=0, keepdims=True)
        hit = eid == idx
        ids.append(idx)
        vals.append(jnp.sum(jnp.where(hit, scores, 0.0), axis=0, keepdims=True))
        chosen = jnp.where(hit, 1.0, chosen)
        cur = jnp.where(hit, neg, cur)
    return ids, vals, chosen


def _outproj_kernel(x_ref, xc_ref, oa_ref, ob_ref, of_ref, obk_ref, r_ref, gg_ref, wo_ref, mod_ref, gf_ref, wr_ref, br_ref,
                    tri_ref, xo_ref, h_ref, e_ref, w_ref, rk_ref, hist_ref, *, n_lat_tiles):
    o = of_ref[0] + obk_ref[0]
    head = lax.broadcasted_iota(jnp.int32, o.shape, 1) // GLA_DV
    o2 = o * o
    rs = jnp.zeros_like(o)
    for h in range(GLA_HEADS):
        ssq = jnp.sum(jnp.where(head == h, o2, 0.0), axis=-1, keepdims=True)
        rs = jnp.where(head == h, lax.rsqrt(ssq / GLA_DV + NORM_EPS), rs)
    oc = (o * rs * gg_ref[...] * _silu(r_ref[0])).astype(BF16)
    c1 = NA_WIDTH + POOL_WIDTH
    acc = _dot(oa_ref[0], wo_ref[0:NA_WIDTH]) + _dot(ob_ref[0], wo_ref[NA_WIDTH:c1]) + _dot(oc, wo_ref[c1:])
    x = _stream_tile(x_ref, xc_ref, n_lat_tiles) + mod_ref[0, 0, 2:3, :] * acc
    xo_ref[0] = x
    h = _rmsnorm(x, gf_ref[...]) * (1.0 + mod_ref[0, 0, 4:5, :]) + mod_ref[0, 0, 3:4, :]
    _store_row_tiles(h_ref, h)

    scores = _sigmoid(_dot_nt(wr_ref[...], h.astype(BF16)))
    ids, vals, chosen = _select_experts(scores, scores + br_ref[...])
    total = vals[0]
    for v in vals[1:]:
        total = total + v
    w_ref[...] = _stack_rows([v / total * ROUTED_SCALE for v in vals])
    e_ref[...] = _stack_rows(ids).astype(jnp.int32)
    chosen_b = chosen.astype(BF16)
    before = _dot(chosen_b, tri_ref[...])
    eid = _row_index(before.shape)
    rk_ref[...] = _stack_rows([jnp.sum(jnp.where(eid == idx, before, 0.0), axis=0, keepdims=True) for idx in ids])
    hist_ref[0] = _dot_nt(jnp.ones((8, chosen.shape[1]), BF16), chosen_b)


def _outproj(x_lat, x_ctx, ctx_blk, oa, ob, o_f, o_b, gg, g_gla4, w_out, mod, g_ffn, w_router_t, b_router,
             n_lat_tiles, n_tiles):
    bsz, _, d = x_lat.shape
    n_exp = w_router_t.shape[0]
    r_blk = (2 * GLA_QK + GLA_WIDTH) // GLA_WIDTH
    tile = lambda w: pl.BlockSpec((1, TM, w), lambda b, i: (b, i, 0))
    const = lambda b, i: (0, 0)
    per_tok = pl.BlockSpec((TOP_K, TM), lambda b, i: (0, b * n_tiles + i))
    t_tok = bsz * n_tiles * TM
    tri = (jnp.arange(TM)[:, None] < jnp.arange(TM)[None, :]).astype(BF16)
    return pl.pallas_call(
        functools.partial(_outproj_kernel, n_lat_tiles=n_lat_tiles),
        grid=(bsz, n_tiles),
        in_specs=[
            *_stream_specs(d, n_lat_tiles, ctx_blk),
            tile(NA_WIDTH), tile(POOL_WIDTH), tile(GLA_WIDTH), tile(GLA_WIDTH),
            pl.BlockSpec((1, TM, GLA_WIDTH), lambda b, i: (b, i, r_blk)),
            pl.BlockSpec((1, GLA_WIDTH), const),
            pl.BlockSpec(w_out.shape, const),
            pl.BlockSpec((1, 1, 8, d), lambda b, i: (b, jnp.where(i >= n_lat_tiles, 0, 1), 0, 0)),
            pl.BlockSpec((1, d), const),
            pl.BlockSpec(w_router_t.shape, const),
            pl.BlockSpec((n_exp, 1), const),
            pl.BlockSpec((TM, TM), const),
        ],
        out_specs=[tile(d), pl.BlockSpec((TM * d // LANES, LANES), lambda b, i: (b * n_tiles + i, 0)),
                   per_tok, per_tok, per_tok,
                   pl.BlockSpec((1, 8, n_exp), lambda b, i: (b * n_tiles + i, 0, 0))],
        out_shape=[
            jax.ShapeDtypeStruct((bsz, n_tiles * TM, d), F32),
            jax.ShapeDtypeStruct((t_tok * d // LANES, LANES), F32),
            jax.ShapeDtypeStruct((TOP_K, t_tok), jnp.int32),
            jax.ShapeDtypeStruct((TOP_K, t_tok), F32),
            jax.ShapeDtypeStruct((TOP_K, t_tok), F32),
            jax.ShapeDtypeStruct((bsz * n_tiles, 8, n_exp), F32),
        ],
        compiler_params=_params("parallel", "parallel"),
        name="outproj_router",
    )(x_lat, x_ctx, oa, ob, o_f, o_b, gg, g_gla4, w_out, mod, g_ffn, w_router_t, b_router, tri)


def _slot_layout(hist):
    n_tiles, n_exp = hist.shape
    counts = hist.sum(0)
    tile_base = jnp.cumsum(hist, axis=0) - hist
    padded = (counts + MOE_BLOCK - 1) // MOE_BLOCK * MOE_BLOCK
    pad_end = jnp.cumsum(padded)
    pad_start = pad_end - padded
    base = (pad_start[None, :] + tile_base).astype(F32).reshape(n_tiles, n_exp, 1)
    n_blocks = -(-(n_tiles * TM * TOP_K) // MOE_BLOCK) + n_exp
    n_used = (pad_end[-1] // MOE_BLOCK).astype(jnp.int32)
    fill = jnp.concatenate([jnp.maximum(pad_end // MOE_BLOCK - 1, 0).astype(jnp.int32),
                            jnp.minimum(n_used + jnp.arange(n_exp, dtype=jnp.int32), n_blocks - 1)])
    first_blk = (pad_start // MOE_BLOCK).astype(jnp.int32)
    blk_count = (padded // MOE_BLOCK).astype(jnp.int32)
    return base, first_blk, blk_count, fill, n_used.reshape(1), n_blocks


def _pos_kernel(e_ref, rk_ref, base_ref, pos_ref, *, pieces):
    eid = lax.broadcasted_iota(jnp.int32, (base_ref.shape[1], e_ref.shape[1]), 0)
    base = base_ref[0]
    rows = [jnp.sum(jnp.where(eid == e_ref[k:k + 1, :], base, 0.0), axis=0, keepdims=True) for k in range(TOP_K)]
    pos_ref[0] = ((_stack_rows(rows) + rk_ref[...]) * float(pieces)).astype(jnp.int32)


def _slot_positions(e_t, rank_t, base, pieces):
    n_tiles, n_exp, _ = base.shape
    per_tok = pl.BlockSpec((TOP_K, TM), lambda i: (0, i))
    return pl.pallas_call(
        functools.partial(_pos_kernel, pieces=pieces),
        grid=(n_tiles,),
        in_specs=[per_tok, per_tok, pl.BlockSpec((1, n_exp, 1), lambda i: (i, 0, 0))],
        out_specs=pl.BlockSpec((1, TOP_K, TM), lambda i: (i, 0, 0)),
        out_shape=jax.ShapeDtypeStruct((n_tiles, TOP_K, TM), jnp.int32),
        compiler_params=_params("parallel"),
        name="moe_positions",
    )(e_t, rank_t, base).reshape(-1)


def _dispatch_kernel(fill_ref, pos_ref, h_ref, xs_hbm, zbuf, sem, zsem):
    @pl.when(pl.program_id(0) == 0)
    def _():
        zbuf[...] = jnp.zeros_like(zbuf)
        n_fill = fill_ref.shape[0]

        def is_new(j):
            return (j == 0) | (fill_ref[j] != fill_ref[jnp.maximum(j - 1, 0)])

        def start(j, c):
            @pl.when(is_new(j))
            def _():
                row0 = pl.multiple_of(fill_ref[j] * zbuf.shape[0], zbuf.shape[0])
                pltpu.make_async_copy(zbuf, xs_hbm.at[pl.ds(row0, zbuf.shape[0]), :], zsem).start()
            return c
        lax.fori_loop(0, n_fill, start, 0)

        def wait(j, c):
            @pl.when(is_new(j))
            def _():
                pltpu.make_async_copy(zbuf, xs_hbm.at[pl.ds(0, zbuf.shape[0]), :], zsem).wait()
            return c
        lax.fori_loop(0, n_fill, wait, 0)

    pieces = h_ref.shape[0] // TM

    def body(t, c):
        src = h_ref.at[pl.ds(pl.multiple_of(t * pieces, pieces), pieces), :]
        for k in range(TOP_K):
            dst = xs_hbm.at[pl.ds(pl.multiple_of(pos_ref[k * TM + t], pieces), pieces), :]
            pltpu.make_async_copy(src, dst, sem).start(priority=k % 2)
        return c
    lax.fori_loop(0, TM, body, 0, unroll=2)
    for _ in range(TOP_K):
        pltpu.make_async_copy(h_ref, xs_hbm.at[pl.ds(0, h_ref.shape[0]), :], sem).wait()


def _dispatch(h2t, pos, fill, n_slots, pieces):
    n_tiles = h2t.shape[0] // (TM * pieces)
    grid_spec = pltpu.PrefetchScalarGridSpec(
        num_scalar_prefetch=1,
        grid=(n_tiles,),
        in_specs=[
            pl.BlockSpec((TOP_K * TM,), lambda i, fill: (i,), memory_space=pltpu.SMEM),
            pl.BlockSpec((TM * pieces, LANES), lambda i, fill: (i, 0)),
        ],
        out_specs=pl.BlockSpec(memory_space=pl.ANY),
        scratch_shapes=[pltpu.VMEM((MOE_BLOCK * pieces, LANES), F32), pltpu.SemaphoreType.DMA,
                        pltpu.SemaphoreType.DMA],
    )
    return pl.pallas_call(
        _dispatch_kernel,
        grid_spec=grid_spec,
        out_shape=jax.ShapeDtypeStruct((n_slots * pieces, LANES), F32),
        compiler_params=_params("arbitrary"),
        name="moe_dispatch",
    )(fill, pos, h2t)


def _expert_kernel(first_ref, cnt_ref, nu_ref, wi_ref, wd_ref, xs_hbm, ys_hbm, xbuf, ybuf, wi_bf, wd_bf, xsem, ysem,
                   *, n_blocks):
    e = pl.program_id(0)
    n_used = nu_ref[0]
    rows = xbuf.shape[1]
    pieces = rows // MOE_BLOCK
    d_exp = wd_ref.shape[2]

    def x_copy(g, s):
        return pltpu.make_async_copy(xs_hbm.at[pl.ds(pl.multiple_of(g * rows, rows), rows), :], xbuf.at[s], xsem.at[s])

    def y_copy(g, s):
        return pltpu.make_async_copy(ybuf.at[s], ys_hbm.at[pl.ds(pl.multiple_of(g * rows, rows), rows), :], ysem.at[s])

    n_xbuf = xbuf.shape[0]
    n_ybuf = ybuf.shape[0]

    @pl.when(e == 0)
    def _():
        for g in range(n_xbuf - 1):
            @pl.when(g < n_used)
            def _():
                x_copy(g, g).start()

    @pl.when(cnt_ref[e] > 0)
    def _():
        wi_bf[...] = wi_ref[0, 0].astype(BF16)
        wd_bf[...] = wd_ref[0, 0].astype(BF16)

    def block(j, c):
        g = first_ref[e] + j
        s = g % n_ybuf
        x_copy(g, g % n_xbuf).wait()

        @pl.when(g + n_xbuf - 1 < n_used)
        def _():
            x_copy(g + n_xbuf - 1, (g + n_xbuf - 1) % n_xbuf).start()

        @pl.when(g >= n_ybuf)
        def _():
            y_copy(g - n_ybuf, s).wait()

        hh = _dot(_load_row_tiles(xbuf.at[g % n_xbuf], MOE_BLOCK, pieces).astype(BF16), wi_bf[...])
        a = (_silu(hh[:, :d_exp]) * hh[:, d_exp:]).astype(BF16)
        _store_row_tiles(ybuf.at[s], _dot(a, wd_bf[...]))
        y_copy(g, s).start(priority=1)
        return c
    lax.fori_loop(0, cnt_ref[e], block, 0)

    @pl.when(e == pl.num_programs(0) - 1)
    def _():
        for back in range(1, n_ybuf + 1):
            @pl.when(n_used >= back)
            def _():
                y_copy(n_used - back, (n_used - back) % n_ybuf).wait()

        ybuf[0] = jnp.zeros(ybuf.shape[1:], ybuf.dtype)

        def start(g, c):
            y_copy(g, 0).start()
            return c
        lax.fori_loop(n_used, n_blocks, start, 0)

        def wait(g, c):
            y_copy(g, 0).wait()
            return c
        lax.fori_loop(n_used, n_blocks, wait, 0)


def _moe_experts(xs, first_blk, blk_count, n_used, n_blocks, w_e_in, w_e_down, layer):
    _, n_exp, d, f2 = w_e_in.shape
    rows = xs.shape[0] // n_blocks
    grid_spec = pltpu.PrefetchScalarGridSpec(
        num_scalar_prefetch=3,
        grid=(n_exp,),
        in_specs=[
            pl.BlockSpec((1, 1, d, f2), lambda e, first, cnt, nu: (layer, e, 0, 0)),
            pl.BlockSpec((1, 1, f2 // 2, d), lambda e, first, cnt, nu: (layer, e, 0, 0)),
            pl.BlockSpec(memory_space=pl.ANY),
        ],
        out_specs=pl.BlockSpec(memory_space=pl.ANY),
        scratch_shapes=[
            pltpu.VMEM((4, rows, LANES), F32),
            pltpu.VMEM((3, rows, LANES), F32),
            pltpu.VMEM((d, f2), BF16),
            pltpu.VMEM((f2 // 2, d), BF16),
            pltpu.SemaphoreType.DMA((4,)),
            pltpu.SemaphoreType.DMA((3,)),
        ],
    )
    return pl.pallas_call(
        functools.partial(_expert_kernel, n_blocks=n_blocks),
        grid_spec=grid_spec,
        out_shape=jax.ShapeDtypeStruct(xs.shape, F32),
        compiler_params=_params("arbitrary"),
        name="moe_experts",
    )(first_blk, blk_count, n_used, w_e_in, w_e_down, xs)


def _combine_kernel(pos0_ref, posn_ref, x_ref, h_ref, w_ref, wsi_ref, wsd_ref, mod_ref, gfin_ref, ys_hbm, o_ref,
                    ybuf, sem, *, final, n_steps):
    step = pl.program_id(0) * pl.num_programs(1) + pl.program_id(1)
    slot = step % 2
    d_exp = wsd_ref.shape[0]
    pieces = h_ref.shape[0] // TMC

    def gather(pos_ref, st, s):
        off = (st % (TM // TMC)) * TMC
        def body(t, c):
            for k in range(TOP_K):
                src = ys_hbm.at[pl.ds(pl.multiple_of(pos_ref[k * TM + off + t], pieces), pieces), :]
                dst = ybuf.at[s, k, pl.ds(pl.multiple_of(t * pieces, pieces), pieces), :]
                pltpu.make_async_copy(src, dst, sem.at[s]).start(priority=k % 2)
            return c
        lax.fori_loop(0, TMC, body, 0, unroll=2)

    @pl.when(step == 0)
    def _():
        gather(pos0_ref, 0, 0)

    @pl.when(step + 1 < n_steps)
    def _():
        gather(posn_ref, step + 1, 1 - slot)

    hh = _dot(_load_row_tiles(h_ref, TMC, pieces).astype(BF16), wsi_ref[...])
    a = (_silu(hh[:, :d_exp]) * hh[:, d_exp:]).astype(BF16)
    y = _dot(a, wsd_ref[...])
    for k in range(TOP_K):
        pltpu.make_async_copy(ys_hbm.at[pl.ds(0, TMC * pieces), :], ybuf.at[slot, k], sem.at[slot]).wait()
    w = w_ref[...]
    for k in range(TOP_K):
        y = y + _load_row_tiles(ybuf.at[slot, k], TMC, pieces) * w[:, k:k + 1]
    x = x_ref[0] + mod_ref[0, 0, 5:6, :] * y
    if final:
        x = _rmsnorm(x, gfin_ref[...])
    o_ref[0] = x


def _combine(x_mid, h2t, ys, pos, w_tok, w_sh_in, w_sh_down, mod, g_final, n_lat_tiles, final):
    bsz, s, d = x_mid.shape
    nt = s // TMC
    n_steps = bsz * nt
    halves = TM // TMC
    lat_steps = n_lat_tiles * halves
    pieces = d // LANES
    const = lambda b, i: (0, 0)
    tile = pl.BlockSpec((1, TMC, d), lambda b, i: (b, i, 0))
    smem_blk = lambda f: pl.BlockSpec((TOP_K * TM,), f, memory_space=pltpu.SMEM)
    return pl.pallas_call(
        functools.partial(_combine_kernel, final=final, n_steps=n_steps),
        grid=(bsz, nt),
        in_specs=[
            smem_blk(lambda b, i: (0,)),
            smem_blk(lambda b, i: (jnp.minimum(b * nt + i + 1, n_steps - 1) // halves,)),
            tile,
            pl.BlockSpec((TMC * pieces, LANES), lambda b, i: (b * nt + i, 0)),
            pl.BlockSpec((TMC, TOP_K), lambda b, i: (b * nt + i, 0)),
            pl.BlockSpec(w_sh_in.shape, const),
            pl.BlockSpec(w_sh_down.shape, const),
            pl.BlockSpec((1, 1, 8, d), lambda b, i: (b, jnp.where(i >= lat_steps, 0, 1), 0, 0)),
            pl.BlockSpec((1, d), const),
            pl.BlockSpec(memory_space=pl.ANY),
        ],
        out_specs=tile,
        out_shape=jax.ShapeDtypeStruct((bsz, s, d), F32),
        scratch_shapes=[pltpu.VMEM((2, TOP_K, TMC * pieces, LANES), F32), pltpu.SemaphoreType.DMA((2,))],
        compiler_params=_params("arbitrary", "arbitrary"),
        name="moe_combine",
    )(pos, pos, x_mid, h2t, w_tok, w_sh_in, w_sh_down, mod, g_final, ys)


def _block_diag(w):
    g, a, b = w.shape
    out = jnp.zeros((g * a, g * b), w.dtype)
    for j in range(g):
        out = out.at[j * a:(j + 1) * a, j * b:(j + 1) * b].set(w[j])
    return out


def kernel(x, c, ctx, c_ctx, w_mod, b_mod, g_mix, w_in, rpb, w_pool, pool_scale, w_gate_f, b_gate_f, w_gate_b, b_gate_b, g_gla, w_out, g_ffn, w_router, b_router, w_e_in, w_e_down, w_sh_in, w_sh_down, g_final):
    bsz, seq, d = x.shape
    n_ctx = ctx.shape[1]
    depth = w_mod.shape[0]
    n_exp = w_router.shape[2]
    assert n_ctx == TM and seq % TM == 0 and seq % GRID_W == 0
    n_lat = seq // TM
    s = seq + n_ctx

    n_rows = -(-(bsz + 1) // 8) * 8
    cc = jnp.zeros((n_rows, d), F32).at[:bsz].set(c).at[bsz].set(c_ctx)
    mod_all = _modulation(cc, w_mod, b_mod).reshape(depth, n_rows, 6, d)
    cos, sin = _rope_tables(seq, s)

    x_lat, x_ctx, ctx_blk = x, ctx, 0
    for layer in range(depth):
        last = layer == depth - 1
        n_tiles = n_lat if last else n_lat + 1
        m = mod_all[layer]
        mod = jnp.stack([jnp.broadcast_to(m[bsz], (bsz, 6, d)), m[:bsz]], axis=1)
        mod = jnp.pad(mod, ((0, 0), (0, 0), (0, 2), (0, 0)))

        wl = w_in[layer]
        w_main = wl[:, :MAIN_W].astype(BF16)
        w_low = jnp.pad(wl[:, MAIN_W:], ((0, 0), (0, LANES - 2 * GLA_GATE_RANK))).astype(BF16)
        w2 = jnp.zeros((LANES, 2 * GLA_QK), F32)
        w2 = w2.at[:GLA_GATE_RANK, :GLA_QK].set(w_gate_f[layer])
        w2 = w2.at[GLA_GATE_RANK:2 * GLA_GATE_RANK, GLA_QK:].set(w_gate_b[layer]).astype(BF16)
        b2 = jnp.concatenate([b_gate_f[layer], b_gate_b[layer]])[None, :]
        qkv, u, gg = _inproj(x_lat, x_ctx, ctx_blk, mod, g_mix[layer][None, :], w_main, w_low, w2, b2, n_lat)

        oa = _neighborhood_attention(qkv, _na_bias_table(rpb[layer]), n_lat, not last)
        ob = _multiscale_pool(u, _block_diag(w_pool[layer]).astype(BF16), pool_scale[layer][None, :], n_lat, n_tiles)
        o_f, o_b = _gla(gg, cos, sin, n_lat)

        x_mid, h2t, e_t, w_t, rank_t, hist = _outproj(
            x_lat, x_ctx, ctx_blk, oa, ob, o_f, o_b, gg, jnp.tile(g_gla[layer], GLA_HEADS)[None, :], w_out[layer].astype(BF16), mod,
            g_ffn[layer][None, :], w_router[layer].T.astype(BF16), b_router[layer].reshape(n_exp, 1), n_lat, n_tiles)
        pieces = d // LANES
        base, first_blk, blk_count, fill, n_used, n_blocks = _slot_layout(hist[:, 0, :].astype(jnp.int32))
        pos = _slot_positions(e_t, rank_t, base, pieces)
        xs = _dispatch(h2t, pos, fill, n_blocks * MOE_BLOCK, pieces)
        ys = _moe_experts(xs, first_blk, blk_count, n_used, n_blocks, w_e_in, w_e_down, layer)
        xa = _combine(x_mid, h2t, ys, pos, w_t.T, w_sh_in[layer].astype(BF16), w_sh_down[layer].astype(BF16), mod,
                      g_final[None, :], n_lat, last)
        x_lat, x_ctx, ctx_blk = xa, xa, n_lat
    return xa
```

```python
import functools

import jax
import jax.numpy as jnp
import numpy as np
from jax import lax
from jax.experimental import pallas as pl
from jax.experimental.pallas import tpu as pltpu

GRID_W = 64
NORM_EPS = 1e-6
NA_HEADS = 8
NA_HEAD_DIM = 64
NA_WIDTH = NA_HEADS * NA_HEAD_DIM
NA_WIN_ROWS = 8
NA_WIN_COLS = 16
POOL_WINDOWS = (2, 4, 8, 16)
POOL_GROUP_DIM = 64
POOL_WIDTH = len(POOL_WINDOWS) * POOL_GROUP_DIM
POOL_REACH = max(POOL_WINDOWS) // 2
assert POOL_WINDOWS == tuple(2 ** (g + 1) for g in range(len(POOL_WINDOWS)))
GLA_HEADS = 4
GLA_DK = 32
GLA_DV = 64
GLA_QK = GLA_HEADS * GLA_DK
GLA_WIDTH = GLA_HEADS * GLA_DV
GLA_GATE_RANK = 16
GLA_TAU = 16.0
GLA_CHUNK = 64
ROPE_BASE = 10000.0
N_EXPERTS = 256
TOP_K = 8
N_GROUPS = 8
TOPK_GROUPS = 4
ROUTED_SCALE = 2.5

TM = 256
TMC = 128
MOE_BLOCK = 256
LANES = 128
MXU_DIM = 256
MASK_VALUE = -1e30
VMEM_LIMIT = 48 * 1024 * 1024

QKV_W = 3 * NA_WIDTH
G_W = 2 * GLA_QK + 2 * GLA_WIDTH + 2 * GLA_QK
MAIN_W = QKV_W + POOL_WIDTH + 2 * GLA_QK + 2 * GLA_WIDTH

BF16 = jnp.bfloat16
F32 = jnp.float32


def _params(*sem):
    return pltpu.CompilerParams(dimension_semantics=sem, vmem_limit_bytes=VMEM_LIMIT)


def _sigmoid(x):
    return 1.0 / (1.0 + jnp.exp(-x))


def _silu(x):
    return x * _sigmoid(x)


def _rmsnorm(x, g):
    return x * lax.rsqrt(jnp.mean(x * x, axis=-1, keepdims=True) + NORM_EPS) * g


def _dot(a, b):
    return jnp.dot(a, b, preferred_element_type=F32)


def _dot_nt(a, b):
    return lax.dot_general(a, b, (((1,), (1,)), ((), ())), preferred_element_type=F32)


def _dot_tn(a, b):
    return lax.dot_general(a, b, (((0,), (0,)), ((), ())), preferred_element_type=F32)


def _mod_kernel(c_ref, w_ref, b_ref, o_ref):
    a = _silu(c_ref[...]).astype(BF16)
    o_ref[0] = _dot(a, w_ref[0].astype(BF16)) + b_ref[0]


def _modulation(cc, w_mod, b_mod):
    depth, d, n = w_mod.shape
    r = cc.shape[0]
    tn = 1024
    return pl.pallas_call(
        _mod_kernel,
        grid=(depth, n // tn),
        in_specs=[
            pl.BlockSpec((r, d), lambda l, j: (0, 0)),
            pl.BlockSpec((1, d, tn), lambda l, j: (l, 0, j)),
            pl.BlockSpec((1, 1, tn), lambda l, j: (l, 0, j)),
        ],
        out_specs=pl.BlockSpec((1, r, tn), lambda l, j: (l, 0, j)),
        out_shape=jax.ShapeDtypeStruct((depth, r, n), F32),
        compiler_params=_params("parallel", "parallel"),
        name="modulation",
    )(cc, w_mod, b_mod.reshape(depth, 1, n))


def _stream_tile(x_ref, xc_ref, n_lat_tiles):
    return jnp.where(pl.program_id(1) >= n_lat_tiles, xc_ref[0], x_ref[0])


def _stream_specs(d, n_lat_tiles, ctx_blk):
    return [pl.BlockSpec((1, TM, d), lambda b, i: (b, jnp.minimum(i, n_lat_tiles - 1), 0)),
            pl.BlockSpec((1, TM, d), lambda b, i: (b, ctx_blk, 0))]


def _inproj_kernel(x_ref, xc_ref, mod_ref, g_ref, wm_ref, wl_ref, w2_ref, b2_ref, qkv_ref, u_ref, gg_ref, *, n_lat_tiles):
    _inproj_tile(_stream_tile(x_ref, xc_ref, n_lat_tiles), mod_ref, g_ref, wm_ref, wl_ref, w2_ref, b2_ref,
                 qkv_ref, u_ref, gg_ref)


def _inproj_tile(x, mod_ref, g_ref, wm_ref, wl_ref, w2_ref, b2_ref, qkv_ref, u_ref, gg_ref):
    h = _rmsnorm(x, g_ref[...]) * (1.0 + mod_ref[0, 0, 1:2, :]) + mod_ref[0, 0, 0:1, :]
    hb = h.astype(BF16)
    q = _dot(hb, wm_ref[:, 0:NA_WIDTH]) * (NA_HEAD_DIM ** -0.5)
    qkv_ref[0, :, 0:NA_WIDTH] = q.astype(BF16)
    qkv_ref[0, :, NA_WIDTH:2 * NA_WIDTH] = _dot(hb, wm_ref[:, NA_WIDTH:2 * NA_WIDTH]).astype(BF16)
    qkv_ref[0, :, 2 * NA_WIDTH:QKV_W] = _dot(hb, wm_ref[:, 2 * NA_WIDTH:QKV_W]).astype(BF16)
    u_ref[0] = _dot(hb, wm_ref[:, QKV_W:QKV_W + POOL_WIDTH])
    c0 = QKV_W + POOL_WIDTH
    gg_ref[0, :, 0:GLA_QK] = _dot(hb, wm_ref[:, c0:c0 + GLA_QK]) * (GLA_DK ** -0.5)
    gg_ref[0, :, GLA_QK:2 * GLA_QK + 2 * GLA_WIDTH] = _dot(hb, wm_ref[:, c0 + GLA_QK:MAIN_W])
    a_low = _dot(hb, wl_ref[...]).astype(BF16)
    lg = _dot(a_low, w2_ref[...]) + b2_ref[...]
    log_sig = jnp.minimum(lg, 0.0) - jnp.log1p(jnp.exp(-jnp.abs(lg)))
    gg_ref[0, :, 2 * GLA_QK + 2 * GLA_WIDTH:G_W] = log_sig / GLA_TAU


def _inproj(x_lat, x_ctx, ctx_blk, mod, g, w_main, w_low, w2, b2, n_lat_tiles):
    bsz, _, d = x_lat.shape
    nt = n_lat_tiles + 1
    s = nt * TM
    const = lambda b, i: (0, 0)
    return pl.pallas_call(
        functools.partial(_inproj_kernel, n_lat_tiles=n_lat_tiles),
        grid=(bsz, nt),
        in_specs=_stream_specs(d, n_lat_tiles, ctx_blk) + [
            pl.BlockSpec((1, 1, 8, d), lambda b, i: (b, jnp.where(i >= n_lat_tiles, 0, 1), 0, 0)),
            pl.BlockSpec((1, d), const),
            pl.BlockSpec(w_main.shape, const),
            pl.BlockSpec(w_low.shape, const),
            pl.BlockSpec(w2.shape, const),
            pl.BlockSpec(b2.shape, const),
        ],
        out_specs=[
            pl.BlockSpec((1, TM, QKV_W), lambda b, i: (b, i, 0)),
            pl.BlockSpec((1, TM, POOL_WIDTH), lambda b, i: (b, i, 0)),
            pl.BlockSpec((1, TM, G_W), lambda b, i: (b, i, 0)),
        ],
        out_shape=[
            jax.ShapeDtypeStruct((bsz, s, QKV_W), BF16),
            jax.ShapeDtypeStruct((bsz, s, POOL_WIDTH), F32),
            jax.ShapeDtypeStruct((bsz, s, G_W), F32),
        ],
        compiler_params=_params("parallel", "parallel"),
        name="inproj",
    )(x_lat, x_ctx, mod, g, w_main, w_low, w2, b2)


HEADS_PER_GROUP = MXU_DIM // NA_HEAD_DIM
NA_GROUPS = NA_HEADS // HEADS_PER_GROUP
NA_LOCAL_KEYS = NA_WIN_ROWS * GRID_W
NA_ROWS_PER_STEP = 4


def _stack_heads(x, width):
    lane = lax.broadcasted_iota(jnp.int32, x.shape, 1) // width
    n_heads = x.shape[1] // width
    return jnp.concatenate([jnp.where(lane == h, x, jnp.zeros_like(x)) for h in range(n_heads)], axis=0)


def _unstack_heads(o, width):
    n_heads = o.shape[1] // width
    r = o.shape[0] // n_heads
    lane = lax.broadcasted_iota(jnp.int32, (r, o.shape[1]), 1) // width
    acc = jnp.zeros((r, o.shape[1]), o.dtype)
    for h in range(n_heads):
        acc = jnp.where(lane == h, o[h * r:(h + 1) * r, :], acc)
    return acc


def _na_kernel(q_ref, k_ref, v_ref, kc_ref, vc_ref, *rest, n_rows):
    bias_refs, o_ref = rest[:-1], rest[-1]
    rows_per_step = len(bias_refs)
    step = pl.program_id(1)

    def attend(g, j, local):
        r = step * rows_per_step + j
        rows = slice(j * GRID_W, (j + 1) * GRID_W)
        cols = slice(g * MXU_DIM, (g + 1) * MXU_DIM)
        qs = _stack_heads(q_ref[0, rows, cols], NA_HEAD_DIM)
        kc = kc_ref[0, :, cols]
        vc = vc_ref[0, :, cols]
        s_ctx = _dot_nt(qs, kc)
        m = jnp.max(s_ctx, axis=-1, keepdims=True)
        if local:
            start = pl.multiple_of(jnp.clip(r - NA_WIN_ROWS // 2, 0, n_rows - NA_WIN_ROWS) * GRID_W, GRID_W)
            kw = k_ref[0, pl.ds(start, NA_LOCAL_KEYS), cols]
            vw = v_ref[0, pl.ds(start, NA_LOCAL_KEYS), cols]
            s_loc = _dot_nt(qs, kw) + bias_refs[j][0, g]
            m = jnp.maximum(m, jnp.max(s_loc, axis=-1, keepdims=True))
            p_loc = jnp.exp(s_loc - m)
        p_ctx = jnp.exp(s_ctx - m)
        den = jnp.sum(p_ctx, axis=-1, keepdims=True)
        o = _dot(p_ctx.astype(BF16), vc)
        if local:
            den = den + jnp.sum(p_loc, axis=-1, keepdims=True)
            o = o + _dot(p_loc.astype(BF16), vw)
        o_ref[0, rows, cols] = _unstack_heads(o / den, NA_HEAD_DIM).astype(o_ref.dtype)

    @pl.when(step * rows_per_step < n_rows)
    def _():
        for j in range(rows_per_step):
            for g in range(NA_GROUPS):
                attend(g, j, True)

    @pl.when(step * rows_per_step >= n_rows)
    def _():
        for j in range(rows_per_step):
            for g in range(NA_GROUPS):
                attend(g, j, False)


def _na_bias_table(rpb):
    o = jnp.arange(NA_WIN_ROWS)
    dr = o[None, :] - o[:, None] + NA_WIN_ROWS - 1
    col = jnp.arange(GRID_W)
    dc = jnp.clip(col[None, :] - col[:, None], -(NA_WIN_COLS - 1), NA_WIN_COLS - 1) + NA_WIN_COLS - 1
    col_start = jnp.clip(col - NA_WIN_COLS // 2, 0, GRID_W - NA_WIN_COLS)
    col_mask = (col[None, :] >= col_start[:, None]) & (col[None, :] < col_start[:, None] + NA_WIN_COLS)
    onehot = (dc[None] == jnp.arange(2 * NA_WIN_COLS - 1)[:, None, None]).astype(F32)
    t = jnp.einsum('howc,cqk->howqk', rpb[:, dr].astype(F32), onehot,
                   precision=lax.Precision.HIGHEST)
    t = jnp.where(col_mask[None, None, None], t, MASK_VALUE)
    t = t.transpose(1, 0, 3, 2, 4)
    return t.reshape(NA_WIN_ROWS, NA_GROUPS, HEADS_PER_GROUP * GRID_W, NA_LOCAL_KEYS)


def _neighborhood_attention(qkv, bias_tab, n_lat_tiles, with_ctx_queries):
    bsz, s, _ = qkv.shape
    seq = n_lat_tiles * TM
    n_rows = seq // GRID_W
    assert n_rows >= NA_WIN_ROWS
    n_ctx_rows = (s - seq) // GRID_W
    rq = NA_ROWS_PER_STEP
    assert n_rows % rq == 0 and n_ctx_rows % rq == 0
    nq = (n_rows + (n_ctx_rows if with_ctx_queries else 0)) // rq
    ctx_blk = seq // (s - seq)

    def bias_spec(j):
        def idx(b, i):
            rr = jnp.minimum(i * rq + j, n_rows - 1)
            return (rr - jnp.clip(rr - NA_WIN_ROWS // 2, 0, n_rows - NA_WIN_ROWS), 0, 0, 0)
        return pl.BlockSpec((1,) + bias_tab.shape[1:], idx)

    return pl.pallas_call(
        functools.partial(_na_kernel, n_rows=n_rows),
        grid=(bsz, nq),
        in_specs=[
            pl.BlockSpec((1, rq * GRID_W, NA_WIDTH), lambda b, i: (b, i, 0)),
            pl.BlockSpec((1, seq, NA_WIDTH), lambda b, i: (b, 0, 1)),
            pl.BlockSpec((1, seq, NA_WIDTH), lambda b, i: (b, 0, 2)),
            pl.BlockSpec((1, s - seq, NA_WIDTH), lambda b, i: (b, ctx_blk, 1)),
            pl.BlockSpec((1, s - seq, NA_WIDTH), lambda b, i: (b, ctx_blk, 2)),
        ] + [bias_spec(j) for j in range(rq)],
        out_specs=pl.BlockSpec((1, rq * GRID_W, NA_WIDTH), lambda b, i: (b, i, 0)),
        out_shape=jax.ShapeDtypeStruct((bsz, nq * rq * GRID_W, NA_WIDTH), BF16),
        compiler_params=_params("parallel", "arbitrary"),
        name="neighborhood_attention",
    )(qkv, qkv, qkv, qkv, qkv, *([bias_tab] * rq))


def _pool_kernel(up_ref, u_ref, un_ref, w_ref, sc_ref, o_ref, buf, s0, s1, s2, *, n_lat_tiles, seq, ctx_len):
    i = pl.program_id(1)
    hw = POOL_REACH
    n = TM + 2 * hw
    is_ctx = i >= n_lat_tiles
    base = jnp.where(is_ctx, 0, i * TM)
    lseq = jnp.where(is_ctx, ctx_len, seq)
    buf[0:hw] = up_ref[0]
    buf[hw:hw + TM] = u_ref[0]
    buf[hw + TM:n] = un_ref[0]
    p = base - hw + lax.broadcasted_iota(jnp.int32, (n, POOL_WIDTH), 0)
    buf[...] = jnp.where((p >= 0) & (p < lseq), buf[...], 0.0)
    s0[0:n - 1] = buf[0:n - 1] + buf[1:n]
    s1[0:n - 3] = s0[0:n - 3] + s0[2:n - 1]
    s2[0:n - 7] = s1[0:n - 7] + s1[4:n - 3]
    shape = (TM, POOL_WIDTH)
    t = base + lax.broadcasted_iota(jnp.int32, shape, 0)
    grp = lax.broadcasted_iota(jnp.int32, shape, 1) // POOL_GROUP_DIM
    acc = jnp.where(grp == 0, s0[hw - 1:hw - 1 + TM],
                    jnp.where(grp == 1, s1[hw - 2:hw - 2 + TM],
                              jnp.where(grp == 2, s2[hw - 4:hw - 4 + TM], s2[0:TM] + s2[hw:hw + TM])))
    win = jnp.left_shift(2, grp)
    back = win // 2
    fwd = win - back - 1
    lo = jnp.clip(t - back, 0, lseq - 1)
    hi = jnp.clip(t + fwd, 0, lseq - 1)
    mean = acc / (hi - lo + 1).astype(F32)
    diff = (mean - u_ref[0]).astype(BF16)
    o_ref[0] = (_dot(diff, w_ref[...]) * sc_ref[...]).astype(o_ref.dtype)


def _multiscale_pool(u, w_bd, scale, n_lat_tiles, n_tiles):
    bsz, s, c = u.shape
    hw = POOL_REACH
    per = TM // hw
    last = s // hw - 1
    return pl.pallas_call(
        functools.partial(_pool_kernel, n_lat_tiles=n_lat_tiles, seq=n_lat_tiles * TM, ctx_len=s - n_lat_tiles * TM),
        grid=(bsz, n_tiles),
        in_specs=[
            pl.BlockSpec((1, hw, c), lambda b, i: (b, jnp.maximum(i * per - 1, 0), 0)),
            pl.BlockSpec((1, TM, c), lambda b, i: (b, i, 0)),
            pl.BlockSpec((1, hw, c), lambda b, i: (b, jnp.minimum((i + 1) * per, last), 0)),
            pl.BlockSpec((c, c), lambda b, i: (0, 0)),
            pl.BlockSpec((1, c), lambda b, i: (0, 0)),
        ],
        out_specs=pl.BlockSpec((1, TM, c), lambda b, i: (b, i, 0)),
        out_shape=jax.ShapeDtypeStruct((bsz, n_tiles * TM, c), BF16),
        scratch_shapes=[pltpu.VMEM((TM + 2 * hw, c), F32)] * 4,
        compiler_params=_params("parallel", "parallel"),
        name="multiscale_pool",
    )(u, u, u, w_bd, scale)


def _gla_tile(qk_ref, v_ref, la_ref, cs_ref, sn_ref, st_ref, o_ref, reverse):
    ch = GLA_CHUNK
    lane = lax.broadcasted_iota(jnp.int32, (ch, GLA_QK), 1)
    row = lax.broadcasted_iota(jnp.int32, (ch, GLA_QK), 0)
    rr = lax.broadcasted_iota(jnp.int32, (GLA_HEADS * ch, ch), 0) % ch
    cc = lax.broadcasted_iota(jnp.int32, (GLA_HEADS * ch, ch), 1)
    causal = (rr <= cc) if reverse else (rr >= cc)
    st_shape = (GLA_WIDTH, GLA_QK)
    head_blk = (lax.broadcasted_iota(jnp.int32, st_shape, 0) // GLA_DV
                == lax.broadcasted_iota(jnp.int32, st_shape, 1) // GLA_DK)
    quarter = GLA_DK // 4
    first_half = (lane % (2 * quarter)) < quarter

    def rope(x, cs, sn):
        partner = jnp.where(first_half, pltpu.roll(x, GLA_QK - quarter, 1), pltpu.roll(x, quarter, 1))
        return x * cs + partner * sn

    chunks = range(TM // ch)
    for c in (reversed(chunks) if reverse else chunks):
        sl = slice(c * ch, (c + 1) * ch)
        cs = cs_ref[sl, :]
        sn = sn_ref[sl, :]
        q = rope(qk_ref[0, sl, 0:GLA_QK], cs, sn)
        k = rope(qk_ref[0, sl, GLA_QK:2 * GLA_QK], cs, sn)
        b = la_ref[0, sl, :]
        d = 1
        while d < ch:
            if reverse:
                b = b + jnp.where(row < ch - d, pltpu.roll(b, ch - d, 0), 0.0)
            else:
                b = b + jnp.where(row >= d, pltpu.roll(b, d, 0), 0.0)
            d *= 2
        b_last = b[0:1, :] if reverse else b[ch - 1:ch, :]
        q_in = q * jnp.exp(b)
        k_in = (k * jnp.exp(-b)).astype(BF16)
        k_st = (k * jnp.exp(b_last - b)).astype(BF16)
        att = _dot_nt(_stack_heads(q_in, GLA_DK).astype(BF16), k_in)
        att = jnp.where(causal, att, 0.0).astype(BF16)
        vb = v_ref[0, sl, :].astype(BF16)
        o = _unstack_heads(_dot(att, vb), GLA_DV)
        st = st_ref[...]
        o_ref[0, sl, :] = o + _dot_nt(q_in.astype(BF16), st.astype(BF16))
        st_ref[...] = st * jnp.exp(b_last) + jnp.where(head_blk, _dot_tn(vb, k_st), 0.0)


def _gla_kernel(qkf, vf, laf, csf, snf, qkb, vb, lab, csb, snb, of_ref, ob_ref, st_f, st_b):
    @pl.when(pl.program_id(1) == 0)
    def _():
        st_f[...] = jnp.zeros_like(st_f)
        st_b[...] = jnp.zeros_like(st_b)

    _gla_tile(qkf, vf, laf, csf, snf, st_f, of_ref, False)
    _gla_tile(qkb, vb, lab, csb, snb, st_b, ob_ref, True)


def _rope_tables(seq, s):
    quarter = GLA_DK // 4
    t = np.arange(s)
    pos_row = np.where(t < seq, t // GRID_W, 0).astype(np.float32)
    pos_col = np.where(t < seq, t % GRID_W, 0).astype(np.float32)
    d = np.arange(GLA_QK) % GLA_DK
    inv_freq = ROPE_BASE ** (-jnp.arange(quarter, dtype=F32) / quarter)
    freq = inv_freq[d % quarter]
    pos = jnp.where((d < GLA_DK // 2)[None, :], pos_row[:, None], pos_col[:, None])
    ang = pos * freq[None, :]
    sign = np.where((d % (2 * quarter)) < quarter, -1.0, 1.0).astype(np.float32)
    return jnp.cos(ang), jnp.sin(ang) * sign[None, :]


def _gla(gg, cos, sin, n_lat_tiles):
    bsz, s, _ = gg.shape
    nt = s // TM
    assert nt == n_lat_tiles + 1
    fwd = lambda i: (i + n_lat_tiles) % nt
    bwd = lambda i: jnp.where(i == 0, n_lat_tiles, n_lat_tiles - i)
    la_f_blk = (2 * GLA_QK + 2 * GLA_WIDTH) // GLA_QK
    la_b_blk = la_f_blk + 1
    v_blk = 2 * GLA_QK // GLA_WIDTH

    def specs(order, la_blk):
        return [
            pl.BlockSpec((1, TM, 2 * GLA_QK), lambda b, i: (b, order(i), 0)),
            pl.BlockSpec((1, TM, GLA_WIDTH), lambda b, i: (b, order(i), v_blk)),
            pl.BlockSpec((1, TM, GLA_QK), lambda b, i: (b, order(i), la_blk)),
            pl.BlockSpec((TM, GLA_QK), lambda b, i: (order(i), 0)),
            pl.BlockSpec((TM, GLA_QK), lambda b, i: (order(i), 0)),
        ]

    return pl.pallas_call(
        _gla_kernel,
        grid=(bsz, nt),
        in_specs=specs(fwd, la_f_blk) + specs(bwd, la_b_blk),
        out_specs=[
            pl.BlockSpec((1, TM, GLA_WIDTH), lambda b, i: (b, fwd(i), 0)),
            pl.BlockSpec((1, TM, GLA_WIDTH), lambda b, i: (b, bwd(i), 0)),
        ],
        out_shape=[jax.ShapeDtypeStruct((bsz, s, GLA_WIDTH), F32)] * 2,
        scratch_shapes=[pltpu.VMEM((GLA_WIDTH, GLA_QK), F32)] * 2,
        compiler_params=_params("arbitrary", "arbitrary"),
        name="gla_bidir",
    )(gg, gg, gg, cos, sin, gg, gg, gg, cos, sin)


def _store_row_tiles(ref, val):
    rows, width = val.shape
    pieces = width // LANES
    for s in range(pieces):
        ref[pl.ds(s, rows, stride=pieces), :] = val[:, s * LANES:(s + 1) * LANES]


def _load_row_tiles(ref, rows, pieces):
    return jnp.concatenate([ref[pl.ds(s, rows, stride=pieces), :] for s in range(pieces)], axis=1)


def _row_index(shape):
    return lax.broadcasted_iota(jnp.int32, shape, 0).astype(F32)


def _stack_rows(rows):
    shape = (len(rows), rows[0].shape[1])
    rid = lax.broadcasted_iota(jnp.int32, shape, 0)
    out = jnp.zeros(shape, rows[0].dtype)
    for k, r in enumerate(rows):
        out = jnp.where(rid == k, r, out)
    return out


def _select_experts(scores, biased):
    n_exp, n_tok = biased.shape
    per = n_exp // N_GROUPS
    neg = -jnp.inf
    sub = _row_index((per, n_tok))
    grp_rows = []
    for g in range(N_GROUPS):
        blk = biased[g * per:(g + 1) * per, :]
        m1 = jnp.max(blk, axis=0, keepdims=True)
        i1 = jnp.min(jnp.where(blk == m1, sub, float(per)), axis=0, keepdims=True)
        m2 = jnp.max(jnp.where(sub == i1, neg, blk), axis=0, keepdims=True)
        grp_rows.append(m1 + m2)
    cur = _stack_rows(grp_rows)
    gid = _row_index(cur.shape)
    picked = jnp.zeros(cur.shape, F32)
    for _ in range(TOPK_GROUPS):
        gm = jnp.max(cur, axis=0, keepdims=True)
        gi = jnp.min(jnp.where(cur == gm, gid, float(N_GROUPS)), axis=0, keepdims=True)
        hit = gid == gi
        picked = jnp.where(hit, 1.0, picked)
        cur = jnp.where(hit, neg, cur)
    cur = jnp.concatenate(
        [jnp.where(picked[g:g + 1, :] > 0.0, biased[g * per:(g + 1) * per, :], neg) for g in range(N_GROUPS)], axis=0)
    eid = _row_index(cur.shape)
    ids, vals = [], []
    chosen = jnp.zeros(cur.shape, F32)
    for _ in range(TOP_K):
        m = jnp.max(cur, axis=0, keepdims=True)
        idx = jnp.min(jnp.where(cur == m, eid, float(n_exp)), axis=0, keepdims=True)
        hit = eid == idx
        ids.append(idx)
        vals.append(jnp.sum(jnp.where(hit, scores, 0.0), axis=0, keepdims=True))
        chosen = jnp.where(hit, 1.0, chosen)
        cur = jnp.where(hit, neg, cur)
    return ids, vals, chosen


def _outproj_kernel(x_ref, xc_ref, oa_ref, ob_ref, of_ref, obk_ref, r_ref, gg_ref, wo_ref, mod_ref, gf_ref, wr_ref, br_ref,
                    tri_ref, xo_ref, h_ref, e_ref, w_ref, rk_ref, hist_ref, *, n_lat_tiles):
    o = of_ref[0] + obk_ref[0]
    head = lax.broadcasted_iota(jnp.int32, o.shape, 1) // GLA_DV
    o2 = o * o
    rs = jnp.zeros_like(o)
    for h in range(GLA_HEADS):
        ssq = jnp.sum(jnp.where(head == h, o2, 0.0), axis=-1, keepdims=True)
        rs = jnp.where(head == h, lax.rsqrt(ssq / GLA_DV + NORM_EPS), rs)
    oc = (o * rs * gg_ref[...] * _silu(r_ref[0])).astype(BF16)
    c1 = NA_WIDTH + POOL_WIDTH
    acc = _dot(oa_ref[0], wo_ref[0:NA_WIDTH]) + _dot(ob_ref[0], wo_ref[NA_WIDTH:c1]) + _dot(oc, wo_ref[c1:])
    x = _stream_tile(x_ref, xc_ref, n_lat_tiles) + mod_ref[0, 0, 2:3, :] * acc
    xo_ref[0] = x
    h = _rmsnorm(x, gf_ref[...]) * (1.0 + mod_ref[0, 0, 4:5, :]) + mod_ref[0, 0, 3:4, :]
    _store_row_tiles(h_ref, h)

    scores = _sigmoid(_dot_nt(wr_ref[...], h.astype(BF16)))
    ids, vals, chosen = _select_experts(scores, scores + br_ref[...])
    total = vals[0]
    for v in vals[1:]:
        total = total + v
    w_ref[...] = _stack_rows([v / total * ROUTED_SCALE for v in vals])
    e_ref[...] = _stack_rows(ids).astype(jnp.int32)
    chosen_b = chosen.astype(BF16)
    before = _dot(chosen_b, tri_ref[...])
    eid = _row_index(before.shape)
    rk_ref[...] = _stack_rows([jnp.sum(jnp.where(eid == idx, before, 0.0), axis=0, keepdims=True) for idx in ids])
    hist_ref[0] = _dot_nt(jnp.ones((8, chosen.shape[1]), BF16), chosen_b)


def _outproj(x_lat, x_ctx, ctx_blk, oa, ob, o_f, o_b, gg, g_gla4, w_out, mod, g_ffn, w_router_t, b_router,
             n_lat_tiles, n_tiles):
    bsz, _, d = x_lat.shape
    n_exp = w_router_t.shape[0]
    r_blk = (2 * GLA_QK + GLA_WIDTH) // GLA_WIDTH
    tile = lambda w: pl.BlockSpec((1, TM, w), lambda b, i: (b, i, 0))
    const = lambda b, i: (0, 0)
    per_tok = pl.BlockSpec((TOP_K, TM), lambda b, i: (0, b * n_tiles + i))
    t_tok = bsz * n_tiles * TM
    tri = (jnp.arange(TM)[:, None] < jnp.arange(TM)[None, :]).astype(BF16)
    return pl.pallas_call(
        functools.partial(_outproj_kernel, n_lat_tiles=n_lat_tiles),
        grid=(bsz, n_tiles),
        in_specs=[
            *_stream_specs(d, n_lat_tiles, ctx_blk),
            tile(NA_WIDTH), tile(POOL_WIDTH), tile(GLA_WIDTH), tile(GLA_WIDTH),
            pl.BlockSpec((1, TM, GLA_WIDTH), lambda b, i: (b, i, r_blk)),
            pl.BlockSpec((1, GLA_WIDTH), const),
            pl.BlockSpec(w_out.shape, const),
            pl.BlockSpec((1, 1, 8, d), lambda b, i: (b, jnp.where(i >= n_lat_tiles, 0, 1), 0, 0)),
            pl.BlockSpec((1, d), const),
            pl.BlockSpec(w_router_t.shape, const),
            pl.BlockSpec((n_exp, 1), const),
            pl.BlockSpec((TM, TM), const),
        ],
        out_specs=[tile(d), pl.BlockSpec((TM * d // LANES, LANES), lambda b, i: (b * n_tiles + i, 0)),
                   per_tok, per_tok, per_tok,
                   pl.BlockSpec((1, 8, n_exp), lambda b, i: (b * n_tiles + i, 0, 0))],
        out_shape=[
            jax.ShapeDtypeStruct((bsz, n_tiles * TM, d), F32),
            jax.ShapeDtypeStruct((t_tok * d // LANES, LANES), F32),
            jax.ShapeDtypeStruct((TOP_K, t_tok), jnp.int32),
            jax.ShapeDtypeStruct((TOP_K, t_tok), F32),
            jax.ShapeDtypeStruct((TOP_K, t_tok), F32),
            jax.ShapeDtypeStruct((bsz * n_tiles, 8, n_exp), F32),
        ],
        compiler_params=_params("parallel", "parallel"),
        name="outproj_router",
    )(x_lat, x_ctx, oa, ob, o_f, o_b, gg, g_gla4, w_out, mod, g_ffn, w_router_t, b_router, tri)


def _slot_layout(hist):
    n_tiles, n_exp = hist.shape
    counts = hist.sum(0)
    tile_base = jnp.cumsum(hist, axis=0) - hist
    padded = (counts + MOE_BLOCK - 1) // MOE_BLOCK * MOE_BLOCK
    pad_end = jnp.cumsum(padded)
    pad_start = pad_end - padded
    base = (pad_start[None, :] + tile_base).astype(F32).reshape(n_tiles, n_exp, 1)
    n_blocks = -(-(n_tiles * TM * TOP_K) // MOE_BLOCK) + n_exp
    n_used = (pad_end[-1] // MOE_BLOCK).astype(jnp.int32)
    fill = jnp.concatenate([jnp.maximum(pad_end // MOE_BLOCK - 1, 0).astype(jnp.int32),
                            jnp.minimum(n_used + jnp.arange(n_exp, dtype=jnp.int32), n_blocks - 1)])
    first_blk = (pad_start // MOE_BLOCK).astype(jnp.int32)
    blk_count = (padded // MOE_BLOCK).astype(jnp.int32)
    return base, first_blk, blk_count, fill, n_used.reshape(1), n_blocks


def _pos_kernel(e_ref, rk_ref, base_ref, pos_ref, *, pieces):
    eid = lax.broadcasted_iota(jnp.int32, (base_ref.shape[1], e_ref.shape[1]), 0)
    base = base_ref[0]
    rows = [jnp.sum(jnp.where(eid == e_ref[k:k + 1, :], base, 0.0), axis=0, keepdims=True) for k in range(TOP_K)]
    pos_ref[0] = ((_stack_rows(rows) + rk_ref[...]) * float(pieces)).astype(jnp.int32)


def _slot_positions(e_t, rank_t, base, pieces):
    n_tiles, n_exp, _ = base.shape
    per_tok = pl.BlockSpec((TOP_K, TM), lambda i: (0, i))
    return pl.pallas_call(
        functools.partial(_pos_kernel, pieces=pieces),
        grid=(n_tiles,),
        in_specs=[per_tok, per_tok, pl.BlockSpec((1, n_exp, 1), lambda i: (i, 0, 0))],
        out_specs=pl.BlockSpec((1, TOP_K, TM), lambda i: (i, 0, 0)),
        out_shape=jax.ShapeDtypeStruct((n_tiles, TOP_K, TM), jnp.int32),
        compiler_params=_params("parallel"),
        name="moe_positions",
    )(e_t, rank_t, base).reshape(-1)


def _dispatch_kernel(fill_ref, pos_ref, h_ref, xs_hbm, zbuf, sem, zsem):
    @pl.when(pl.program_id(0) == 0)
    def _():
        zbuf[...] = jnp.zeros_like(zbuf)
        n_fill = fill_ref.shape[0]

        def is_new(j):
            return (j == 0) | (fill_ref[j] != fill_ref[jnp.maximum(j - 1, 0)])

        def start(j, c):
            @pl.when(is_new(j))
            def _():
                row0 = pl.multiple_of(fill_ref[j] * zbuf.shape[0], zbuf.shape[0])
                pltpu.make_async_copy(zbuf, xs_hbm.at[pl.ds(row0, zbuf.shape[0]), :], zsem).start()
            return c
        lax.fori_loop(0, n_fill, start, 0)

        def wait(j, c):
            @pl.when(is_new(j))
            def _():
                pltpu.make_async_copy(zbuf, xs_hbm.at[pl.ds(0, zbuf.shape[0]), :], zsem).wait()
            return c
        lax.fori_loop(0, n_fill, wait, 0)

    pieces = h_ref.shape[0] // TM

    def body(t, c):
        src = h_ref.at[pl.ds(pl.multiple_of(t * pieces, pieces), pieces), :]
        for k in range(TOP_K):
            dst = xs_hbm.at[pl.ds(pl.multiple_of(pos_ref[k * TM + t], pieces), pieces), :]
            pltpu.make_async_copy(src, dst, sem).start(priority=k % 2)
        return c
    lax.fori_loop(0, TM, body, 0, unroll=2)
    for _ in range(TOP_K):
        pltpu.make_async_copy(h_ref, xs_hbm.at[pl.ds(0, h_ref.shape[0]), :], sem).wait()


def _dispatch(h2t, pos, fill, n_slots, pieces):
    n_tiles = h2t.shape[0] // (TM * pieces)
    grid_spec = pltpu.PrefetchScalarGridSpec(
        num_scalar_prefetch=1,
        grid=(n_tiles,),
        in_specs=[
            pl.BlockSpec((TOP_K * TM,), lambda i, fill: (i,), memory_space=pltpu.SMEM),
            pl.BlockSpec((TM * pieces, LANES), lambda i, fill: (i, 0)),
        ],
        out_specs=pl.BlockSpec(memory_space=pl.ANY),
        scratch_shapes=[pltpu.VMEM((MOE_BLOCK * pieces, LANES), F32), pltpu.SemaphoreType.DMA,
                        pltpu.SemaphoreType.DMA],
    )
    return pl.pallas_call(
        _dispatch_kernel,
        grid_spec=grid_spec,
        out_shape=jax.ShapeDtypeStruct((n_slots * pieces, LANES), F32),
        compiler_params=_params("arbitrary"),
        name="moe_dispatch",
    )(fill, pos, h2t)


def _expert_kernel(first_ref, cnt_ref, nu_ref, wi_ref, wd_ref, xs_hbm, ys_hbm, xbuf, ybuf, wi_bf, wd_bf, xsem, ysem,
                   *, n_blocks):
    e = pl.program_id(0)
    n_used = nu_ref[0]
    rows = xbuf.shape[1]
    pieces = rows // MOE_BLOCK
    d_exp = wd_ref.shape[2]

    def x_copy(g, s):
        return pltpu.make_async_copy(xs_hbm.at[pl.ds(pl.multiple_of(g * rows, rows), rows), :], xbuf.at[s], xsem.at[s])

    def y_copy(g, s):
        return pltpu.make_async_copy(ybuf.at[s], ys_hbm.at[pl.ds(pl.multiple_of(g * rows, rows), rows), :], ysem.at[s])

    n_xbuf = xbuf.shape[0]
    n_ybuf = ybuf.shape[0]

    @pl.when(e == 0)
    def _():
        for g in range(n_xbuf - 1):
            @pl.when(g < n_used)
            def _():
                x_copy(g, g).start()

    @pl.when(cnt_ref[e] > 0)
    def _():
        wi_bf[...] = wi_ref[0, 0].astype(BF16)
        wd_bf[...] = wd_ref[0, 0].astype(BF16)

    def block(j, c):
        g = first_ref[e] + j
        s = g % n_ybuf
        x_copy(g, g % n_xbuf).wait()

        @pl.when(g + n_xbuf - 1 < n_used)
        def _():
            x_copy(g + n_xbuf - 1, (g + n_xbuf - 1) % n_xbuf).start()

        @pl.when(g >= n_ybuf)
        def _():
            y_copy(g - n_ybuf, s).wait()

        hh = _dot(_load_row_tiles(xbuf.at[g % n_xbuf], MOE_BLOCK, pieces).astype(BF16), wi_bf[...])
        a = (_silu(hh[:, :d_exp]) * hh[:, d_exp:]).astype(BF16)
        _store_row_tiles(ybuf.at[s], _dot(a, wd_bf[...]))
        y_copy(g, s).start(priority=1)
        return c
    lax.fori_loop(0, cnt_ref[e], block, 0)

    @pl.when(e == pl.num_programs(0) - 1)
    def _():
        for back in range(1, n_ybuf + 1):
            @pl.when(n_used >= back)
            def _():
                y_copy(n_used - back, (n_used - back) % n_ybuf).wait()

        ybuf[0] = jnp.zeros(ybuf.shape[1:], ybuf.dtype)

        def start(g, c):
            y_copy(g, 0).start()
            return c
        lax.fori_loop(n_used, n_blocks, start, 0)

        def wait(g, c):
            y_copy(g, 0).wait()
            return c
        lax.fori_loop(n_used, n_blocks, wait, 0)


def _moe_experts(xs, first_blk, blk_count, n_used, n_blocks, w_e_in, w_e_down, layer):
    _, n_exp, d, f2 = w_e_in.shape
    rows = xs.shape[0] // n_blocks
    grid_spec = pltpu.PrefetchScalarGridSpec(
        num_scalar_prefetch=3,
        grid=(n_exp,),
        in_specs=[
            pl.BlockSpec((1, 1, d, f2), lambda e, first, cnt, nu: (layer, e, 0, 0)),
            pl.BlockSpec((1, 1, f2 // 2, d), lambda e, first, cnt, nu: (layer, e, 0, 0)),
            pl.BlockSpec(memory_space=pl.ANY),
        ],
        out_specs=pl.BlockSpec(memory_space=pl.ANY),
        scratch_shapes=[
            pltpu.VMEM((4, rows, LANES), F32),
            pltpu.VMEM((3, rows, LANES), F32),
            pltpu.VMEM((d, f2), BF16),
            pltpu.VMEM((f2 // 2, d), BF16),
            pltpu.SemaphoreType.DMA((4,)),
            pltpu.SemaphoreType.DMA((3,)),
        ],
    )
    return pl.pallas_call(
        functools.partial(_expert_kernel, n_blocks=n_blocks),
        grid_spec=grid_spec,
        out_shape=jax.ShapeDtypeStruct(xs.shape, F32),
        compiler_params=_params("arbitrary"),
        name="moe_experts",
    )(first_blk, blk_count, n_used, w_e_in, w_e_down, xs)


def _combine_kernel(pos0_ref, posn_ref, x_ref, h_ref, w_ref, wsi_ref, wsd_ref, mod_ref, gfin_ref, ys_hbm, *rest,
                    final, n_steps, n_next):
    next_in, o_ref, next_out = rest[:n_next], rest[n_next], rest[n_next + 1:-2]
    ybuf, sem = rest[-2:]
    step = pl.program_id(0) * pl.num_programs(1) + pl.program_id(1)
    slot = step % 2
    d_exp = wsd_ref.shape[0]
    pieces = h_ref.shape[0] // TMC

    def gather(pos_ref, st, s):
        off = (st % (TM // TMC)) * TMC
        def body(t, c):
            for k in range(TOP_K):
                src = ys_hbm.at[pl.ds(pl.multiple_of(pos_ref[k * TM + off + t], pieces), pieces), :]
                dst = ybuf.at[s, k, pl.ds(pl.multiple_of(t * pieces, pieces), pieces), :]
                pltpu.make_async_copy(src, dst, sem.at[s]).start(priority=k % 2)
            return c
        lax.fori_loop(0, TMC, body, 0, unroll=2)

    @pl.when(step == 0)
    def _():
        gather(pos0_ref, 0, 0)

    @pl.when(step + 1 < n_steps)
    def _():
        gather(posn_ref, step + 1, 1 - slot)

    hh = _dot(_load_row_tiles(h_ref, TMC, pieces).astype(BF16), wsi_ref[...])
    a = (_silu(hh[:, :d_exp]) * hh[:, d_exp:]).astype(BF16)
    y = _dot(a, wsd_ref[...])
    for k in range(TOP_K):
        pltpu.make_async_copy(ys_hbm.at[pl.ds(0, TMC * pieces), :], ybuf.at[slot, k], sem.at[slot]).wait()
    w = w_ref[...]
    for k in range(TOP_K):
        y = y + _load_row_tiles(ybuf.at[slot, k], TMC, pieces) * w[:, k:k + 1]
    x = x_ref[0] + mod_ref[0, 0, 5:6, :] * y
    if final:
        x = _rmsnorm(x, gfin_ref[...])
    o_ref[0] = x
    if next_in:
        _inproj_tile(x, *next_in, *next_out)


def _combine(x_mid, h2t, ys, pos, w_tok, w_sh_in, w_sh_down, mod, g_final, n_lat_tiles, final, next_proj=None):
    bsz, s, d = x_mid.shape
    nt = s // TMC
    n_steps = bsz * nt
    halves = TM // TMC
    lat_steps = n_lat_tiles * halves
    pieces = d // LANES
    const = lambda b, i: (0, 0)
    tile = pl.BlockSpec((1, TMC, d), lambda b, i: (b, i, 0))
    smem_blk = lambda f: pl.BlockSpec((TOP_K * TM,), f, memory_space=pltpu.SMEM)
    wide = lambda w: pl.BlockSpec((1, TMC, w), lambda b, i: (b, i, 0))
    mod_spec = pl.BlockSpec((1, 1, 8, d), lambda b, i: (b, jnp.where(i >= lat_steps, 0, 1), 0, 0))
    next_args, next_specs, next_outs = (), [], []
    if next_proj is not None:
        next_args = tuple(next_proj)
        next_specs = [mod_spec] + [pl.BlockSpec(a.shape, const) for a in next_proj[1:]]
        next_outs = [(QKV_W, BF16), (POOL_WIDTH, F32), (G_W, F32)]
    out = pl.pallas_call(
        functools.partial(_combine_kernel, final=final, n_steps=n_steps, n_next=len(next_args)),
        grid=(bsz, nt),
        in_specs=[
            smem_blk(lambda b, i: (0,)),
            smem_blk(lambda b, i: (jnp.minimum(b * nt + i + 1, n_steps - 1) // halves,)),
            tile,
            pl.BlockSpec((TMC * pieces, LANES), lambda b, i: (b * nt + i, 0)),
            pl.BlockSpec((TMC, TOP_K), lambda b, i: (b * nt + i, 0)),
            pl.BlockSpec(w_sh_in.shape, const),
            pl.BlockSpec(w_sh_down.shape, const),
            pl.BlockSpec((1, 1, 8, d), lambda b, i: (b, jnp.where(i >= lat_steps, 0, 1), 0, 0)),
            pl.BlockSpec((1, d), const),
            pl.BlockSpec(memory_space=pl.ANY),
        ] + next_specs,
        out_specs=[tile] + [wide(w) for w, _ in next_outs],
        out_shape=[jax.ShapeDtypeStruct((bsz, s, d), F32)]
        + [jax.ShapeDtypeStruct((bsz, s, w), dt) for w, dt in next_outs],
        scratch_shapes=[pltpu.VMEM((2, TOP_K, TMC * pieces, LANES), F32), pltpu.SemaphoreType.DMA((2,))],
        compiler_params=_params("arbitrary", "arbitrary"),
        name="moe_combine",
    )(pos, pos, x_mid, h2t, w_tok, w_sh_in, w_sh_down, mod, g_final, ys, *next_args)
    return out if next_proj is not None else out[0]


def _block_diag(w):
    g, a, b = w.shape
    out = jnp.zeros((g * a, g * b), w.dtype)
    for j in range(g):
        out = out.at[j * a:(j + 1) * a, j * b:(j + 1) * b].set(w[j])
    return out


def kernel(x, c, ctx, c_ctx, w_mod, b_mod, g_mix, w_in, rpb, w_pool, pool_scale, w_gate_f, b_gate_f, w_gate_b, b_gate_b, g_gla, w_out, g_ffn, w_router, b_router, w_e_in, w_e_down, w_sh_in, w_sh_down, g_final):
    bsz, seq, d = x.shape
    n_ctx = ctx.shape[1]
    depth = w_mod.shape[0]
    n_exp = w_router.shape[2]
    assert n_ctx == TM and seq % TM == 0 and seq % GRID_W == 0
    n_lat = seq // TM
    s = seq + n_ctx

    n_rows = -(-(bsz + 1) // 8) * 8
    cc = jnp.zeros((n_rows, d), F32).at[:bsz].set(c).at[bsz].set(c_ctx)
    mod_all = _modulation(cc, w_mod, b_mod).reshape(depth, n_rows, 6, d)
    cos, sin = _rope_tables(seq, s)

    def projection_operands(layer):
        m = mod_all[layer]
        mod = jnp.stack([jnp.broadcast_to(m[bsz], (bsz, 6, d)), m[:bsz]], axis=1)
        mod = jnp.pad(mod, ((0, 0), (0, 0), (0, 2), (0, 0)))
        wl = w_in[layer]
        w_main = wl[:, :MAIN_W].astype(BF16)
        w_low = jnp.pad(wl[:, MAIN_W:], ((0, 0), (0, LANES - 2 * GLA_GATE_RANK))).astype(BF16)
        w2 = jnp.zeros((LANES, 2 * GLA_QK), F32)
        w2 = w2.at[:GLA_GATE_RANK, :GLA_QK].set(w_gate_f[layer])
        w2 = w2.at[GLA_GATE_RANK:2 * GLA_GATE_RANK, GLA_QK:].set(w_gate_b[layer]).astype(BF16)
        b2 = jnp.concatenate([b_gate_f[layer], b_gate_b[layer]])[None, :]
        return mod, g_mix[layer][None, :], w_main, w_low, w2, b2

    proj = [projection_operands(layer) for layer in range(depth)]
    x_lat, x_ctx, ctx_blk = x, ctx, 0
    projected = _inproj(x_lat, x_ctx, ctx_blk, *proj[0], n_lat)
    for layer in range(depth):
        last = layer == depth - 1
        n_tiles = n_lat if last else n_lat + 1
        mod = proj[layer][0]
        qkv, u, gg = projected

        oa = _neighborhood_attention(qkv, _na_bias_table(rpb[layer]), n_lat, not last)
        ob = _multiscale_pool(u, _block_diag(w_pool[layer]).astype(BF16), pool_scale[layer][None, :], n_lat, n_tiles)
        o_f, o_b = _gla(gg, cos, sin, n_lat)

        x_mid, h2t, e_t, w_t, rank_t, hist = _outproj(
            x_lat, x_ctx, ctx_blk, oa, ob, o_f, o_b, gg, jnp.tile(g_gla[layer], GLA_HEADS)[None, :], w_out[layer].astype(BF16), mod,
            g_ffn[layer][None, :], w_router[layer].T.astype(BF16), b_router[layer].reshape(n_exp, 1), n_lat, n_tiles)
        pieces = d // LANES
        base, first_blk, blk_count, fill, n_used, n_blocks = _slot_layout(hist[:, 0, :].astype(jnp.int32))
        pos = _slot_positions(e_t, rank_t, base, pieces)
        xs = _dispatch(h2t, pos, fill, n_blocks * MOE_BLOCK, pieces)
        ys = _moe_experts(xs, first_blk, blk_count, n_used, n_blocks, w_e_in, w_e_down, layer)
        out = _combine(x_mid, h2t, ys, pos, w_t.T, w_sh_in[layer].astype(BF16), w_sh_down[layer].astype(BF16), mod,
                       g_final[None, :], n_lat, last, None if last else proj[layer + 1])
        if last:
            return out
        xa, *projected = out
        x_lat, x_ctx, ctx_blk = xa, xa, n_lat
```

```python
import functools

import jax
import jax.numpy as jnp
import numpy as np
from jax import lax
from jax.experimental import pallas as pl
from jax.experimental.pallas import tpu as pltpu

GRID_W = 64
NORM_EPS = 1e-6
NA_HEADS = 8
NA_HEAD_DIM = 64
NA_WIDTH = NA_HEADS * NA_HEAD_DIM
NA_WIN_ROWS = 8
NA_WIN_COLS = 16
POOL_WINDOWS = (2, 4, 8, 16)
POOL_GROUP_DIM = 64
POOL_WIDTH = len(POOL_WINDOWS) * POOL_GROUP_DIM
POOL_REACH = max(POOL_WINDOWS) // 2
assert POOL_WINDOWS == tuple(2 ** (g + 1) for g in range(len(POOL_WINDOWS)))
GLA_HEADS = 4
GLA_DK = 32
GLA_DV = 64
GLA_QK = GLA_HEADS * GLA_DK
GLA_WIDTH = GLA_HEADS * GLA_DV
GLA_GATE_RANK = 16
GLA_TAU = 16.0
GLA_CHUNK = 64
ROPE_BASE = 10000.0
N_EXPERTS = 256
TOP_K = 8
N_GROUPS = 8
TOPK_GROUPS = 4
ROUTED_SCALE = 2.5

TM = 256
TMC = 128
MOE_BLOCK = 256
LANES = 128
MXU_DIM = 256
MASK_VALUE = -1e30
VMEM_LIMIT = 48 * 1024 * 1024

QKV_W = 3 * NA_WIDTH
G_W = 2 * GLA_QK + 2 * GLA_WIDTH + 2 * GLA_QK
MAIN_W = QKV_W + POOL_WIDTH + 2 * GLA_QK + 2 * GLA_WIDTH

BF16 = jnp.bfloat16
F32 = jnp.float32


def _params(*sem):
    return pltpu.CompilerParams(dimension_semantics=sem, vmem_limit_bytes=VMEM_LIMIT)


def _sigmoid(x):
    return 1.0 / (1.0 + jnp.exp(-x))


def _silu(x):
    return x * _sigmoid(x)


def _rmsnorm(x, g):
    return x * lax.rsqrt(jnp.mean(x * x, axis=-1, keepdims=True) + NORM_EPS) * g


def _dot(a, b):
    return jnp.dot(a, b, preferred_element_type=F32)


def _dot_nt(a, b):
    return lax.dot_general(a, b, (((1,), (1,)), ((), ())), preferred_element_type=F32)


def _dot_tn(a, b):
    return lax.dot_general(a, b, (((0,), (0,)), ((), ())), preferred_element_type=F32)


def _mod_kernel(c_ref, w_ref, b_ref, o_ref):
    a = _silu(c_ref[...]).astype(BF16)
    o_ref[0] = _dot(a, w_ref[0].astype(BF16)) + b_ref[0]


def _modulation(cc, w_mod, b_mod):
    depth, d, n = w_mod.shape
    r = cc.shape[0]
    tn = 1024
    return pl.pallas_call(
        _mod_kernel,
        grid=(depth, n // tn),
        in_specs=[
            pl.BlockSpec((r, d), lambda l, j: (0, 0)),
            pl.BlockSpec((1, d, tn), lambda l, j: (l, 0, j)),
            pl.BlockSpec((1, 1, tn), lambda l, j: (l, 0, j)),
        ],
        out_specs=pl.BlockSpec((1, r, tn), lambda l, j: (l, 0, j)),
        out_shape=jax.ShapeDtypeStruct((depth, r, n), F32),
        compiler_params=_params("parallel", "parallel"),
        name="modulation",
    )(cc, w_mod, b_mod.reshape(depth, 1, n))


def _stream_tile(x_ref, xc_ref, n_lat_tiles):
    return jnp.where(pl.program_id(1) >= n_lat_tiles, xc_ref[0], x_ref[0])


def _stream_specs(d, n_lat_tiles, ctx_blk):
    return [pl.BlockSpec((1, TM, d), lambda b, i: (b, jnp.minimum(i, n_lat_tiles - 1), 0)),
            pl.BlockSpec((1, TM, d), lambda b, i: (b, ctx_blk, 0))]


def _inproj_kernel(x_ref, xc_ref, mod_ref, g_ref, wm_ref, wl_ref, w2_ref, b2_ref, qkv_ref, u_ref, gg_ref, *, n_lat_tiles):
    _inproj_tile(_stream_tile(x_ref, xc_ref, n_lat_tiles), mod_ref, g_ref, wm_ref, wl_ref, w2_ref, b2_ref,
                 qkv_ref, u_ref, gg_ref)


def _inproj_tile(x, mod_ref, g_ref, wm_ref, wl_ref, w2_ref, b2_ref, qkv_ref, u_ref, gg_ref):
    h = _rmsnorm(x, g_ref[...]) * (1.0 + mod_ref[0, 0, 1:2, :]) + mod_ref[0, 0, 0:1, :]
    hb = h.astype(BF16)
    q = _dot(hb, wm_ref[:, 0:NA_WIDTH]) * (NA_HEAD_DIM ** -0.5)
    qkv_ref[0, :, 0:NA_WIDTH] = q.astype(BF16)
    qkv_ref[0, :, NA_WIDTH:2 * NA_WIDTH] = _dot(hb, wm_ref[:, NA_WIDTH:2 * NA_WIDTH]).astype(BF16)
    qkv_ref[0, :, 2 * NA_WIDTH:QKV_W] = _dot(hb, wm_ref[:, 2 * NA_WIDTH:QKV_W]).astype(BF16)
    u_ref[0] = _dot(hb, wm_ref[:, QKV_W:QKV_W + POOL_WIDTH])
    c0 = QKV_W + POOL_WIDTH
    gg_ref[0, :, 0:GLA_QK] = _dot(hb, wm_ref[:, c0:c0 + GLA_QK]) * (GLA_DK ** -0.5)
    gg_ref[0, :, GLA_QK:2 * GLA_QK + 2 * GLA_WIDTH] = _dot(hb, wm_ref[:, c0 + GLA_QK:MAIN_W])
    a_low = _dot(hb, wl_ref[...]).astype(BF16)
    lg = _dot(a_low, w2_ref[...]) + b2_ref[...]
    log_sig = jnp.minimum(lg, 0.0) - jnp.log1p(jnp.exp(-jnp.abs(lg)))
    gg_ref[0, :, 2 * GLA_QK + 2 * GLA_WIDTH:G_W] = log_sig / GLA_TAU


def _inproj(x_lat, x_ctx, ctx_blk, mod, g, w_main, w_low, w2, b2, n_lat_tiles):
    bsz, _, d = x_lat.shape
    nt = n_lat_tiles + 1
    s = nt * TM
    const = lambda b, i: (0, 0)
    return pl.pallas_call(
        functools.partial(_inproj_kernel, n_lat_tiles=n_lat_tiles),
        grid=(bsz, nt),
        in_specs=_stream_specs(d, n_lat_tiles, ctx_blk) + [
            pl.BlockSpec((1, 1, 8, d), lambda b, i: (b, jnp.where(i >= n_lat_tiles, 0, 1), 0, 0)),
            pl.BlockSpec((1, d), const),
            pl.BlockSpec(w_main.shape, const),
            pl.BlockSpec(w_low.shape, const),
            pl.BlockSpec(w2.shape, const),
            pl.BlockSpec(b2.shape, const),
        ],
        out_specs=[
            pl.BlockSpec((1, TM, QKV_W), lambda b, i: (b, i, 0)),
            pl.BlockSpec((1, TM, POOL_WIDTH), lambda b, i: (b, i, 0)),
            pl.BlockSpec((1, TM, G_W), lambda b, i: (b, i, 0)),
        ],
        out_shape=[
            jax.ShapeDtypeStruct((bsz, s, QKV_W), BF16),
            jax.ShapeDtypeStruct((bsz, s, POOL_WIDTH), F32),
            jax.ShapeDtypeStruct((bsz, s, G_W), F32),
        ],
        compiler_params=_params("parallel", "parallel"),
        name="inproj",
    )(x_lat, x_ctx, mod, g, w_main, w_low, w2, b2)


HEADS_PER_GROUP = MXU_DIM // NA_HEAD_DIM
NA_GROUPS = NA_HEADS // HEADS_PER_GROUP
NA_LOCAL_KEYS = NA_WIN_ROWS * GRID_W
NA_ROWS_PER_STEP = 4


def _stack_heads(x, width):
    lane = lax.broadcasted_iota(jnp.int32, x.shape, 1) // width
    n_heads = x.shape[1] // width
    return jnp.concatenate([jnp.where(lane == h, x, jnp.zeros_like(x)) for h in range(n_heads)], axis=0)


def _unstack_heads(o, width):
    n_heads = o.shape[1] // width
    r = o.shape[0] // n_heads
    lane = lax.broadcasted_iota(jnp.int32, (r, o.shape[1]), 1) // width
    acc = jnp.zeros((r, o.shape[1]), o.dtype)
    for h in range(n_heads):
        acc = jnp.where(lane == h, o[h * r:(h + 1) * r, :], acc)
    return acc


def _na_kernel(q_ref, k_ref, v_ref, kc_ref, vc_ref, *rest, n_rows):
    bias_refs, o_ref = rest[:-1], rest[-1]
    rows_per_step = len(bias_refs)
    step = pl.program_id(1)

    def attend(g, j, local):
        r = step * rows_per_step + j
        rows = slice(j * GRID_W, (j + 1) * GRID_W)
        cols = slice(g * MXU_DIM, (g + 1) * MXU_DIM)
        qs = _stack_heads(q_ref[0, rows, cols], NA_HEAD_DIM)
        kc = kc_ref[0, :, cols]
        vc = vc_ref[0, :, cols]
        s_ctx = _dot_nt(qs, kc)
        m = jnp.max(s_ctx, axis=-1, keepdims=True)
        if local:
            start = pl.multiple_of(jnp.clip(r - NA_WIN_ROWS // 2, 0, n_rows - NA_WIN_ROWS) * GRID_W, GRID_W)
            kw = k_ref[0, pl.ds(start, NA_LOCAL_KEYS), cols]
            vw = v_ref[0, pl.ds(start, NA_LOCAL_KEYS), cols]
            s_loc = _dot_nt(qs, kw) + bias_refs[j][0, g]
            m = jnp.maximum(m, jnp.max(s_loc, axis=-1, keepdims=True))
            p_loc = jnp.exp(s_loc - m)
        p_ctx = jnp.exp(s_ctx - m)
        den = jnp.sum(p_ctx, axis=-1, keepdims=True)
        o = _dot(p_ctx.astype(BF16), vc)
        if local:
            den = den + jnp.sum(p_loc, axis=-1, keepdims=True)
            o = o + _dot(p_loc.astype(BF16), vw)
        o_ref[0, rows, cols] = _unstack_heads(o / den, NA_HEAD_DIM).astype(o_ref.dtype)

    @pl.when(step * rows_per_step < n_rows)
    def _():
        for j in range(rows_per_step):
            for g in range(NA_GROUPS):
                attend(g, j, True)

    @pl.when(step * rows_per_step >= n_rows)
    def _():
        for j in range(rows_per_step):
            for g in range(NA_GROUPS):
                attend(g, j, False)


def _na_bias_table(rpb):
    o = jnp.arange(NA_WIN_ROWS)
    dr = o[None, :] - o[:, None] + NA_WIN_ROWS - 1
    col = jnp.arange(GRID_W)
    dc = jnp.clip(col[None, :] - col[:, None], -(NA_WIN_COLS - 1), NA_WIN_COLS - 1) + NA_WIN_COLS - 1
    col_start = jnp.clip(col - NA_WIN_COLS // 2, 0, GRID_W - NA_WIN_COLS)
    col_mask = (col[None, :] >= col_start[:, None]) & (col[None, :] < col_start[:, None] + NA_WIN_COLS)
    onehot = (dc[None] == jnp.arange(2 * NA_WIN_COLS - 1)[:, None, None]).astype(F32)
    t = jnp.einsum('howc,cqk->howqk', rpb[:, dr].astype(F32), onehot,
                   precision=lax.Precision.HIGHEST)
    t = jnp.where(col_mask[None, None, None], t, MASK_VALUE)
    t = t.transpose(1, 0, 3, 2, 4)
    return t.reshape(NA_WIN_ROWS, NA_GROUPS, HEADS_PER_GROUP * GRID_W, NA_LOCAL_KEYS)


def _neighborhood_attention(qkv, bias_tab, n_lat_tiles, with_ctx_queries):
    bsz, s, _ = qkv.shape
    seq = n_lat_tiles * TM
    n_rows = seq // GRID_W
    assert n_rows >= NA_WIN_ROWS
    n_ctx_rows = (s - seq) // GRID_W
    rq = NA_ROWS_PER_STEP
    assert n_rows % rq == 0 and n_ctx_rows % rq == 0
    nq = (n_rows + (n_ctx_rows if with_ctx_queries else 0)) // rq
    ctx_blk = seq // (s - seq)

    def bias_spec(j):
        def idx(b, i):
            rr = jnp.minimum(i * rq + j, n_rows - 1)
            return (rr - jnp.clip(rr - NA_WIN_ROWS // 2, 0, n_rows - NA_WIN_ROWS), 0, 0, 0)
        return pl.BlockSpec((1,) + bias_tab.shape[1:], idx)

    return pl.pallas_call(
        functools.partial(_na_kernel, n_rows=n_rows),
        grid=(bsz, nq),
        in_specs=[
            pl.BlockSpec((1, rq * GRID_W, NA_WIDTH), lambda b, i: (b, i, 0)),
            pl.BlockSpec((1, seq, NA_WIDTH), lambda b, i: (b, 0, 1)),
            pl.BlockSpec((1, seq, NA_WIDTH), lambda b, i: (b, 0, 2)),
            pl.BlockSpec((1, s - seq, NA_WIDTH), lambda b, i: (b, ctx_blk, 1)),
            pl.BlockSpec((1, s - seq, NA_WIDTH), lambda b, i: (b, ctx_blk, 2)),
        ] + [bias_spec(j) for j in range(rq)],
        out_specs=pl.BlockSpec((1, rq * GRID_W, NA_WIDTH), lambda b, i: (b, i, 0)),
        out_shape=jax.ShapeDtypeStruct((bsz, nq * rq * GRID_W, NA_WIDTH), BF16),
        compiler_params=_params("parallel", "arbitrary"),
        name="neighborhood_attention",
    )(qkv, qkv, qkv, qkv, qkv, *([bias_tab] * rq))


def _pool_kernel(up_ref, u_ref, un_ref, w_ref, sc_ref, o_ref, buf, s0, s1, s2, *, n_lat_tiles, seq, ctx_len):
    i = pl.program_id(1)
    hw = POOL_REACH
    n = TM + 2 * hw
    is_ctx = i >= n_lat_tiles
    base = jnp.where(is_ctx, 0, i * TM)
    lseq = jnp.where(is_ctx, ctx_len, seq)
    buf[0:hw] = up_ref[0]
    buf[hw:hw + TM] = u_ref[0]
    buf[hw + TM:n] = un_ref[0]
    p = base - hw + lax.broadcasted_iota(jnp.int32, (n, POOL_WIDTH), 0)
    buf[...] = jnp.where((p >= 0) & (p < lseq), buf[...], 0.0)
    s0[0:n - 1] = buf[0:n - 1] + buf[1:n]
    s1[0:n - 3] = s0[0:n - 3] + s0[2:n - 1]
    s2[0:n - 7] = s1[0:n - 7] + s1[4:n - 3]
    shape = (TM, POOL_WIDTH)
    t = base + lax.broadcasted_iota(jnp.int32, shape, 0)
    grp = lax.broadcasted_iota(jnp.int32, shape, 1) // POOL_GROUP_DIM
    acc = jnp.where(grp == 0, s0[hw - 1:hw - 1 + TM],
                    jnp.where(grp == 1, s1[hw - 2:hw - 2 + TM],
                              jnp.where(grp == 2, s2[hw - 4:hw - 4 + TM], s2[0:TM] + s2[hw:hw + TM])))
    win = jnp.left_shift(2, grp)
    back = win // 2
    fwd = win - back - 1
    lo = jnp.clip(t - back, 0, lseq - 1)
    hi = jnp.clip(t + fwd, 0, lseq - 1)
    mean = acc / (hi - lo + 1).astype(F32)
    diff = (mean - u_ref[0]).astype(BF16)
    o_ref[0] = (_dot(diff, w_ref[...]) * sc_ref[...]).astype(o_ref.dtype)


def _multiscale_pool(u, w_bd, scale, n_lat_tiles, n_tiles):
    bsz, s, c = u.shape
    hw = POOL_REACH
    per = TM // hw
    last = s // hw - 1
    return pl.pallas_call(
        functools.partial(_pool_kernel, n_lat_tiles=n_lat_tiles, seq=n_lat_tiles * TM, ctx_len=s - n_lat_tiles * TM),
        grid=(bsz, n_tiles),
        in_specs=[
            pl.BlockSpec((1, hw, c), lambda b, i: (b, jnp.maximum(i * per - 1, 0), 0)),
            pl.BlockSpec((1, TM, c), lambda b, i: (b, i, 0)),
            pl.BlockSpec((1, hw, c), lambda b, i: (b, jnp.minimum((i + 1) * per, last), 0)),
            pl.BlockSpec((c, c), lambda b, i: (0, 0)),
            pl.BlockSpec((1, c), lambda b, i: (0, 0)),
        ],
        out_specs=pl.BlockSpec((1, TM, c), lambda b, i: (b, i, 0)),
        out_shape=jax.ShapeDtypeStruct((bsz, n_tiles * TM, c), BF16),
        scratch_shapes=[pltpu.VMEM((TM + 2 * hw, c), F32)] * 4,
        compiler_params=_params("parallel", "parallel"),
        name="multiscale_pool",
    )(u, u, u, w_bd, scale)


def _gla_tile(qk_ref, v_ref, la_ref, cs_ref, sn_ref, st_ref, o_ref, reverse):
    ch = GLA_CHUNK
    lane = lax.broadcasted_iota(jnp.int32, (ch, GLA_QK), 1)
    row = lax.broadcasted_iota(jnp.int32, (ch, GLA_QK), 0)
    rr = lax.broadcasted_iota(jnp.int32, (GLA_HEADS * ch, ch), 0) % ch
    cc = lax.broadcasted_iota(jnp.int32, (GLA_HEADS * ch, ch), 1)
    causal = (rr <= cc) if reverse else (rr >= cc)
    st_shape = (GLA_WIDTH, GLA_QK)
    head_blk = (lax.broadcasted_iota(jnp.int32, st_shape, 0) // GLA_DV
                == lax.broadcasted_iota(jnp.int32, st_shape, 1) // GLA_DK)
    quarter = GLA_DK // 4
    first_half = (lane % (2 * quarter)) < quarter

    def rope(x, cs, sn):
        partner = jnp.where(first_half, pltpu.roll(x, GLA_QK - quarter, 1), pltpu.roll(x, quarter, 1))
        return x * cs + partner * sn

    chunks = range(TM // ch)
    for c in (reversed(chunks) if reverse else chunks):
        sl = slice(c * ch, (c + 1) * ch)
        cs = cs_ref[sl, :]
        sn = sn_ref[sl, :]
        q = rope(qk_ref[0, sl, 0:GLA_QK], cs, sn)
        k = rope(qk_ref[0, sl, GLA_QK:2 * GLA_QK], cs, sn)
        b = la_ref[0, sl, :]
        d = 1
        while d < ch:
            if reverse:
                b = b + jnp.where(row < ch - d, pltpu.roll(b, ch - d, 0), 0.0)
            else:
                b = b + jnp.where(row >= d, pltpu.roll(b, d, 0), 0.0)
            d *= 2
        b_last = b[0:1, :] if reverse else b[ch - 1:ch, :]
        q_in = q * jnp.exp(b)
        k_in = (k * jnp.exp(-b)).astype(BF16)
        k_st = (k * jnp.exp(b_last - b)).astype(BF16)
        att = _dot_nt(_stack_heads(q_in, GLA_DK).astype(BF16), k_in)
        att = jnp.where(causal, att, 0.0).astype(BF16)
        vb = v_ref[0, sl, :].astype(BF16)
        o = _unstack_heads(_dot(att, vb), GLA_DV)
        st = st_ref[...]
        o_ref[0, sl, :] = o + _dot_nt(q_in.astype(BF16), st.astype(BF16))
        st_ref[...] = st * jnp.exp(b_last) + jnp.where(head_blk, _dot_tn(vb, k_st), 0.0)


def _gla_kernel(qkf, vf, laf, csf, snf, qkb, vb, lab, csb, snb, of_ref, ob_ref, st_f, st_b):
    @pl.when(pl.program_id(1) == 0)
    def _():
        st_f[...] = jnp.zeros_like(st_f)
        st_b[...] = jnp.zeros_like(st_b)

    _gla_tile(qkf, vf, laf, csf, snf, st_f, of_ref, False)
    _gla_tile(qkb, vb, lab, csb, snb, st_b, ob_ref, True)


def _rope_tables(seq, s):
    quarter = GLA_DK // 4
    t = np.arange(s)
    pos_row = np.where(t < seq, t // GRID_W, 0).astype(np.float32)
    pos_col = np.where(t < seq, t % GRID_W, 0).astype(np.float32)
    d = np.arange(GLA_QK) % GLA_DK
    inv_freq = ROPE_BASE ** (-jnp.arange(quarter, dtype=F32) / quarter)
    freq = inv_freq[d % quarter]
    pos = jnp.where((d < GLA_DK // 2)[None, :], pos_row[:, None], pos_col[:, None])
    ang = pos * freq[None, :]
    sign = np.where((d % (2 * quarter)) < quarter, -1.0, 1.0).astype(np.float32)
    return jnp.cos(ang), jnp.sin(ang) * sign[None, :]


def _gla(gg, cos, sin, n_lat_tiles):
    bsz, s, _ = gg.shape
    nt = s // TM
    assert nt == n_lat_tiles + 1
    fwd = lambda i: (i + n_lat_tiles) % nt
    bwd = lambda i: jnp.where(i == 0, n_lat_tiles, n_lat_tiles - i)
    la_f_blk = (2 * GLA_QK + 2 * GLA_WIDTH) // GLA_QK
    la_b_blk = la_f_blk + 1
    v_blk = 2 * GLA_QK // GLA_WIDTH

    def specs(order, la_blk):
        return [
            pl.BlockSpec((1, TM, 2 * GLA_QK), lambda b, i: (b, order(i), 0)),
            pl.BlockSpec((1, TM, GLA_WIDTH), lambda b, i: (b, order(i), v_blk)),
            pl.BlockSpec((1, TM, GLA_QK), lambda b, i: (b, order(i), la_blk)),
            pl.BlockSpec((TM, GLA_QK), lambda b, i: (order(i), 0)),
            pl.BlockSpec((TM, GLA_QK), lambda b, i: (order(i), 0)),
        ]

    return pl.pallas_call(
        _gla_kernel,
        grid=(bsz, nt),
        in_specs=specs(fwd, la_f_blk) + specs(bwd, la_b_blk),
        out_specs=[
            pl.BlockSpec((1, TM, GLA_WIDTH), lambda b, i: (b, fwd(i), 0)),
            pl.BlockSpec((1, TM, GLA_WIDTH), lambda b, i: (b, bwd(i), 0)),
        ],
        out_shape=[jax.ShapeDtypeStruct((bsz, s, GLA_WIDTH), F32)] * 2,
        scratch_shapes=[pltpu.VMEM((GLA_WIDTH, GLA_QK), F32)] * 2,
        compiler_params=_params("arbitrary", "arbitrary"),
        name="gla_bidir",
    )(gg, gg, gg, cos, sin, gg, gg, gg, cos, sin)


def _store_row_tiles(ref, val):
    rows, width = val.shape
    pieces = width // LANES
    for s in range(pieces):
        ref[pl.ds(s, rows, stride=pieces), :] = val[:, s * LANES:(s + 1) * LANES]


def _load_row_tiles(ref, rows, pieces):
    return jnp.concatenate([ref[pl.ds(s, rows, stride=pieces), :] for s in range(pieces)], axis=1)


def _row_index(shape):
    return lax.broadcasted_iota(jnp.int32, shape, 0).astype(F32)


def _stack_rows(rows):
    shape = (len(rows), rows[0].shape[1])
    rid = lax.broadcasted_iota(jnp.int32, shape, 0)
    out = jnp.zeros(shape, rows[0].dtype)
    for k, r in enumerate(rows):
        out = jnp.where(rid == k, r, out)
    return out


def _select_experts(scores, biased):
    n_exp, n_tok = biased.shape
    per = n_exp // N_GROUPS
    neg = -jnp.inf
    sub = _row_index((per, n_tok))
    grp_rows = []
    for g in range(N_GROUPS):
        blk = biased[g * per:(g + 1) * per, :]
        m1 = jnp.max(blk, axis=0, keepdims=True)
        i1 = jnp.min(jnp.where(blk == m1, sub, float(per)), axis=0, keepdims=True)
        m2 = jnp.max(jnp.where(sub == i1, neg, blk), axis=0, keepdims=True)
        grp_rows.append(m1 + m2)
    cur = _stack_rows(grp_rows)
    gid = _row_index(cur.shape)
    picked = jnp.zeros(cur.shape, F32)
    for _ in range(TOPK_GROUPS):
        gm = jnp.max(cur, axis=0, keepdims=True)
        gi = jnp.min(jnp.where(cur == gm, gid, float(N_GROUPS)), axis=0, keepdims=True)
        hit = gid == gi
        picked = jnp.where(hit, 1.0, picked)
        cur = jnp.where(hit, neg, cur)
    cur = jnp.concatenate(
        [jnp.where(picked[g:g + 1, :] > 0.0, biased[g * per:(g + 1) * per, :], neg) for g in range(N_GROUPS)], axis=0)
    eid = _row_index(cur.shape)
    ids, vals = [], []
    chosen = jnp.zeros(cur.shape, F32)
    for _ in range(TOP_K):
        m = jnp.max(cur, axis=0, keepdims=True)
        idx = jnp.min(jnp.where(cur == m, eid, float(n_exp)), axis=0, keepdims=True)
        hit = eid == idx
        ids.append(idx)
        vals.append(jnp.sum(jnp.where(hit, scores, 0.0), axis=0, keepdims=True))
        chosen = jnp.where(hit, 1.0, chosen)
        cur = jnp.where(hit, neg, cur)
    return ids, vals, chosen


def _outproj_kernel(x_ref, xc_ref, oa_ref, ob_ref, of_ref, obk_ref, r_ref, gg_ref, wo_ref, mod_ref, gf_ref, wr_ref, br_ref,
                    tri_ref, xo_ref, h_ref, e_ref, w_ref, rk_ref, hist_ref, *, n_lat_tiles):
    o = of_ref[0] + obk_ref[0]
    head = lax.broadcasted_iota(jnp.int32, o.shape, 1) // GLA_DV
    o2 = o * o
    rs = jnp.zeros_like(o)
    for h in range(GLA_HEADS):
        ssq = jnp.sum(jnp.where(head == h, o2, 0.0), axis=-1, keepdims=True)
        rs = jnp.where(head == h, lax.rsqrt(ssq / GLA_DV + NORM_EPS), rs)
    oc = (o * rs * gg_ref[...] * _silu(r_ref[0])).astype(BF16)
    c1 = NA_WIDTH + POOL_WIDTH
    acc = _dot(oa_ref[0], wo_ref[0:NA_WIDTH]) + _dot(ob_ref[0], wo_ref[NA_WIDTH:c1]) + _dot(oc, wo_ref[c1:])
    x = _stream_tile(x_ref, xc_ref, n_lat_tiles) + mod_ref[0, 0, 2:3, :] * acc
    xo_ref[0] = x
    h = _rmsnorm(x, gf_ref[...]) * (1.0 + mod_ref[0, 0, 4:5, :]) + mod_ref[0, 0, 3:4, :]
    _store_row_tiles(h_ref, h)

    scores = _sigmoid(_dot_nt(wr_ref[...], h.astype(BF16)))
    ids, vals, chosen = _select_experts(scores, scores + br_ref[...])
    total = vals[0]
    for v in vals[1:]:
        total = total + v
    w_ref[...] = _stack_rows([v / total * ROUTED_SCALE for v in vals])
    e_ref[...] = _stack_rows(ids).astype(jnp.int32)
    chosen_b = chosen.astype(BF16)
    before = _dot(chosen_b, tri_ref[...])
    eid = _row_index(before.shape)
    rk_ref[...] = _stack_rows([jnp.sum(jnp.where(eid == idx, before, 0.0), axis=0, keepdims=True) for idx in ids])
    hist_ref[0] = _dot_nt(jnp.ones((8, chosen.shape[1]), BF16), chosen_b)


def _outproj(x_lat, x_ctx, ctx_blk, oa, ob, o_f, o_b, gg, g_gla4, w_out, mod, g_ffn, w_router_t, b_router,
             n_lat_tiles, n_tiles):
    bsz, _, d = x_lat.shape
    n_exp = w_router_t.shape[0]
    r_blk = (2 * GLA_QK + GLA_WIDTH) // GLA_WIDTH
    tile = lambda w: pl.BlockSpec((1, TM, w), lambda b, i: (b, i, 0))
    const = lambda b, i: (0, 0)
    per_tok = pl.BlockSpec((TOP_K, TM), lambda b, i: (0, b * n_tiles + i))
    t_tok = bsz * n_tiles * TM
    tri = (jnp.arange(TM)[:, None] < jnp.arange(TM)[None, :]).astype(BF16)
    return pl.pallas_call(
        functools.partial(_outproj_kernel, n_lat_tiles=n_lat_tiles),
        grid=(bsz, n_tiles),
        in_specs=[
            *_stream_specs(d, n_lat_tiles, ctx_blk),
            tile(NA_WIDTH), tile(POOL_WIDTH), tile(GLA_WIDTH), tile(GLA_WIDTH),
            pl.BlockSpec((1, TM, GLA_WIDTH), lambda b, i: (b, i, r_blk)),
            pl.BlockSpec((1, GLA_WIDTH), const),
            pl.BlockSpec(w_out.shape, const),
            pl.BlockSpec((1, 1, 8, d), lambda b, i: (b, jnp.where(i >= n_lat_tiles, 0, 1), 0, 0)),
            pl.BlockSpec((1, d), const),
            pl.BlockSpec(w_router_t.shape, const),
            pl.BlockSpec((n_exp, 1), const),
            pl.BlockSpec((TM, TM), const),
        ],
        out_specs=[tile(d), pl.BlockSpec((TM * d // LANES, LANES), lambda b, i: (b * n_tiles + i, 0)),
                   per_tok, per_tok, per_tok,
                   pl.BlockSpec((1, 8, n_exp), lambda b, i: (b * n_tiles + i, 0, 0))],
        out_shape=[
            jax.ShapeDtypeStruct((bsz, n_tiles * TM, d), F32),
            jax.ShapeDtypeStruct((t_tok * d // LANES, LANES), F32),
            jax.ShapeDtypeStruct((TOP_K, t_tok), jnp.int32),
            jax.ShapeDtypeStruct((TOP_K, t_tok), F32),
            jax.ShapeDtypeStruct((TOP_K, t_tok), F32),
            jax.ShapeDtypeStruct((bsz * n_tiles, 8, n_exp), F32),
        ],
        compiler_params=_params("parallel", "parallel"),
        name="outproj_router",
    )(x_lat, x_ctx, oa, ob, o_f, o_b, gg, g_gla4, w_out, mod, g_ffn, w_router_t, b_router, tri)


def _slot_layout(hist):
    n_tiles, n_exp = hist.shape
    counts = hist.sum(0)
    tile_base = jnp.cumsum(hist, axis=0) - hist
    padded = (counts + MOE_BLOCK - 1) // MOE_BLOCK * MOE_BLOCK
    pad_end = jnp.cumsum(padded)
    pad_start = pad_end - padded
    base = (pad_start[None, :] + tile_base).astype(F32).reshape(n_tiles, n_exp, 1)
    n_blocks = -(-(n_tiles * TM * TOP_K) // MOE_BLOCK) + n_exp
    n_used = (pad_end[-1] // MOE_BLOCK).astype(jnp.int32)
    fill = jnp.concatenate([jnp.maximum(pad_end // MOE_BLOCK - 1, 0).astype(jnp.int32),
                            jnp.minimum(n_used + jnp.arange(n_exp, dtype=jnp.int32), n_blocks - 1)])
    first_blk = (pad_start // MOE_BLOCK).astype(jnp.int32)
    blk_count = (padded // MOE_BLOCK).astype(jnp.int32)
    return base, first_blk, blk_count, fill, n_used.reshape(1), n_blocks


def _pos_kernel(e_ref, rk_ref, base_ref, pos_ref, *, pieces):
    eid = lax.broadcasted_iota(jnp.int32, (base_ref.shape[1], e_ref.shape[1]), 0)
    base = base_ref[0]
    rows = [jnp.sum(jnp.where(eid == e_ref[k:k + 1, :], base, 0.0), axis=0, keepdims=True) for k in range(TOP_K)]
    pos_ref[0] = ((_stack_rows(rows) + rk_ref[...]) * float(pieces)).astype(jnp.int32)


def _slot_positions(e_t, rank_t, base, pieces):
    n_tiles, n_exp, _ = base.shape
    per_tok = pl.BlockSpec((TOP_K, TM), lambda i: (0, i))
    return pl.pallas_call(
        functools.partial(_pos_kernel, pieces=pieces),
        grid=(n_tiles,),
        in_specs=[per_tok, per_tok, pl.BlockSpec((1, n_exp, 1), lambda i: (i, 0, 0))],
        out_specs=pl.BlockSpec((1, TOP_K, TM), lambda i: (i, 0, 0)),
        out_shape=jax.ShapeDtypeStruct((n_tiles, TOP_K, TM), jnp.int32),
        compiler_params=_params("parallel"),
        name="moe_positions",
    )(e_t, rank_t, base).reshape(-1)


def _dispatch_kernel(fill_ref, pos_ref, h_ref, xs_hbm, zbuf, sem, zsem):
    @pl.when(pl.program_id(0) == 0)
    def _():
        zbuf[...] = jnp.zeros_like(zbuf)
        n_fill = fill_ref.shape[0]

        def is_new(j):
            return (j == 0) | (fill_ref[j] != fill_ref[jnp.maximum(j - 1, 0)])

        def start(j, c):
            @pl.when(is_new(j))
            def _():
                row0 = pl.multiple_of(fill_ref[j] * zbuf.shape[0], zbuf.shape[0])
                pltpu.make_async_copy(zbuf, xs_hbm.at[pl.ds(row0, zbuf.shape[0]), :], zsem).start()
            return c
        lax.fori_loop(0, n_fill, start, 0)

        def wait(j, c):
            @pl.when(is_new(j))
            def _():
                pltpu.make_async_copy(zbuf, xs_hbm.at[pl.ds(0, zbuf.shape[0]), :], zsem).wait()
            return c
        lax.fori_loop(0, n_fill, wait, 0)

    pieces = h_ref.shape[0] // TM

    def body(t, c):
        src = h_ref.at[pl.ds(pl.multiple_of(t * pieces, pieces), pieces), :]
        for k in range(TOP_K):
            dst = xs_hbm.at[pl.ds(pl.multiple_of(pos_ref[k * TM + t], pieces), pieces), :]
            pltpu.make_async_copy(src, dst, sem).start(priority=k % 2)
        return c
    lax.fori_loop(0, TM, body, 0, unroll=2)
    for _ in range(TOP_K):
        pltpu.make_async_copy(h_ref, xs_hbm.at[pl.ds(0, h_ref.shape[0]), :], sem).wait()


def _dispatch(h2t, pos, fill, n_slots, pieces):
    n_tiles = h2t.shape[0] // (TM * pieces)
    grid_spec = pltpu.PrefetchScalarGridSpec(
        num_scalar_prefetch=1,
        grid=(n_tiles,),
        in_specs=[
            pl.BlockSpec((TOP_K * TM,), lambda i, fill: (i,), memory_space=pltpu.SMEM),
            pl.BlockSpec((TM * pieces, LANES), lambda i, fill: (i, 0)),
        ],
        out_specs=pl.BlockSpec(memory_space=pl.ANY),
        scratch_shapes=[pltpu.VMEM((MOE_BLOCK * pieces, LANES), F32), pltpu.SemaphoreType.DMA,
                        pltpu.SemaphoreType.DMA],
    )
    return pl.pallas_call(
        _dispatch_kernel,
        grid_spec=grid_spec,
        out_shape=jax.ShapeDtypeStruct((n_slots * pieces, LANES), F32),
        compiler_params=_params("arbitrary"),
        name="moe_dispatch",
    )(fill, pos, h2t)


def _expert_kernel(first_ref, cnt_ref, nu_ref, wi_ref, wd_ref, xs_hbm, ys_hbm, xbuf, ybuf, wi_bf, wd_bf, xsem, ysem,
                   *, n_blocks):
    e = pl.program_id(0)
    n_used = nu_ref[0]
    rows = xbuf.shape[1]
    pieces = rows // MOE_BLOCK
    d_exp = wd_ref.shape[2]

    def x_copy(g, s):
        return pltpu.make_async_copy(xs_hbm.at[pl.ds(pl.multiple_of(g * rows, rows), rows), :], xbuf.at[s], xsem.at[s])

    def y_copy(g, s):
        return pltpu.make_async_copy(ybuf.at[s], ys_hbm.at[pl.ds(pl.multiple_of(g * rows, rows), rows), :], ysem.at[s])

    n_xbuf = xbuf.shape[0]
    n_ybuf = ybuf.shape[0]

    @pl.when(e == 0)
    def _():
        for g in range(n_xbuf - 1):
            @pl.when(g < n_used)
            def _():
                x_copy(g, g).start()

    @pl.when(cnt_ref[e] > 0)
    def _():
        wi_bf[...] = wi_ref[0, 0].astype(BF16)
        wd_bf[...] = wd_ref[0, 0].astype(BF16)

    def block(j, c):
        g = first_ref[e] + j
        s = g % n_ybuf
        x_copy(g, g % n_xbuf).wait()

        @pl.when(g + n_xbuf - 1 < n_used)
        def _():
            x_copy(g + n_xbuf - 1, (g + n_xbuf - 1) % n_xbuf).start()

        @pl.when(g >= n_ybuf)
        def _():
            y_copy(g - n_ybuf, s).wait()

        hh = _dot(_load_row_tiles(xbuf.at[g % n_xbuf], MOE_BLOCK, pieces).astype(BF16), wi_bf[...])
        a = (_silu(hh[:, :d_exp]) * hh[:, d_exp:]).astype(BF16)
        _store_row_tiles(ybuf.at[s], _dot(a, wd_bf[...]))
        y_copy(g, s).start(priority=1)
        return c
    lax.fori_loop(0, cnt_ref[e], block, 0)

    @pl.when(e == pl.num_programs(0) - 1)
    def _():
        for back in range(1, n_ybuf + 1):
            @pl.when(n_used >= back)
            def _():
                y_copy(n_used - back, (n_used - back) % n_ybuf).wait()

        ybuf[0] = jnp.zeros(ybuf.shape[1:], ybuf.dtype)

        def start(g, c):
            y_copy(g, 0).start()
            return c
        lax.fori_loop(n_used, n_blocks, start, 0)

        def wait(g, c):
            y_copy(g, 0).wait()
            return c
        lax.fori_loop(n_used, n_blocks, wait, 0)


def _moe_experts(xs, first_blk, blk_count, n_used, n_blocks, w_e_in, w_e_down, layer):
    _, n_exp, d, f2 = w_e_in.shape
    rows = xs.shape[0] // n_blocks
    grid_spec = pltpu.PrefetchScalarGridSpec(
        num_scalar_prefetch=3,
        grid=(n_exp,),
        in_specs=[
            pl.BlockSpec((1, 1, d, f2), lambda e, first, cnt, nu: (layer, e, 0, 0)),
            pl.BlockSpec((1, 1, f2 // 2, d), lambda e, first, cnt, nu: (layer, e, 0, 0)),
            pl.BlockSpec(memory_space=pl.ANY),
        ],
        out_specs=pl.BlockSpec(memory_space=pl.ANY),
        scratch_shapes=[
            pltpu.VMEM((4, rows, LANES), F32),
            pltpu.VMEM((3, rows, LANES), F32),
            pltpu.VMEM((d, f2), BF16),
            pltpu.VMEM((f2 // 2, d), BF16),
            pltpu.SemaphoreType.DMA((4,)),
            pltpu.SemaphoreType.DMA((3,)),
        ],
    )
    return pl.pallas_call(
        functools.partial(_expert_kernel, n_blocks=n_blocks),
        grid_spec=grid_spec,
        out_shape=jax.ShapeDtypeStruct(xs.shape, F32),
        compiler_params=_params("arbitrary"),
        name="moe_experts",
    )(first_blk, blk_count, n_used, w_e_in, w_e_down, xs)


def _combine_kernel(pos0_ref, posn_ref, x_ref, h_ref, w_ref, wsi_ref, wsd_ref, mod_ref, gfin_ref, ys_hbm, *rest,
                    final, n_steps, n_next):
    next_in, o_ref, next_out = rest[:n_next], rest[n_next], rest[n_next + 1:-2]
    ybuf, sem = rest[-2:]
    step = pl.program_id(0) * pl.num_programs(1) + pl.program_id(1)
    slot = step % 2
    d_exp = wsd_ref.shape[0]
    pieces = h_ref.shape[0] // TMC

    def gather(pos_ref, st, s, lo, hi):
        off = (st % (TM // TMC)) * TMC
        def body(t, c):
            for k in range(TOP_K):
                src = ys_hbm.at[pl.ds(pl.multiple_of(pos_ref[k * TM + off + t], pieces), pieces), :]
                dst = ybuf.at[s, k, pl.ds(pl.multiple_of(t * pieces, pieces), pieces), :]
                pltpu.make_async_copy(src, dst, sem.at[s]).start(priority=k % 2)
            return c
        lax.fori_loop(lo, hi, body, 0, unroll=2)

    n_parts = 2 * TOP_K
    per = TMC // n_parts

    def prefetch(first, last):
        @pl.when(step + 1 < n_steps)
        def _():
            gather(posn_ref, step + 1, 1 - slot, first * per, last * per)

    @pl.when(step == 0)
    def _():
        gather(pos0_ref, 0, 0, 0, TMC)

    prefetch(0, 2)
    hh = _dot(_load_row_tiles(h_ref, TMC, pieces).astype(BF16), wsi_ref[...])
    a = (_silu(hh[:, :d_exp]) * hh[:, d_exp:]).astype(BF16)
    y = _dot(a, wsd_ref[...])
    for k in range(TOP_K):
        pltpu.make_async_copy(ys_hbm.at[pl.ds(0, TMC * pieces), :], ybuf.at[slot, k], sem.at[slot]).wait()
    w = w_ref[...]
    for k in range(TOP_K):
        prefetch(2 + k, 3 + k)
        y = y + _load_row_tiles(ybuf.at[slot, k], TMC, pieces) * w[:, k:k + 1]
    prefetch(2 + TOP_K, n_parts)
    x = x_ref[0] + mod_ref[0, 0, 5:6, :] * y
    if final:
        x = _rmsnorm(x, gfin_ref[...])
    o_ref[0] = x
    if next_in:
        _inproj_tile(x, *next_in, *next_out)


def _combine(x_mid, h2t, ys, pos, w_tok, w_sh_in, w_sh_down, mod, g_final, n_lat_tiles, final, next_proj=None):
    bsz, s, d = x_mid.shape
    nt = s // TMC
    n_steps = bsz * nt
    halves = TM // TMC
    lat_steps = n_lat_tiles * halves
    pieces = d // LANES
    const = lambda b, i: (0, 0)
    tile = pl.BlockSpec((1, TMC, d), lambda b, i: (b, i, 0))
    smem_blk = lambda f: pl.BlockSpec((TOP_K * TM,), f, memory_space=pltpu.SMEM)
    wide = lambda w: pl.BlockSpec((1, TMC, w), lambda b, i: (b, i, 0))
    mod_spec = pl.BlockSpec((1, 1, 8, d), lambda b, i: (b, jnp.where(i >= lat_steps, 0, 1), 0, 0))
    next_args, next_specs, next_outs = (), [], []
    if next_proj is not None:
        next_args = tuple(next_proj)
        next_specs = [mod_spec] + [pl.BlockSpec(a.shape, const) for a in next_proj[1:]]
        next_outs = [(QKV_W, BF16), (POOL_WIDTH, F32), (G_W, F32)]
    out = pl.pallas_call(
        functools.partial(_combine_kernel, final=final, n_steps=n_steps, n_next=len(next_args)),
        grid=(bsz, nt),
        in_specs=[
            smem_blk(lambda b, i: (0,)),
            smem_blk(lambda b, i: (jnp.minimum(b * nt + i + 1, n_steps - 1) // halves,)),
            tile,
            pl.BlockSpec((TMC * pieces, LANES), lambda b, i: (b * nt + i, 0)),
            pl.BlockSpec((TMC, TOP_K), lambda b, i: (b * nt + i, 0)),
            pl.BlockSpec(w_sh_in.shape, const),
            pl.BlockSpec(w_sh_down.shape, const),
            pl.BlockSpec((1, 1, 8, d), lambda b, i: (b, jnp.where(i >= lat_steps, 0, 1), 0, 0)),
            pl.BlockSpec((1, d), const),
            pl.BlockSpec(memory_space=pl.ANY),
        ] + next_specs,
        out_specs=[tile] + [wide(w) for w, _ in next_outs],
        out_shape=[jax.ShapeDtypeStruct((bsz, s, d), F32)]
        + [jax.ShapeDtypeStruct((bsz, s, w), dt) for w, dt in next_outs],
        scratch_shapes=[pltpu.VMEM((2, TOP_K, TMC * pieces, LANES), F32), pltpu.SemaphoreType.DMA((2,))],
        compiler_params=_params("arbitrary", "arbitrary"),
        name="moe_combine",
    )(pos, pos, x_mid, h2t, w_tok, w_sh_in, w_sh_down, mod, g_final, ys, *next_args)
    return out if next_proj is not None else out[0]


def _block_diag(w):
    g, a, b = w.shape
    out = jnp.zeros((g * a, g * b), w.dtype)
    for j in range(g):
        out = out.at[j * a:(j + 1) * a, j * b:(j + 1) * b].set(w[j])
    return out


def kernel(x, c, ctx, c_ctx, w_mod, b_mod, g_mix, w_in, rpb, w_pool, pool_scale, w_gate_f, b_gate_f, w_gate_b, b_gate_b, g_gla, w_out, g_ffn, w_router, b_router, w_e_in, w_e_down, w_sh_in, w_sh_down, g_final):
    bsz, seq, d = x.shape
    n_ctx = ctx.shape[1]
    depth = w_mod.shape[0]
    n_exp = w_router.shape[2]
    assert n_ctx == TM and seq % TM == 0 and seq % GRID_W == 0
    n_lat = seq // TM
    s = seq + n_ctx

    n_rows = -(-(bsz + 1) // 8) * 8
    cc = jnp.zeros((n_rows, d), F32).at[:bsz].set(c).at[bsz].set(c_ctx)
    mod_all = _modulation(cc, w_mod, b_mod).reshape(depth, n_rows, 6, d)
    cos, sin = _rope_tables(seq, s)

    def projection_operands(layer):
        m = mod_all[layer]
        mod = jnp.stack([jnp.broadcast_to(m[bsz], (bsz, 6, d)), m[:bsz]], axis=1)
        mod = jnp.pad(mod, ((0, 0), (0, 0), (0, 2), (0, 0)))
        wl = w_in[layer]
        w_main = wl[:, :MAIN_W].astype(BF16)
        w_low = jnp.pad(wl[:, MAIN_W:], ((0, 0), (0, LANES - 2 * GLA_GATE_RANK))).astype(BF16)
        w2 = jnp.zeros((LANES, 2 * GLA_QK), F32)
        w2 = w2.at[:GLA_GATE_RANK, :GLA_QK].set(w_gate_f[layer])
        w2 = w2.at[GLA_GATE_RANK:2 * GLA_GATE_RANK, GLA_QK:].set(w_gate_b[layer]).astype(BF16)
        b2 = jnp.concatenate([b_gate_f[layer], b_gate_b[layer]])[None, :]
        return mod, g_mix[layer][None, :], w_main, w_low, w2, b2

    proj = [projection_operands(layer) for layer in range(depth)]
    x_lat, x_ctx, ctx_blk = x, ctx, 0
    projected = _inproj(x_lat, x_ctx, ctx_blk, *proj[0], n_lat)
    for layer in range(depth):
        last = layer == depth - 1
        n_tiles = n_lat if last else n_lat + 1
        mod = proj[layer][0]
        qkv, u, gg = projected

        oa = _neighborhood_attention(qkv, _na_bias_table(rpb[layer]), n_lat, not last)
        ob = _multiscale_pool(u, _block_diag(w_pool[layer]).astype(BF16), pool_scale[layer][None, :], n_lat, n_tiles)
        o_f, o_b = _gla(gg, cos, sin, n_lat)

        x_mid, h2t, e_t, w_t, rank_t, hist = _outproj(
            x_lat, x_ctx, ctx_blk, oa, ob, o_f, o_b, gg, jnp.tile(g_gla[layer], GLA_HEADS)[None, :], w_out[layer].astype(BF16), mod,
            g_ffn[layer][None, :], w_router[layer].T.astype(BF16), b_router[layer].reshape(n_exp, 1), n_lat, n_tiles)
        pieces = d // LANES
        base, first_blk, blk_count, fill, n_used, n_blocks = _slot_layout(hist[:, 0, :].astype(jnp.int32))
        pos = _slot_positions(e_t, rank_t, base, pieces)
        xs = _dispatch(h2t, pos, fill, n_blocks * MOE_BLOCK, pieces)
        ys = _moe_experts(xs, first_blk, blk_count, n_used, n_blocks, w_e_in, w_e_down, layer)
        out = _combine(x_mid, h2t, ys, pos, w_t.T, w_sh_in[layer].astype(BF16), w_sh_down[layer].astype(BF16), mod,
                       g_final[None, :], n_lat, last, None if last else proj[layer + 1])
        if last:
            return out
        xa, *projected = out
        x_lat, x_ctx, ctx_blk = xa, xa, n_lat
```

```python
import functools

import jax
import jax.numpy as jnp
import numpy as np
from jax import lax
from jax.experimental import pallas as pl
from jax.experimental.pallas import tpu as pltpu

GRID_W = 64
NORM_EPS = 1e-6
NA_HEADS = 8
NA_HEAD_DIM = 64
NA_WIDTH = NA_HEADS * NA_HEAD_DIM
NA_WIN_ROWS = 8
NA_WIN_COLS = 16
POOL_WINDOWS = (2, 4, 8, 16)
POOL_GROUP_DIM = 64
POOL_WIDTH = len(POOL_WINDOWS) * POOL_GROUP_DIM
POOL_REACH = max(POOL_WINDOWS) // 2
assert POOL_WINDOWS == tuple(2 ** (g + 1) for g in range(len(POOL_WINDOWS)))
GLA_HEADS = 4
GLA_DK = 32
GLA_DV = 64
GLA_QK = GLA_HEADS * GLA_DK
GLA_WIDTH = GLA_HEADS * GLA_DV
GLA_GATE_RANK = 16
GLA_TAU = 16.0
GLA_CHUNK = 64
ROPE_BASE = 10000.0
N_EXPERTS = 256
TOP_K = 8
N_GROUPS = 8
TOPK_GROUPS = 4
ROUTED_SCALE = 2.5

TM = 256
TMC = 128
MOE_BLOCK = 256
LANES = 128
MXU_DIM = 256
MASK_VALUE = -1e30
VMEM_LIMIT = 48 * 1024 * 1024

QKV_W = 3 * NA_WIDTH
G_W = 2 * GLA_QK + 2 * GLA_WIDTH + 2 * GLA_QK
MAIN_W = QKV_W + POOL_WIDTH + 2 * GLA_QK + 2 * GLA_WIDTH

BF16 = jnp.bfloat16
F32 = jnp.float32


def _params(*sem):
    return pltpu.CompilerParams(dimension_semantics=sem, vmem_limit_bytes=VMEM_LIMIT)


def _sigmoid(x):
    return 1.0 / (1.0 + jnp.exp(-x))


def _silu(x):
    return x * _sigmoid(x)


def _rmsnorm(x, g):
    return x * lax.rsqrt(jnp.mean(x * x, axis=-1, keepdims=True) + NORM_EPS) * g


def _dot(a, b):
    return jnp.dot(a, b, preferred_element_type=F32)


def _dot_nt(a, b):
    return lax.dot_general(a, b, (((1,), (1,)), ((), ())), preferred_element_type=F32)


def _dot_tn(a, b):
    return lax.dot_general(a, b, (((0,), (0,)), ((), ())), preferred_element_type=F32)


def _mod_kernel(c_ref, w_ref, b_ref, o_ref):
    a = _silu(c_ref[...]).astype(BF16)
    o_ref[0] = _dot(a, w_ref[0].astype(BF16)) + b_ref[0]


def _modulation(cc, w_mod, b_mod):
    depth, d, n = w_mod.shape
    r = cc.shape[0]
    tn = 1024
    return pl.pallas_call(
        _mod_kernel,
        grid=(depth, n // tn),
        in_specs=[
            pl.BlockSpec((r, d), lambda l, j: (0, 0)),
            pl.BlockSpec((1, d, tn), lambda l, j: (l, 0, j)),
            pl.BlockSpec((1, 1, tn), lambda l, j: (l, 0, j)),
        ],
        out_specs=pl.BlockSpec((1, r, tn), lambda l, j: (l, 0, j)),
        out_shape=jax.ShapeDtypeStruct((depth, r, n), F32),
        compiler_params=_params("parallel", "parallel"),
        name="modulation",
    )(cc, w_mod, b_mod.reshape(depth, 1, n))


def _stream_tile(x_ref, xc_ref, n_lat_tiles):
    return jnp.where(pl.program_id(1) >= n_lat_tiles, xc_ref[0], x_ref[0])


def _stream_specs(d, n_lat_tiles, ctx_blk):
    return [pl.BlockSpec((1, TM, d), lambda b, i: (b, jnp.minimum(i, n_lat_tiles - 1), 0)),
            pl.BlockSpec((1, TM, d), lambda b, i: (b, ctx_blk, 0))]


def _inproj_kernel(x_ref, xc_ref, mod_ref, g_ref, wm_ref, wl_ref, w2_ref, b2_ref, qkv_ref, u_ref, gg_ref, *, n_lat_tiles):
    x = _stream_tile(x_ref, xc_ref, n_lat_tiles)
    h = _rmsnorm(x, g_ref[...]) * (1.0 + mod_ref[0, 0, 1:2, :]) + mod_ref[0, 0, 0:1, :]
    hb = h.astype(BF16)
    q = _dot(hb, wm_ref[:, 0:NA_WIDTH]) * (NA_HEAD_DIM ** -0.5)
    qkv_ref[0, :, 0:NA_WIDTH] = q.astype(BF16)
    qkv_ref[0, :, NA_WIDTH:2 * NA_WIDTH] = _dot(hb, wm_ref[:, NA_WIDTH:2 * NA_WIDTH]).astype(BF16)
    qkv_ref[0, :, 2 * NA_WIDTH:QKV_W] = _dot(hb, wm_ref[:, 2 * NA_WIDTH:QKV_W]).astype(BF16)
    u_ref[0] = _dot(hb, wm_ref[:, QKV_W:QKV_W + POOL_WIDTH])
    c0 = QKV_W + POOL_WIDTH
    gg_ref[0, :, 0:GLA_QK] = _dot(hb, wm_ref[:, c0:c0 + GLA_QK]) * (GLA_DK ** -0.5)
    gg_ref[0, :, GLA_QK:2 * GLA_QK + 2 * GLA_WIDTH] = _dot(hb, wm_ref[:, c0 + GLA_QK:MAIN_W])
    a_low = _dot(hb, wl_ref[...]).astype(BF16)
    lg = _dot(a_low, w2_ref[...]) + b2_ref[...]
    log_sig = jnp.minimum(lg, 0.0) - jnp.log1p(jnp.exp(-jnp.abs(lg)))
    gg_ref[0, :, 2 * GLA_QK + 2 * GLA_WIDTH:G_W] = log_sig / GLA_TAU


def _inproj(x_lat, x_ctx, ctx_blk, mod, g, w_main, w_low, w2, b2, n_lat_tiles):
    bsz, _, d = x_lat.shape
    nt = n_lat_tiles + 1
    s = nt * TM
    const = lambda b, i: (0, 0)
    return pl.pallas_call(
        functools.partial(_inproj_kernel, n_lat_tiles=n_lat_tiles),
        grid=(bsz, nt),
        in_specs=_stream_specs(d, n_lat_tiles, ctx_blk) + [
            pl.BlockSpec((1, 1, 8, d), lambda b, i: (b, jnp.where(i >= n_lat_tiles, 0, 1), 0, 0)),
            pl.BlockSpec((1, d), const),
            pl.BlockSpec(w_main.shape, const),
            pl.BlockSpec(w_low.shape, const),
            pl.BlockSpec(w2.shape, const),
            pl.BlockSpec(b2.shape, const),
        ],
        out_specs=[
            pl.BlockSpec((1, TM, QKV_W), lambda b, i: (b, i, 0)),
            pl.BlockSpec((1, TM, POOL_WIDTH), lambda b, i: (b, i, 0)),
            pl.BlockSpec((1, TM, G_W), lambda b, i: (b, i, 0)),
        ],
        out_shape=[
            jax.ShapeDtypeStruct((bsz, s, QKV_W), BF16),
            jax.ShapeDtypeStruct((bsz, s, POOL_WIDTH), F32),
            jax.ShapeDtypeStruct((bsz, s, G_W), F32),
        ],
        compiler_params=_params("parallel", "parallel"),
        name="inproj",
    )(x_lat, x_ctx, mod, g, w_main, w_low, w2, b2)


HEADS_PER_GROUP = MXU_DIM // NA_HEAD_DIM
NA_GROUPS = NA_HEADS // HEADS_PER_GROUP
NA_LOCAL_KEYS = NA_WIN_ROWS * GRID_W
NA_ROWS_PER_STEP = 4


def _stack_heads(x, width):
    lane = lax.broadcasted_iota(jnp.int32, x.shape, 1) // width
    n_heads = x.shape[1] // width
    return jnp.concatenate([jnp.where(lane == h, x, jnp.zeros_like(x)) for h in range(n_heads)], axis=0)


def _unstack_heads(o, width):
    n_heads = o.shape[1] // width
    r = o.shape[0] // n_heads
    lane = lax.broadcasted_iota(jnp.int32, (r, o.shape[1]), 1) // width
    acc = jnp.zeros((r, o.shape[1]), o.dtype)
    for h in range(n_heads):
        acc = jnp.where(lane == h, o[h * r:(h + 1) * r, :], acc)
    return acc


def _na_kernel(q_ref, k_ref, v_ref, kc_ref, vc_ref, *rest, n_rows):
    bias_refs, o_ref = rest[:-1], rest[-1]
    rows_per_step = len(bias_refs)
    step = pl.program_id(1)

    def attend(g, j, local):
        r = step * rows_per_step + j
        rows = slice(j * GRID_W, (j + 1) * GRID_W)
        cols = slice(g * MXU_DIM, (g + 1) * MXU_DIM)
        qs = _stack_heads(q_ref[0, rows, cols], NA_HEAD_DIM)
        kc = kc_ref[0, :, cols]
        vc = vc_ref[0, :, cols]
        s_ctx = _dot_nt(qs, kc)
        m = jnp.max(s_ctx, axis=-1, keepdims=True)
        if local:
            start = pl.multiple_of(jnp.clip(r - NA_WIN_ROWS // 2, 0, n_rows - NA_WIN_ROWS) * GRID_W, GRID_W)
            kw = k_ref[0, pl.ds(start, NA_LOCAL_KEYS), cols]
            vw = v_ref[0, pl.ds(start, NA_LOCAL_KEYS), cols]
            s_loc = _dot_nt(qs, kw) + bias_refs[j][0, g]
            m = jnp.maximum(m, jnp.max(s_loc, axis=-1, keepdims=True))
            p_loc = jnp.exp(s_loc - m)
        p_ctx = jnp.exp(s_ctx - m)
        den = jnp.sum(p_ctx, axis=-1, keepdims=True)
        o = _dot(p_ctx.astype(BF16), vc)
        if local:
            den = den + jnp.sum(p_loc, axis=-1, keepdims=True)
            o = o + _dot(p_loc.astype(BF16), vw)
        o_ref[0, rows, cols] = _unstack_heads(o / den, NA_HEAD_DIM).astype(o_ref.dtype)

    @pl.when(step * rows_per_step < n_rows)
    def _():
        for j in range(rows_per_step):
            for g in range(NA_GROUPS):
                attend(g, j, True)

    @pl.when(step * rows_per_step >= n_rows)
    def _():
        for j in range(rows_per_step):
            for g in range(NA_GROUPS):
                attend(g, j, False)


def _na_bias_table(rpb):
    o = jnp.arange(NA_WIN_ROWS)
    dr = o[None, :] - o[:, None] + NA_WIN_ROWS - 1
    col = jnp.arange(GRID_W)
    dc = jnp.clip(col[None, :] - col[:, None], -(NA_WIN_COLS - 1), NA_WIN_COLS - 1) + NA_WIN_COLS - 1
    col_start = jnp.clip(col - NA_WIN_COLS // 2, 0, GRID_W - NA_WIN_COLS)
    col_mask = (col[None, :] >= col_start[:, None]) & (col[None, :] < col_start[:, None] + NA_WIN_COLS)
    onehot = (dc[None] == jnp.arange(2 * NA_WIN_COLS - 1)[:, None, None]).astype(F32)
    t = jnp.einsum('howc,cqk->howqk', rpb[:, dr].astype(F32), onehot,
                   precision=lax.Precision.HIGHEST)
    t = jnp.where(col_mask[None, None, None], t, MASK_VALUE)
    t = t.transpose(1, 0, 3, 2, 4)
    return t.reshape(NA_WIN_ROWS, NA_GROUPS, HEADS_PER_GROUP * GRID_W, NA_LOCAL_KEYS)


def _neighborhood_attention(qkv, bias_tab, n_lat_tiles, with_ctx_queries):
    bsz, s, _ = qkv.shape
    seq = n_lat_tiles * TM
    n_rows = seq // GRID_W
    assert n_rows >= NA_WIN_ROWS
    n_ctx_rows = (s - seq) // GRID_W
    rq = NA_ROWS_PER_STEP
    assert n_rows % rq == 0 and n_ctx_rows % rq == 0
    nq = (n_rows + (n_ctx_rows if with_ctx_queries else 0)) // rq
    ctx_blk = seq // (s - seq)

    def bias_spec(j):
        def idx(b, i):
            rr = jnp.minimum(i * rq + j, n_rows - 1)
            return (rr - jnp.clip(rr - NA_WIN_ROWS // 2, 0, n_rows - NA_WIN_ROWS), 0, 0, 0)
        return pl.BlockSpec((1,) + bias_tab.shape[1:], idx)

    return pl.pallas_call(
        functools.partial(_na_kernel, n_rows=n_rows),
        grid=(bsz, nq),
        in_specs=[
            pl.BlockSpec((1, rq * GRID_W, NA_WIDTH), lambda b, i: (b, i, 0)),
            pl.BlockSpec((1, seq, NA_WIDTH), lambda b, i: (b, 0, 1)),
            pl.BlockSpec((1, seq, NA_WIDTH), lambda b, i: (b, 0, 2)),
            pl.BlockSpec((1, s - seq, NA_WIDTH), lambda b, i: (b, ctx_blk, 1)),
            pl.BlockSpec((1, s - seq, NA_WIDTH), lambda b, i: (b, ctx_blk, 2)),
        ] + [bias_spec(j) for j in range(rq)],
        out_specs=pl.BlockSpec((1, rq * GRID_W, NA_WIDTH), lambda b, i: (b, i, 0)),
        out_shape=jax.ShapeDtypeStruct((bsz, nq * rq * GRID_W, NA_WIDTH), BF16),
        compiler_params=_params("parallel", "arbitrary"),
        name="neighborhood_attention",
    )(qkv, qkv, qkv, qkv, qkv, *([bias_tab] * rq))


def _pool_kernel(up_ref, u_ref, un_ref, w_ref, sc_ref, o_ref, buf, s0, s1, s2, *, n_lat_tiles, seq, ctx_len):
    i = pl.program_id(1)
    hw = POOL_REACH
    n = TM + 2 * hw
    is_ctx = i >= n_lat_tiles
    base = jnp.where(is_ctx, 0, i * TM)
    lseq = jnp.where(is_ctx, ctx_len, seq)
    buf[0:hw] = up_ref[0]
    buf[hw:hw + TM] = u_ref[0]
    buf[hw + TM:n] = un_ref[0]
    p = base - hw + lax.broadcasted_iota(jnp.int32, (n, POOL_WIDTH), 0)
    buf[...] = jnp.where((p >= 0) & (p < lseq), buf[...], 0.0)
    s0[0:n - 1] = buf[0:n - 1] + buf[1:n]
    s1[0:n - 3] = s0[0:n - 3] + s0[2:n - 1]
    s2[0:n - 7] = s1[0:n - 7] + s1[4:n - 3]
    shape = (TM, POOL_WIDTH)
    t = base + lax.broadcasted_iota(jnp.int32, shape, 0)
    grp = lax.broadcasted_iota(jnp.int32, shape, 1) // POOL_GROUP_DIM
    acc = jnp.where(grp == 0, s0[hw - 1:hw - 1 + TM],
                    jnp.where(grp == 1, s1[hw - 2:hw - 2 + TM],
                              jnp.where(grp == 2, s2[hw - 4:hw - 4 + TM], s2[0:TM] + s2[hw:hw + TM])))
    win = jnp.left_shift(2, grp)
    back = win // 2
    fwd = win - back - 1
    lo = jnp.clip(t - back, 0, lseq - 1)
    hi = jnp.clip(t + fwd, 0, lseq - 1)
    mean = acc / (hi - lo + 1).astype(F32)
    diff = (mean - u_ref[0]).astype(BF16)
    o_ref[0] = (_dot(diff, w_ref[...]) * sc_ref[...]).astype(o_ref.dtype)


def _multiscale_pool(u, w_bd, scale, n_lat_tiles, n_tiles):
    bsz, s, c = u.shape
    hw = POOL_REACH
    per = TM // hw
    last = s // hw - 1
    return pl.pallas_call(
        functools.partial(_pool_kernel, n_lat_tiles=n_lat_tiles, seq=n_lat_tiles * TM, ctx_len=s - n_lat_tiles * TM),
        grid=(bsz, n_tiles),
        in_specs=[
            pl.BlockSpec((1, hw, c), lambda b, i: (b, jnp.maximum(i * per - 1, 0), 0)),
            pl.BlockSpec((1, TM, c), lambda b, i: (b, i, 0)),
            pl.BlockSpec((1, hw, c), lambda b, i: (b, jnp.minimum((i + 1) * per, last), 0)),
            pl.BlockSpec((c, c), lambda b, i: (0, 0)),
            pl.BlockSpec((1, c), lambda b, i: (0, 0)),
        ],
        out_specs=pl.BlockSpec((1, TM, c), lambda b, i: (b, i, 0)),
        out_shape=jax.ShapeDtypeStruct((bsz, n_tiles * TM, c), BF16),
        scratch_shapes=[pltpu.VMEM((TM + 2 * hw, c), F32)] * 4,
        compiler_params=_params("parallel", "parallel"),
        name="multiscale_pool",
    )(u, u, u, w_bd, scale)


def _gla_tile(qk_ref, v_ref, la_ref, cs_ref, sn_ref, st_ref, o_ref, reverse):
    ch = GLA_CHUNK
    lane = lax.broadcasted_iota(jnp.int32, (ch, GLA_QK), 1)
    row = lax.broadcasted_iota(jnp.int32, (ch, GLA_QK), 0)
    rr = lax.broadcasted_iota(jnp.int32, (GLA_HEADS * ch, ch), 0) % ch
    cc = lax.broadcasted_iota(jnp.int32, (GLA_HEADS * ch, ch), 1)
    causal = (rr <= cc) if reverse else (rr >= cc)
    st_shape = (GLA_WIDTH, GLA_QK)
    head_blk = (lax.broadcasted_iota(jnp.int32, st_shape, 0) // GLA_DV
                == lax.broadcasted_iota(jnp.int32, st_shape, 1) // GLA_DK)
    quarter = GLA_DK // 4
    first_half = (lane % (2 * quarter)) < quarter

    def rope(x, cs, sn):
        partner = jnp.where(first_half, pltpu.roll(x, GLA_QK - quarter, 1), pltpu.roll(x, quarter, 1))
        return x * cs + partner * sn

    chunks = range(TM // ch)
    for c in (reversed(chunks) if reverse else chunks):
        sl = slice(c * ch, (c + 1) * ch)
        cs = cs_ref[sl, :]
        sn = sn_ref[sl, :]
        q = rope(qk_ref[0, sl, 0:GLA_QK], cs, sn)
        k = rope(qk_ref[0, sl, GLA_QK:2 * GLA_QK], cs, sn)
        b = la_ref[0, sl, :]
        d = 1
        while d < ch:
            if reverse:
                b = b + jnp.where(row < ch - d, pltpu.roll(b, ch - d, 0), 0.0)
            else:
                b = b + jnp.where(row >= d, pltpu.roll(b, d, 0), 0.0)
            d *= 2
        b_last = b[0:1, :] if reverse else b[ch - 1:ch, :]
        q_in = q * jnp.exp(b)
        k_in = (k * jnp.exp(-b)).astype(BF16)
        k_st = (k * jnp.exp(b_last - b)).astype(BF16)
        att = _dot_nt(_stack_heads(q_in, GLA_DK).astype(BF16), k_in)
        att = jnp.where(causal, att, 0.0).astype(BF16)
        vb = v_ref[0, sl, :].astype(BF16)
        o = _unstack_heads(_dot(att, vb), GLA_DV)
        st = st_ref[...]
        o_ref[0, sl, :] = o + _dot_nt(q_in.astype(BF16), st.astype(BF16))
        st_ref[...] = st * jnp.exp(b_last) + jnp.where(head_blk, _dot_tn(vb, k_st), 0.0)


def _gla_kernel(qkf, vf, laf, csf, snf, qkb, vb, lab, csb, snb, of_ref, ob_ref, st_f, st_b):
    @pl.when(pl.program_id(1) == 0)
    def _():
        st_f[...] = jnp.zeros_like(st_f)
        st_b[...] = jnp.zeros_like(st_b)

    _gla_tile(qkf, vf, laf, csf, snf, st_f, of_ref, False)
    _gla_tile(qkb, vb, lab, csb, snb, st_b, ob_ref, True)


def _rope_tables(seq, s):
    quarter = GLA_DK // 4
    t = np.arange(s)
    pos_row = np.where(t < seq, t // GRID_W, 0).astype(np.float32)
    pos_col = np.where(t < seq, t % GRID_W, 0).astype(np.float32)
    d = np.arange(GLA_QK) % GLA_DK
    inv_freq = ROPE_BASE ** (-jnp.arange(quarter, dtype=F32) / quarter)
    freq = inv_freq[d % quarter]
    pos = jnp.where((d < GLA_DK // 2)[None, :], pos_row[:, None], pos_col[:, None])
    ang = pos * freq[None, :]
    sign = np.where((d % (2 * quarter)) < quarter, -1.0, 1.0).astype(np.float32)
    return jnp.cos(ang), jnp.sin(ang) * sign[None, :]


def _gla(gg, cos, sin, n_lat_tiles):
    bsz, s, _ = gg.shape
    nt = s // TM
    assert nt == n_lat_tiles + 1
    fwd = lambda i: (i + n_lat_tiles) % nt
    bwd = lambda i: jnp.where(i == 0, n_lat_tiles, n_lat_tiles - i)
    la_f_blk = (2 * GLA_QK + 2 * GLA_WIDTH) // GLA_QK
    la_b_blk = la_f_blk + 1
    v_blk = 2 * GLA_QK // GLA_WIDTH

    def specs(order, la_blk):
        return [
            pl.BlockSpec((1, TM, 2 * GLA_QK), lambda b, i: (b, order(i), 0)),
            pl.BlockSpec((1, TM, GLA_WIDTH), lambda b, i: (b, order(i), v_blk)),
            pl.BlockSpec((1, TM, GLA_QK), lambda b, i: (b, order(i), la_blk)),
            pl.BlockSpec((TM, GLA_QK), lambda b, i: (order(i), 0)),
            pl.BlockSpec((TM, GLA_QK), lambda b, i: (order(i), 0)),
        ]

    return pl.pallas_call(
        _gla_kernel,
        grid=(bsz, nt),
        in_specs=specs(fwd, la_f_blk) + specs(bwd, la_b_blk),
        out_specs=[
            pl.BlockSpec((1, TM, GLA_WIDTH), lambda b, i: (b, fwd(i), 0)),
            pl.BlockSpec((1, TM, GLA_WIDTH), lambda b, i: (b, bwd(i), 0)),
        ],
        out_shape=[jax.ShapeDtypeStruct((bsz, s, GLA_WIDTH), F32)] * 2,
        scratch_shapes=[pltpu.VMEM((GLA_WIDTH, GLA_QK), F32)] * 2,
        compiler_params=_params("arbitrary", "arbitrary"),
        name="gla_bidir",
    )(gg, gg, gg, cos, sin, gg, gg, gg, cos, sin)


def _store_row_tiles(ref, val):
    rows, width = val.shape
    pieces = width // LANES
    for s in range(pieces):
        ref[pl.ds(s, rows, stride=pieces), :] = val[:, s * LANES:(s + 1) * LANES]


def _load_row_tiles(ref, rows, pieces):
    return jnp.concatenate([ref[pl.ds(s, rows, stride=pieces), :] for s in range(pieces)], axis=1)


SLOTS_PER_Y_TILE = 2


def _pack_bf16_pairs(y):
    half = y.shape[1] // 2
    bits = lax.bitcast_convert_type(y.astype(jnp.bfloat16).astype(F32), jnp.uint32)
    return (bits[:, :half] >> 16) | (bits[:, half:] & jnp.uint32(0xFFFF0000))


def _unpack_bf16_pairs(words):
    lo = lax.bitcast_convert_type(words << 16, F32)
    hi = lax.bitcast_convert_type(words & jnp.uint32(0xFFFF0000), F32)
    return jnp.concatenate([lo, hi], axis=1)


def _row_index(shape):
    return lax.broadcasted_iota(jnp.int32, shape, 0).astype(F32)


def _stack_rows(rows):
    shape = (len(rows), rows[0].shape[1])
    rid = lax.broadcasted_iota(jnp.int32, shape, 0)
    out = jnp.zeros(shape, rows[0].dtype)
    for k, r in enumerate(rows):
        out = jnp.where(rid == k, r, out)
    return out


def _select_experts(scores, biased):
    n_exp, n_tok = biased.shape
    per = n_exp // N_GROUPS
    neg = -jnp.inf
    sub = _row_index((per, n_tok))
    grp_rows = []
    for g in range(N_GROUPS):
        blk = biased[g * per:(g + 1) * per, :]
        m1 = jnp.max(blk, axis=0, keepdims=True)
        i1 = jnp.min(jnp.where(blk == m1, sub, float(per)), axis=0, keepdims=True)
        m2 = jnp.max(jnp.where(sub == i1, neg, blk), axis=0, keepdims=True)
        grp_rows.append(m1 + m2)
    cur = _stack_rows(grp_rows)
    gid = _row_index(cur.shape)
    picked = jnp.zeros(cur.shape, F32)
    for _ in range(TOPK_GROUPS):
        gm = jnp.max(cur, axis=0, keepdims=True)
        gi = jnp.min(jnp.where(cur == gm, gid, float(N_GROUPS)), axis=0, keepdims=True)
        hit = gid == gi
        picked = jnp.where(hit, 1.0, picked)
        cur = jnp.where(hit, neg, cur)
    cur = jnp.concatenate(
        [jnp.where(picked[g:g + 1, :] > 0.0, biased[g * per:(g + 1) * per, :], neg) for g in range(N_GROUPS)], axis=0)
    eid = _row_index(cur.shape)
    ids, vals = [], []
    chosen = jnp.zeros(cur.shape, F32)
    for _ in range(TOP_K):
        m = jnp.max(cur, axis=0, keepdims=True)
        idx = jnp.min(jnp.where(cur == m, eid, float(n_exp)), axis=0, keepdims=True)
        hit = eid == idx
        ids.append(idx)
        vals.append(jnp.sum(jnp.where(hit, scores, 0.0), axis=0, keepdims=True))
        chosen = jnp.where(hit, 1.0, chosen)
        cur = jnp.where(hit, neg, cur)
    return ids, vals, chosen


def _outproj_kernel(x_ref, xc_ref, oa_ref, ob_ref, of_ref, obk_ref, r_ref, gg_ref, wo_ref, mod_ref, gf_ref, wr_ref, br_ref,
                    tri_ref, xo_ref, h_ref, e_ref, w_ref, rk_ref, hist_ref, *, n_lat_tiles):
    o = of_ref[0] + obk_ref[0]
    head = lax.broadcasted_iota(jnp.int32, o.shape, 1) // GLA_DV
    o2 = o * o
    rs = jnp.zeros_like(o)
    for h in range(GLA_HEADS):
        ssq = jnp.sum(jnp.where(head == h, o2, 0.0), axis=-1, keepdims=True)
        rs = jnp.where(head == h, lax.rsqrt(ssq / GLA_DV + NORM_EPS), rs)
    oc = (o * rs * gg_ref[...] * _silu(r_ref[0])).astype(BF16)
    c1 = NA_WIDTH + POOL_WIDTH
    acc = _dot(oa_ref[0], wo_ref[0:NA_WIDTH]) + _dot(ob_ref[0], wo_ref[NA_WIDTH:c1]) + _dot(oc, wo_ref[c1:])
    x = _stream_tile(x_ref, xc_ref, n_lat_tiles) + mod_ref[0, 0, 2:3, :] * acc
    xo_ref[0] = x
    h = _rmsnorm(x, gf_ref[...]) * (1.0 + mod_ref[0, 0, 4:5, :]) + mod_ref[0, 0, 3:4, :]
    _store_row_tiles(h_ref, h)

    scores = _sigmoid(_dot_nt(wr_ref[...], h.astype(BF16)))
    ids, vals, chosen = _select_experts(scores, scores + br_ref[...])
    total = vals[0]
    for v in vals[1:]:
        total = total + v
    w_ref[...] = _stack_rows([v / total * ROUTED_SCALE for v in vals])
    e_ref[...] = _stack_rows(ids).astype(jnp.int32)
    chosen_b = chosen.astype(BF16)
    before = _dot(chosen_b, tri_ref[...])
    eid = _row_index(before.shape)
    rk_ref[...] = _stack_rows([jnp.sum(jnp.where(eid == idx, before, 0.0), axis=0, keepdims=True) for idx in ids])
    hist_ref[0] = _dot_nt(jnp.ones((8, chosen.shape[1]), BF16), chosen_b)


def _outproj(x_lat, x_ctx, ctx_blk, oa, ob, o_f, o_b, gg, g_gla4, w_out, mod, g_ffn, w_router_t, b_router,
             n_lat_tiles, n_tiles):
    bsz, _, d = x_lat.shape
    n_exp = w_router_t.shape[0]
    r_blk = (2 * GLA_QK + GLA_WIDTH) // GLA_WIDTH
    tile = lambda w: pl.BlockSpec((1, TM, w), lambda b, i: (b, i, 0))
    const = lambda b, i: (0, 0)
    per_tok = pl.BlockSpec((TOP_K, TM), lambda b, i: (0, b * n_tiles + i))
    t_tok = bsz * n_tiles * TM
    tri = (jnp.arange(TM)[:, None] < jnp.arange(TM)[None, :]).astype(BF16)
    return pl.pallas_call(
        functools.partial(_outproj_kernel, n_lat_tiles=n_lat_tiles),
        grid=(bsz, n_tiles),
        in_specs=[
            *_stream_specs(d, n_lat_tiles, ctx_blk),
            tile(NA_WIDTH), tile(POOL_WIDTH), tile(GLA_WIDTH), tile(GLA_WIDTH),
            pl.BlockSpec((1, TM, GLA_WIDTH), lambda b, i: (b, i, r_blk)),
            pl.BlockSpec((1, GLA_WIDTH), const),
            pl.BlockSpec(w_out.shape, const),
            pl.BlockSpec((1, 1, 8, d), lambda b, i: (b, jnp.where(i >= n_lat_tiles, 0, 1), 0, 0)),
            pl.BlockSpec((1, d), const),
            pl.BlockSpec(w_router_t.shape, const),
            pl.BlockSpec((n_exp, 1), const),
            pl.BlockSpec((TM, TM), const),
        ],
        out_specs=[tile(d), pl.BlockSpec((TM * d // LANES, LANES), lambda b, i: (b * n_tiles + i, 0)),
                   per_tok, per_tok, per_tok,
                   pl.BlockSpec((1, 8, n_exp), lambda b, i: (b * n_tiles + i, 0, 0))],
        out_shape=[
            jax.ShapeDtypeStruct((bsz, n_tiles * TM, d), F32),
            jax.ShapeDtypeStruct((t_tok * d // LANES, LANES), F32),
            jax.ShapeDtypeStruct((TOP_K, t_tok), jnp.int32),
            jax.ShapeDtypeStruct((TOP_K, t_tok), F32),
            jax.ShapeDtypeStruct((TOP_K, t_tok), F32),
            jax.ShapeDtypeStruct((bsz * n_tiles, 8, n_exp), F32),
        ],
        compiler_params=_params("parallel", "parallel"),
        name="outproj_router",
    )(x_lat, x_ctx, oa, ob, o_f, o_b, gg, g_gla4, w_out, mod, g_ffn, w_router_t, b_router, tri)


def _slot_layout(hist):
    n_tiles, n_exp = hist.shape
    counts = hist.sum(0)
    tile_base = jnp.cumsum(hist, axis=0) - hist
    padded = (counts + MOE_BLOCK - 1) // MOE_BLOCK * MOE_BLOCK
    pad_end = jnp.cumsum(padded)
    pad_start = pad_end - padded
    base = (pad_start[None, :] + tile_base).astype(F32).reshape(n_tiles, n_exp, 1)
    n_blocks = -(-(n_tiles * TM * TOP_K) // MOE_BLOCK) + n_exp
    n_used = (pad_end[-1] // MOE_BLOCK).astype(jnp.int32)
    fill = jnp.concatenate([jnp.maximum(pad_end // MOE_BLOCK - 1, 0).astype(jnp.int32),
                            jnp.minimum(n_used + jnp.arange(n_exp, dtype=jnp.int32), n_blocks - 1)])
    first_blk = (pad_start // MOE_BLOCK).astype(jnp.int32)
    blk_count = (padded // MOE_BLOCK).astype(jnp.int32)
    return base, first_blk, blk_count, fill, n_used.reshape(1), n_blocks


def _pos_kernel(e_ref, rk_ref, base_ref, posx_ref, posy_ref, half_ref, *, pieces):
    eid = lax.broadcasted_iota(jnp.int32, (base_ref.shape[1], e_ref.shape[1]), 0)
    base = base_ref[0]
    rows = [jnp.sum(jnp.where(eid == e_ref[k:k + 1, :], base, 0.0), axis=0, keepdims=True) for k in range(TOP_K)]
    slot = (_stack_rows(rows) + rk_ref[...]).astype(jnp.int32)
    posx_ref[0] = slot * pieces
    posy_ref[0] = (slot // SLOTS_PER_Y_TILE) * pieces
    half_ref[...] = slot % SLOTS_PER_Y_TILE


def _slot_positions(e_t, rank_t, base, pieces):
    n_tiles, n_exp, _ = base.shape
    per_tok = pl.BlockSpec((TOP_K, TM), lambda i: (0, i))
    flat = pl.BlockSpec((1, TOP_K, TM), lambda i: (i, 0, 0))
    pos_x, pos_y, half = pl.pallas_call(
        functools.partial(_pos_kernel, pieces=pieces),
        grid=(n_tiles,),
        in_specs=[per_tok, per_tok, pl.BlockSpec((1, n_exp, 1), lambda i: (i, 0, 0))],
        out_specs=[flat, flat, per_tok],
        out_shape=[jax.ShapeDtypeStruct((n_tiles, TOP_K, TM), jnp.int32)] * 2
        + [jax.ShapeDtypeStruct(e_t.shape, jnp.int32)],
        compiler_params=_params("parallel"),
        name="moe_positions",
    )(e_t, rank_t, base)
    return pos_x.reshape(-1), pos_y.reshape(-1), half


def _dispatch_kernel(fill_ref, pos_ref, h_ref, xs_hbm, zbuf, sem, zsem):
    @pl.when(pl.program_id(0) == 0)
    def _():
        zbuf[...] = jnp.zeros_like(zbuf)
        n_fill = fill_ref.shape[0]

        def is_new(j):
            return (j == 0) | (fill_ref[j] != fill_ref[jnp.maximum(j - 1, 0)])

        def start(j, c):
            @pl.when(is_new(j))
            def _():
                row0 = pl.multiple_of(fill_ref[j] * zbuf.shape[0], zbuf.shape[0])
                pltpu.make_async_copy(zbuf, xs_hbm.at[pl.ds(row0, zbuf.shape[0]), :], zsem).start()
            return c
        lax.fori_loop(0, n_fill, start, 0)

        def wait(j, c):
            @pl.when(is_new(j))
            def _():
                pltpu.make_async_copy(zbuf, xs_hbm.at[pl.ds(0, zbuf.shape[0]), :], zsem).wait()
            return c
        lax.fori_loop(0, n_fill, wait, 0)

    pieces = h_ref.shape[0] // TM

    def body(t, c):
        src = h_ref.at[pl.ds(pl.multiple_of(t * pieces, pieces), pieces), :]
        for k in range(TOP_K):
            dst = xs_hbm.at[pl.ds(pl.multiple_of(pos_ref[k * TM + t], pieces), pieces), :]
            pltpu.make_async_copy(src, dst, sem).start(priority=k % 2)
        return c
    lax.fori_loop(0, TM, body, 0, unroll=2)
    for _ in range(TOP_K):
        pltpu.make_async_copy(h_ref, xs_hbm.at[pl.ds(0, h_ref.shape[0]), :], sem).wait()


def _dispatch(h2t, pos, fill, n_slots, pieces):
    n_tiles = h2t.shape[0] // (TM * pieces)
    grid_spec = pltpu.PrefetchScalarGridSpec(
        num_scalar_prefetch=1,
        grid=(n_tiles,),
        in_specs=[
            pl.BlockSpec((TOP_K * TM,), lambda i, fill: (i,), memory_space=pltpu.SMEM),
            pl.BlockSpec((TM * pieces, LANES), lambda i, fill: (i, 0)),
        ],
        out_specs=pl.BlockSpec(memory_space=pl.ANY),
        scratch_shapes=[pltpu.VMEM((MOE_BLOCK * pieces, LANES), F32), pltpu.SemaphoreType.DMA,
                        pltpu.SemaphoreType.DMA],
    )
    return pl.pallas_call(
        _dispatch_kernel,
        grid_spec=grid_spec,
        out_shape=jax.ShapeDtypeStruct((n_slots * pieces, LANES), F32),
        compiler_params=_params("arbitrary"),
        name="moe_dispatch",
    )(fill, pos, h2t)


def _expert_kernel(first_ref, cnt_ref, nu_ref, wi_ref, wd_ref, xs_hbm, ys_hbm, xbuf, ybuf, wi_bf, wd_bf, xsem, ysem,
                   *, n_blocks):
    e = pl.program_id(0)
    n_used = nu_ref[0]
    rows = xbuf.shape[1]
    pieces = rows // MOE_BLOCK
    d_exp = wd_ref.shape[2]

    def x_copy(g, s):
        return pltpu.make_async_copy(xs_hbm.at[pl.ds(pl.multiple_of(g * rows, rows), rows), :], xbuf.at[s], xsem.at[s])

    y_rows = ybuf.shape[1]

    def y_copy(g, s):
        return pltpu.make_async_copy(ybuf.at[s], ys_hbm.at[pl.ds(pl.multiple_of(g * y_rows, y_rows), y_rows), :],
                                     ysem.at[s])

    n_xbuf = xbuf.shape[0]
    n_ybuf = ybuf.shape[0]

    @pl.when(e == 0)
    def _():
        for g in range(n_xbuf - 1):
            @pl.when(g < n_used)
            def _():
                x_copy(g, g).start()

    @pl.when(cnt_ref[e] > 0)
    def _():
        wi_bf[...] = wi_ref[0, 0].astype(BF16)
        wd_bf[...] = wd_ref[0, 0].astype(BF16)

    def block(j, c):
        g = first_ref[e] + j
        s = g % n_ybuf
        x_copy(g, g % n_xbuf).wait()

        @pl.when(g + n_xbuf - 1 < n_used)
        def _():
            x_copy(g + n_xbuf - 1, (g + n_xbuf - 1) % n_xbuf).start()

        @pl.when(g >= n_ybuf)
        def _():
            y_copy(g - n_ybuf, s).wait()

        hh = _dot(_load_row_tiles(xbuf.at[g % n_xbuf], MOE_BLOCK, pieces).astype(BF16), wi_bf[...])
        a = (_silu(hh[:, :d_exp]) * hh[:, d_exp:]).astype(BF16)
        _store_row_tiles(ybuf.at[s], _pack_bf16_pairs(_dot(a, wd_bf[...])))
        y_copy(g, s).start(priority=1)
        return c
    lax.fori_loop(0, cnt_ref[e], block, 0)

    @pl.when(e == pl.num_programs(0) - 1)
    def _():
        for back in range(1, n_ybuf + 1):
            @pl.when(n_used >= back)
            def _():
                y_copy(n_used - back, (n_used - back) % n_ybuf).wait()

        ybuf[0] = jnp.zeros(ybuf.shape[1:], ybuf.dtype)

        def start(g, c):
            y_copy(g, 0).start()
            return c
        lax.fori_loop(n_used, n_blocks, start, 0)

        def wait(g, c):
            y_copy(g, 0).wait()
            return c
        lax.fori_loop(n_used, n_blocks, wait, 0)


def _moe_experts(xs, first_blk, blk_count, n_used, n_blocks, w_e_in, w_e_down, layer):
    _, n_exp, d, f2 = w_e_in.shape
    rows = xs.shape[0] // n_blocks
    grid_spec = pltpu.PrefetchScalarGridSpec(
        num_scalar_prefetch=3,
        grid=(n_exp,),
        in_specs=[
            pl.BlockSpec((1, 1, d, f2), lambda e, first, cnt, nu: (layer, e, 0, 0)),
            pl.BlockSpec((1, 1, f2 // 2, d), lambda e, first, cnt, nu: (layer, e, 0, 0)),
            pl.BlockSpec(memory_space=pl.ANY),
        ],
        out_specs=pl.BlockSpec(memory_space=pl.ANY),
        scratch_shapes=[
            pltpu.VMEM((4, rows, LANES), F32),
            pltpu.VMEM((3, rows // SLOTS_PER_Y_TILE, LANES), jnp.uint32),
            pltpu.VMEM((d, f2), BF16),
            pltpu.VMEM((f2 // 2, d), BF16),
            pltpu.SemaphoreType.DMA((4,)),
            pltpu.SemaphoreType.DMA((3,)),
        ],
    )
    return pl.pallas_call(
        functools.partial(_expert_kernel, n_blocks=n_blocks),
        grid_spec=grid_spec,
        out_shape=jax.ShapeDtypeStruct((xs.shape[0] // SLOTS_PER_Y_TILE, LANES), jnp.uint32),
        compiler_params=_params("arbitrary"),
        name="moe_experts",
    )(first_blk, blk_count, n_used, w_e_in, w_e_down, xs)


def _combine_kernel(pos0_ref, posn_ref, x_ref, h_ref, w_ref, half_ref, wsi_ref, wsd_ref, mod_ref, gfin_ref, ys_hbm,
                    o_ref, ybuf, sem, *, final, n_steps):
    step = pl.program_id(0) * pl.num_programs(1) + pl.program_id(1)
    slot = step % 2
    d_exp = wsd_ref.shape[0]
    pieces = h_ref.shape[0] // TMC

    def gather(pos_ref, st, s):
        off = (st % (TM // TMC)) * TMC
        def body(t, c):
            for k in range(TOP_K):
                src = ys_hbm.at[pl.ds(pl.multiple_of(pos_ref[k * TM + off + t], pieces), pieces), :]
                dst = ybuf.at[s, k, pl.ds(pl.multiple_of(t * pieces, pieces), pieces), :]
                pltpu.make_async_copy(src, dst, sem.at[s]).start(priority=k % 2)
            return c
        lax.fori_loop(0, TMC, body, 0, unroll=2)

    @pl.when(step == 0)
    def _():
        gather(pos0_ref, 0, 0)

    @pl.when(step + 1 < n_steps)
    def _():
        gather(posn_ref, step + 1, 1 - slot)

    hh = _dot(_load_row_tiles(h_ref, TMC, pieces).astype(BF16), wsi_ref[...])
    a = (_silu(hh[:, :d_exp]) * hh[:, d_exp:]).astype(BF16)
    y = _dot(a, wsd_ref[...])
    for k in range(TOP_K):
        pltpu.make_async_copy(ys_hbm.at[pl.ds(0, TMC * pieces), :], ybuf.at[slot, k], sem.at[slot]).wait()
    w = w_ref[...]
    half = half_ref[...]
    for k in range(TOP_K):
        both = _load_row_tiles(ybuf.at[slot, k], TMC, pieces)
        mine = jnp.where(half[:, k:k + 1] > 0, both[:, both.shape[1] // 2:], both[:, :both.shape[1] // 2])
        y = y + _unpack_bf16_pairs(mine) * w[:, k:k + 1]
    x = x_ref[0] + mod_ref[0, 0, 5:6, :] * y
    if final:
        x = _rmsnorm(x, gfin_ref[...])
    o_ref[0] = x


def _combine(x_mid, h2t, ys, pos, w_tok, half_tok, w_sh_in, w_sh_down, mod, g_final, n_lat_tiles, final):
    bsz, s, d = x_mid.shape
    nt = s // TMC
    n_steps = bsz * nt
    halves = TM // TMC
    lat_steps = n_lat_tiles * halves
    pieces = d // LANES
    const = lambda b, i: (0, 0)
    tile = pl.BlockSpec((1, TMC, d), lambda b, i: (b, i, 0))
    smem_blk = lambda f: pl.BlockSpec((TOP_K * TM,), f, memory_space=pltpu.SMEM)
    return pl.pallas_call(
        functools.partial(_combine_kernel, final=final, n_steps=n_steps),
        grid=(bsz, nt),
        in_specs=[
            smem_blk(lambda b, i: (0,)),
            smem_blk(lambda b, i: (jnp.minimum(b * nt + i + 1, n_steps - 1) // halves,)),
            tile,
            pl.BlockSpec((TMC * pieces, LANES), lambda b, i: (b * nt + i, 0)),
            pl.BlockSpec((TMC, TOP_K), lambda b, i: (b * nt + i, 0)),
            pl.BlockSpec((TMC, TOP_K), lambda b, i: (b * nt + i, 0)),
            pl.BlockSpec(w_sh_in.shape, const),
            pl.BlockSpec(w_sh_down.shape, const),
            pl.BlockSpec((1, 1, 8, d), lambda b, i: (b, jnp.where(i >= lat_steps, 0, 1), 0, 0)),
            pl.BlockSpec((1, d), const),
            pl.BlockSpec(memory_space=pl.ANY),
        ],
        out_specs=tile,
        out_shape=jax.ShapeDtypeStruct((bsz, s, d), F32),
        scratch_shapes=[pltpu.VMEM((2, TOP_K, TMC * pieces, LANES), jnp.uint32), pltpu.SemaphoreType.DMA((2,))],
        compiler_params=_params("arbitrary", "arbitrary"),
        name="moe_combine",
    )(pos, pos, x_mid, h2t, w_tok, half_tok, w_sh_in, w_sh_down, mod, g_final, ys)


def _block_diag(w):
    g, a, b = w.shape
    out = jnp.zeros((g * a, g * b), w.dtype)
    for j in range(g):
        out = out.at[j * a:(j + 1) * a, j * b:(j + 1) * b].set(w[j])
    return out


def kernel(x, c, ctx, c_ctx, w_mod, b_mod, g_mix, w_in, rpb, w_pool, pool_scale, w_gate_f, b_gate_f, w_gate_b, b_gate_b, g_gla, w_out, g_ffn, w_router, b_router, w_e_in, w_e_down, w_sh_in, w_sh_down, g_final):
    bsz, seq, d = x.shape
    n_ctx = ctx.shape[1]
    depth = w_mod.shape[0]
    n_exp = w_router.shape[2]
    assert n_ctx == TM and seq % TM == 0 and seq % GRID_W == 0
    n_lat = seq // TM
    s = seq + n_ctx

    n_rows = -(-(bsz + 1) // 8) * 8
    cc = jnp.zeros((n_rows, d), F32).at[:bsz].set(c).at[bsz].set(c_ctx)
    mod_all = _modulation(cc, w_mod, b_mod).reshape(depth, n_rows, 6, d)
    cos, sin = _rope_tables(seq, s)

    x_lat, x_ctx, ctx_blk = x, ctx, 0
    for layer in range(depth):
        last = layer == depth - 1
        n_tiles = n_lat if last else n_lat + 1
        m = mod_all[layer]
        mod = jnp.stack([jnp.broadcast_to(m[bsz], (bsz, 6, d)), m[:bsz]], axis=1)
        mod = jnp.pad(mod, ((0, 0), (0, 0), (0, 2), (0, 0)))
        wl = w_in[layer]
        w_main = wl[:, :MAIN_W].astype(BF16)
        w_low = jnp.pad(wl[:, MAIN_W:], ((0, 0), (0, LANES - 2 * GLA_GATE_RANK))).astype(BF16)
        w2 = jnp.zeros((LANES, 2 * GLA_QK), F32)
        w2 = w2.at[:GLA_GATE_RANK, :GLA_QK].set(w_gate_f[layer])
        w2 = w2.at[GLA_GATE_RANK:2 * GLA_GATE_RANK, GLA_QK:].set(w_gate_b[layer]).astype(BF16)
        b2 = jnp.concatenate([b_gate_f[layer], b_gate_b[layer]])[None, :]
        qkv, u, gg = _inproj(x_lat, x_ctx, ctx_blk, mod, g_mix[layer][None, :], w_main, w_low, w2, b2, n_lat)

        oa = _neighborhood_attention(qkv, _na_bias_table(rpb[layer]), n_lat, not last)
        ob = _multiscale_pool(u, _block_diag(w_pool[layer]).astype(BF16), pool_scale[layer][None, :], n_lat, n_tiles)
        o_f, o_b = _gla(gg, cos, sin, n_lat)

        x_mid, h2t, e_t, w_t, rank_t, hist = _outproj(
            x_lat, x_ctx, ctx_blk, oa, ob, o_f, o_b, gg, jnp.tile(g_gla[layer], GLA_HEADS)[None, :], w_out[layer].astype(BF16), mod,
            g_ffn[layer][None, :], w_router[layer].T.astype(BF16), b_router[layer].reshape(n_exp, 1), n_lat, n_tiles)
        pieces = d // LANES
        base, first_blk, blk_count, fill, n_used, n_blocks = _slot_layout(hist[:, 0, :].astype(jnp.int32))
        pos_x, pos_y, half = _slot_positions(e_t, rank_t, base, pieces)
        xs = _dispatch(h2t, pos_x, fill, n_blocks * MOE_BLOCK, pieces)
        ys = _moe_experts(xs, first_blk, blk_count, n_used, n_blocks, w_e_in, w_e_down, layer)
        xa = _combine(x_mid, h2t, ys, pos_y, w_t.T, half.T, w_sh_in[layer].astype(BF16),
                      w_sh_down[layer].astype(BF16), mod, g_final[None, :], n_lat, last)
        x_lat, x_ctx, ctx_blk = xa, xa, n_lat
    return xa
```

```python
import functools

import jax
import jax.numpy as jnp
import numpy as np
from jax import lax
from jax.experimental import pallas as pl
from jax.experimental.pallas import tpu as pltpu

GRID_W = 64
NORM_EPS = 1e-6
NA_HEADS = 8
NA_HEAD_DIM = 64
NA_WIDTH = NA_HEADS * NA_HEAD_DIM
NA_WIN_ROWS = 8
NA_WIN_COLS = 16
POOL_WINDOWS = (2, 4, 8, 16)
POOL_GROUP_DIM = 64
POOL_WIDTH = len(POOL_WINDOWS) * POOL_GROUP_DIM
POOL_REACH = max(POOL_WINDOWS) // 2
assert POOL_WINDOWS == tuple(2 ** (g + 1) for g in range(len(POOL_WINDOWS)))
GLA_HEADS = 4
GLA_DK = 32
GLA_DV = 64
GLA_QK = GLA_HEADS * GLA_DK
GLA_WIDTH = GLA_HEADS * GLA_DV
GLA_GATE_RANK = 16
GLA_TAU = 16.0
GLA_CHUNK = 64
ROPE_BASE = 10000.0
N_EXPERTS = 256
TOP_K = 8
N_GROUPS = 8
TOPK_GROUPS = 4
ROUTED_SCALE = 2.5

TM = 256
TMC = 256
MOE_BLOCK = 256
LANES = 128
SUBLANES = 8
MXU_DIM = 256
MASK_VALUE = -1e30
VMEM_LIMIT = 48 * 1024 * 1024

QKV_W = 3 * NA_WIDTH
G_W = 2 * GLA_QK + 2 * GLA_WIDTH + 2 * GLA_QK
MAIN_W = QKV_W + POOL_WIDTH + 2 * GLA_QK + 2 * GLA_WIDTH

BF16 = jnp.bfloat16
F32 = jnp.float32


def _params(*sem):
    return pltpu.CompilerParams(dimension_semantics=sem, vmem_limit_bytes=VMEM_LIMIT)


def _sigmoid(x):
    return 1.0 / (1.0 + jnp.exp(-x))


def _silu(x):
    return x * _sigmoid(x)


def _rmsnorm(x, g):
    return x * lax.rsqrt(jnp.mean(x * x, axis=-1, keepdims=True) + NORM_EPS) * g


def _dot(a, b):
    return jnp.dot(a, b, preferred_element_type=F32)


def _dot_nt(a, b):
    return lax.dot_general(a, b, (((1,), (1,)), ((), ())), preferred_element_type=F32)


def _dot_tn(a, b):
    return lax.dot_general(a, b, (((0,), (0,)), ((), ())), preferred_element_type=F32)


def _mod_kernel(c_ref, w_ref, b_ref, o_ref):
    a = _silu(c_ref[...]).astype(BF16)
    o_ref[0] = _dot(a, w_ref[0].astype(BF16)) + b_ref[0]


def _modulation(cc, w_mod, b_mod):
    depth, d, n = w_mod.shape
    r = cc.shape[0]
    tn = d
    return pl.pallas_call(
        _mod_kernel,
        grid=(depth, n // tn),
        in_specs=[
            pl.BlockSpec((r, d), lambda l, j: (0, 0)),
            pl.BlockSpec((1, d, tn), lambda l, j: (l, 0, j)),
            pl.BlockSpec((1, 1, tn), lambda l, j: (l, 0, j)),
        ],
        out_specs=pl.BlockSpec((1, r, tn), lambda l, j: (l, 0, j)),
        out_shape=jax.ShapeDtypeStruct((depth, r, n), F32),
        compiler_params=_params("parallel", "parallel"),
        name="modulation",
    )(cc, w_mod, b_mod.reshape(depth, 1, n))


def _stream_tile(x_ref, xc_ref, n_lat_tiles):
    return jnp.where(pl.program_id(1) >= n_lat_tiles, xc_ref[0], x_ref[0])


def _stream_specs(d, n_lat_tiles, ctx_blk):
    return [pl.BlockSpec((1, TM, d), lambda b, i: (b, jnp.minimum(i, n_lat_tiles - 1), 0)),
            pl.BlockSpec((1, TM, d), lambda b, i: (b, ctx_blk, 0))]


def _inproj_kernel(x_ref, xc_ref, mod_ref, g_ref, wm_ref, wl_ref, w2_ref, b2_ref, qkv_ref, u_ref, gg_ref, *, n_lat_tiles):
    x = _stream_tile(x_ref, xc_ref, n_lat_tiles)
    h = _rmsnorm(x, g_ref[...]) * (1.0 + mod_ref[0, 0, 1:2, :]) + mod_ref[0, 0, 0:1, :]
    hb = h.astype(BF16)
    q = _dot(hb, wm_ref[:, 0:NA_WIDTH]) * (NA_HEAD_DIM ** -0.5)
    qkv_ref[0, :, 0:NA_WIDTH] = q.astype(BF16)
    qkv_ref[0, :, NA_WIDTH:2 * NA_WIDTH] = _dot(hb, wm_ref[:, NA_WIDTH:2 * NA_WIDTH]).astype(BF16)
    qkv_ref[0, :, 2 * NA_WIDTH:QKV_W] = _dot(hb, wm_ref[:, 2 * NA_WIDTH:QKV_W]).astype(BF16)
    u_ref[0] = _dot(hb, wm_ref[:, QKV_W:QKV_W + POOL_WIDTH])
    c0 = QKV_W + POOL_WIDTH
    gg_ref[0, :, 0:GLA_QK] = _dot(hb, wm_ref[:, c0:c0 + GLA_QK]) * (GLA_DK ** -0.5)
    gg_ref[0, :, GLA_QK:2 * GLA_QK + 2 * GLA_WIDTH] = _dot(hb, wm_ref[:, c0 + GLA_QK:MAIN_W])
    a_low = _dot(hb, wl_ref[...]).astype(BF16)
    lg = _dot(a_low, w2_ref[...]) + b2_ref[...]
    log_sig = jnp.minimum(lg, 0.0) - jnp.log1p(jnp.exp(-jnp.abs(lg)))
    gg_ref[0, :, 2 * GLA_QK + 2 * GLA_WIDTH:G_W] = log_sig / GLA_TAU


def _inproj(x_lat, x_ctx, ctx_blk, mod, g, w_main, w_low, w2, b2, n_lat_tiles):
    bsz, _, d = x_lat.shape
    nt = n_lat_tiles + 1
    s = nt * TM
    const = lambda b, i: (0, 0)
    return pl.pallas_call(
        functools.partial(_inproj_kernel, n_lat_tiles=n_lat_tiles),
        grid=(bsz, nt),
        in_specs=_stream_specs(d, n_lat_tiles, ctx_blk) + [
            pl.BlockSpec((1, 1, SUBLANES, d), lambda b, i: (b, jnp.where(i >= n_lat_tiles, 0, 1), 0, 0)),
            pl.BlockSpec((1, d), const),
            pl.BlockSpec(w_main.shape, const),
            pl.BlockSpec(w_low.shape, const),
            pl.BlockSpec(w2.shape, const),
            pl.BlockSpec(b2.shape, const),
        ],
        out_specs=[
            pl.BlockSpec((1, TM, QKV_W), lambda b, i: (b, i, 0)),
            pl.BlockSpec((1, TM, POOL_WIDTH), lambda b, i: (b, i, 0)),
            pl.BlockSpec((1, TM, G_W), lambda b, i: (b, i, 0)),
        ],
        out_shape=[
            jax.ShapeDtypeStruct((bsz, s, QKV_W), BF16),
            jax.ShapeDtypeStruct((bsz, s, POOL_WIDTH), F32),
            jax.ShapeDtypeStruct((bsz, s, G_W), F32),
        ],
        compiler_params=_params("parallel", "parallel"),
        name="inproj",
    )(x_lat, x_ctx, mod, g, w_main, w_low, w2, b2)


HEADS_PER_GROUP = MXU_DIM // NA_HEAD_DIM
NA_GROUPS = NA_HEADS // HEADS_PER_GROUP
NA_LOCAL_KEYS = NA_WIN_ROWS * GRID_W
NA_ROWS_PER_STEP = 4


def _stack_heads(x, width):
    lane = lax.broadcasted_iota(jnp.int32, x.shape, 1) // width
    n_heads = x.shape[1] // width
    return jnp.concatenate([jnp.where(lane == h, x, jnp.zeros_like(x)) for h in range(n_heads)], axis=0)


def _unstack_heads(o, width):
    n_heads = o.shape[1] // width
    r = o.shape[0] // n_heads
    lane = lax.broadcasted_iota(jnp.int32, (r, o.shape[1]), 1) // width
    acc = jnp.zeros((r, o.shape[1]), o.dtype)
    for h in range(n_heads):
        acc = jnp.where(lane == h, o[h * r:(h + 1) * r, :], acc)
    return acc


def _na_kernel(q_ref, k_ref, v_ref, kc_ref, vc_ref, *rest, n_rows):
    bias_refs, o_ref = rest[:-1], rest[-1]
    rows_per_step = len(bias_refs)
    step = pl.program_id(1)

    def attend(g, j, local):
        r = step * rows_per_step + j
        rows = slice(j * GRID_W, (j + 1) * GRID_W)
        cols = slice(g * MXU_DIM, (g + 1) * MXU_DIM)
        qs = _stack_heads(q_ref[0, rows, cols], NA_HEAD_DIM)
        kc = kc_ref[0, :, cols]
        vc = vc_ref[0, :, cols]
        s_ctx = _dot_nt(qs, kc)
        m = jnp.max(s_ctx, axis=-1, keepdims=True)
        if local:
            start = pl.multiple_of(jnp.clip(r - NA_WIN_ROWS // 2, 0, n_rows - NA_WIN_ROWS) * GRID_W, GRID_W)
            kw = k_ref[0, pl.ds(start, NA_LOCAL_KEYS), cols]
            vw = v_ref[0, pl.ds(start, NA_LOCAL_KEYS), cols]
            s_loc = _dot_nt(qs, kw) + bias_refs[j][0, g]
            m = jnp.maximum(m, jnp.max(s_loc, axis=-1, keepdims=True))
            p_loc = jnp.exp(s_loc - m)
        p_ctx = jnp.exp(s_ctx - m)
        den = jnp.sum(p_ctx, axis=-1, keepdims=True)
        o = _dot(p_ctx.astype(BF16), vc)
        if local:
            den = den + jnp.sum(p_loc, axis=-1, keepdims=True)
            o = o + _dot(p_loc.astype(BF16), vw)
        o_ref[0, rows, cols] = _unstack_heads(o / den, NA_HEAD_DIM).astype(o_ref.dtype)

    @pl.when(step * rows_per_step < n_rows)
    def _():
        for j in range(rows_per_step):
            for g in range(NA_GROUPS):
                attend(g, j, True)

    @pl.when(step * rows_per_step >= n_rows)
    def _():
        for j in range(rows_per_step):
            for g in range(NA_GROUPS):
                attend(g, j, False)


def _na_bias_table(rpb):
    o = jnp.arange(NA_WIN_ROWS)
    dr = o[None, :] - o[:, None] + NA_WIN_ROWS - 1
    col = jnp.arange(GRID_W)
    dc = jnp.clip(col[None, :] - col[:, None], -(NA_WIN_COLS - 1), NA_WIN_COLS - 1) + NA_WIN_COLS - 1
    col_start = jnp.clip(col - NA_WIN_COLS // 2, 0, GRID_W - NA_WIN_COLS)
    col_mask = (col[None, :] >= col_start[:, None]) & (col[None, :] < col_start[:, None] + NA_WIN_COLS)
    onehot = (dc[None] == jnp.arange(2 * NA_WIN_COLS - 1)[:, None, None]).astype(F32)
    t = jnp.einsum('howc,cqk->howqk', rpb[:, dr].astype(F32), onehot,
                   precision=lax.Precision.HIGHEST)
    t = jnp.where(col_mask[None, None, None], t, MASK_VALUE)
    t = t.transpose(1, 0, 3, 2, 4)
    return t.reshape(NA_WIN_ROWS, NA_GROUPS, HEADS_PER_GROUP * GRID_W, NA_LOCAL_KEYS)


def _neighborhood_attention(qkv, bias_tab, n_lat_tiles, with_ctx_queries):
    bsz, s, _ = qkv.shape
    seq = n_lat_tiles * TM
    n_rows = seq // GRID_W
    assert n_rows >= NA_WIN_ROWS
    n_ctx_rows = (s - seq) // GRID_W
    rq = NA_ROWS_PER_STEP
    assert n_rows % rq == 0 and n_ctx_rows % rq == 0
    nq = (n_rows + (n_ctx_rows if with_ctx_queries else 0)) // rq
    ctx_blk = seq // (s - seq)

    def bias_spec(j):
        def idx(b, i):
            rr = jnp.minimum(i * rq + j, n_rows - 1)
            return (rr - jnp.clip(rr - NA_WIN_ROWS // 2, 0, n_rows - NA_WIN_ROWS), 0, 0, 0)
        return pl.BlockSpec((1,) + bias_tab.shape[1:], idx)

    return pl.pallas_call(
        functools.partial(_na_kernel, n_rows=n_rows),
        grid=(bsz, nq),
        in_specs=[
            pl.BlockSpec((1, rq * GRID_W, NA_WIDTH), lambda b, i: (b, i, 0)),
            pl.BlockSpec((1, seq, NA_WIDTH), lambda b, i: (b, 0, 1)),
            pl.BlockSpec((1, seq, NA_WIDTH), lambda b, i: (b, 0, 2)),
            pl.BlockSpec((1, s - seq, NA_WIDTH), lambda b, i: (b, ctx_blk, 1)),
            pl.BlockSpec((1, s - seq, NA_WIDTH), lambda b, i: (b, ctx_blk, 2)),
        ] + [bias_spec(j) for j in range(rq)],
        out_specs=pl.BlockSpec((1, rq * GRID_W, NA_WIDTH), lambda b, i: (b, i, 0)),
        out_shape=jax.ShapeDtypeStruct((bsz, nq * rq * GRID_W, NA_WIDTH), BF16),
        compiler_params=_params("parallel", "arbitrary"),
        name="neighborhood_attention",
    )(qkv, qkv, qkv, qkv, qkv, *([bias_tab] * rq))


def _pool_kernel(up_ref, u_ref, un_ref, w_ref, sc_ref, o_ref, buf, s0, s1, s2, *, n_lat_tiles, seq, ctx_len):
    i = pl.program_id(1)
    hw = POOL_REACH
    n = TM + 2 * hw
    is_ctx = i >= n_lat_tiles
    base = jnp.where(is_ctx, 0, i * TM)
    lseq = jnp.where(is_ctx, ctx_len, seq)
    buf[0:hw] = up_ref[0]
    buf[hw:hw + TM] = u_ref[0]
    buf[hw + TM:n] = un_ref[0]
    p = base - hw + lax.broadcasted_iota(jnp.int32, (n, POOL_WIDTH), 0)
    buf[...] = jnp.where((p >= 0) & (p < lseq), buf[...], 0.0)
    s0[0:n - 1] = buf[0:n - 1] + buf[1:n]
    s1[0:n - 3] = s0[0:n - 3] + s0[2:n - 1]
    s2[0:n - 7] = s1[0:n - 7] + s1[4:n - 3]
    shape = (TM, POOL_WIDTH)
    t = base + lax.broadcasted_iota(jnp.int32, shape, 0)
    grp = lax.broadcasted_iota(jnp.int32, shape, 1) // POOL_GROUP_DIM
    acc = jnp.where(grp == 0, s0[hw - 1:hw - 1 + TM],
                    jnp.where(grp == 1, s1[hw - 2:hw - 2 + TM],
                              jnp.where(grp == 2, s2[hw - 4:hw - 4 + TM], s2[0:TM] + s2[hw:hw + TM])))
    win = jnp.left_shift(2, grp)
    back = win // 2
    fwd = win - back - 1
    lo = jnp.clip(t - back, 0, lseq - 1)
    hi = jnp.clip(t + fwd, 0, lseq - 1)
    mean = acc / (hi - lo + 1).astype(F32)
    diff = (mean - u_ref[0]).astype(BF16)
    o_ref[0] = (_dot(diff, w_ref[...]) * sc_ref[...]).astype(o_ref.dtype)


def _multiscale_pool(u, w_bd, scale, n_lat_tiles, n_tiles):
    bsz, s, c = u.shape
    hw = POOL_REACH
    per = TM // hw
    last = s // hw - 1
    return pl.pallas_call(
        functools.partial(_pool_kernel, n_lat_tiles=n_lat_tiles, seq=n_lat_tiles * TM, ctx_len=s - n_lat_tiles * TM),
        grid=(bsz, n_tiles),
        in_specs=[
            pl.BlockSpec((1, hw, c), lambda b, i: (b, jnp.maximum(i * per - 1, 0), 0)),
            pl.BlockSpec((1, TM, c), lambda b, i: (b, i, 0)),
            pl.BlockSpec((1, hw, c), lambda b, i: (b, jnp.minimum((i + 1) * per, last), 0)),
            pl.BlockSpec((c, c), lambda b, i: (0, 0)),
            pl.BlockSpec((1, c), lambda b, i: (0, 0)),
        ],
        out_specs=pl.BlockSpec((1, TM, c), lambda b, i: (b, i, 0)),
        out_shape=jax.ShapeDtypeStruct((bsz, n_tiles * TM, c), BF16),
        scratch_shapes=[pltpu.VMEM((TM + 2 * hw, c), F32)] * 4,
        compiler_params=_params("parallel", "parallel"),
        name="multiscale_pool",
    )(u, u, u, w_bd, scale)


def _gla_tile(qk_ref, v_ref, la_ref, cs_ref, sn_ref, st_ref, o_ref, reverse):
    ch = GLA_CHUNK
    lane = lax.broadcasted_iota(jnp.int32, (ch, GLA_QK), 1)
    row = lax.broadcasted_iota(jnp.int32, (ch, GLA_QK), 0)
    rr = lax.broadcasted_iota(jnp.int32, (GLA_HEADS * ch, ch), 0) % ch
    cc = lax.broadcasted_iota(jnp.int32, (GLA_HEADS * ch, ch), 1)
    causal = (rr <= cc) if reverse else (rr >= cc)
    st_shape = (GLA_WIDTH, GLA_QK)
    head_blk = (lax.broadcasted_iota(jnp.int32, st_shape, 0) // GLA_DV
                == lax.broadcasted_iota(jnp.int32, st_shape, 1) // GLA_DK)
    quarter = GLA_DK // 4
    first_half = (lane % (2 * quarter)) < quarter

    def rope(x, cs, sn):
        partner = jnp.where(first_half, pltpu.roll(x, GLA_QK - quarter, 1), pltpu.roll(x, quarter, 1))
        return x * cs + partner * sn

    chunks = range(TM // ch)
    for c in (reversed(chunks) if reverse else chunks):
        sl = slice(c * ch, (c + 1) * ch)
        cs = cs_ref[sl, :]
        sn = sn_ref[sl, :]
        q = rope(qk_ref[0, sl, 0:GLA_QK], cs, sn)
        k = rope(qk_ref[0, sl, GLA_QK:2 * GLA_QK], cs, sn)
        b = la_ref[0, sl, :]
        d = 1
        while d < ch:
            if reverse:
                b = b + jnp.where(row < ch - d, pltpu.roll(b, ch - d, 0), 0.0)
            else:
                b = b + jnp.where(row >= d, pltpu.roll(b, d, 0), 0.0)
            d *= 2
        b_last = b[0:1, :] if reverse else b[ch - 1:ch, :]
        q_in = q * jnp.exp(b)
        k_in = (k * jnp.exp(-b)).astype(BF16)
        k_st = (k * jnp.exp(b_last - b)).astype(BF16)
        att = _dot_nt(_stack_heads(q_in, GLA_DK).astype(BF16), k_in)
        att = jnp.where(causal, att, 0.0).astype(BF16)
        vb = v_ref[0, sl, :].astype(BF16)
        o = _unstack_heads(_dot(att, vb), GLA_DV)
        st = st_ref[...]
        o_ref[0, sl, :] = o + _dot_nt(q_in.astype(BF16), st.astype(BF16))
        st_ref[...] = st * jnp.exp(b_last) + jnp.where(head_blk, _dot_tn(vb, k_st), 0.0)


def _gla_kernel(qkf, vf, laf, csf, snf, qkb, vb, lab, csb, snb, of_ref, ob_ref, st_f, st_b):
    @pl.when(pl.program_id(1) == 0)
    def _():
        st_f[...] = jnp.zeros_like(st_f)
        st_b[...] = jnp.zeros_like(st_b)

    _gla_tile(qkf, vf, laf, csf, snf, st_f, of_ref, False)
    _gla_tile(qkb, vb, lab, csb, snb, st_b, ob_ref, True)


def _rope_tables(seq, s):
    quarter = GLA_DK // 4
    t = np.arange(s)
    pos_row = np.where(t < seq, t // GRID_W, 0).astype(np.float32)
    pos_col = np.where(t < seq, t % GRID_W, 0).astype(np.float32)
    d = np.arange(GLA_QK) % GLA_DK
    inv_freq = ROPE_BASE ** (-jnp.arange(quarter, dtype=F32) / quarter)
    freq = inv_freq[d % quarter]
    pos = jnp.where((d < GLA_DK // 2)[None, :], pos_row[:, None], pos_col[:, None])
    ang = pos * freq[None, :]
    sign = np.where((d % (2 * quarter)) < quarter, -1.0, 1.0).astype(np.float32)
    return jnp.cos(ang), jnp.sin(ang) * sign[None, :]


def _gla(gg, cos, sin, n_lat_tiles):
    bsz, s, _ = gg.shape
    nt = s // TM
    assert nt == n_lat_tiles + 1
    fwd = lambda i: (i + n_lat_tiles) % nt
    bwd = lambda i: jnp.where(i == 0, n_lat_tiles, n_lat_tiles - i)
    la_f_blk = (2 * GLA_QK + 2 * GLA_WIDTH) // GLA_QK
    la_b_blk = la_f_blk + 1
    v_blk = 2 * GLA_QK // GLA_WIDTH

    def specs(order, la_blk):
        return [
            pl.BlockSpec((1, TM, 2 * GLA_QK), lambda b, i: (b, order(i), 0)),
            pl.BlockSpec((1, TM, GLA_WIDTH), lambda b, i: (b, order(i), v_blk)),
            pl.BlockSpec((1, TM, GLA_QK), lambda b, i: (b, order(i), la_blk)),
            pl.BlockSpec((TM, GLA_QK), lambda b, i: (order(i), 0)),
            pl.BlockSpec((TM, GLA_QK), lambda b, i: (order(i), 0)),
        ]

    return pl.pallas_call(
        _gla_kernel,
        grid=(bsz, nt),
        in_specs=specs(fwd, la_f_blk) + specs(bwd, la_b_blk),
        out_specs=[
            pl.BlockSpec((1, TM, GLA_WIDTH), lambda b, i: (b, fwd(i), 0)),
            pl.BlockSpec((1, TM, GLA_WIDTH), lambda b, i: (b, bwd(i), 0)),
        ],
        out_shape=[jax.ShapeDtypeStruct((bsz, s, GLA_WIDTH), F32)] * 2,
        scratch_shapes=[pltpu.VMEM((GLA_WIDTH, GLA_QK), F32)] * 2,
        compiler_params=_params("arbitrary", "arbitrary"),
        name="gla_bidir",
    )(gg, gg, gg, cos, sin, gg, gg, gg, cos, sin)


def _store_row_tiles(ref, val):
    rows, width = val.shape
    pieces = width // LANES
    for s in range(pieces):
        ref[pl.ds(s, rows, stride=pieces), :] = val[:, s * LANES:(s + 1) * LANES]


def _load_row_tiles(ref, rows, pieces):
    return jnp.concatenate([ref[pl.ds(s, rows, stride=pieces), :] for s in range(pieces)], axis=1)


SLOTS_PER_Y_TILE = 2


def _pack_bf16_pairs(y):
    half = y.shape[1] // 2
    bits = lax.bitcast_convert_type(y.astype(jnp.bfloat16).astype(F32), jnp.uint32)
    return (bits[:, :half] >> 16) | (bits[:, half:] & jnp.uint32(0xFFFF0000))


def _unpack_bf16_pairs(words):
    lo = lax.bitcast_convert_type(words << 16, F32)
    hi = lax.bitcast_convert_type(words & jnp.uint32(0xFFFF0000), F32)
    return jnp.concatenate([lo, hi], axis=1)


def _row_index(shape):
    return lax.broadcasted_iota(jnp.int32, shape, 0).astype(F32)


def _stack_rows(rows):
    shape = (len(rows), rows[0].shape[1])
    rid = lax.broadcasted_iota(jnp.int32, shape, 0)
    out = jnp.zeros(shape, rows[0].dtype)
    for k, r in enumerate(rows):
        out = jnp.where(rid == k, r, out)
    return out


def _select_experts(scores, biased):
    n_exp, n_tok = biased.shape
    per = n_exp // N_GROUPS
    neg = -jnp.inf
    sub = _row_index((per, n_tok))
    grp_rows = []
    for g in range(N_GROUPS):
        blk = biased[g * per:(g + 1) * per, :]
        m1 = jnp.max(blk, axis=0, keepdims=True)
        i1 = jnp.min(jnp.where(blk == m1, sub, float(per)), axis=0, keepdims=True)
        m2 = jnp.max(jnp.where(sub == i1, neg, blk), axis=0, keepdims=True)
        grp_rows.append(m1 + m2)
    cur = _stack_rows(grp_rows)
    gid = _row_index(cur.shape)
    picked = jnp.zeros(cur.shape, F32)
    for _ in range(TOPK_GROUPS):
        gm = jnp.max(cur, axis=0, keepdims=True)
        gi = jnp.min(jnp.where(cur == gm, gid, float(N_GROUPS)), axis=0, keepdims=True)
        hit = gid == gi
        picked = jnp.where(hit, 1.0, picked)
        cur = jnp.where(hit, neg, cur)
    cur = jnp.concatenate(
        [jnp.where(picked[g:g + 1, :] > 0.0, biased[g * per:(g + 1) * per, :], neg) for g in range(N_GROUPS)], axis=0)
    eid = _row_index(cur.shape)
    ids, vals = [], []
    chosen = jnp.zeros(cur.shape, F32)
    for _ in range(TOP_K):
        m = jnp.max(cur, axis=0, keepdims=True)
        idx = jnp.min(jnp.where(cur == m, eid, float(n_exp)), axis=0, keepdims=True)
        hit = eid == idx
        ids.append(idx)
        vals.append(jnp.sum(jnp.where(hit, scores, 0.0), axis=0, keepdims=True))
        chosen = jnp.where(hit, 1.0, chosen)
        cur = jnp.where(hit, neg, cur)
    return ids, vals, chosen


def _outproj_kernel(x_ref, xc_ref, oa_ref, ob_ref, of_ref, obk_ref, r_ref, gg_ref, wo_ref, mod_ref, gf_ref, wr_ref, br_ref,
                    tri_ref, xo_ref, h_ref, e_ref, w_ref, rk_ref, hist_ref, *, n_lat_tiles):
    o = of_ref[0] + obk_ref[0]
    head = lax.broadcasted_iota(jnp.int32, o.shape, 1) // GLA_DV
    o2 = o * o
    rs = jnp.zeros_like(o)
    for h in range(GLA_HEADS):
        ssq = jnp.sum(jnp.where(head == h, o2, 0.0), axis=-1, keepdims=True)
        rs = jnp.where(head == h, lax.rsqrt(ssq / GLA_DV + NORM_EPS), rs)
    oc = (o * rs * gg_ref[...] * _silu(r_ref[0])).astype(BF16)
    c1 = NA_WIDTH + POOL_WIDTH
    acc = _dot(oa_ref[0], wo_ref[0:NA_WIDTH]) + _dot(ob_ref[0], wo_ref[NA_WIDTH:c1]) + _dot(oc, wo_ref[c1:])
    x = _stream_tile(x_ref, xc_ref, n_lat_tiles) + mod_ref[0, 0, 2:3, :] * acc
    xo_ref[0] = x
    h = _rmsnorm(x, gf_ref[...]) * (1.0 + mod_ref[0, 0, 4:5, :]) + mod_ref[0, 0, 3:4, :]
    _store_row_tiles(h_ref, h)

    scores = _sigmoid(_dot_nt(wr_ref[...], h.astype(BF16)))
    ids, vals, chosen = _select_experts(scores, scores + br_ref[...])
    total = vals[0]
    for v in vals[1:]:
        total = total + v
    w_ref[...] = _stack_rows([v / total * ROUTED_SCALE for v in vals])
    e_ref[...] = _stack_rows(ids).astype(jnp.int32)
    chosen_b = chosen.astype(BF16)
    before = _dot(chosen_b, tri_ref[...])
    eid = _row_index(before.shape)
    rk_ref[...] = _stack_rows([jnp.sum(jnp.where(eid == idx, before, 0.0), axis=0, keepdims=True) for idx in ids])
    hist_ref[0] = _dot_nt(jnp.ones((SUBLANES, chosen.shape[1]), BF16), chosen_b)


def _outproj(x_lat, x_ctx, ctx_blk, oa, ob, o_f, o_b, gg, g_gla4, w_out, mod, g_ffn, w_router_t, b_router,
             n_lat_tiles, n_tiles):
    bsz, _, d = x_lat.shape
    n_exp = w_router_t.shape[0]
    r_blk = (2 * GLA_QK + GLA_WIDTH) // GLA_WIDTH
    tile = lambda w: pl.BlockSpec((1, TM, w), lambda b, i: (b, i, 0))
    const = lambda b, i: (0, 0)
    per_tok = pl.BlockSpec((TOP_K, TM), lambda b, i: (0, b * n_tiles + i))
    t_tok = bsz * n_tiles * TM
    tri = (jnp.arange(TM)[:, None] < jnp.arange(TM)[None, :]).astype(BF16)
    return pl.pallas_call(
        functools.partial(_outproj_kernel, n_lat_tiles=n_lat_tiles),
        grid=(bsz, n_tiles),
        in_specs=[
            *_stream_specs(d, n_lat_tiles, ctx_blk),
            tile(NA_WIDTH), tile(POOL_WIDTH), tile(GLA_WIDTH), tile(GLA_WIDTH),
            pl.BlockSpec((1, TM, GLA_WIDTH), lambda b, i: (b, i, r_blk)),
            pl.BlockSpec((1, GLA_WIDTH), const),
            pl.BlockSpec(w_out.shape, const),
            pl.BlockSpec((1, 1, SUBLANES, d), lambda b, i: (b, jnp.where(i >= n_lat_tiles, 0, 1), 0, 0)),
            pl.BlockSpec((1, d), const),
            pl.BlockSpec(w_router_t.shape, const),
            pl.BlockSpec((n_exp, 1), const),
            pl.BlockSpec((TM, TM), const),
        ],
        out_specs=[tile(d), pl.BlockSpec((TM * d // LANES, LANES), lambda b, i: (b * n_tiles + i, 0)),
                   per_tok, per_tok, per_tok,
                   pl.BlockSpec((1, SUBLANES, n_exp), lambda b, i: (b * n_tiles + i, 0, 0))],
        out_shape=[
            jax.ShapeDtypeStruct((bsz, n_tiles * TM, d), F32),
            jax.ShapeDtypeStruct((t_tok * d // LANES, LANES), F32),
            jax.ShapeDtypeStruct((TOP_K, t_tok), jnp.int32),
            jax.ShapeDtypeStruct((TOP_K, t_tok), F32),
            jax.ShapeDtypeStruct((TOP_K, t_tok), F32),
            jax.ShapeDtypeStruct((bsz * n_tiles, SUBLANES, n_exp), F32),
        ],
        compiler_params=_params("parallel", "parallel"),
        name="outproj_router",
    )(x_lat, x_ctx, oa, ob, o_f, o_b, gg, g_gla4, w_out, mod, g_ffn, w_router_t, b_router, tri)


def _slot_layout(hist):
    n_tiles, n_exp = hist.shape
    counts = hist.sum(0)
    tile_base = jnp.cumsum(hist, axis=0) - hist
    padded = (counts + MOE_BLOCK - 1) // MOE_BLOCK * MOE_BLOCK
    pad_end = jnp.cumsum(padded)
    pad_start = pad_end - padded
    base = (pad_start[None, :] + tile_base).astype(F32).reshape(n_tiles, n_exp, 1)
    n_blocks = -(-(n_tiles * TM * TOP_K) // MOE_BLOCK) + n_exp
    n_used = (pad_end[-1] // MOE_BLOCK).astype(jnp.int32)
    fill = jnp.concatenate([jnp.maximum(pad_end // MOE_BLOCK - 1, 0).astype(jnp.int32),
                            jnp.minimum(n_used + jnp.arange(n_exp, dtype=jnp.int32), n_blocks - 1)])
    first_blk = (pad_start // MOE_BLOCK).astype(jnp.int32)
    blk_count = (padded // MOE_BLOCK).astype(jnp.int32)
    return base, first_blk, blk_count, fill, n_used.reshape(1), n_blocks


def _pos_kernel(e_ref, rk_ref, base_ref, posx_ref, posy_ref, half_ref, *, pieces):
    eid = lax.broadcasted_iota(jnp.int32, (base_ref.shape[1], e_ref.shape[1]), 0)
    base = base_ref[0]
    rows = [jnp.sum(jnp.where(eid == e_ref[k:k + 1, :], base, 0.0), axis=0, keepdims=True) for k in range(TOP_K)]
    slot = (_stack_rows(rows) + rk_ref[...]).astype(jnp.int32)
    posx_ref[0] = slot * pieces
    posy_ref[0] = (slot // SLOTS_PER_Y_TILE) * pieces
    half_ref[...] = slot % SLOTS_PER_Y_TILE


def _slot_positions(e_t, rank_t, base, pieces):
    n_tiles, n_exp, _ = base.shape
    per_tok = pl.BlockSpec((TOP_K, TM), lambda i: (0, i))
    flat = pl.BlockSpec((1, TOP_K, TM), lambda i: (i, 0, 0))
    pos_x, pos_y, half = pl.pallas_call(
        functools.partial(_pos_kernel, pieces=pieces),
        grid=(n_tiles,),
        in_specs=[per_tok, per_tok, pl.BlockSpec((1, n_exp, 1), lambda i: (i, 0, 0))],
        out_specs=[flat, flat, per_tok],
        out_shape=[jax.ShapeDtypeStruct((n_tiles, TOP_K, TM), jnp.int32)] * 2
        + [jax.ShapeDtypeStruct(e_t.shape, jnp.int32)],
        compiler_params=_params("parallel"),
        name="moe_positions",
    )(e_t, rank_t, base)
    return pos_x.reshape(-1), pos_y.reshape(-1), half


def _dispatch_kernel(fill_ref, pos_ref, h_ref, xs_hbm, zbuf, sem, zsem):
    @pl.when(pl.program_id(0) == 0)
    def _():
        zbuf[...] = jnp.zeros_like(zbuf)
        n_fill = fill_ref.shape[0]

        def is_new(j):
            return (j == 0) | (fill_ref[j] != fill_ref[jnp.maximum(j - 1, 0)])

        def start(j, c):
            @pl.when(is_new(j))
            def _():
                row0 = pl.multiple_of(fill_ref[j] * zbuf.shape[0], zbuf.shape[0])
                pltpu.make_async_copy(zbuf, xs_hbm.at[pl.ds(row0, zbuf.shape[0]), :], zsem).start()
            return c
        lax.fori_loop(0, n_fill, start, 0)

        def wait(j, c):
            @pl.when(is_new(j))
            def _():
                pltpu.make_async_copy(zbuf, xs_hbm.at[pl.ds(0, zbuf.shape[0]), :], zsem).wait()
            return c
        lax.fori_loop(0, n_fill, wait, 0)

    pieces = h_ref.shape[0] // TM

    def body(t, c):
        src = h_ref.at[pl.ds(pl.multiple_of(t * pieces, pieces), pieces), :]
        for k in range(TOP_K):
            dst = xs_hbm.at[pl.ds(pl.multiple_of(pos_ref[k * TM + t], pieces), pieces), :]
            pltpu.make_async_copy(src, dst, sem).start(priority=k % 2)
        return c
    lax.fori_loop(0, TM, body, 0, unroll=2)
    for _ in range(TOP_K):
        pltpu.make_async_copy(h_ref, xs_hbm.at[pl.ds(0, h_ref.shape[0]), :], sem).wait()


def _dispatch(h2t, pos, fill, n_slots, pieces):
    n_tiles = h2t.shape[0] // (TM * pieces)
    grid_spec = pltpu.PrefetchScalarGridSpec(
        num_scalar_prefetch=1,
        grid=(n_tiles,),
        in_specs=[
            pl.BlockSpec((TOP_K * TM,), lambda i, fill: (i,), memory_space=pltpu.SMEM),
            pl.BlockSpec((TM * pieces, LANES), lambda i, fill: (i, 0)),
        ],
        out_specs=pl.BlockSpec(memory_space=pl.ANY),
        scratch_shapes=[pltpu.VMEM((MOE_BLOCK * pieces, LANES), F32), pltpu.SemaphoreType.DMA,
                        pltpu.SemaphoreType.DMA],
    )
    return pl.pallas_call(
        _dispatch_kernel,
        grid_spec=grid_spec,
        out_shape=jax.ShapeDtypeStruct((n_slots * pieces, LANES), F32),
        compiler_params=_params("arbitrary"),
        name="moe_dispatch",
    )(fill, pos, h2t)


def _expert_kernel(first_ref, cnt_ref, nu_ref, wi_ref, wd_ref, xs_hbm, ys_hbm, xbuf, ybuf, wi_bf, wd_bf, xsem, ysem,
                   *, n_blocks):
    e = pl.program_id(0)
    n_used = nu_ref[0]
    rows = xbuf.shape[1]
    pieces = rows // MOE_BLOCK
    d_exp = wd_ref.shape[2]

    def x_copy(g, s):
        return pltpu.make_async_copy(xs_hbm.at[pl.ds(pl.multiple_of(g * rows, rows), rows), :], xbuf.at[s], xsem.at[s])

    y_rows = ybuf.shape[1]

    def y_copy(g, s):
        return pltpu.make_async_copy(ybuf.at[s], ys_hbm.at[pl.ds(pl.multiple_of(g * y_rows, y_rows), y_rows), :],
                                     ysem.at[s])

    n_xbuf = xbuf.shape[0]
    n_ybuf = ybuf.shape[0]

    @pl.when(e == 0)
    def _():
        for g in range(n_xbuf - 1):
            @pl.when(g < n_used)
            def _():
                x_copy(g, g).start()

    @pl.when(cnt_ref[e] > 0)
    def _():
        wi_bf[...] = wi_ref[0, 0].astype(BF16)
        wd_bf[...] = wd_ref[0, 0].astype(BF16)

    def block(j, c):
        g = first_ref[e] + j
        s = g % n_ybuf
        x_copy(g, g % n_xbuf).wait()

        @pl.when(g + n_xbuf - 1 < n_used)
        def _():
            x_copy(g + n_xbuf - 1, (g + n_xbuf - 1) % n_xbuf).start()

        @pl.when(g >= n_ybuf)
        def _():
            y_copy(g - n_ybuf, s).wait()

        hh = _dot(_load_row_tiles(xbuf.at[g % n_xbuf], MOE_BLOCK, pieces).astype(BF16), wi_bf[...])
        a = (_silu(hh[:, :d_exp]) * hh[:, d_exp:]).astype(BF16)
        _store_row_tiles(ybuf.at[s], _pack_bf16_pairs(_dot(a, wd_bf[...])))
        y_copy(g, s).start(priority=1)
        return c
    lax.fori_loop(0, cnt_ref[e], block, 0)

    @pl.when(e == pl.num_programs(0) - 1)
    def _():
        for back in range(1, n_ybuf + 1):
            @pl.when(n_used >= back)
            def _():
                y_copy(n_used - back, (n_used - back) % n_ybuf).wait()

        ybuf[0] = jnp.zeros(ybuf.shape[1:], ybuf.dtype)

        def start(g, c):
            y_copy(g, 0).start()
            return c
        lax.fori_loop(n_used, n_blocks, start, 0)

        def wait(g, c):
            y_copy(g, 0).wait()
            return c
        lax.fori_loop(n_used, n_blocks, wait, 0)


def _moe_experts(xs, first_blk, blk_count, n_used, n_blocks, w_e_in, w_e_down, layer):
    _, n_exp, d, f2 = w_e_in.shape
    rows = xs.shape[0] // n_blocks
    grid_spec = pltpu.PrefetchScalarGridSpec(
        num_scalar_prefetch=3,
        grid=(n_exp,),
        in_specs=[
            pl.BlockSpec((1, 1, d, f2), lambda e, first, cnt, nu: (layer, e, 0, 0)),
            pl.BlockSpec((1, 1, f2 // 2, d), lambda e, first, cnt, nu: (layer, e, 0, 0)),
            pl.BlockSpec(memory_space=pl.ANY),
        ],
        out_specs=pl.BlockSpec(memory_space=pl.ANY),
        scratch_shapes=[
            pltpu.VMEM((4, rows, LANES), F32),
            pltpu.VMEM((3, rows // SLOTS_PER_Y_TILE, LANES), jnp.uint32),
            pltpu.VMEM((d, f2), BF16),
            pltpu.VMEM((f2 // 2, d), BF16),
            pltpu.SemaphoreType.DMA((4,)),
            pltpu.SemaphoreType.DMA((3,)),
        ],
    )
    return pl.pallas_call(
        functools.partial(_expert_kernel, n_blocks=n_blocks),
        grid_spec=grid_spec,
        out_shape=jax.ShapeDtypeStruct((xs.shape[0] // SLOTS_PER_Y_TILE, LANES), jnp.uint32),
        compiler_params=_params("arbitrary"),
        name="moe_experts",
    )(first_blk, blk_count, n_used, w_e_in, w_e_down, xs)


def _combine_kernel(pos0_ref, posn_ref, x_ref, h_ref, w_ref, half_ref, wsi_ref, wsd_ref, mod_ref, gfin_ref, ys_hbm,
                    o_ref, ybuf, sem, *, final, n_steps):
    step = pl.program_id(0) * pl.num_programs(1) + pl.program_id(1)
    slot = step % 2
    d_exp = wsd_ref.shape[0]
    pieces = h_ref.shape[0] // TMC

    def gather(pos_ref, st, s):
        off = (st % (TM // TMC)) * TMC
        def body(t, c):
            for k in range(TOP_K):
                src = ys_hbm.at[pl.ds(pl.multiple_of(pos_ref[k * TM + off + t], pieces), pieces), :]
                dst = ybuf.at[s, k, pl.ds(pl.multiple_of(t * pieces, pieces), pieces), :]
                pltpu.make_async_copy(src, dst, sem.at[s]).start(priority=k % 2)
            return c
        lax.fori_loop(0, TMC, body, 0, unroll=2)

    @pl.when(step == 0)
    def _():
        gather(pos0_ref, 0, 0)

    @pl.when(step + 1 < n_steps)
    def _():
        gather(posn_ref, step + 1, 1 - slot)

    hh = _dot(_load_row_tiles(h_ref, TMC, pieces).astype(BF16), wsi_ref[...])
    a = (_silu(hh[:, :d_exp]) * hh[:, d_exp:]).astype(BF16)
    y = _dot(a, wsd_ref[...])
    for k in range(TOP_K):
        pltpu.make_async_copy(ys_hbm.at[pl.ds(0, TMC * pieces), :], ybuf.at[slot, k], sem.at[slot]).wait()
    w = w_ref[...]
    half = half_ref[...]
    for k in range(TOP_K):
        both = _load_row_tiles(ybuf.at[slot, k], TMC, pieces)
        mine = jnp.where(half[:, k:k + 1] > 0, both[:, both.shape[1] // 2:], both[:, :both.shape[1] // 2])
        y = y + _unpack_bf16_pairs(mine) * w[:, k:k + 1]
    x = x_ref[0] + mod_ref[0, 0, 5:6, :] * y
    if final:
        x = _rmsnorm(x, gfin_ref[...])
    o_ref[0] = x


def _combine(x_mid, h2t, ys, pos, w_tok, half_tok, w_sh_in, w_sh_down, mod, g_final, n_lat_tiles, final):
    bsz, s, d = x_mid.shape
    nt = s // TMC
    n_steps = bsz * nt
    halves = TM // TMC
    lat_steps = n_lat_tiles * halves
    pieces = d // LANES
    const = lambda b, i: (0, 0)
    tile = pl.BlockSpec((1, TMC, d), lambda b, i: (b, i, 0))
    smem_blk = lambda f: pl.BlockSpec((TOP_K * TM,), f, memory_space=pltpu.SMEM)
    return pl.pallas_call(
        functools.partial(_combine_kernel, final=final, n_steps=n_steps),
        grid=(bsz, nt),
        in_specs=[
            smem_blk(lambda b, i: (0,)),
            smem_blk(lambda b, i: (jnp.minimum(b * nt + i + 1, n_steps - 1) // halves,)),
            tile,
            pl.BlockSpec((TMC * pieces, LANES), lambda b, i: (b * nt + i, 0)),
            pl.BlockSpec((TMC, TOP_K), lambda b, i: (b * nt + i, 0)),
            pl.BlockSpec((TMC, TOP_K), lambda b, i: (b * nt + i, 0)),
            pl.BlockSpec(w_sh_in.shape, const),
            pl.BlockSpec(w_sh_down.shape, const),
            pl.BlockSpec((1, 1, SUBLANES, d), lambda b, i: (b, jnp.where(i >= lat_steps, 0, 1), 0, 0)),
            pl.BlockSpec((1, d), const),
            pl.BlockSpec(memory_space=pl.ANY),
        ],
        out_specs=tile,
        out_shape=jax.ShapeDtypeStruct((bsz, s, d), F32),
        scratch_shapes=[pltpu.VMEM((2, TOP_K, TMC * pieces, LANES), jnp.uint32), pltpu.SemaphoreType.DMA((2,))],
        compiler_params=_params("arbitrary", "arbitrary"),
        name="moe_combine",
    )(pos, pos, x_mid, h2t, w_tok, half_tok, w_sh_in, w_sh_down, mod, g_final, ys)


def _block_diag(w):
    g, a, b = w.shape
    out = jnp.zeros((g * a, g * b), w.dtype)
    for j in range(g):
        out = out.at[j * a:(j + 1) * a, j * b:(j + 1) * b].set(w[j])
    return out


def kernel(x, c, ctx, c_ctx, w_mod, b_mod, g_mix, w_in, rpb, w_pool, pool_scale, w_gate_f, b_gate_f, w_gate_b, b_gate_b, g_gla, w_out, g_ffn, w_router, b_router, w_e_in, w_e_down, w_sh_in, w_sh_down, g_final):
    bsz, seq, d = x.shape
    n_ctx = ctx.shape[1]
    depth = w_mod.shape[0]
    n_exp = w_router.shape[2]
    assert n_ctx == TM and seq % TM == 0 and seq % GRID_W == 0
    n_lat = seq // TM
    s = seq + n_ctx

    n_rows = -(-(bsz + 1) // SUBLANES) * SUBLANES
    cc = jnp.zeros((n_rows, d), F32).at[:bsz].set(c).at[bsz].set(c_ctx)
    mod_all = _modulation(cc, w_mod, b_mod).reshape(depth, n_rows, 6, d)
    cos, sin = _rope_tables(seq, s)

    x_lat, x_ctx, ctx_blk = x, ctx, 0
    for layer in range(depth):
        last = layer == depth - 1
        n_tiles = n_lat if last else n_lat + 1
        m = mod_all[layer]
        mod = jnp.stack([jnp.broadcast_to(m[bsz], (bsz, 6, d)), m[:bsz]], axis=1)
        mod = jnp.pad(mod, ((0, 0), (0, 0), (0, SUBLANES - 6), (0, 0)))
        wl = w_in[layer]
        w_main = wl[:, :MAIN_W].astype(BF16)
        w_low = jnp.pad(wl[:, MAIN_W:], ((0, 0), (0, LANES - 2 * GLA_GATE_RANK))).astype(BF16)
        w2 = jnp.zeros((LANES, 2 * GLA_QK), F32)
        w2 = w2.at[:GLA_GATE_RANK, :GLA_QK].set(w_gate_f[layer])
        w2 = w2.at[GLA_GATE_RANK:2 * GLA_GATE_RANK, GLA_QK:].set(w_gate_b[layer]).astype(BF16)
        b2 = jnp.concatenate([b_gate_f[layer], b_gate_b[layer]])[None, :]
        qkv, u, gg = _inproj(x_lat, x_ctx, ctx_blk, mod, g_mix[layer][None, :], w_main, w_low, w2, b2, n_lat)

        oa = _neighborhood_attention(qkv, _na_bias_table(rpb[layer]), n_lat, not last)
        ob = _multiscale_pool(u, _block_diag(w_pool[layer]).astype(BF16), pool_scale[layer][None, :], n_lat, n_tiles)
        o_f, o_b = _gla(gg, cos, sin, n_lat)

        x_mid, h2t, e_t, w_t, rank_t, hist = _outproj(
            x_lat, x_ctx, ctx_blk, oa, ob, o_f, o_b, gg, jnp.tile(g_gla[layer], GLA_HEADS)[None, :], w_out[layer].astype(BF16), mod,
            g_ffn[layer][None, :], w_router[layer].T.astype(BF16), b_router[layer].reshape(n_exp, 1), n_lat, n_tiles)
        pieces = d // LANES
        base, first_blk, blk_count, fill, n_used, n_blocks = _slot_layout(hist[:, 0, :].astype(jnp.int32))
        pos_x, pos_y, half = _slot_positions(e_t, rank_t, base, pieces)
        xs = _dispatch(h2t, pos_x, fill, n_blocks * MOE_BLOCK, pieces)
        ys = _moe_experts(xs, first_blk, blk_count, n_used, n_blocks, w_e_in, w_e_down, layer)
        xa = _combine(x_mid, h2t, ys, pos_y, w_t.T, half.T, w_sh_in[layer].astype(BF16),
                      w_sh_down[layer].astype(BF16), mod, g_final[None, :], n_lat, last)
        x_lat, x_ctx, ctx_blk = xa, xa, n_lat
    return xa
```

```python
import functools

import jax
import jax.numpy as jnp
import numpy as np
from jax import lax
from jax.experimental import pallas as pl
from jax.experimental.pallas import tpu as pltpu

GRID_W = 64
NORM_EPS = 1e-6
NA_HEADS = 8
NA_HEAD_DIM = 64
NA_WIDTH = NA_HEADS * NA_HEAD_DIM
NA_WIN_ROWS = 8
NA_WIN_COLS = 16
POOL_WINDOWS = (2, 4, 8, 16)
POOL_GROUP_DIM = 64
POOL_WIDTH = len(POOL_WINDOWS) * POOL_GROUP_DIM
POOL_REACH = max(POOL_WINDOWS) // 2
assert POOL_WINDOWS == tuple(2 ** (g + 1) for g in range(len(POOL_WINDOWS)))
GLA_HEADS = 4
GLA_DK = 32
GLA_DV = 64
GLA_QK = GLA_HEADS * GLA_DK
GLA_WIDTH = GLA_HEADS * GLA_DV
GLA_GATE_RANK = 16
GLA_TAU = 16.0
GLA_CHUNK = 64
ROPE_BASE = 10000.0
N_EXPERTS = 256
TOP_K = 8
N_GROUPS = 8
TOPK_GROUPS = 4
ROUTED_SCALE = 2.5

TM = 256
TMC = 256
DISPATCH_TILES = 4
MOE_BLOCK = 256
LANES = 128
SUBLANES = 8
MXU_DIM = 256
MASK_VALUE = -1e30
VMEM_LIMIT = 48 * 1024 * 1024

QKV_W = 3 * NA_WIDTH
G_W = 2 * GLA_QK + 2 * GLA_WIDTH + 2 * GLA_QK
MAIN_W = QKV_W + POOL_WIDTH + 2 * GLA_QK + 2 * GLA_WIDTH

BF16 = jnp.bfloat16
F32 = jnp.float32


def _params(*sem):
    return pltpu.CompilerParams(dimension_semantics=sem, vmem_limit_bytes=VMEM_LIMIT)


def _sigmoid(x):
    return 1.0 / (1.0 + jnp.exp(-x))


def _silu(x):
    return x * _sigmoid(x)


def _rmsnorm(x, g):
    return x * lax.rsqrt(jnp.mean(x * x, axis=-1, keepdims=True) + NORM_EPS) * g


def _dot(a, b):
    return jnp.dot(a, b, preferred_element_type=F32)


def _dot_nt(a, b):
    return lax.dot_general(a, b, (((1,), (1,)), ((), ())), preferred_element_type=F32)


def _dot_tn(a, b):
    return lax.dot_general(a, b, (((0,), (0,)), ((), ())), preferred_element_type=F32)


def _mod_kernel(c_ref, w_ref, b_ref, o_ref):
    a = _silu(c_ref[...]).astype(BF16)
    o_ref[0] = _dot(a, w_ref[0].astype(BF16)) + b_ref[0]


def _modulation(cc, w_mod, b_mod):
    depth, d, n = w_mod.shape
    r = cc.shape[0]
    tn = d
    return pl.pallas_call(
        _mod_kernel,
        grid=(depth, n // tn),
        in_specs=[
            pl.BlockSpec((r, d), lambda l, j: (0, 0)),
            pl.BlockSpec((1, d, tn), lambda l, j: (l, 0, j)),
            pl.BlockSpec((1, 1, tn), lambda l, j: (l, 0, j)),
        ],
        out_specs=pl.BlockSpec((1, r, tn), lambda l, j: (l, 0, j)),
        out_shape=jax.ShapeDtypeStruct((depth, r, n), F32),
        compiler_params=_params("parallel", "parallel"),
        name="modulation",
    )(cc, w_mod, b_mod.reshape(depth, 1, n))


def _stream_tile(x_ref, xc_ref, n_lat_tiles):
    return jnp.where(pl.program_id(1) >= n_lat_tiles, xc_ref[0], x_ref[0])


def _stream_specs(d, n_lat_tiles, ctx_blk):
    return [pl.BlockSpec((1, TM, d), lambda b, i: (b, jnp.minimum(i, n_lat_tiles - 1), 0)),
            pl.BlockSpec((1, TM, d), lambda b, i: (b, ctx_blk, 0))]


def _inproj_kernel(x_ref, xc_ref, mod_ref, g_ref, wm_ref, wl_ref, w2_ref, b2_ref, qkv_ref, u_ref, gg_ref, *, n_lat_tiles):
    x = _stream_tile(x_ref, xc_ref, n_lat_tiles)
    h = _rmsnorm(x, g_ref[...]) * (1.0 + mod_ref[0, 0, 1:2, :]) + mod_ref[0, 0, 0:1, :]
    hb = h.astype(BF16)
    q = _dot(hb, wm_ref[:, 0:NA_WIDTH]) * (NA_HEAD_DIM ** -0.5)
    qkv_ref[0, :, 0:NA_WIDTH] = q.astype(BF16)
    qkv_ref[0, :, NA_WIDTH:2 * NA_WIDTH] = _dot(hb, wm_ref[:, NA_WIDTH:2 * NA_WIDTH]).astype(BF16)
    qkv_ref[0, :, 2 * NA_WIDTH:QKV_W] = _dot(hb, wm_ref[:, 2 * NA_WIDTH:QKV_W]).astype(BF16)
    u_ref[0] = _dot(hb, wm_ref[:, QKV_W:QKV_W + POOL_WIDTH])
    c0 = QKV_W + POOL_WIDTH
    gg_ref[0, :, 0:GLA_QK] = _dot(hb, wm_ref[:, c0:c0 + GLA_QK]) * (GLA_DK ** -0.5)
    gg_ref[0, :, GLA_QK:2 * GLA_QK + 2 * GLA_WIDTH] = _dot(hb, wm_ref[:, c0 + GLA_QK:MAIN_W])
    a_low = _dot(hb, wl_ref[...]).astype(BF16)
    lg = _dot(a_low, w2_ref[...]) + b2_ref[...]
    log_sig = jnp.minimum(lg, 0.0) - jnp.log1p(jnp.exp(-jnp.abs(lg)))
    gg_ref[0, :, 2 * GLA_QK + 2 * GLA_WIDTH:G_W] = log_sig / GLA_TAU


def _inproj(x_lat, x_ctx, ctx_blk, mod, g, w_main, w_low, w2, b2, n_lat_tiles):
    bsz, _, d = x_lat.shape
    nt = n_lat_tiles + 1
    s = nt * TM
    const = lambda b, i: (0, 0)
    return pl.pallas_call(
        functools.partial(_inproj_kernel, n_lat_tiles=n_lat_tiles),
        grid=(bsz, nt),
        in_specs=_stream_specs(d, n_lat_tiles, ctx_blk) + [
            pl.BlockSpec((1, 1, SUBLANES, d), lambda b, i: (b, jnp.where(i >= n_lat_tiles, 0, 1), 0, 0)),
            pl.BlockSpec((1, d), const),
            pl.BlockSpec(w_main.shape, const),
            pl.BlockSpec(w_low.shape, const),
            pl.BlockSpec(w2.shape, const),
            pl.BlockSpec(b2.shape, const),
        ],
        out_specs=[
            pl.BlockSpec((1, TM, QKV_W), lambda b, i: (b, i, 0)),
            pl.BlockSpec((1, TM, POOL_WIDTH), lambda b, i: (b, i, 0)),
            pl.BlockSpec((1, TM, G_W), lambda b, i: (b, i, 0)),
        ],
        out_shape=[
            jax.ShapeDtypeStruct((bsz, s, QKV_W), BF16),
            jax.ShapeDtypeStruct((bsz, s, POOL_WIDTH), F32),
            jax.ShapeDtypeStruct((bsz, s, G_W), F32),
        ],
        compiler_params=_params("parallel", "parallel"),
        name="inproj",
    )(x_lat, x_ctx, mod, g, w_main, w_low, w2, b2)


HEADS_PER_GROUP = MXU_DIM // NA_HEAD_DIM
NA_GROUPS = NA_HEADS // HEADS_PER_GROUP
NA_LOCAL_KEYS = NA_WIN_ROWS * GRID_W
NA_ROWS_PER_STEP = 4


def _stack_heads(x, width):
    lane = lax.broadcasted_iota(jnp.int32, x.shape, 1) // width
    n_heads = x.shape[1] // width
    return jnp.concatenate([jnp.where(lane == h, x, jnp.zeros_like(x)) for h in range(n_heads)], axis=0)


def _unstack_heads(o, width):
    n_heads = o.shape[1] // width
    r = o.shape[0] // n_heads
    lane = lax.broadcasted_iota(jnp.int32, (r, o.shape[1]), 1) // width
    acc = jnp.zeros((r, o.shape[1]), o.dtype)
    for h in range(n_heads):
        acc = jnp.where(lane == h, o[h * r:(h + 1) * r, :], acc)
    return acc


def _na_kernel(q_ref, k_ref, v_ref, kc_ref, vc_ref, *rest, n_rows):
    bias_refs, o_ref = rest[:-1], rest[-1]
    rows_per_step = len(bias_refs)
    step = pl.program_id(1)

    def attend(g, j, local):
        r = step * rows_per_step + j
        rows = slice(j * GRID_W, (j + 1) * GRID_W)
        cols = slice(g * MXU_DIM, (g + 1) * MXU_DIM)
        qs = _stack_heads(q_ref[0, rows, cols], NA_HEAD_DIM)
        kc = kc_ref[0, :, cols]
        vc = vc_ref[0, :, cols]
        s_ctx = _dot_nt(qs, kc)
        m = jnp.max(s_ctx, axis=-1, keepdims=True)
        if local:
            start = pl.multiple_of(jnp.clip(r - NA_WIN_ROWS // 2, 0, n_rows - NA_WIN_ROWS) * GRID_W, GRID_W)
            kw = k_ref[0, pl.ds(start, NA_LOCAL_KEYS), cols]
            vw = v_ref[0, pl.ds(start, NA_LOCAL_KEYS), cols]
            s_loc = _dot_nt(qs, kw) + bias_refs[j][0, g]
            m = jnp.maximum(m, jnp.max(s_loc, axis=-1, keepdims=True))
            p_loc = jnp.exp(s_loc - m)
        p_ctx = jnp.exp(s_ctx - m)
        den = jnp.sum(p_ctx, axis=-1, keepdims=True)
        o = _dot(p_ctx.astype(BF16), vc)
        if local:
            den = den + jnp.sum(p_loc, axis=-1, keepdims=True)
            o = o + _dot(p_loc.astype(BF16), vw)
        o_ref[0, rows, cols] = _unstack_heads(o / den, NA_HEAD_DIM).astype(o_ref.dtype)

    @pl.when(step * rows_per_step < n_rows)
    def _():
        for j in range(rows_per_step):
            for g in range(NA_GROUPS):
                attend(g, j, True)

    @pl.when(step * rows_per_step >= n_rows)
    def _():
        for j in range(rows_per_step):
            for g in range(NA_GROUPS):
                attend(g, j, False)


def _na_bias_table(rpb):
    o = jnp.arange(NA_WIN_ROWS)
    dr = o[None, :] - o[:, None] + NA_WIN_ROWS - 1
    col = jnp.arange(GRID_W)
    dc = jnp.clip(col[None, :] - col[:, None], -(NA_WIN_COLS - 1), NA_WIN_COLS - 1) + NA_WIN_COLS - 1
    col_start = jnp.clip(col - NA_WIN_COLS // 2, 0, GRID_W - NA_WIN_COLS)
    col_mask = (col[None, :] >= col_start[:, None]) & (col[None, :] < col_start[:, None] + NA_WIN_COLS)
    onehot = (dc[None] == jnp.arange(2 * NA_WIN_COLS - 1)[:, None, None]).astype(F32)
    t = jnp.einsum('howc,cqk->howqk', rpb[:, dr].astype(F32), onehot,
                   precision=lax.Precision.HIGHEST)
    t = jnp.where(col_mask[None, None, None], t, MASK_VALUE)
    t = t.transpose(1, 0, 3, 2, 4)
    return t.reshape(NA_WIN_ROWS, NA_GROUPS, HEADS_PER_GROUP * GRID_W, NA_LOCAL_KEYS)


def _neighborhood_attention(qkv, bias_tab, n_lat_tiles, with_ctx_queries):
    bsz, s, _ = qkv.shape
    seq = n_lat_tiles * TM
    n_rows = seq // GRID_W
    assert n_rows >= NA_WIN_ROWS
    n_ctx_rows = (s - seq) // GRID_W
    rq = NA_ROWS_PER_STEP
    assert n_rows % rq == 0 and n_ctx_rows % rq == 0
    nq = (n_rows + (n_ctx_rows if with_ctx_queries else 0)) // rq
    ctx_blk = seq // (s - seq)

    def bias_spec(j):
        def idx(b, i):
            rr = jnp.minimum(i * rq + j, n_rows - 1)
            return (rr - jnp.clip(rr - NA_WIN_ROWS // 2, 0, n_rows - NA_WIN_ROWS), 0, 0, 0)
        return pl.BlockSpec((1,) + bias_tab.shape[1:], idx)

    return pl.pallas_call(
        functools.partial(_na_kernel, n_rows=n_rows),
        grid=(bsz, nq),
        in_specs=[
            pl.BlockSpec((1, rq * GRID_W, NA_WIDTH), lambda b, i: (b, i, 0)),
            pl.BlockSpec((1, seq, NA_WIDTH), lambda b, i: (b, 0, 1)),
            pl.BlockSpec((1, seq, NA_WIDTH), lambda b, i: (b, 0, 2)),
            pl.BlockSpec((1, s - seq, NA_WIDTH), lambda b, i: (b, ctx_blk, 1)),
            pl.BlockSpec((1, s - seq, NA_WIDTH), lambda b, i: (b, ctx_blk, 2)),
        ] + [bias_spec(j) for j in range(rq)],
        out_specs=pl.BlockSpec((1, rq * GRID_W, NA_WIDTH), lambda b, i: (b, i, 0)),
        out_shape=jax.ShapeDtypeStruct((bsz, nq * rq * GRID_W, NA_WIDTH), BF16),
        compiler_params=_params("parallel", "arbitrary"),
        name="neighborhood_attention",
    )(qkv, qkv, qkv, qkv, qkv, *([bias_tab] * rq))


def _pool_kernel(up_ref, u_ref, un_ref, w_ref, sc_ref, o_ref, buf, s0, s1, s2, *, n_lat_tiles, seq, ctx_len):
    i = pl.program_id(1)
    hw = POOL_REACH
    n = TM + 2 * hw
    is_ctx = i >= n_lat_tiles
    base = jnp.where(is_ctx, 0, i * TM)
    lseq = jnp.where(is_ctx, ctx_len, seq)
    buf[0:hw] = up_ref[0]
    buf[hw:hw + TM] = u_ref[0]
    buf[hw + TM:n] = un_ref[0]
    p = base - hw + lax.broadcasted_iota(jnp.int32, (n, POOL_WIDTH), 0)
    buf[...] = jnp.where((p >= 0) & (p < lseq), buf[...], 0.0)
    s0[0:n - 1] = buf[0:n - 1] + buf[1:n]
    s1[0:n - 3] = s0[0:n - 3] + s0[2:n - 1]
    s2[0:n - 7] = s1[0:n - 7] + s1[4:n - 3]
    shape = (TM, POOL_WIDTH)
    t = base + lax.broadcasted_iota(jnp.int32, shape, 0)
    grp = lax.broadcasted_iota(jnp.int32, shape, 1) // POOL_GROUP_DIM
    acc = jnp.where(grp == 0, s0[hw - 1:hw - 1 + TM],
                    jnp.where(grp == 1, s1[hw - 2:hw - 2 + TM],
                              jnp.where(grp == 2, s2[hw - 4:hw - 4 + TM], s2[0:TM] + s2[hw:hw + TM])))
    win = jnp.left_shift(2, grp)
    back = win // 2
    fwd = win - back - 1
    lo = jnp.clip(t - back, 0, lseq - 1)
    hi = jnp.clip(t + fwd, 0, lseq - 1)
    mean = acc / (hi - lo + 1).astype(F32)
    diff = (mean - u_ref[0]).astype(BF16)
    o_ref[0] = (_dot(diff, w_ref[...]) * sc_ref[...]).astype(o_ref.dtype)


def _multiscale_pool(u, w_bd, scale, n_lat_tiles, n_tiles):
    bsz, s, c = u.shape
    hw = POOL_REACH
    per = TM // hw
    last = s // hw - 1
    return pl.pallas_call(
        functools.partial(_pool_kernel, n_lat_tiles=n_lat_tiles, seq=n_lat_tiles * TM, ctx_len=s - n_lat_tiles * TM),
        grid=(bsz, n_tiles),
        in_specs=[
            pl.BlockSpec((1, hw, c), lambda b, i: (b, jnp.maximum(i * per - 1, 0), 0)),
            pl.BlockSpec((1, TM, c), lambda b, i: (b, i, 0)),
            pl.BlockSpec((1, hw, c), lambda b, i: (b, jnp.minimum((i + 1) * per, last), 0)),
            pl.BlockSpec((c, c), lambda b, i: (0, 0)),
            pl.BlockSpec((1, c), lambda b, i: (0, 0)),
        ],
        out_specs=pl.BlockSpec((1, TM, c), lambda b, i: (b, i, 0)),
        out_shape=jax.ShapeDtypeStruct((bsz, n_tiles * TM, c), BF16),
        scratch_shapes=[pltpu.VMEM((TM + 2 * hw, c), F32)] * 4,
        compiler_params=_params("parallel", "parallel"),
        name="multiscale_pool",
    )(u, u, u, w_bd, scale)


def _gla_tile(qk_ref, v_ref, la_ref, cs_ref, sn_ref, st_ref, o_ref, reverse):
    ch = GLA_CHUNK
    lane = lax.broadcasted_iota(jnp.int32, (ch, GLA_QK), 1)
    row = lax.broadcasted_iota(jnp.int32, (ch, GLA_QK), 0)
    rr = lax.broadcasted_iota(jnp.int32, (GLA_HEADS * ch, ch), 0) % ch
    cc = lax.broadcasted_iota(jnp.int32, (GLA_HEADS * ch, ch), 1)
    causal = (rr <= cc) if reverse else (rr >= cc)
    st_shape = (GLA_WIDTH, GLA_QK)
    head_blk = (lax.broadcasted_iota(jnp.int32, st_shape, 0) // GLA_DV
                == lax.broadcasted_iota(jnp.int32, st_shape, 1) // GLA_DK)
    quarter = GLA_DK // 4
    first_half = (lane % (2 * quarter)) < quarter

    def rope(x, cs, sn):
        partner = jnp.where(first_half, pltpu.roll(x, GLA_QK - quarter, 1), pltpu.roll(x, quarter, 1))
        return x * cs + partner * sn

    chunks = range(TM // ch)
    for c in (reversed(chunks) if reverse else chunks):
        sl = slice(c * ch, (c + 1) * ch)
        cs = cs_ref[sl, :]
        sn = sn_ref[sl, :]
        q = rope(qk_ref[0, sl, 0:GLA_QK], cs, sn)
        k = rope(qk_ref[0, sl, GLA_QK:2 * GLA_QK], cs, sn)
        b = la_ref[0, sl, :]
        d = 1
        while d < ch:
            if reverse:
                b = b + jnp.where(row < ch - d, pltpu.roll(b, ch - d, 0), 0.0)
            else:
                b = b + jnp.where(row >= d, pltpu.roll(b, d, 0), 0.0)
            d *= 2
        b_last = b[0:1, :] if reverse else b[ch - 1:ch, :]
        q_in = q * jnp.exp(b)
        k_in = (k * jnp.exp(-b)).astype(BF16)
        k_st = (k * jnp.exp(b_last - b)).astype(BF16)
        att = _dot_nt(_stack_heads(q_in, GLA_DK).astype(BF16), k_in)
        att = jnp.where(causal, att, 0.0).astype(BF16)
        vb = v_ref[0, sl, :].astype(BF16)
        o = _unstack_heads(_dot(att, vb), GLA_DV)
        st = st_ref[...]
        o_ref[0, sl, :] = o + _dot_nt(q_in.astype(BF16), st.astype(BF16))
        st_ref[...] = st * jnp.exp(b_last) + jnp.where(head_blk, _dot_tn(vb, k_st), 0.0)


def _gla_kernel(qkf, vf, laf, csf, snf, qkb, vb, lab, csb, snb, of_ref, ob_ref, st_f, st_b):
    @pl.when(pl.program_id(1) == 0)
    def _():
        st_f[...] = jnp.zeros_like(st_f)
        st_b[...] = jnp.zeros_like(st_b)

    _gla_tile(qkf, vf, laf, csf, snf, st_f, of_ref, False)
    _gla_tile(qkb, vb, lab, csb, snb, st_b, ob_ref, True)


def _rope_tables(seq, s):
    quarter = GLA_DK // 4
    t = np.arange(s)
    pos_row = np.where(t < seq, t // GRID_W, 0).astype(np.float32)
    pos_col = np.where(t < seq, t % GRID_W, 0).astype(np.float32)
    d = np.arange(GLA_QK) % GLA_DK
    inv_freq = ROPE_BASE ** (-jnp.arange(quarter, dtype=F32) / quarter)
    freq = inv_freq[d % quarter]
    pos = jnp.where((d < GLA_DK // 2)[None, :], pos_row[:, None], pos_col[:, None])
    ang = pos * freq[None, :]
    sign = np.where((d % (2 * quarter)) < quarter, -1.0, 1.0).astype(np.float32)
    return jnp.cos(ang), jnp.sin(ang) * sign[None, :]


def _gla(gg, cos, sin, n_lat_tiles):
    bsz, s, _ = gg.shape
    nt = s // TM
    assert nt == n_lat_tiles + 1
    fwd = lambda i: (i + n_lat_tiles) % nt
    bwd = lambda i: jnp.where(i == 0, n_lat_tiles, n_lat_tiles - i)
    la_f_blk = (2 * GLA_QK + 2 * GLA_WIDTH) // GLA_QK
    la_b_blk = la_f_blk + 1
    v_blk = 2 * GLA_QK // GLA_WIDTH

    def specs(order, la_blk):
        return [
            pl.BlockSpec((1, TM, 2 * GLA_QK), lambda b, i: (b, order(i), 0)),
            pl.BlockSpec((1, TM, GLA_WIDTH), lambda b, i: (b, order(i), v_blk)),
            pl.BlockSpec((1, TM, GLA_QK), lambda b, i: (b, order(i), la_blk)),
            pl.BlockSpec((TM, GLA_QK), lambda b, i: (order(i), 0)),
            pl.BlockSpec((TM, GLA_QK), lambda b, i: (order(i), 0)),
        ]

    return pl.pallas_call(
        _gla_kernel,
        grid=(bsz, nt),
        in_specs=specs(fwd, la_f_blk) + specs(bwd, la_b_blk),
        out_specs=[
            pl.BlockSpec((1, TM, GLA_WIDTH), lambda b, i: (b, fwd(i), 0)),
            pl.BlockSpec((1, TM, GLA_WIDTH), lambda b, i: (b, bwd(i), 0)),
        ],
        out_shape=[jax.ShapeDtypeStruct((bsz, s, GLA_WIDTH), F32)] * 2,
        scratch_shapes=[pltpu.VMEM((GLA_WIDTH, GLA_QK), F32)] * 2,
        compiler_params=_params("arbitrary", "arbitrary"),
        name="gla_bidir",
    )(gg, gg, gg, cos, sin, gg, gg, gg, cos, sin)


def _store_row_tiles(ref, val):
    rows, width = val.shape
    pieces = width // LANES
    for s in range(pieces):
        ref[pl.ds(s, rows, stride=pieces), :] = val[:, s * LANES:(s + 1) * LANES]


def _load_row_tiles(ref, rows, pieces):
    return jnp.concatenate([ref[pl.ds(s, rows, stride=pieces), :] for s in range(pieces)], axis=1)


SLOTS_PER_Y_TILE = 2


def _pack_bf16_pairs(y):
    half = y.shape[1] // 2
    bits = lax.bitcast_convert_type(y.astype(jnp.bfloat16).astype(F32), jnp.uint32)
    return (bits[:, :half] >> 16) | (bits[:, half:] & jnp.uint32(0xFFFF0000))


def _unpack_bf16_pairs(words):
    lo = lax.bitcast_convert_type(words << 16, F32)
    hi = lax.bitcast_convert_type(words & jnp.uint32(0xFFFF0000), F32)
    return jnp.concatenate([lo, hi], axis=1)


def _row_index(shape):
    return lax.broadcasted_iota(jnp.int32, shape, 0).astype(F32)


def _stack_rows(rows):
    shape = (len(rows), rows[0].shape[1])
    rid = lax.broadcasted_iota(jnp.int32, shape, 0)
    out = jnp.zeros(shape, rows[0].dtype)
    for k, r in enumerate(rows):
        out = jnp.where(rid == k, r, out)
    return out


def _select_experts(scores, biased):
    n_exp, n_tok = biased.shape
    per = n_exp // N_GROUPS
    neg = -jnp.inf
    sub = _row_index((per, n_tok))
    grp_rows = []
    for g in range(N_GROUPS):
        blk = biased[g * per:(g + 1) * per, :]
        m1 = jnp.max(blk, axis=0, keepdims=True)
        i1 = jnp.min(jnp.where(blk == m1, sub, float(per)), axis=0, keepdims=True)
        m2 = jnp.max(jnp.where(sub == i1, neg, blk), axis=0, keepdims=True)
        grp_rows.append(m1 + m2)
    cur = _stack_rows(grp_rows)
    gid = _row_index(cur.shape)
    picked = jnp.zeros(cur.shape, F32)
    for _ in range(TOPK_GROUPS):
        gm = jnp.max(cur, axis=0, keepdims=True)
        gi = jnp.min(jnp.where(cur == gm, gid, float(N_GROUPS)), axis=0, keepdims=True)
        hit = gid == gi
        picked = jnp.where(hit, 1.0, picked)
        cur = jnp.where(hit, neg, cur)
    cur = jnp.concatenate(
        [jnp.where(picked[g:g + 1, :] > 0.0, biased[g * per:(g + 1) * per, :], neg) for g in range(N_GROUPS)], axis=0)
    eid = _row_index(cur.shape)
    ids, vals = [], []
    chosen = jnp.zeros(cur.shape, F32)
    for _ in range(TOP_K):
        m = jnp.max(cur, axis=0, keepdims=True)
        idx = jnp.min(jnp.where(cur == m, eid, float(n_exp)), axis=0, keepdims=True)
        hit = eid == idx
        ids.append(idx)
        vals.append(jnp.sum(jnp.where(hit, scores, 0.0), axis=0, keepdims=True))
        chosen = jnp.where(hit, 1.0, chosen)
        cur = jnp.where(hit, neg, cur)
    return ids, vals, chosen


def _outproj_kernel(x_ref, xc_ref, oa_ref, ob_ref, of_ref, obk_ref, r_ref, gg_ref, wo_ref, mod_ref, gf_ref, wr_ref, br_ref,
                    tri_ref, xo_ref, h_ref, e_ref, w_ref, rk_ref, hist_ref, *, n_lat_tiles):
    o = of_ref[0] + obk_ref[0]
    head = lax.broadcasted_iota(jnp.int32, o.shape, 1) // GLA_DV
    o2 = o * o
    rs = jnp.zeros_like(o)
    for h in range(GLA_HEADS):
        ssq = jnp.sum(jnp.where(head == h, o2, 0.0), axis=-1, keepdims=True)
        rs = jnp.where(head == h, lax.rsqrt(ssq / GLA_DV + NORM_EPS), rs)
    oc = (o * rs * gg_ref[...] * _silu(r_ref[0])).astype(BF16)
    c1 = NA_WIDTH + POOL_WIDTH
    acc = _dot(oa_ref[0], wo_ref[0:NA_WIDTH]) + _dot(ob_ref[0], wo_ref[NA_WIDTH:c1]) + _dot(oc, wo_ref[c1:])
    x = _stream_tile(x_ref, xc_ref, n_lat_tiles) + mod_ref[0, 0, 2:3, :] * acc
    xo_ref[0] = x
    h = _rmsnorm(x, gf_ref[...]) * (1.0 + mod_ref[0, 0, 4:5, :]) + mod_ref[0, 0, 3:4, :]
    _store_row_tiles(h_ref, h)

    scores = _sigmoid(_dot_nt(wr_ref[...], h.astype(BF16)))
    ids, vals, chosen = _select_experts(scores, scores + br_ref[...])
    total = vals[0]
    for v in vals[1:]:
        total = total + v
    w_ref[...] = _stack_rows([v / total * ROUTED_SCALE for v in vals])
    e_ref[...] = _stack_rows(ids).astype(jnp.int32)
    chosen_b = chosen.astype(BF16)
    before = _dot(chosen_b, tri_ref[...])
    eid = _row_index(before.shape)
    rk_ref[...] = _stack_rows([jnp.sum(jnp.where(eid == idx, before, 0.0), axis=0, keepdims=True) for idx in ids])
    hist_ref[0] = _dot_nt(jnp.ones((SUBLANES, chosen.shape[1]), BF16), chosen_b)


def _outproj(x_lat, x_ctx, ctx_blk, oa, ob, o_f, o_b, gg, g_gla4, w_out, mod, g_ffn, w_router_t, b_router,
             n_lat_tiles, n_tiles):
    bsz, _, d = x_lat.shape
    n_exp = w_router_t.shape[0]
    r_blk = (2 * GLA_QK + GLA_WIDTH) // GLA_WIDTH
    tile = lambda w: pl.BlockSpec((1, TM, w), lambda b, i: (b, i, 0))
    const = lambda b, i: (0, 0)
    per_tok = pl.BlockSpec((TOP_K, TM), lambda b, i: (0, b * n_tiles + i))
    t_tok = bsz * n_tiles * TM
    tri = (jnp.arange(TM)[:, None] < jnp.arange(TM)[None, :]).astype(BF16)
    return pl.pallas_call(
        functools.partial(_outproj_kernel, n_lat_tiles=n_lat_tiles),
        grid=(bsz, n_tiles),
        in_specs=[
            *_stream_specs(d, n_lat_tiles, ctx_blk),
            tile(NA_WIDTH), tile(POOL_WIDTH), tile(GLA_WIDTH), tile(GLA_WIDTH),
            pl.BlockSpec((1, TM, GLA_WIDTH), lambda b, i: (b, i, r_blk)),
            pl.BlockSpec((1, GLA_WIDTH), const),
            pl.BlockSpec(w_out.shape, const),
            pl.BlockSpec((1, 1, SUBLANES, d), lambda b, i: (b, jnp.where(i >= n_lat_tiles, 0, 1), 0, 0)),
            pl.BlockSpec((1, d), const),
            pl.BlockSpec(w_router_t.shape, const),
            pl.BlockSpec((n_exp, 1), const),
            pl.BlockSpec((TM, TM), const),
        ],
        out_specs=[tile(d), pl.BlockSpec((TM * d // LANES, LANES), lambda b, i: (b * n_tiles + i, 0)),
                   per_tok, per_tok, per_tok,
                   pl.BlockSpec((1, SUBLANES, n_exp), lambda b, i: (b * n_tiles + i, 0, 0))],
        out_shape=[
            jax.ShapeDtypeStruct((bsz, n_tiles * TM, d), F32),
            jax.ShapeDtypeStruct((t_tok * d // LANES, LANES), F32),
            jax.ShapeDtypeStruct((TOP_K, t_tok), jnp.int32),
            jax.ShapeDtypeStruct((TOP_K, t_tok), F32),
            jax.ShapeDtypeStruct((TOP_K, t_tok), F32),
            jax.ShapeDtypeStruct((bsz * n_tiles, SUBLANES, n_exp), F32),
        ],
        compiler_params=_params("parallel", "parallel"),
        name="outproj_router",
    )(x_lat, x_ctx, oa, ob, o_f, o_b, gg, g_gla4, w_out, mod, g_ffn, w_router_t, b_router, tri)


def _slot_layout(hist):
    n_tiles, n_exp = hist.shape
    counts = hist.sum(0)
    tile_base = jnp.cumsum(hist, axis=0) - hist
    padded = (counts + MOE_BLOCK - 1) // MOE_BLOCK * MOE_BLOCK
    pad_end = jnp.cumsum(padded)
    pad_start = pad_end - padded
    base = (pad_start[None, :] + tile_base).astype(F32).reshape(n_tiles, n_exp, 1)
    n_blocks = -(-(n_tiles * TM * TOP_K) // MOE_BLOCK) + n_exp
    n_used = (pad_end[-1] // MOE_BLOCK).astype(jnp.int32)
    fill = jnp.concatenate([jnp.maximum(pad_end // MOE_BLOCK - 1, 0).astype(jnp.int32),
                            jnp.minimum(n_used + jnp.arange(n_exp, dtype=jnp.int32), n_blocks - 1)])
    first_blk = (pad_start // MOE_BLOCK).astype(jnp.int32)
    blk_count = (padded // MOE_BLOCK).astype(jnp.int32)
    return base, first_blk, blk_count, fill, n_used.reshape(1), n_blocks


def _pos_kernel(e_ref, rk_ref, base_ref, posx_ref, posy_ref, half_ref, *, pieces):
    eid = lax.broadcasted_iota(jnp.int32, (base_ref.shape[1], e_ref.shape[1]), 0)
    base = base_ref[0]
    rows = [jnp.sum(jnp.where(eid == e_ref[k:k + 1, :], base, 0.0), axis=0, keepdims=True) for k in range(TOP_K)]
    slot = (_stack_rows(rows) + rk_ref[...]).astype(jnp.int32)
    posx_ref[0] = slot * pieces
    posy_ref[0] = (slot // SLOTS_PER_Y_TILE) * pieces
    half_ref[...] = slot % SLOTS_PER_Y_TILE


def _slot_positions(e_t, rank_t, base, pieces):
    n_tiles, n_exp, _ = base.shape
    per_tok = pl.BlockSpec((TOP_K, TM), lambda i: (0, i))
    flat = pl.BlockSpec((1, TOP_K, TM), lambda i: (i, 0, 0))
    pos_x, pos_y, half = pl.pallas_call(
        functools.partial(_pos_kernel, pieces=pieces),
        grid=(n_tiles,),
        in_specs=[per_tok, per_tok, pl.BlockSpec((1, n_exp, 1), lambda i: (i, 0, 0))],
        out_specs=[flat, flat, per_tok],
        out_shape=[jax.ShapeDtypeStruct((n_tiles, TOP_K, TM), jnp.int32)] * 2
        + [jax.ShapeDtypeStruct(e_t.shape, jnp.int32)],
        compiler_params=_params("parallel"),
        name="moe_positions",
    )(e_t, rank_t, base)
    return pos_x.reshape(-1), pos_y.reshape(-1), half


def _dispatch_kernel(fill_ref, pos_ref, h_ref, xs_hbm, zbuf, sem, zsem):
    @pl.when(pl.program_id(0) == 0)
    def _():
        zbuf[...] = jnp.zeros_like(zbuf)
        n_fill = fill_ref.shape[0]

        def is_new(j):
            return (j == 0) | (fill_ref[j] != fill_ref[jnp.maximum(j - 1, 0)])

        def start(j, c):
            @pl.when(is_new(j))
            def _():
                row0 = pl.multiple_of(fill_ref[j] * zbuf.shape[0], zbuf.shape[0])
                pltpu.make_async_copy(zbuf, xs_hbm.at[pl.ds(row0, zbuf.shape[0]), :], zsem).start()
            return c
        lax.fori_loop(0, n_fill, start, 0)

        def wait(j, c):
            @pl.when(is_new(j))
            def _():
                pltpu.make_async_copy(zbuf, xs_hbm.at[pl.ds(0, zbuf.shape[0]), :], zsem).wait()
            return c
        lax.fori_loop(0, n_fill, wait, 0)

    tiles = pos_ref.shape[0] // (TOP_K * TM)
    pieces = h_ref.shape[0] // (tiles * TM)

    for j in range(tiles):
        def body(t, c):
            src = h_ref.at[pl.ds(pl.multiple_of((j * TM + t) * pieces, pieces), pieces), :]
            for k in range(TOP_K):
                dst = xs_hbm.at[pl.ds(pl.multiple_of(pos_ref[(j * TOP_K + k) * TM + t], pieces), pieces), :]
                pltpu.make_async_copy(src, dst, sem).start(priority=k % 2)
            return c
        lax.fori_loop(0, TM, body, 0, unroll=2)
    for _ in range(TOP_K):
        pltpu.make_async_copy(h_ref, xs_hbm.at[pl.ds(0, h_ref.shape[0]), :], sem).wait()


def _dispatch(h2t, pos, fill, n_slots, pieces):
    n_tiles = h2t.shape[0] // (TM * pieces)
    per_step = max(p for p in range(1, DISPATCH_TILES + 1) if n_tiles % p == 0)
    grid_spec = pltpu.PrefetchScalarGridSpec(
        num_scalar_prefetch=1,
        grid=(n_tiles // per_step,),
        in_specs=[
            pl.BlockSpec((per_step * TOP_K * TM,), lambda i, fill: (i,), memory_space=pltpu.SMEM),
            pl.BlockSpec((per_step * TM * pieces, LANES), lambda i, fill: (i, 0)),
        ],
        out_specs=pl.BlockSpec(memory_space=pl.ANY),
        scratch_shapes=[pltpu.VMEM((MOE_BLOCK * pieces, LANES), F32), pltpu.SemaphoreType.DMA,
                        pltpu.SemaphoreType.DMA],
    )
    return pl.pallas_call(
        _dispatch_kernel,
        grid_spec=grid_spec,
        out_shape=jax.ShapeDtypeStruct((n_slots * pieces, LANES), F32),
        compiler_params=_params("arbitrary"),
        name="moe_dispatch",
    )(fill, pos, h2t)


def _expert_kernel(first_ref, cnt_ref, nu_ref, wi_ref, wd_ref, xs_hbm, ys_hbm, xbuf, ybuf, wi_bf, wd_bf, xsem, ysem,
                   *, n_blocks):
    e = pl.program_id(0)
    n_used = nu_ref[0]
    rows = xbuf.shape[1]
    pieces = rows // MOE_BLOCK
    d_exp = wd_ref.shape[2]

    def x_copy(g, s):
        return pltpu.make_async_copy(xs_hbm.at[pl.ds(pl.multiple_of(g * rows, rows), rows), :], xbuf.at[s], xsem.at[s])

    y_rows = ybuf.shape[1]

    def y_copy(g, s):
        return pltpu.make_async_copy(ybuf.at[s], ys_hbm.at[pl.ds(pl.multiple_of(g * y_rows, y_rows), y_rows), :],
                                     ysem.at[s])

    n_xbuf = xbuf.shape[0]
    n_ybuf = ybuf.shape[0]

    @pl.when(e == 0)
    def _():
        for g in range(n_xbuf - 1):
            @pl.when(g < n_used)
            def _():
                x_copy(g, g).start()

    @pl.when(cnt_ref[e] > 0)
    def _():
        wi_bf[...] = wi_ref[0, 0].astype(BF16)
        wd_bf[...] = wd_ref[0, 0].astype(BF16)

    def block(j, c):
        g = first_ref[e] + j
        s = g % n_ybuf
        x_copy(g, g % n_xbuf).wait()

        @pl.when(g + n_xbuf - 1 < n_used)
        def _():
            x_copy(g + n_xbuf - 1, (g + n_xbuf - 1) % n_xbuf).start()

        @pl.when(g >= n_ybuf)
        def _():
            y_copy(g - n_ybuf, s).wait()

        hh = _dot(_load_row_tiles(xbuf.at[g % n_xbuf], MOE_BLOCK, pieces).astype(BF16), wi_bf[...])
        a = (_silu(hh[:, :d_exp]) * hh[:, d_exp:]).astype(BF16)
        _store_row_tiles(ybuf.at[s], _pack_bf16_pairs(_dot(a, wd_bf[...])))
        y_copy(g, s).start(priority=1)
        return c
    lax.fori_loop(0, cnt_ref[e], block, 0)

    @pl.when(e == pl.num_programs(0) - 1)
    def _():
        for back in range(1, n_ybuf + 1):
            @pl.when(n_used >= back)
            def _():
                y_copy(n_used - back, (n_used - back) % n_ybuf).wait()

        ybuf[0] = jnp.zeros(ybuf.shape[1:], ybuf.dtype)

        def start(g, c):
            y_copy(g, 0).start()
            return c
        lax.fori_loop(n_used, n_blocks, start, 0)

        def wait(g, c):
            y_copy(g, 0).wait()
            return c
        lax.fori_loop(n_used, n_blocks, wait, 0)


def _moe_experts(xs, first_blk, blk_count, n_used, n_blocks, w_e_in, w_e_down, layer):
    _, n_exp, d, f2 = w_e_in.shape
    rows = xs.shape[0] // n_blocks
    grid_spec = pltpu.PrefetchScalarGridSpec(
        num_scalar_prefetch=3,
        grid=(n_exp,),
        in_specs=[
            pl.BlockSpec((1, 1, d, f2), lambda e, first, cnt, nu: (layer, e, 0, 0)),
            pl.BlockSpec((1, 1, f2 // 2, d), lambda e, first, cnt, nu: (layer, e, 0, 0)),
            pl.BlockSpec(memory_space=pl.ANY),
        ],
        out_specs=pl.BlockSpec(memory_space=pl.ANY),
        scratch_shapes=[
            pltpu.VMEM((4, rows, LANES), F32),
            pltpu.VMEM((3, rows // SLOTS_PER_Y_TILE, LANES), jnp.uint32),
            pltpu.VMEM((d, f2), BF16),
            pltpu.VMEM((f2 // 2, d), BF16),
            pltpu.SemaphoreType.DMA((4,)),
            pltpu.SemaphoreType.DMA((3,)),
        ],
    )
    return pl.pallas_call(
        functools.partial(_expert_kernel, n_blocks=n_blocks),
        grid_spec=grid_spec,
        out_shape=jax.ShapeDtypeStruct((xs.shape[0] // SLOTS_PER_Y_TILE, LANES), jnp.uint32),
        compiler_params=_params("arbitrary"),
        name="moe_experts",
    )(first_blk, blk_count, n_used, w_e_in, w_e_down, xs)


def _combine_kernel(pos0_ref, posn_ref, x_ref, h_ref, w_ref, half_ref, wsi_ref, wsd_ref, mod_ref, gfin_ref, ys_hbm,
                    o_ref, ybuf, sem, *, final, n_steps):
    step = pl.program_id(0) * pl.num_programs(1) + pl.program_id(1)
    slot = step % 2
    d_exp = wsd_ref.shape[0]
    pieces = h_ref.shape[0] // TMC

    def gather(pos_ref, st, s):
        off = (st % (TM // TMC)) * TMC
        def body(t, c):
            for k in range(TOP_K):
                src = ys_hbm.at[pl.ds(pl.multiple_of(pos_ref[k * TM + off + t], pieces), pieces), :]
                dst = ybuf.at[s, k, pl.ds(pl.multiple_of(t * pieces, pieces), pieces), :]
                pltpu.make_async_copy(src, dst, sem.at[s]).start(priority=k % 2)
            return c
        lax.fori_loop(0, TMC, body, 0, unroll=2)

    @pl.when(step == 0)
    def _():
        gather(pos0_ref, 0, 0)

    @pl.when(step + 1 < n_steps)
    def _():
        gather(posn_ref, step + 1, 1 - slot)

    hh = _dot(_load_row_tiles(h_ref, TMC, pieces).astype(BF16), wsi_ref[...])
    a = (_silu(hh[:, :d_exp]) * hh[:, d_exp:]).astype(BF16)
    y = _dot(a, wsd_ref[...])
    for k in range(TOP_K):
        pltpu.make_async_copy(ys_hbm.at[pl.ds(0, TMC * pieces), :], ybuf.at[slot, k], sem.at[slot]).wait()
    w = w_ref[...]
    half = half_ref[...]
    for k in range(TOP_K):
        both = _load_row_tiles(ybuf.at[slot, k], TMC, pieces)
        mine = jnp.where(half[:, k:k + 1] > 0, both[:, both.shape[1] // 2:], both[:, :both.shape[1] // 2])
        y = y + _unpack_bf16_pairs(mine) * w[:, k:k + 1]
    x = x_ref[0] + mod_ref[0, 0, 5:6, :] * y
    if final:
        x = _rmsnorm(x, gfin_ref[...])
    o_ref[0] = x


def _combine(x_mid, h2t, ys, pos, w_tok, half_tok, w_sh_in, w_sh_down, mod, g_final, n_lat_tiles, final):
    bsz, s, d = x_mid.shape
    nt = s // TMC
    n_steps = bsz * nt
    halves = TM // TMC
    lat_steps = n_lat_tiles * halves
    pieces = d // LANES
    const = lambda b, i: (0, 0)
    tile = pl.BlockSpec((1, TMC, d), lambda b, i: (b, i, 0))
    smem_blk = lambda f: pl.BlockSpec((TOP_K * TM,), f, memory_space=pltpu.SMEM)
    return pl.pallas_call(
        functools.partial(_combine_kernel, final=final, n_steps=n_steps),
        grid=(bsz, nt),
        in_specs=[
            smem_blk(lambda b, i: (0,)),
            smem_blk(lambda b, i: (jnp.minimum(b * nt + i + 1, n_steps - 1) // halves,)),
            tile,
            pl.BlockSpec((TMC * pieces, LANES), lambda b, i: (b * nt + i, 0)),
            pl.BlockSpec((TMC, TOP_K), lambda b, i: (b * nt + i, 0)),
            pl.BlockSpec((TMC, TOP_K), lambda b, i: (b * nt + i, 0)),
            pl.BlockSpec(w_sh_in.shape, const),
            pl.BlockSpec(w_sh_down.shape, const),
            pl.BlockSpec((1, 1, SUBLANES, d), lambda b, i: (b, jnp.where(i >= lat_steps, 0, 1), 0, 0)),
            pl.BlockSpec((1, d), const),
            pl.BlockSpec(memory_space=pl.ANY),
        ],
        out_specs=tile,
        out_shape=jax.ShapeDtypeStruct((bsz, s, d), F32),
        scratch_shapes=[pltpu.VMEM((2, TOP_K, TMC * pieces, LANES), jnp.uint32), pltpu.SemaphoreType.DMA((2,))],
        compiler_params=_params("arbitrary", "arbitrary"),
        name="moe_combine",
    )(pos, pos, x_mid, h2t, w_tok, half_tok, w_sh_in, w_sh_down, mod, g_final, ys)


def _block_diag(w):
    g, a, b = w.shape
    out = jnp.zeros((g * a, g * b), w.dtype)
    for j in range(g):
        out = out.at[j * a:(j + 1) * a, j * b:(j + 1) * b].set(w[j])
    return out


def kernel(x, c, ctx, c_ctx, w_mod, b_mod, g_mix, w_in, rpb, w_pool, pool_scale, w_gate_f, b_gate_f, w_gate_b, b_gate_b, g_gla, w_out, g_ffn, w_router, b_router, w_e_in, w_e_down, w_sh_in, w_sh_down, g_final):
    bsz, seq, d = x.shape
    n_ctx = ctx.shape[1]
    depth = w_mod.shape[0]
    n_exp = w_router.shape[2]
    assert n_ctx == TM and seq % TM == 0 and seq % GRID_W == 0
    n_lat = seq // TM
    s = seq + n_ctx

    n_rows = -(-(bsz + 1) // SUBLANES) * SUBLANES
    cc = jnp.zeros((n_rows, d), F32).at[:bsz].set(c).at[bsz].set(c_ctx)
    mod_all = _modulation(cc, w_mod, b_mod).reshape(depth, n_rows, 6, d)
    cos, sin = _rope_tables(seq, s)

    x_lat, x_ctx, ctx_blk = x, ctx, 0
    for layer in range(depth):
        last = layer == depth - 1
        n_tiles = n_lat if last else n_lat + 1
        m = mod_all[layer]
        mod = jnp.stack([jnp.broadcast_to(m[bsz], (bsz, 6, d)), m[:bsz]], axis=1)
        mod = jnp.pad(mod, ((0, 0), (0, 0), (0, SUBLANES - 6), (0, 0)))
        wl = w_in[layer]
        w_main = wl[:, :MAIN_W].astype(BF16)
        w_low = jnp.pad(wl[:, MAIN_W:], ((0, 0), (0, LANES - 2 * GLA_GATE_RANK))).astype(BF16)
        w2 = jnp.zeros((LANES, 2 * GLA_QK), F32)
        w2 = w2.at[:GLA_GATE_RANK, :GLA_QK].set(w_gate_f[layer])
        w2 = w2.at[GLA_GATE_RANK:2 * GLA_GATE_RANK, GLA_QK:].set(w_gate_b[layer]).astype(BF16)
        b2 = jnp.concatenate([b_gate_f[layer], b_gate_b[layer]])[None, :]
        qkv, u, gg = _inproj(x_lat, x_ctx, ctx_blk, mod, g_mix[layer][None, :], w_main, w_low, w2, b2, n_lat)

        oa = _neighborhood_attention(qkv, _na_bias_table(rpb[layer]), n_lat, not last)
        ob = _multiscale_pool(u, _block_diag(w_pool[layer]).astype(BF16), pool_scale[layer][None, :], n_lat, n_tiles)
        o_f, o_b = _gla(gg, cos, sin, n_lat)

        x_mid, h2t, e_t, w_t, rank_t, hist = _outproj(
            x_lat, x_ctx, ctx_blk, oa, ob, o_f, o_b, gg, jnp.tile(g_gla[layer], GLA_HEADS)[None, :], w_out[layer].astype(BF16), mod,
            g_ffn[layer][None, :], w_router[layer].T.astype(BF16), b_router[layer].reshape(n_exp, 1), n_lat, n_tiles)
        pieces = d // LANES
        base, first_blk, blk_count, fill, n_used, n_blocks = _slot_layout(hist[:, 0, :].astype(jnp.int32))
        pos_x, pos_y, half = _slot_positions(e_t, rank_t, base, pieces)
        xs = _dispatch(h2t, pos_x, fill, n_blocks * MOE_BLOCK, pieces)
        ys = _moe_experts(xs, first_blk, blk_count, n_used, n_blocks, w_e_in, w_e_down, layer)
        xa = _combine(x_mid, h2t, ys, pos_y, w_t.T, half.T, w_sh_in[layer].astype(BF16),
                      w_sh_down[layer].astype(BF16), mod, g_final[None, :], n_lat, last)
        x_lat, x_ctx, ctx_blk = xa, xa, n_lat
    return xa
```

```python
import functools

import jax
import jax.numpy as jnp
import numpy as np
from jax import lax
from jax.experimental import pallas as pl
from jax.experimental.pallas import tpu as pltpu

GRID_W = 64
NORM_EPS = 1e-6
NA_HEADS = 8
NA_HEAD_DIM = 64
NA_WIDTH = NA_HEADS * NA_HEAD_DIM
NA_WIN_ROWS = 8
NA_WIN_COLS = 16
POOL_WINDOWS = (2, 4, 8, 16)
POOL_GROUP_DIM = 64
POOL_WIDTH = len(POOL_WINDOWS) * POOL_GROUP_DIM
POOL_REACH = max(POOL_WINDOWS) // 2
assert POOL_WINDOWS == tuple(2 ** (g + 1) for g in range(len(POOL_WINDOWS)))
GLA_HEADS = 4
GLA_DK = 32
GLA_DV = 64
GLA_QK = GLA_HEADS * GLA_DK
GLA_WIDTH = GLA_HEADS * GLA_DV
GLA_GATE_RANK = 16
GLA_TAU = 16.0
GLA_CHUNK = 64
ROPE_BASE = 10000.0
N_EXPERTS = 256
TOP_K = 8
N_GROUPS = 8
TOPK_GROUPS = 4
ROUTED_SCALE = 2.5

TM = 256
TMC = 256
DISPATCH_TILES = 8
MOE_BLOCK = 256
LANES = 128
SUBLANES = 8
MXU_DIM = 256
MASK_VALUE = -1e30
VMEM_LIMIT = 48 * 1024 * 1024

QKV_W = 3 * NA_WIDTH
G_W = 2 * GLA_QK + 2 * GLA_WIDTH + 2 * GLA_QK
MAIN_W = QKV_W + POOL_WIDTH + 2 * GLA_QK + 2 * GLA_WIDTH

BF16 = jnp.bfloat16
F32 = jnp.float32


def _params(*sem):
    return pltpu.CompilerParams(dimension_semantics=sem, vmem_limit_bytes=VMEM_LIMIT)


def _sigmoid(x):
    return 1.0 / (1.0 + jnp.exp(-x))


def _silu(x):
    return x * _sigmoid(x)


def _rmsnorm(x, g):
    return x * lax.rsqrt(jnp.mean(x * x, axis=-1, keepdims=True) + NORM_EPS) * g


def _dot(a, b):
    return jnp.dot(a, b, preferred_element_type=F32)


def _dot_nt(a, b):
    return lax.dot_general(a, b, (((1,), (1,)), ((), ())), preferred_element_type=F32)


def _dot_tn(a, b):
    return lax.dot_general(a, b, (((0,), (0,)), ((), ())), preferred_element_type=F32)


def _mod_kernel(c_ref, w_ref, b_ref, o_ref):
    a = _silu(c_ref[...]).astype(BF16)
    o_ref[0] = _dot(a, w_ref[0].astype(BF16)) + b_ref[0]


def _modulation(cc, w_mod, b_mod):
    depth, d, n = w_mod.shape
    r = cc.shape[0]
    tn = d
    return pl.pallas_call(
        _mod_kernel,
        grid=(depth, n // tn),
        in_specs=[
            pl.BlockSpec((r, d), lambda l, j: (0, 0)),
            pl.BlockSpec((1, d, tn), lambda l, j: (l, 0, j)),
            pl.BlockSpec((1, 1, tn), lambda l, j: (l, 0, j)),
        ],
        out_specs=pl.BlockSpec((1, r, tn), lambda l, j: (l, 0, j)),
        out_shape=jax.ShapeDtypeStruct((depth, r, n), F32),
        compiler_params=_params("parallel", "parallel"),
        name="modulation",
    )(cc, w_mod, b_mod.reshape(depth, 1, n))


def _stream_tile(x_ref, xc_ref, n_lat_tiles):
    return jnp.where(pl.program_id(1) >= n_lat_tiles, xc_ref[0], x_ref[0])


def _stream_specs(d, n_lat_tiles, ctx_blk):
    return [pl.BlockSpec((1, TM, d), lambda b, i: (b, jnp.minimum(i, n_lat_tiles - 1), 0)),
            pl.BlockSpec((1, TM, d), lambda b, i: (b, ctx_blk, 0))]


def _inproj_kernel(x_ref, xc_ref, mod_ref, g_ref, wm_ref, wl_ref, w2_ref, b2_ref, qkv_ref, u_ref, gg_ref, *, n_lat_tiles):
    x = _stream_tile(x_ref, xc_ref, n_lat_tiles)
    h = _rmsnorm(x, g_ref[...]) * (1.0 + mod_ref[0, 0, 1:2, :]) + mod_ref[0, 0, 0:1, :]
    hb = h.astype(BF16)
    q = _dot(hb, wm_ref[:, 0:NA_WIDTH]) * (NA_HEAD_DIM ** -0.5)
    qkv_ref[0, :, 0:NA_WIDTH] = q.astype(BF16)
    qkv_ref[0, :, NA_WIDTH:2 * NA_WIDTH] = _dot(hb, wm_ref[:, NA_WIDTH:2 * NA_WIDTH]).astype(BF16)
    qkv_ref[0, :, 2 * NA_WIDTH:QKV_W] = _dot(hb, wm_ref[:, 2 * NA_WIDTH:QKV_W]).astype(BF16)
    u_ref[0] = _dot(hb, wm_ref[:, QKV_W:QKV_W + POOL_WIDTH])
    c0 = QKV_W + POOL_WIDTH
    gg_ref[0, :, 0:GLA_QK] = _dot(hb, wm_ref[:, c0:c0 + GLA_QK]) * (GLA_DK ** -0.5)
    gg_ref[0, :, GLA_QK:2 * GLA_QK + 2 * GLA_WIDTH] = _dot(hb, wm_ref[:, c0 + GLA_QK:MAIN_W])
    a_low = _dot(hb, wl_ref[...]).astype(BF16)
    lg = _dot(a_low, w2_ref[...]) + b2_ref[...]
    log_sig = jnp.minimum(lg, 0.0) - jnp.log1p(jnp.exp(-jnp.abs(lg)))
    gg_ref[0, :, 2 * GLA_QK + 2 * GLA_WIDTH:G_W] = log_sig / GLA_TAU


def _inproj(x_lat, x_ctx, ctx_blk, mod, g, w_main, w_low, w2, b2, n_lat_tiles):
    bsz, _, d = x_lat.shape
    nt = n_lat_tiles + 1
    s = nt * TM
    const = lambda b, i: (0, 0)
    return pl.pallas_call(
        functools.partial(_inproj_kernel, n_lat_tiles=n_lat_tiles),
        grid=(bsz, nt),
        in_specs=_stream_specs(d, n_lat_tiles, ctx_blk) + [
            pl.BlockSpec((1, 1, SUBLANES, d), lambda b, i: (b, jnp.where(i >= n_lat_tiles, 0, 1), 0, 0)),
            pl.BlockSpec((1, d), const),
            pl.BlockSpec(w_main.shape, const),
            pl.BlockSpec(w_low.shape, const),
            pl.BlockSpec(w2.shape, const),
            pl.BlockSpec(b2.shape, const),
        ],
        out_specs=[
            pl.BlockSpec((1, TM, QKV_W), lambda b, i: (b, i, 0)),
            pl.BlockSpec((1, TM, POOL_WIDTH), lambda b, i: (b, i, 0)),
            pl.BlockSpec((1, TM, G_W), lambda b, i: (b, i, 0)),
        ],
        out_shape=[
            jax.ShapeDtypeStruct((bsz, s, QKV_W), BF16),
            jax.ShapeDtypeStruct((bsz, s, POOL_WIDTH), F32),
            jax.ShapeDtypeStruct((bsz, s, G_W), F32),
        ],
        compiler_params=_params("parallel", "parallel"),
        name="inproj",
    )(x_lat, x_ctx, mod, g, w_main, w_low, w2, b2)


HEADS_PER_GROUP = MXU_DIM // NA_HEAD_DIM
NA_GROUPS = NA_HEADS // HEADS_PER_GROUP
NA_LOCAL_KEYS = NA_WIN_ROWS * GRID_W
NA_ROWS_PER_STEP = 4


def _stack_heads(x, width):
    lane = lax.broadcasted_iota(jnp.int32, x.shape, 1) // width
    n_heads = x.shape[1] // width
    return jnp.concatenate([jnp.where(lane == h, x, jnp.zeros_like(x)) for h in range(n_heads)], axis=0)


def _unstack_heads(o, width):
    n_heads = o.shape[1] // width
    r = o.shape[0] // n_heads
    lane = lax.broadcasted_iota(jnp.int32, (r, o.shape[1]), 1) // width
    acc = jnp.zeros((r, o.shape[1]), o.dtype)
    for h in range(n_heads):
        acc = jnp.where(lane == h, o[h * r:(h + 1) * r, :], acc)
    return acc


def _na_kernel(q_ref, k_ref, v_ref, kc_ref, vc_ref, *rest, n_rows):
    bias_refs, o_ref = rest[:-1], rest[-1]
    rows_per_step = len(bias_refs)
    step = pl.program_id(1)

    def attend(g, j, local):
        r = step * rows_per_step + j
        rows = slice(j * GRID_W, (j + 1) * GRID_W)
        cols = slice(g * MXU_DIM, (g + 1) * MXU_DIM)
        qs = _stack_heads(q_ref[0, rows, cols], NA_HEAD_DIM)
        kc = kc_ref[0, :, cols]
        vc = vc_ref[0, :, cols]
        s_ctx = _dot_nt(qs, kc)
        m = jnp.max(s_ctx, axis=-1, keepdims=True)
        if local:
            start = pl.multiple_of(jnp.clip(r - NA_WIN_ROWS // 2, 0, n_rows - NA_WIN_ROWS) * GRID_W, GRID_W)
            kw = k_ref[0, pl.ds(start, NA_LOCAL_KEYS), cols]
            vw = v_ref[0, pl.ds(start, NA_LOCAL_KEYS), cols]
            s_loc = _dot_nt(qs, kw) + bias_refs[j][0, g]
            m = jnp.maximum(m, jnp.max(s_loc, axis=-1, keepdims=True))
            p_loc = jnp.exp(s_loc - m)
        p_ctx = jnp.exp(s_ctx - m)
        den = jnp.sum(p_ctx, axis=-1, keepdims=True)
        o = _dot(p_ctx.astype(BF16), vc)
        if local:
            den = den + jnp.sum(p_loc, axis=-1, keepdims=True)
            o = o + _dot(p_loc.astype(BF16), vw)
        o_ref[0, rows, cols] = _unstack_heads(o / den, NA_HEAD_DIM).astype(o_ref.dtype)

    @pl.when(step * rows_per_step < n_rows)
    def _():
        for j in range(rows_per_step):
            for g in range(NA_GROUPS):
                attend(g, j, True)

    @pl.when(step * rows_per_step >= n_rows)
    def _():
        for j in range(rows_per_step):
            for g in range(NA_GROUPS):
                attend(g, j, False)


def _na_bias_table(rpb):
    o = jnp.arange(NA_WIN_ROWS)
    dr = o[None, :] - o[:, None] + NA_WIN_ROWS - 1
    col = jnp.arange(GRID_W)
    dc = jnp.clip(col[None, :] - col[:, None], -(NA_WIN_COLS - 1), NA_WIN_COLS - 1) + NA_WIN_COLS - 1
    col_start = jnp.clip(col - NA_WIN_COLS // 2, 0, GRID_W - NA_WIN_COLS)
    col_mask = (col[None, :] >= col_start[:, None]) & (col[None, :] < col_start[:, None] + NA_WIN_COLS)
    onehot = (dc[None] == jnp.arange(2 * NA_WIN_COLS - 1)[:, None, None]).astype(F32)
    t = jnp.einsum('howc,cqk->howqk', rpb[:, dr].astype(F32), onehot,
                   precision=lax.Precision.HIGHEST)
    t = jnp.where(col_mask[None, None, None], t, MASK_VALUE)
    t = t.transpose(1, 0, 3, 2, 4)
    return t.reshape(NA_WIN_ROWS, NA_GROUPS, HEADS_PER_GROUP * GRID_W, NA_LOCAL_KEYS)


def _neighborhood_attention(qkv, bias_tab, n_lat_tiles, with_ctx_queries):
    bsz, s, _ = qkv.shape
    seq = n_lat_tiles * TM
    n_rows = seq // GRID_W
    assert n_rows >= NA_WIN_ROWS
    n_ctx_rows = (s - seq) // GRID_W
    rq = NA_ROWS_PER_STEP
    assert n_rows % rq == 0 and n_ctx_rows % rq == 0
    nq = (n_rows + (n_ctx_rows if with_ctx_queries else 0)) // rq
    ctx_blk = seq // (s - seq)

    def bias_spec(j):
        def idx(b, i):
            rr = jnp.minimum(i * rq + j, n_rows - 1)
            return (rr - jnp.clip(rr - NA_WIN_ROWS // 2, 0, n_rows - NA_WIN_ROWS), 0, 0, 0)
        return pl.BlockSpec((1,) + bias_tab.shape[1:], idx)

    return pl.pallas_call(
        functools.partial(_na_kernel, n_rows=n_rows),
        grid=(bsz, nq),
        in_specs=[
            pl.BlockSpec((1, rq * GRID_W, NA_WIDTH), lambda b, i: (b, i, 0)),
            pl.BlockSpec((1, seq, NA_WIDTH), lambda b, i: (b, 0, 1)),
            pl.BlockSpec((1, seq, NA_WIDTH), lambda b, i: (b, 0, 2)),
            pl.BlockSpec((1, s - seq, NA_WIDTH), lambda b, i: (b, ctx_blk, 1)),
            pl.BlockSpec((1, s - seq, NA_WIDTH), lambda b, i: (b, ctx_blk, 2)),
        ] + [bias_spec(j) for j in range(rq)],
        out_specs=pl.BlockSpec((1, rq * GRID_W, NA_WIDTH), lambda b, i: (b, i, 0)),
        out_shape=jax.ShapeDtypeStruct((bsz, nq * rq * GRID_W, NA_WIDTH), BF16),
        compiler_params=_params("parallel", "arbitrary"),
        name="neighborhood_attention",
    )(qkv, qkv, qkv, qkv, qkv, *([bias_tab] * rq))


def _pool_kernel(up_ref, u_ref, un_ref, w_ref, sc_ref, o_ref, buf, s0, s1, s2, *, n_lat_tiles, seq, ctx_len):
    i = pl.program_id(1)
    hw = POOL_REACH
    n = TM + 2 * hw
    is_ctx = i >= n_lat_tiles
    base = jnp.where(is_ctx, 0, i * TM)
    lseq = jnp.where(is_ctx, ctx_len, seq)
    buf[0:hw] = up_ref[0]
    buf[hw:hw + TM] = u_ref[0]
    buf[hw + TM:n] = un_ref[0]
    p = base - hw + lax.broadcasted_iota(jnp.int32, (n, POOL_WIDTH), 0)
    buf[...] = jnp.where((p >= 0) & (p < lseq), buf[...], 0.0)
    s0[0:n - 1] = buf[0:n - 1] + buf[1:n]
    s1[0:n - 3] = s0[0:n - 3] + s0[2:n - 1]
    s2[0:n - 7] = s1[0:n - 7] + s1[4:n - 3]
    shape = (TM, POOL_WIDTH)
    t = base + lax.broadcasted_iota(jnp.int32, shape, 0)
    grp = lax.broadcasted_iota(jnp.int32, shape, 1) // POOL_GROUP_DIM
    acc = jnp.where(grp == 0, s0[hw - 1:hw - 1 + TM],
                    jnp.where(grp == 1, s1[hw - 2:hw - 2 + TM],
                              jnp.where(grp == 2, s2[hw - 4:hw - 4 + TM], s2[0:TM] + s2[hw:hw + TM])))
    win = jnp.left_shift(2, grp)
    back = win // 2
    fwd = win - back - 1
    lo = jnp.clip(t - back, 0, lseq - 1)
    hi = jnp.clip(t + fwd, 0, lseq - 1)
    mean = acc / (hi - lo + 1).astype(F32)
    diff = (mean - u_ref[0]).astype(BF16)
    o_ref[0] = (_dot(diff, w_ref[...]) * sc_ref[...]).astype(o_ref.dtype)


def _multiscale_pool(u, w_bd, scale, n_lat_tiles, n_tiles):
    bsz, s, c = u.shape
    hw = POOL_REACH
    per = TM // hw
    last = s // hw - 1
    return pl.pallas_call(
        functools.partial(_pool_kernel, n_lat_tiles=n_lat_tiles, seq=n_lat_tiles * TM, ctx_len=s - n_lat_tiles * TM),
        grid=(bsz, n_tiles),
        in_specs=[
            pl.BlockSpec((1, hw, c), lambda b, i: (b, jnp.maximum(i * per - 1, 0), 0)),
            pl.BlockSpec((1, TM, c), lambda b, i: (b, i, 0)),
            pl.BlockSpec((1, hw, c), lambda b, i: (b, jnp.minimum((i + 1) * per, last), 0)),
            pl.BlockSpec((c, c), lambda b, i: (0, 0)),
            pl.BlockSpec((1, c), lambda b, i: (0, 0)),
        ],
        out_specs=pl.BlockSpec((1, TM, c), lambda b, i: (b, i, 0)),
        out_shape=jax.ShapeDtypeStruct((bsz, n_tiles * TM, c), BF16),
        scratch_shapes=[pltpu.VMEM((TM + 2 * hw, c), F32)] * 4,
        compiler_params=_params("parallel", "parallel"),
        name="multiscale_pool",
    )(u, u, u, w_bd, scale)


def _gla_tile(qk_ref, v_ref, la_ref, cs_ref, sn_ref, st_ref, o_ref, reverse):
    ch = GLA_CHUNK
    lane = lax.broadcasted_iota(jnp.int32, (ch, GLA_QK), 1)
    row = lax.broadcasted_iota(jnp.int32, (ch, GLA_QK), 0)
    rr = lax.broadcasted_iota(jnp.int32, (GLA_HEADS * ch, ch), 0) % ch
    cc = lax.broadcasted_iota(jnp.int32, (GLA_HEADS * ch, ch), 1)
    causal = (rr <= cc) if reverse else (rr >= cc)
    st_shape = (GLA_WIDTH, GLA_QK)
    head_blk = (lax.broadcasted_iota(jnp.int32, st_shape, 0) // GLA_DV
                == lax.broadcasted_iota(jnp.int32, st_shape, 1) // GLA_DK)
    quarter = GLA_DK // 4
    first_half = (lane % (2 * quarter)) < quarter

    def rope(x, cs, sn):
        partner = jnp.where(first_half, pltpu.roll(x, GLA_QK - quarter, 1), pltpu.roll(x, quarter, 1))
        return x * cs + partner * sn

    chunks = range(TM // ch)
    for c in (reversed(chunks) if reverse else chunks):
        sl = slice(c * ch, (c + 1) * ch)
        cs = cs_ref[sl, :]
        sn = sn_ref[sl, :]
        q = rope(qk_ref[0, sl, 0:GLA_QK], cs, sn)
        k = rope(qk_ref[0, sl, GLA_QK:2 * GLA_QK], cs, sn)
        b = la_ref[0, sl, :]
        d = 1
        while d < ch:
            if reverse:
                b = b + jnp.where(row < ch - d, pltpu.roll(b, ch - d, 0), 0.0)
            else:
                b = b + jnp.where(row >= d, pltpu.roll(b, d, 0), 0.0)
            d *= 2
        b_last = b[0:1, :] if reverse else b[ch - 1:ch, :]
        q_in = q * jnp.exp(b)
        k_in = (k * jnp.exp(-b)).astype(BF16)
        k_st = (k * jnp.exp(b_last - b)).astype(BF16)
        att = _dot_nt(_stack_heads(q_in, GLA_DK).astype(BF16), k_in)
        att = jnp.where(causal, att, 0.0).astype(BF16)
        vb = v_ref[0, sl, :].astype(BF16)
        o = _unstack_heads(_dot(att, vb), GLA_DV)
        st = st_ref[...]
        o_ref[0, sl, :] = o + _dot_nt(q_in.astype(BF16), st.astype(BF16))
        st_ref[...] = st * jnp.exp(b_last) + jnp.where(head_blk, _dot_tn(vb, k_st), 0.0)


def _gla_kernel(qkf, vf, laf, csf, snf, qkb, vb, lab, csb, snb, of_ref, ob_ref, st_f, st_b):
    @pl.when(pl.program_id(1) == 0)
    def _():
        st_f[...] = jnp.zeros_like(st_f)
        st_b[...] = jnp.zeros_like(st_b)

    _gla_tile(qkf, vf, laf, csf, snf, st_f, of_ref, False)
    _gla_tile(qkb, vb, lab, csb, snb, st_b, ob_ref, True)


def _rope_tables(seq, s):
    quarter = GLA_DK // 4
    t = np.arange(s)
    pos_row = np.where(t < seq, t // GRID_W, 0).astype(np.float32)
    pos_col = np.where(t < seq, t % GRID_W, 0).astype(np.float32)
    d = np.arange(GLA_QK) % GLA_DK
    inv_freq = ROPE_BASE ** (-jnp.arange(quarter, dtype=F32) / quarter)
    freq = inv_freq[d % quarter]
    pos = jnp.where((d < GLA_DK // 2)[None, :], pos_row[:, None], pos_col[:, None])
    ang = pos * freq[None, :]
    sign = np.where((d % (2 * quarter)) < quarter, -1.0, 1.0).astype(np.float32)
    return jnp.cos(ang), jnp.sin(ang) * sign[None, :]


def _gla(gg, cos, sin, n_lat_tiles):
    bsz, s, _ = gg.shape
    nt = s // TM
    assert nt == n_lat_tiles + 1
    fwd = lambda i: (i + n_lat_tiles) % nt
    bwd = lambda i: jnp.where(i == 0, n_lat_tiles, n_lat_tiles - i)
    la_f_blk = (2 * GLA_QK + 2 * GLA_WIDTH) // GLA_QK
    la_b_blk = la_f_blk + 1
    v_blk = 2 * GLA_QK // GLA_WIDTH

    def specs(order, la_blk):
        return [
            pl.BlockSpec((1, TM, 2 * GLA_QK), lambda b, i: (b, order(i), 0)),
            pl.BlockSpec((1, TM, GLA_WIDTH), lambda b, i: (b, order(i), v_blk)),
            pl.BlockSpec((1, TM, GLA_QK), lambda b, i: (b, order(i), la_blk)),
            pl.BlockSpec((TM, GLA_QK), lambda b, i: (order(i), 0)),
            pl.BlockSpec((TM, GLA_QK), lambda b, i: (order(i), 0)),
        ]

    return pl.pallas_call(
        _gla_kernel,
        grid=(bsz, nt),
        in_specs=specs(fwd, la_f_blk) + specs(bwd, la_b_blk),
        out_specs=[
            pl.BlockSpec((1, TM, GLA_WIDTH), lambda b, i: (b, fwd(i), 0)),
            pl.BlockSpec((1, TM, GLA_WIDTH), lambda b, i: (b, bwd(i), 0)),
        ],
        out_shape=[jax.ShapeDtypeStruct((bsz, s, GLA_WIDTH), F32)] * 2,
        scratch_shapes=[pltpu.VMEM((GLA_WIDTH, GLA_QK), F32)] * 2,
        compiler_params=_params("arbitrary", "arbitrary"),
        name="gla_bidir",
    )(gg, gg, gg, cos, sin, gg, gg, gg, cos, sin)


def _store_row_tiles(ref, val):
    rows, width = val.shape
    pieces = width // LANES
    for s in range(pieces):
        ref[pl.ds(s, rows, stride=pieces), :] = val[:, s * LANES:(s + 1) * LANES]


def _load_row_tiles(ref, rows, pieces):
    return jnp.concatenate([ref[pl.ds(s, rows, stride=pieces), :] for s in range(pieces)], axis=1)


SLOTS_PER_Y_TILE = 2


def _pack_bf16_pairs(y):
    half = y.shape[1] // 2
    bits = lax.bitcast_convert_type(y.astype(jnp.bfloat16).astype(F32), jnp.uint32)
    return (bits[:, :half] >> 16) | (bits[:, half:] & jnp.uint32(0xFFFF0000))


def _unpack_bf16_pairs(words):
    lo = lax.bitcast_convert_type(words << 16, F32)
    hi = lax.bitcast_convert_type(words & jnp.uint32(0xFFFF0000), F32)
    return jnp.concatenate([lo, hi], axis=1)


def _row_index(shape):
    return lax.broadcasted_iota(jnp.int32, shape, 0).astype(F32)


def _stack_rows(rows):
    shape = (len(rows), rows[0].shape[1])
    rid = lax.broadcasted_iota(jnp.int32, shape, 0)
    out = jnp.zeros(shape, rows[0].dtype)
    for k, r in enumerate(rows):
        out = jnp.where(rid == k, r, out)
    return out


def _select_experts(scores, biased):
    n_exp, n_tok = biased.shape
    per = n_exp // N_GROUPS
    neg = -jnp.inf
    sub = _row_index((per, n_tok))
    grp_rows = []
    for g in range(N_GROUPS):
        blk = biased[g * per:(g + 1) * per, :]
        m1 = jnp.max(blk, axis=0, keepdims=True)
        i1 = jnp.min(jnp.where(blk == m1, sub, float(per)), axis=0, keepdims=True)
        m2 = jnp.max(jnp.where(sub == i1, neg, blk), axis=0, keepdims=True)
        grp_rows.append(m1 + m2)
    cur = _stack_rows(grp_rows)
    gid = _row_index(cur.shape)
    picked = jnp.zeros(cur.shape, F32)
    for _ in range(TOPK_GROUPS):
        gm = jnp.max(cur, axis=0, keepdims=True)
        gi = jnp.min(jnp.where(cur == gm, gid, float(N_GROUPS)), axis=0, keepdims=True)
        hit = gid == gi
        picked = jnp.where(hit, 1.0, picked)
        cur = jnp.where(hit, neg, cur)
    cur = jnp.concatenate(
        [jnp.where(picked[g:g + 1, :] > 0.0, biased[g * per:(g + 1) * per, :], neg) for g in range(N_GROUPS)], axis=0)
    eid = _row_index(cur.shape)
    ids, vals = [], []
    chosen = jnp.zeros(cur.shape, F32)
    for _ in range(TOP_K):
        m = jnp.max(cur, axis=0, keepdims=True)
        idx = jnp.min(jnp.where(cur == m, eid, float(n_exp)), axis=0, keepdims=True)
        hit = eid == idx
        ids.append(idx)
        vals.append(jnp.sum(jnp.where(hit, scores, 0.0), axis=0, keepdims=True))
        chosen = jnp.where(hit, 1.0, chosen)
        cur = jnp.where(hit, neg, cur)
    return ids, vals, chosen


def _outproj_kernel(x_ref, xc_ref, oa_ref, ob_ref, of_ref, obk_ref, r_ref, gg_ref, wo_ref, mod_ref, gf_ref, wr_ref, br_ref,
                    tri_ref, xo_ref, h_ref, e_ref, w_ref, rk_ref, hist_ref, *, n_lat_tiles):
    o = of_ref[0] + obk_ref[0]
    head = lax.broadcasted_iota(jnp.int32, o.shape, 1) // GLA_DV
    o2 = o * o
    rs = jnp.zeros_like(o)
    for h in range(GLA_HEADS):
        ssq = jnp.sum(jnp.where(head == h, o2, 0.0), axis=-1, keepdims=True)
        rs = jnp.where(head == h, lax.rsqrt(ssq / GLA_DV + NORM_EPS), rs)
    oc = (o * rs * gg_ref[...] * _silu(r_ref[0])).astype(BF16)
    c1 = NA_WIDTH + POOL_WIDTH
    acc = _dot(oa_ref[0], wo_ref[0:NA_WIDTH]) + _dot(ob_ref[0], wo_ref[NA_WIDTH:c1]) + _dot(oc, wo_ref[c1:])
    x = _stream_tile(x_ref, xc_ref, n_lat_tiles) + mod_ref[0, 0, 2:3, :] * acc
    xo_ref[0] = x
    h = _rmsnorm(x, gf_ref[...]) * (1.0 + mod_ref[0, 0, 4:5, :]) + mod_ref[0, 0, 3:4, :]
    _store_row_tiles(h_ref, h)

    scores = _sigmoid(_dot_nt(wr_ref[...], h.astype(BF16)))
    ids, vals, chosen = _select_experts(scores, scores + br_ref[...])
    total = vals[0]
    for v in vals[1:]:
        total = total + v
    w_ref[...] = _stack_rows([v / total * ROUTED_SCALE for v in vals])
    e_ref[...] = _stack_rows(ids).astype(jnp.int32)
    chosen_b = chosen.astype(BF16)
    before = _dot(chosen_b, tri_ref[...])
    eid = _row_index(before.shape)
    rk_ref[...] = _stack_rows([jnp.sum(jnp.where(eid == idx, before, 0.0), axis=0, keepdims=True) for idx in ids])
    hist_ref[0] = _dot_nt(jnp.ones((SUBLANES, chosen.shape[1]), BF16), chosen_b)


def _outproj(x_lat, x_ctx, ctx_blk, oa, ob, o_f, o_b, gg, g_gla4, w_out, mod, g_ffn, w_router_t, b_router,
             n_lat_tiles, n_tiles):
    bsz, _, d = x_lat.shape
    n_exp = w_router_t.shape[0]
    r_blk = (2 * GLA_QK + GLA_WIDTH) // GLA_WIDTH
    tile = lambda w: pl.BlockSpec((1, TM, w), lambda b, i: (b, i, 0))
    const = lambda b, i: (0, 0)
    per_tok = pl.BlockSpec((TOP_K, TM), lambda b, i: (0, b * n_tiles + i))
    t_tok = bsz * n_tiles * TM
    tri = (jnp.arange(TM)[:, None] < jnp.arange(TM)[None, :]).astype(BF16)
    return pl.pallas_call(
        functools.partial(_outproj_kernel, n_lat_tiles=n_lat_tiles),
        grid=(bsz, n_tiles),
        in_specs=[
            *_stream_specs(d, n_lat_tiles, ctx_blk),
            tile(NA_WIDTH), tile(POOL_WIDTH), tile(GLA_WIDTH), tile(GLA_WIDTH),
            pl.BlockSpec((1, TM, GLA_WIDTH), lambda b, i: (b, i, r_blk)),
            pl.BlockSpec((1, GLA_WIDTH), const),
            pl.BlockSpec(w_out.shape, const),
            pl.BlockSpec((1, 1, SUBLANES, d), lambda b, i: (b, jnp.where(i >= n_lat_tiles, 0, 1), 0, 0)),
            pl.BlockSpec((1, d), const),
            pl.BlockSpec(w_router_t.shape, const),
            pl.BlockSpec((n_exp, 1), const),
            pl.BlockSpec((TM, TM), const),
        ],
        out_specs=[tile(d), pl.BlockSpec((TM * d // LANES, LANES), lambda b, i: (b * n_tiles + i, 0)),
                   per_tok, per_tok, per_tok,
                   pl.BlockSpec((1, SUBLANES, n_exp), lambda b, i: (b * n_tiles + i, 0, 0))],
        out_shape=[
            jax.ShapeDtypeStruct((bsz, n_tiles * TM, d), F32),
            jax.ShapeDtypeStruct((t_tok * d // LANES, LANES), F32),
            jax.ShapeDtypeStruct((TOP_K, t_tok), jnp.int32),
            jax.ShapeDtypeStruct((TOP_K, t_tok), F32),
            jax.ShapeDtypeStruct((TOP_K, t_tok), F32),
            jax.ShapeDtypeStruct((bsz * n_tiles, SUBLANES, n_exp), F32),
        ],
        compiler_params=_params("parallel", "parallel"),
        name="outproj_router",
    )(x_lat, x_ctx, oa, ob, o_f, o_b, gg, g_gla4, w_out, mod, g_ffn, w_router_t, b_router, tri)


def _slot_layout(hist):
    n_tiles, n_exp = hist.shape
    counts = hist.sum(0)
    tile_base = jnp.cumsum(hist, axis=0) - hist
    padded = (counts + MOE_BLOCK - 1) // MOE_BLOCK * MOE_BLOCK
    pad_end = jnp.cumsum(padded)
    pad_start = pad_end - padded
    base = (pad_start[None, :] + tile_base).astype(F32).reshape(n_tiles, n_exp, 1)
    n_blocks = -(-(n_tiles * TM * TOP_K) // MOE_BLOCK) + n_exp
    n_used = (pad_end[-1] // MOE_BLOCK).astype(jnp.int32)
    fill = jnp.concatenate([jnp.maximum(pad_end // MOE_BLOCK - 1, 0).astype(jnp.int32),
                            jnp.minimum(n_used + jnp.arange(n_exp, dtype=jnp.int32), n_blocks - 1)])
    first_blk = (pad_start // MOE_BLOCK).astype(jnp.int32)
    blk_count = (padded // MOE_BLOCK).astype(jnp.int32)
    return base, first_blk, blk_count, fill, n_used.reshape(1), n_blocks


def _pos_kernel(e_ref, rk_ref, base_ref, posx_ref, posy_ref, half_ref, *, pieces):
    eid = lax.broadcasted_iota(jnp.int32, (base_ref.shape[1], TM), 0)
    for j in range(base_ref.shape[0]):
        cols = slice(j * TM, (j + 1) * TM)
        base = base_ref[j]
        rows = [jnp.sum(jnp.where(eid == e_ref[k:k + 1, cols], base, 0.0), axis=0, keepdims=True)
                for k in range(TOP_K)]
        slot = (_stack_rows(rows) + rk_ref[:, cols]).astype(jnp.int32)
        posx_ref[j] = slot * pieces
        posy_ref[j] = (slot // SLOTS_PER_Y_TILE) * pieces
        half_ref[:, cols] = slot % SLOTS_PER_Y_TILE


def _slot_positions(e_t, rank_t, base, pieces):
    n_tiles, n_exp, _ = base.shape
    per_step = max(p for p in range(1, DISPATCH_TILES + 1) if n_tiles % p == 0)
    per_tok = pl.BlockSpec((TOP_K, per_step * TM), lambda i: (0, i))
    flat = pl.BlockSpec((per_step, TOP_K, TM), lambda i: (i, 0, 0))
    pos_x, pos_y, half = pl.pallas_call(
        functools.partial(_pos_kernel, pieces=pieces),
        grid=(n_tiles // per_step,),
        in_specs=[per_tok, per_tok, pl.BlockSpec((per_step, n_exp, 1), lambda i: (i, 0, 0))],
        out_specs=[flat, flat, per_tok],
        out_shape=[jax.ShapeDtypeStruct((n_tiles, TOP_K, TM), jnp.int32)] * 2
        + [jax.ShapeDtypeStruct(e_t.shape, jnp.int32)],
        compiler_params=_params("parallel"),
        name="moe_positions",
    )(e_t, rank_t, base)
    return pos_x.reshape(-1), pos_y.reshape(-1), half


def _dispatch_kernel(fill_ref, pos_ref, h_ref, xs_hbm, zbuf, sem, zsem):
    @pl.when(pl.program_id(0) == 0)
    def _():
        zbuf[...] = jnp.zeros_like(zbuf)
        n_fill = fill_ref.shape[0]

        def is_new(j):
            return (j == 0) | (fill_ref[j] != fill_ref[jnp.maximum(j - 1, 0)])

        def start(j, c):
            @pl.when(is_new(j))
            def _():
                row0 = pl.multiple_of(fill_ref[j] * zbuf.shape[0], zbuf.shape[0])
                pltpu.make_async_copy(zbuf, xs_hbm.at[pl.ds(row0, zbuf.shape[0]), :], zsem).start()
            return c
        lax.fori_loop(0, n_fill, start, 0)

        def wait(j, c):
            @pl.when(is_new(j))
            def _():
                pltpu.make_async_copy(zbuf, xs_hbm.at[pl.ds(0, zbuf.shape[0]), :], zsem).wait()
            return c
        lax.fori_loop(0, n_fill, wait, 0)

    tiles = pos_ref.shape[0] // (TOP_K * TM)
    pieces = h_ref.shape[0] // (tiles * TM)

    for j in range(tiles):
        def body(t, c):
            src = h_ref.at[pl.ds(pl.multiple_of((j * TM + t) * pieces, pieces), pieces), :]
            for k in range(TOP_K):
                dst = xs_hbm.at[pl.ds(pl.multiple_of(pos_ref[(j * TOP_K + k) * TM + t], pieces), pieces), :]
                pltpu.make_async_copy(src, dst, sem).start(priority=k % 2)
            return c
        lax.fori_loop(0, TM, body, 0, unroll=2)
    for _ in range(TOP_K):
        pltpu.make_async_copy(h_ref, xs_hbm.at[pl.ds(0, h_ref.shape[0]), :], sem).wait()


def _dispatch(h2t, pos, fill, n_slots, pieces):
    n_tiles = h2t.shape[0] // (TM * pieces)
    per_step = max(p for p in range(1, DISPATCH_TILES + 1) if n_tiles % p == 0)
    grid_spec = pltpu.PrefetchScalarGridSpec(
        num_scalar_prefetch=1,
        grid=(n_tiles // per_step,),
        in_specs=[
            pl.BlockSpec((per_step * TOP_K * TM,), lambda i, fill: (i,), memory_space=pltpu.SMEM),
            pl.BlockSpec((per_step * TM * pieces, LANES), lambda i, fill: (i, 0)),
        ],
        out_specs=pl.BlockSpec(memory_space=pl.ANY),
        scratch_shapes=[pltpu.VMEM((MOE_BLOCK * pieces, LANES), F32), pltpu.SemaphoreType.DMA,
                        pltpu.SemaphoreType.DMA],
    )
    return pl.pallas_call(
        _dispatch_kernel,
        grid_spec=grid_spec,
        out_shape=jax.ShapeDtypeStruct((n_slots * pieces, LANES), F32),
        compiler_params=_params("arbitrary"),
        name="moe_dispatch",
    )(fill, pos, h2t)


def _expert_kernel(first_ref, cnt_ref, nu_ref, wi_ref, wd_ref, xs_hbm, ys_hbm, xbuf, ybuf, wi_bf, wd_bf, xsem, ysem,
                   *, n_blocks):
    e = pl.program_id(0)
    n_used = nu_ref[0]
    rows = xbuf.shape[1]
    pieces = rows // MOE_BLOCK
    d_exp = wd_ref.shape[2]

    def x_copy(g, s):
        return pltpu.make_async_copy(xs_hbm.at[pl.ds(pl.multiple_of(g * rows, rows), rows), :], xbuf.at[s], xsem.at[s])

    y_rows = ybuf.shape[1]

    def y_copy(g, s):
        return pltpu.make_async_copy(ybuf.at[s], ys_hbm.at[pl.ds(pl.multiple_of(g * y_rows, y_rows), y_rows), :],
                                     ysem.at[s])

    n_xbuf = xbuf.shape[0]
    n_ybuf = ybuf.shape[0]

    @pl.when(e == 0)
    def _():
        for g in range(n_xbuf - 1):
            @pl.when(g < n_used)
            def _():
                x_copy(g, g).start()

    @pl.when(cnt_ref[e] > 0)
    def _():
        wi_bf[...] = wi_ref[0, 0].astype(BF16)
        wd_bf[...] = wd_ref[0, 0].astype(BF16)

    def block(j, c):
        g = first_ref[e] + j
        s = g % n_ybuf
        x_copy(g, g % n_xbuf).wait()

        @pl.when(g + n_xbuf - 1 < n_used)
        def _():
            x_copy(g + n_xbuf - 1, (g + n_xbuf - 1) % n_xbuf).start()

        @pl.when(g >= n_ybuf)
        def _():
            y_copy(g - n_ybuf, s).wait()

        hh = _dot(_load_row_tiles(xbuf.at[g % n_xbuf], MOE_BLOCK, pieces).astype(BF16), wi_bf[...])
        a = (_silu(hh[:, :d_exp]) * hh[:, d_exp:]).astype(BF16)
        _store_row_tiles(ybuf.at[s], _pack_bf16_pairs(_dot(a, wd_bf[...])))
        y_copy(g, s).start(priority=1)
        return c
    lax.fori_loop(0, cnt_ref[e], block, 0)

    @pl.when(e == pl.num_programs(0) - 1)
    def _():
        for back in range(1, n_ybuf + 1):
            @pl.when(n_used >= back)
            def _():
                y_copy(n_used - back, (n_used - back) % n_ybuf).wait()

        ybuf[0] = jnp.zeros(ybuf.shape[1:], ybuf.dtype)

        def start(g, c):
            y_copy(g, 0).start()
            return c
        lax.fori_loop(n_used, n_blocks, start, 0)

        def wait(g, c):
            y_copy(g, 0).wait()
            return c
        lax.fori_loop(n_used, n_blocks, wait, 0)


def _moe_experts(xs, first_blk, blk_count, n_used, n_blocks, w_e_in, w_e_down, layer):
    _, n_exp, d, f2 = w_e_in.shape
    rows = xs.shape[0] // n_blocks
    grid_spec = pltpu.PrefetchScalarGridSpec(
        num_scalar_prefetch=3,
        grid=(n_exp,),
        in_specs=[
            pl.BlockSpec((1, 1, d, f2), lambda e, first, cnt, nu: (layer, e, 0, 0)),
            pl.BlockSpec((1, 1, f2 // 2, d), lambda e, first, cnt, nu: (layer, e, 0, 0)),
            pl.BlockSpec(memory_space=pl.ANY),
        ],
        out_specs=pl.BlockSpec(memory_space=pl.ANY),
        scratch_shapes=[
            pltpu.VMEM((4, rows, LANES), F32),
            pltpu.VMEM((3, rows // SLOTS_PER_Y_TILE, LANES), jnp.uint32),
            pltpu.VMEM((d, f2), BF16),
            pltpu.VMEM((f2 // 2, d), BF16),
            pltpu.SemaphoreType.DMA((4,)),
            pltpu.SemaphoreType.DMA((3,)),
        ],
    )
    return pl.pallas_call(
        functools.partial(_expert_kernel, n_blocks=n_blocks),
        grid_spec=grid_spec,
        out_shape=jax.ShapeDtypeStruct((xs.shape[0] // SLOTS_PER_Y_TILE, LANES), jnp.uint32),
        compiler_params=_params("arbitrary"),
        name="moe_experts",
    )(first_blk, blk_count, n_used, w_e_in, w_e_down, xs)


def _combine_kernel(pos0_ref, posn_ref, x_ref, h_ref, w_ref, half_ref, wsi_ref, wsd_ref, mod_ref, gfin_ref, ys_hbm,
                    o_ref, ybuf, sem, *, final, n_steps):
    step = pl.program_id(0) * pl.num_programs(1) + pl.program_id(1)
    slot = step % 2
    d_exp = wsd_ref.shape[0]
    pieces = h_ref.shape[0] // TMC

    def gather(pos_ref, st, s):
        off = (st % (TM // TMC)) * TMC
        def body(t, c):
            for k in range(TOP_K):
                src = ys_hbm.at[pl.ds(pl.multiple_of(pos_ref[k * TM + off + t], pieces), pieces), :]
                dst = ybuf.at[s, k, pl.ds(pl.multiple_of(t * pieces, pieces), pieces), :]
                pltpu.make_async_copy(src, dst, sem.at[s]).start(priority=k % 2)
            return c
        lax.fori_loop(0, TMC, body, 0, unroll=2)

    @pl.when(step == 0)
    def _():
        gather(pos0_ref, 0, 0)

    @pl.when(step + 1 < n_steps)
    def _():
        gather(posn_ref, step + 1, 1 - slot)

    hh = _dot(_load_row_tiles(h_ref, TMC, pieces).astype(BF16), wsi_ref[...])
    a = (_silu(hh[:, :d_exp]) * hh[:, d_exp:]).astype(BF16)
    y = _dot(a, wsd_ref[...])
    for k in range(TOP_K):
        pltpu.make_async_copy(ys_hbm.at[pl.ds(0, TMC * pieces), :], ybuf.at[slot, k], sem.at[slot]).wait()
    w = w_ref[...]
    half = half_ref[...]
    for k in range(TOP_K):
        both = _load_row_tiles(ybuf.at[slot, k], TMC, pieces)
        mine = jnp.where(half[:, k:k + 1] > 0, both[:, both.shape[1] // 2:], both[:, :both.shape[1] // 2])
        y = y + _unpack_bf16_pairs(mine) * w[:, k:k + 1]
    x = x_ref[0] + mod_ref[0, 0, 5:6, :] * y
    if final:
        x = _rmsnorm(x, gfin_ref[...])
    o_ref[0] = x


def _combine(x_mid, h2t, ys, pos, w_tok, half_tok, w_sh_in, w_sh_down, mod, g_final, n_lat_tiles, final):
    bsz, s, d = x_mid.shape
    nt = s // TMC
    n_steps = bsz * nt
    halves = TM // TMC
    lat_steps = n_lat_tiles * halves
    pieces = d // LANES
    const = lambda b, i: (0, 0)
    tile = pl.BlockSpec((1, TMC, d), lambda b, i: (b, i, 0))
    smem_blk = lambda f: pl.BlockSpec((TOP_K * TM,), f, memory_space=pltpu.SMEM)
    return pl.pallas_call(
        functools.partial(_combine_kernel, final=final, n_steps=n_steps),
        grid=(bsz, nt),
        in_specs=[
            smem_blk(lambda b, i: (0,)),
            smem_blk(lambda b, i: (jnp.minimum(b * nt + i + 1, n_steps - 1) // halves,)),
            tile,
            pl.BlockSpec((TMC * pieces, LANES), lambda b, i: (b * nt + i, 0)),
            pl.BlockSpec((TMC, TOP_K), lambda b, i: (b * nt + i, 0)),
            pl.BlockSpec((TMC, TOP_K), lambda b, i: (b * nt + i, 0)),
            pl.BlockSpec(w_sh_in.shape, const),
            pl.BlockSpec(w_sh_down.shape, const),
            pl.BlockSpec((1, 1, SUBLANES, d), lambda b, i: (b, jnp.where(i >= lat_steps, 0, 1), 0, 0)),
            pl.BlockSpec((1, d), const),
            pl.BlockSpec(memory_space=pl.ANY),
        ],
        out_specs=tile,
        out_shape=jax.ShapeDtypeStruct((bsz, s, d), F32),
        scratch_shapes=[pltpu.VMEM((2, TOP_K, TMC * pieces, LANES), jnp.uint32), pltpu.SemaphoreType.DMA((2,))],
        compiler_params=_params("arbitrary", "arbitrary"),
        name="moe_combine",
    )(pos, pos, x_mid, h2t, w_tok, half_tok, w_sh_in, w_sh_down, mod, g_final, ys)


def _block_diag(w):
    g, a, b = w.shape
    out = jnp.zeros((g * a, g * b), w.dtype)
    for j in range(g):
        out = out.at[j * a:(j + 1) * a, j * b:(j + 1) * b].set(w[j])
    return out


def kernel(x, c, ctx, c_ctx, w_mod, b_mod, g_mix, w_in, rpb, w_pool, pool_scale, w_gate_f, b_gate_f, w_gate_b, b_gate_b, g_gla, w_out, g_ffn, w_router, b_router, w_e_in, w_e_down, w_sh_in, w_sh_down, g_final):
    bsz, seq, d = x.shape
    n_ctx = ctx.shape[1]
    depth = w_mod.shape[0]
    n_exp = w_router.shape[2]
    assert n_ctx == TM and seq % TM == 0 and seq % GRID_W == 0
    n_lat = seq // TM
    s = seq + n_ctx

    n_rows = -(-(bsz + 1) // SUBLANES) * SUBLANES
    cc = jnp.zeros((n_rows, d), F32).at[:bsz].set(c).at[bsz].set(c_ctx)
    mod_all = _modulation(cc, w_mod, b_mod).reshape(depth, n_rows, 6, d)
    cos, sin = _rope_tables(seq, s)

    x_lat, x_ctx, ctx_blk = x, ctx, 0
    for layer in range(depth):
        last = layer == depth - 1
        n_tiles = n_lat if last else n_lat + 1
        m = mod_all[layer]
        mod = jnp.stack([jnp.broadcast_to(m[bsz], (bsz, 6, d)), m[:bsz]], axis=1)
        mod = jnp.pad(mod, ((0, 0), (0, 0), (0, SUBLANES - 6), (0, 0)))
        wl = w_in[layer]
        w_main = wl[:, :MAIN_W].astype(BF16)
        w_low = jnp.pad(wl[:, MAIN_W:], ((0, 0), (0, LANES - 2 * GLA_GATE_RANK))).astype(BF16)
        w2 = jnp.zeros((LANES, 2 * GLA_QK), F32)
        w2 = w2.at[:GLA_GATE_RANK, :GLA_QK].set(w_gate_f[layer])
        w2 = w2.at[GLA_GATE_RANK:2 * GLA_GATE_RANK, GLA_QK:].set(w_gate_b[layer]).astype(BF16)
        b2 = jnp.concatenate([b_gate_f[layer], b_gate_b[layer]])[None, :]
        qkv, u, gg = _inproj(x_lat, x_ctx, ctx_blk, mod, g_mix[layer][None, :], w_main, w_low, w2, b2, n_lat)

        oa = _neighborhood_attention(qkv, _na_bias_table(rpb[layer]), n_lat, not last)
        ob = _multiscale_pool(u, _block_diag(w_pool[layer]).astype(BF16), pool_scale[layer][None, :], n_lat, n_tiles)
        o_f, o_b = _gla(gg, cos, sin, n_lat)

        x_mid, h2t, e_t, w_t, rank_t, hist = _outproj(
            x_lat, x_ctx, ctx_blk, oa, ob, o_f, o_b, gg, jnp.tile(g_gla[layer], GLA_HEADS)[None, :], w_out[layer].astype(BF16), mod,
            g_ffn[layer][None, :], w_router[layer].T.astype(BF16), b_router[layer].reshape(n_exp, 1), n_lat, n_tiles)
        pieces = d // LANES
        base, first_blk, blk_count, fill, n_used, n_blocks = _slot_layout(hist[:, 0, :].astype(jnp.int32))
        pos_x, pos_y, half = _slot_positions(e_t, rank_t, base, pieces)
        xs = _dispatch(h2t, pos_x, fill, n_blocks * MOE_BLOCK, pieces)
        ys = _moe_experts(xs, first_blk, blk_count, n_used, n_blocks, w_e_in, w_e_down, layer)
        xa = _combine(x_mid, h2t, ys, pos_y, w_t.T, half.T, w_sh_in[layer].astype(BF16),
                      w_sh_down[layer].astype(BF16), mod, g_final[None, :], n_lat, last)
        x_lat, x_ctx, ctx_blk = xa, xa, n_lat
    return xa
```

```python
import functools

import jax
import jax.numpy as jnp
import numpy as np
from jax import lax
from jax.experimental import pallas as pl
from jax.experimental.pallas import tpu as pltpu

GRID_W = 64
NORM_EPS = 1e-6
NA_HEADS = 8
NA_HEAD_DIM = 64
NA_WIDTH = NA_HEADS * NA_HEAD_DIM
NA_WIN_ROWS = 8
NA_WIN_COLS = 16
POOL_WINDOWS = (2, 4, 8, 16)
POOL_GROUP_DIM = 64
POOL_WIDTH = len(POOL_WINDOWS) * POOL_GROUP_DIM
POOL_REACH = max(POOL_WINDOWS) // 2
assert POOL_WINDOWS == tuple(2 ** (g + 1) for g in range(len(POOL_WINDOWS)))
GLA_HEADS = 4
GLA_DK = 32
GLA_DV = 64
GLA_QK = GLA_HEADS * GLA_DK
GLA_WIDTH = GLA_HEADS * GLA_DV
GLA_GATE_RANK = 16
GLA_TAU = 16.0
GLA_CHUNK = 64
ROPE_BASE = 10000.0
N_EXPERTS = 256
TOP_K = 8
N_GROUPS = 8
TOPK_GROUPS = 4
ROUTED_SCALE = 2.5

TM = 256
TMC = 256
DISPATCH_TILES = 8
EXPERT_X_BUFFERS = 6
EXPERT_Y_BUFFERS = 4
MOE_BLOCK = 256
LANES = 128
SUBLANES = 8
MXU_DIM = 256
MASK_VALUE = -1e30
V7X_VMEM_BYTES = 64 * 1024 * 1024
VMEM_LIMIT = V7X_VMEM_BYTES * 3 // 4

QKV_W = 3 * NA_WIDTH
G_W = 2 * GLA_QK + 2 * GLA_WIDTH + 2 * GLA_QK
MAIN_W = QKV_W + POOL_WIDTH + 2 * GLA_QK + 2 * GLA_WIDTH

BF16 = jnp.bfloat16
F32 = jnp.float32


def _params(*sem):
    return pltpu.CompilerParams(dimension_semantics=sem, vmem_limit_bytes=VMEM_LIMIT)


def _sigmoid(x):
    return 1.0 / (1.0 + jnp.exp(-x))


def _silu(x):
    return x * _sigmoid(x)


def _rmsnorm(x, g):
    return x * lax.rsqrt(jnp.mean(x * x, axis=-1, keepdims=True) + NORM_EPS) * g


def _dot(a, b):
    return jnp.dot(a, b, preferred_element_type=F32)


def _dot_nt(a, b):
    return lax.dot_general(a, b, (((1,), (1,)), ((), ())), preferred_element_type=F32)


def _dot_tn(a, b):
    return lax.dot_general(a, b, (((0,), (0,)), ((), ())), preferred_element_type=F32)


def _mod_kernel(c_ref, w_ref, b_ref, o_ref):
    a = _silu(c_ref[...]).astype(BF16)
    o_ref[0] = _dot(a, w_ref[0].astype(BF16)) + b_ref[0]


def _modulation(cc, w_mod, b_mod):
    depth, d, n = w_mod.shape
    r = cc.shape[0]
    tn = d
    return pl.pallas_call(
        _mod_kernel,
        grid=(depth, n // tn),
        in_specs=[
            pl.BlockSpec((r, d), lambda l, j: (0, 0)),
            pl.BlockSpec((1, d, tn), lambda l, j: (l, 0, j)),
            pl.BlockSpec((1, 1, tn), lambda l, j: (l, 0, j)),
        ],
        out_specs=pl.BlockSpec((1, r, tn), lambda l, j: (l, 0, j)),
        out_shape=jax.ShapeDtypeStruct((depth, r, n), F32),
        compiler_params=_params("parallel", "parallel"),
        name="modulation",
    )(cc, w_mod, b_mod.reshape(depth, 1, n))


def _stream_tile(x_ref, xc_ref, n_lat_tiles):
    return jnp.where(pl.program_id(1) >= n_lat_tiles, xc_ref[0], x_ref[0])


def _stream_specs(d, n_lat_tiles, ctx_blk):
    return [pl.BlockSpec((1, TM, d), lambda b, i: (b, jnp.minimum(i, n_lat_tiles - 1), 0)),
            pl.BlockSpec((1, TM, d), lambda b, i: (b, ctx_blk, 0))]


def _inproj_kernel(x_ref, xc_ref, mod_ref, g_ref, wm_ref, wl_ref, w2_ref, b2_ref, qkv_ref, u_ref, gg_ref, *, n_lat_tiles):
    x = _stream_tile(x_ref, xc_ref, n_lat_tiles)
    h = _rmsnorm(x, g_ref[...]) * (1.0 + mod_ref[0, 0, 1:2, :]) + mod_ref[0, 0, 0:1, :]
    hb = h.astype(BF16)
    q = _dot(hb, wm_ref[:, 0:NA_WIDTH]) * (NA_HEAD_DIM ** -0.5)
    qkv_ref[0, :, 0:NA_WIDTH] = q.astype(BF16)
    qkv_ref[0, :, NA_WIDTH:2 * NA_WIDTH] = _dot(hb, wm_ref[:, NA_WIDTH:2 * NA_WIDTH]).astype(BF16)
    qkv_ref[0, :, 2 * NA_WIDTH:QKV_W] = _dot(hb, wm_ref[:, 2 * NA_WIDTH:QKV_W]).astype(BF16)
    u_ref[0] = _dot(hb, wm_ref[:, QKV_W:QKV_W + POOL_WIDTH])
    c0 = QKV_W + POOL_WIDTH
    gg_ref[0, :, 0:GLA_QK] = _dot(hb, wm_ref[:, c0:c0 + GLA_QK]) * (GLA_DK ** -0.5)
    gg_ref[0, :, GLA_QK:2 * GLA_QK + 2 * GLA_WIDTH] = _dot(hb, wm_ref[:, c0 + GLA_QK:MAIN_W])
    a_low = _dot(hb, wl_ref[...]).astype(BF16)
    lg = _dot(a_low, w2_ref[...]) + b2_ref[...]
    log_sig = jnp.minimum(lg, 0.0) - jnp.log1p(jnp.exp(-jnp.abs(lg)))
    gg_ref[0, :, 2 * GLA_QK + 2 * GLA_WIDTH:G_W] = log_sig / GLA_TAU


def _inproj(x_lat, x_ctx, ctx_blk, mod, g, w_main, w_low, w2, b2, n_lat_tiles):
    bsz, _, d = x_lat.shape
    nt = n_lat_tiles + 1
    s = nt * TM
    const = lambda b, i: (0, 0)
    return pl.pallas_call(
        functools.partial(_inproj_kernel, n_lat_tiles=n_lat_tiles),
        grid=(bsz, nt),
        in_specs=_stream_specs(d, n_lat_tiles, ctx_blk) + [
            pl.BlockSpec((1, 1, SUBLANES, d), lambda b, i: (b, jnp.where(i >= n_lat_tiles, 0, 1), 0, 0)),
            pl.BlockSpec((1, d), const),
            pl.BlockSpec(w_main.shape, const),
            pl.BlockSpec(w_low.shape, const),
            pl.BlockSpec(w2.shape, const),
            pl.BlockSpec(b2.shape, const),
        ],
        out_specs=[
            pl.BlockSpec((1, TM, QKV_W), lambda b, i: (b, i, 0)),
            pl.BlockSpec((1, TM, POOL_WIDTH), lambda b, i: (b, i, 0)),
            pl.BlockSpec((1, TM, G_W), lambda b, i: (b, i, 0)),
        ],
        out_shape=[
            jax.ShapeDtypeStruct((bsz, s, QKV_W), BF16),
            jax.ShapeDtypeStruct((bsz, s, POOL_WIDTH), F32),
            jax.ShapeDtypeStruct((bsz, s, G_W), F32),
        ],
        compiler_params=_params("parallel", "parallel"),
        name="inproj",
    )(x_lat, x_ctx, mod, g, w_main, w_low, w2, b2)


HEADS_PER_GROUP = MXU_DIM // NA_HEAD_DIM
NA_GROUPS = NA_HEADS // HEADS_PER_GROUP
NA_LOCAL_KEYS = NA_WIN_ROWS * GRID_W
NA_ROWS_PER_STEP = 4


def _stack_heads(x, width):
    lane = lax.broadcasted_iota(jnp.int32, x.shape, 1) // width
    n_heads = x.shape[1] // width
    return jnp.concatenate([jnp.where(lane == h, x, jnp.zeros_like(x)) for h in range(n_heads)], axis=0)


def _unstack_heads(o, width):
    n_heads = o.shape[1] // width
    r = o.shape[0] // n_heads
    lane = lax.broadcasted_iota(jnp.int32, (r, o.shape[1]), 1) // width
    acc = jnp.zeros((r, o.shape[1]), o.dtype)
    for h in range(n_heads):
        acc = jnp.where(lane == h, o[h * r:(h + 1) * r, :], acc)
    return acc


def _na_kernel(q_ref, k_ref, v_ref, kc_ref, vc_ref, *rest, n_rows):
    bias_refs, o_ref = rest[:-1], rest[-1]
    rows_per_step = len(bias_refs)
    step = pl.program_id(1)

    def attend(g, j, local):
        r = step * rows_per_step + j
        rows = slice(j * GRID_W, (j + 1) * GRID_W)
        cols = slice(g * MXU_DIM, (g + 1) * MXU_DIM)
        qs = _stack_heads(q_ref[0, rows, cols], NA_HEAD_DIM)
        kc = kc_ref[0, :, cols]
        vc = vc_ref[0, :, cols]
        s_ctx = _dot_nt(qs, kc)
        m = jnp.max(s_ctx, axis=-1, keepdims=True)
        if local:
            start = pl.multiple_of(jnp.clip(r - NA_WIN_ROWS // 2, 0, n_rows - NA_WIN_ROWS) * GRID_W, GRID_W)
            kw = k_ref[0, pl.ds(start, NA_LOCAL_KEYS), cols]
            vw = v_ref[0, pl.ds(start, NA_LOCAL_KEYS), cols]
            s_loc = _dot_nt(qs, kw) + bias_refs[j][0, g]
            m = jnp.maximum(m, jnp.max(s_loc, axis=-1, keepdims=True))
            p_loc = jnp.exp(s_loc - m)
        p_ctx = jnp.exp(s_ctx - m)
        den = jnp.sum(p_ctx, axis=-1, keepdims=True)
        o = _dot(p_ctx.astype(BF16), vc)
        if local:
            den = den + jnp.sum(p_loc, axis=-1, keepdims=True)
            o = o + _dot(p_loc.astype(BF16), vw)
        o_ref[0, rows, cols] = _unstack_heads(o / den, NA_HEAD_DIM).astype(o_ref.dtype)

    @pl.when(step * rows_per_step < n_rows)
    def _():
        for j in range(rows_per_step):
            for g in range(NA_GROUPS):
                attend(g, j, True)

    @pl.when(step * rows_per_step >= n_rows)
    def _():
        for j in range(rows_per_step):
            for g in range(NA_GROUPS):
                attend(g, j, False)


def _na_bias_table(rpb):
    o = jnp.arange(NA_WIN_ROWS)
    dr = o[None, :] - o[:, None] + NA_WIN_ROWS - 1
    col = jnp.arange(GRID_W)
    dc = jnp.clip(col[None, :] - col[:, None], -(NA_WIN_COLS - 1), NA_WIN_COLS - 1) + NA_WIN_COLS - 1
    col_start = jnp.clip(col - NA_WIN_COLS // 2, 0, GRID_W - NA_WIN_COLS)
    col_mask = (col[None, :] >= col_start[:, None]) & (col[None, :] < col_start[:, None] + NA_WIN_COLS)
    onehot = (dc[None] == jnp.arange(2 * NA_WIN_COLS - 1)[:, None, None]).astype(F32)
    t = jnp.einsum('howc,cqk->howqk', rpb[:, dr].astype(F32), onehot,
                   precision=lax.Precision.HIGHEST)
    t = jnp.where(col_mask[None, None, None], t, MASK_VALUE)
    t = t.transpose(1, 0, 3, 2, 4)
    return t.reshape(NA_WIN_ROWS, NA_GROUPS, HEADS_PER_GROUP * GRID_W, NA_LOCAL_KEYS)


def _neighborhood_attention(qkv, bias_tab, n_lat_tiles, with_ctx_queries):
    bsz, s, _ = qkv.shape
    seq = n_lat_tiles * TM
    n_rows = seq // GRID_W
    assert n_rows >= NA_WIN_ROWS
    n_ctx_rows = (s - seq) // GRID_W
    rq = NA_ROWS_PER_STEP
    assert n_rows % rq == 0 and n_ctx_rows % rq == 0
    nq = (n_rows + (n_ctx_rows if with_ctx_queries else 0)) // rq
    ctx_blk = seq // (s - seq)

    def bias_spec(j):
        def idx(b, i):
            rr = jnp.minimum(i * rq + j, n_rows - 1)
            return (rr - jnp.clip(rr - NA_WIN_ROWS // 2, 0, n_rows - NA_WIN_ROWS), 0, 0, 0)
        return pl.BlockSpec((1,) + bias_tab.shape[1:], idx)

    return pl.pallas_call(
        functools.partial(_na_kernel, n_rows=n_rows),
        grid=(bsz, nq),
        in_specs=[
            pl.BlockSpec((1, rq * GRID_W, NA_WIDTH), lambda b, i: (b, i, 0)),
            pl.BlockSpec((1, seq, NA_WIDTH), lambda b, i: (b, 0, 1)),
            pl.BlockSpec((1, seq, NA_WIDTH), lambda b, i: (b, 0, 2)),
            pl.BlockSpec((1, s - seq, NA_WIDTH), lambda b, i: (b, ctx_blk, 1)),
            pl.BlockSpec((1, s - seq, NA_WIDTH), lambda b, i: (b, ctx_blk, 2)),
        ] + [bias_spec(j) for j in range(rq)],
        out_specs=pl.BlockSpec((1, rq * GRID_W, NA_WIDTH), lambda b, i: (b, i, 0)),
        out_shape=jax.ShapeDtypeStruct((bsz, nq * rq * GRID_W, NA_WIDTH), BF16),
        compiler_params=_params("parallel", "arbitrary"),
        name="neighborhood_attention",
    )(qkv, qkv, qkv, qkv, qkv, *([bias_tab] * rq))


def _pool_kernel(up_ref, u_ref, un_ref, w_ref, sc_ref, o_ref, buf, s0, s1, s2, *, n_lat_tiles, seq, ctx_len):
    i = pl.program_id(1)
    hw = POOL_REACH
    n = TM + 2 * hw
    is_ctx = i >= n_lat_tiles
    base = jnp.where(is_ctx, 0, i * TM)
    lseq = jnp.where(is_ctx, ctx_len, seq)
    buf[0:hw] = up_ref[0]
    buf[hw:hw + TM] = u_ref[0]
    buf[hw + TM:n] = un_ref[0]
    p = base - hw + lax.broadcasted_iota(jnp.int32, (n, POOL_WIDTH), 0)
    buf[...] = jnp.where((p >= 0) & (p < lseq), buf[...], 0.0)
    s0[0:n - 1] = buf[0:n - 1] + buf[1:n]
    s1[0:n - 3] = s0[0:n - 3] + s0[2:n - 1]
    s2[0:n - 7] = s1[0:n - 7] + s1[4:n - 3]
    shape = (TM, POOL_WIDTH)
    t = base + lax.broadcasted_iota(jnp.int32, shape, 0)
    grp = lax.broadcasted_iota(jnp.int32, shape, 1) // POOL_GROUP_DIM
    acc = jnp.where(grp == 0, s0[hw - 1:hw - 1 + TM],
                    jnp.where(grp == 1, s1[hw - 2:hw - 2 + TM],
                              jnp.where(grp == 2, s2[hw - 4:hw - 4 + TM], s2[0:TM] + s2[hw:hw + TM])))
    win = jnp.left_shift(2, grp)
    back = win // 2
    fwd = win - back - 1
    lo = jnp.clip(t - back, 0, lseq - 1)
    hi = jnp.clip(t + fwd, 0, lseq - 1)
    mean = acc / (hi - lo + 1).astype(F32)
    diff = (mean - u_ref[0]).astype(BF16)
    o_ref[0] = (_dot(diff, w_ref[...]) * sc_ref[...]).astype(o_ref.dtype)


def _multiscale_pool(u, w_bd, scale, n_lat_tiles, n_tiles):
    bsz, s, c = u.shape
    hw = POOL_REACH
    per = TM // hw
    last = s // hw - 1
    return pl.pallas_call(
        functools.partial(_pool_kernel, n_lat_tiles=n_lat_tiles, seq=n_lat_tiles * TM, ctx_len=s - n_lat_tiles * TM),
        grid=(bsz, n_tiles),
        in_specs=[
            pl.BlockSpec((1, hw, c), lambda b, i: (b, jnp.maximum(i * per - 1, 0), 0)),
            pl.BlockSpec((1, TM, c), lambda b, i: (b, i, 0)),
            pl.BlockSpec((1, hw, c), lambda b, i: (b, jnp.minimum((i + 1) * per, last), 0)),
            pl.BlockSpec((c, c), lambda b, i: (0, 0)),
            pl.BlockSpec((1, c), lambda b, i: (0, 0)),
        ],
        out_specs=pl.BlockSpec((1, TM, c), lambda b, i: (b, i, 0)),
        out_shape=jax.ShapeDtypeStruct((bsz, n_tiles * TM, c), BF16),
        scratch_shapes=[pltpu.VMEM((TM + 2 * hw, c), F32)] * 4,
        compiler_params=_params("parallel", "parallel"),
        name="multiscale_pool",
    )(u, u, u, w_bd, scale)


def _gla_tile(qk_ref, v_ref, la_ref, cs_ref, sn_ref, st_ref, o_ref, reverse):
    ch = GLA_CHUNK
    lane = lax.broadcasted_iota(jnp.int32, (ch, GLA_QK), 1)
    row = lax.broadcasted_iota(jnp.int32, (ch, GLA_QK), 0)
    rr = lax.broadcasted_iota(jnp.int32, (GLA_HEADS * ch, ch), 0) % ch
    cc = lax.broadcasted_iota(jnp.int32, (GLA_HEADS * ch, ch), 1)
    causal = (rr <= cc) if reverse else (rr >= cc)
    st_shape = (GLA_WIDTH, GLA_QK)
    head_blk = (lax.broadcasted_iota(jnp.int32, st_shape, 0) // GLA_DV
                == lax.broadcasted_iota(jnp.int32, st_shape, 1) // GLA_DK)
    quarter = GLA_DK // 4
    first_half = (lane % (2 * quarter)) < quarter

    def rope(x, cs, sn):
        partner = jnp.where(first_half, pltpu.roll(x, GLA_QK - quarter, 1), pltpu.roll(x, quarter, 1))
        return x * cs + partner * sn

    chunks = range(TM // ch)
    for c in (reversed(chunks) if reverse else chunks):
        sl = slice(c * ch, (c + 1) * ch)
        cs = cs_ref[sl, :]
        sn = sn_ref[sl, :]
        q = rope(qk_ref[0, sl, 0:GLA_QK], cs, sn)
        k = rope(qk_ref[0, sl, GLA_QK:2 * GLA_QK], cs, sn)
        b = la_ref[0, sl, :]
        d = 1
        while d < ch:
            if reverse:
                b = b + jnp.where(row < ch - d, pltpu.roll(b, ch - d, 0), 0.0)
            else:
                b = b + jnp.where(row >= d, pltpu.roll(b, d, 0), 0.0)
            d *= 2
        b_last = b[0:1, :] if reverse else b[ch - 1:ch, :]
        q_in = q * jnp.exp(b)
        k_in = (k * jnp.exp(-b)).astype(BF16)
        k_st = (k * jnp.exp(b_last - b)).astype(BF16)
        att = _dot_nt(_stack_heads(q_in, GLA_DK).astype(BF16), k_in)
        att = jnp.where(causal, att, 0.0).astype(BF16)
        vb = v_ref[0, sl, :].astype(BF16)
        o = _unstack_heads(_dot(att, vb), GLA_DV)
        st = st_ref[...]
        o_ref[0, sl, :] = o + _dot_nt(q_in.astype(BF16), st.astype(BF16))
        st_ref[...] = st * jnp.exp(b_last) + jnp.where(head_blk, _dot_tn(vb, k_st), 0.0)


def _gla_kernel(qkf, vf, laf, csf, snf, qkb, vb, lab, csb, snb, of_ref, ob_ref, st_f, st_b):
    @pl.when(pl.program_id(1) == 0)
    def _():
        st_f[...] = jnp.zeros_like(st_f)
        st_b[...] = jnp.zeros_like(st_b)

    _gla_tile(qkf, vf, laf, csf, snf, st_f, of_ref, False)
    _gla_tile(qkb, vb, lab, csb, snb, st_b, ob_ref, True)


def _rope_tables(seq, s):
    quarter = GLA_DK // 4
    t = np.arange(s)
    pos_row = np.where(t < seq, t // GRID_W, 0).astype(np.float32)
    pos_col = np.where(t < seq, t % GRID_W, 0).astype(np.float32)
    d = np.arange(GLA_QK) % GLA_DK
    inv_freq = ROPE_BASE ** (-jnp.arange(quarter, dtype=F32) / quarter)
    freq = inv_freq[d % quarter]
    pos = jnp.where((d < GLA_DK // 2)[None, :], pos_row[:, None], pos_col[:, None])
    ang = pos * freq[None, :]
    sign = np.where((d % (2 * quarter)) < quarter, -1.0, 1.0).astype(np.float32)
    return jnp.cos(ang), jnp.sin(ang) * sign[None, :]


def _gla(gg, cos, sin, n_lat_tiles):
    bsz, s, _ = gg.shape
    nt = s // TM
    assert nt == n_lat_tiles + 1
    fwd = lambda i: (i + n_lat_tiles) % nt
    bwd = lambda i: jnp.where(i == 0, n_lat_tiles, n_lat_tiles - i)
    la_f_blk = (2 * GLA_QK + 2 * GLA_WIDTH) // GLA_QK
    la_b_blk = la_f_blk + 1
    v_blk = 2 * GLA_QK // GLA_WIDTH

    def specs(order, la_blk):
        return [
            pl.BlockSpec((1, TM, 2 * GLA_QK), lambda b, i: (b, order(i), 0)),
            pl.BlockSpec((1, TM, GLA_WIDTH), lambda b, i: (b, order(i), v_blk)),
            pl.BlockSpec((1, TM, GLA_QK), lambda b, i: (b, order(i), la_blk)),
            pl.BlockSpec((TM, GLA_QK), lambda b, i: (order(i), 0)),
            pl.BlockSpec((TM, GLA_QK), lambda b, i: (order(i), 0)),
        ]

    return pl.pallas_call(
        _gla_kernel,
        grid=(bsz, nt),
        in_specs=specs(fwd, la_f_blk) + specs(bwd, la_b_blk),
        out_specs=[
            pl.BlockSpec((1, TM, GLA_WIDTH), lambda b, i: (b, fwd(i), 0)),
            pl.BlockSpec((1, TM, GLA_WIDTH), lambda b, i: (b, bwd(i), 0)),
        ],
        out_shape=[jax.ShapeDtypeStruct((bsz, s, GLA_WIDTH), F32)] * 2,
        scratch_shapes=[pltpu.VMEM((GLA_WIDTH, GLA_QK), F32)] * 2,
        compiler_params=_params("arbitrary", "arbitrary"),
        name="gla_bidir",
    )(gg, gg, gg, cos, sin, gg, gg, gg, cos, sin)


def _store_row_tiles(ref, val):
    rows, width = val.shape
    pieces = width // LANES
    for s in range(pieces):
        ref[pl.ds(s, rows, stride=pieces), :] = val[:, s * LANES:(s + 1) * LANES]


def _load_row_tiles(ref, rows, pieces):
    return jnp.concatenate([ref[pl.ds(s, rows, stride=pieces), :] for s in range(pieces)], axis=1)


SLOTS_PER_Y_TILE = 2


def _pack_bf16_pairs(y):
    half = y.shape[1] // 2
    bits = lax.bitcast_convert_type(y.astype(jnp.bfloat16).astype(F32), jnp.uint32)
    return (bits[:, :half] >> 16) | (bits[:, half:] & jnp.uint32(0xFFFF0000))


def _unpack_bf16_pairs(words):
    lo = lax.bitcast_convert_type(words << 16, F32)
    hi = lax.bitcast_convert_type(words & jnp.uint32(0xFFFF0000), F32)
    return jnp.concatenate([lo, hi], axis=1)


def _row_index(shape):
    return lax.broadcasted_iota(jnp.int32, shape, 0).astype(F32)


def _stack_rows(rows):
    shape = (len(rows), rows[0].shape[1])
    rid = lax.broadcasted_iota(jnp.int32, shape, 0)
    out = jnp.zeros(shape, rows[0].dtype)
    for k, r in enumerate(rows):
        out = jnp.where(rid == k, r, out)
    return out


def _select_experts(scores, biased):
    n_exp, n_tok = biased.shape
    per = n_exp // N_GROUPS
    neg = -jnp.inf
    sub = _row_index((per, n_tok))
    grp_rows = []
    for g in range(N_GROUPS):
        blk = biased[g * per:(g + 1) * per, :]
        m1 = jnp.max(blk, axis=0, keepdims=True)
        i1 = jnp.min(jnp.where(blk == m1, sub, float(per)), axis=0, keepdims=True)
        m2 = jnp.max(jnp.where(sub == i1, neg, blk), axis=0, keepdims=True)
        grp_rows.append(m1 + m2)
    cur = _stack_rows(grp_rows)
    gid = _row_index(cur.shape)
    picked = jnp.zeros(cur.shape, F32)
    for _ in range(TOPK_GROUPS):
        gm = jnp.max(cur, axis=0, keepdims=True)
        gi = jnp.min(jnp.where(cur == gm, gid, float(N_GROUPS)), axis=0, keepdims=True)
        hit = gid == gi
        picked = jnp.where(hit, 1.0, picked)
        cur = jnp.where(hit, neg, cur)
    cur = jnp.concatenate(
        [jnp.where(picked[g:g + 1, :] > 0.0, biased[g * per:(g + 1) * per, :], neg) for g in range(N_GROUPS)], axis=0)
    eid = _row_index(cur.shape)
    ids, vals = [], []
    chosen = jnp.zeros(cur.shape, F32)
    for _ in range(TOP_K):
        m = jnp.max(cur, axis=0, keepdims=True)
        idx = jnp.min(jnp.where(cur == m, eid, float(n_exp)), axis=0, keepdims=True)
        hit = eid == idx
        ids.append(idx)
        vals.append(jnp.sum(jnp.where(hit, scores, 0.0), axis=0, keepdims=True))
        chosen = jnp.where(hit, 1.0, chosen)
        cur = jnp.where(hit, neg, cur)
    return ids, vals, chosen


def _outproj_kernel(x_ref, xc_ref, oa_ref, ob_ref, of_ref, obk_ref, r_ref, gg_ref, wo_ref, mod_ref, gf_ref, wr_ref, br_ref,
                    tri_ref, xo_ref, h_ref, e_ref, w_ref, rk_ref, hist_ref, *, n_lat_tiles):
    o = of_ref[0] + obk_ref[0]
    head = lax.broadcasted_iota(jnp.int32, o.shape, 1) // GLA_DV
    o2 = o * o
    rs = jnp.zeros_like(o)
    for h in range(GLA_HEADS):
        ssq = jnp.sum(jnp.where(head == h, o2, 0.0), axis=-1, keepdims=True)
        rs = jnp.where(head == h, lax.rsqrt(ssq / GLA_DV + NORM_EPS), rs)
    oc = (o * rs * gg_ref[...] * _silu(r_ref[0])).astype(BF16)
    c1 = NA_WIDTH + POOL_WIDTH
    acc = _dot(oa_ref[0], wo_ref[0:NA_WIDTH]) + _dot(ob_ref[0], wo_ref[NA_WIDTH:c1]) + _dot(oc, wo_ref[c1:])
    x = _stream_tile(x_ref, xc_ref, n_lat_tiles) + mod_ref[0, 0, 2:3, :] * acc
    xo_ref[0] = x
    h = _rmsnorm(x, gf_ref[...]) * (1.0 + mod_ref[0, 0, 4:5, :]) + mod_ref[0, 0, 3:4, :]
    _store_row_tiles(h_ref, h)

    scores = _sigmoid(_dot_nt(wr_ref[...], h.astype(BF16)))
    ids, vals, chosen = _select_experts(scores, scores + br_ref[...])
    total = vals[0]
    for v in vals[1:]:
        total = total + v
    w_ref[...] = _stack_rows([v / total * ROUTED_SCALE for v in vals])
    e_ref[...] = _stack_rows(ids).astype(jnp.int32)
    chosen_b = chosen.astype(BF16)
    before = _dot(chosen_b, tri_ref[...])
    eid = _row_index(before.shape)
    rk_ref[...] = _stack_rows([jnp.sum(jnp.where(eid == idx, before, 0.0), axis=0, keepdims=True) for idx in ids])
    hist_ref[0] = _dot_nt(jnp.ones((SUBLANES, chosen.shape[1]), BF16), chosen_b)


def _outproj(x_lat, x_ctx, ctx_blk, oa, ob, o_f, o_b, gg, g_gla4, w_out, mod, g_ffn, w_router_t, b_router,
             n_lat_tiles, n_tiles):
    bsz, _, d = x_lat.shape
    n_exp = w_router_t.shape[0]
    r_blk = (2 * GLA_QK + GLA_WIDTH) // GLA_WIDTH
    tile = lambda w: pl.BlockSpec((1, TM, w), lambda b, i: (b, i, 0))
    const = lambda b, i: (0, 0)
    per_tok = pl.BlockSpec((TOP_K, TM), lambda b, i: (0, b * n_tiles + i))
    t_tok = bsz * n_tiles * TM
    tri = (jnp.arange(TM)[:, None] < jnp.arange(TM)[None, :]).astype(BF16)
    return pl.pallas_call(
        functools.partial(_outproj_kernel, n_lat_tiles=n_lat_tiles),
        grid=(bsz, n_tiles),
        in_specs=[
            *_stream_specs(d, n_lat_tiles, ctx_blk),
            tile(NA_WIDTH), tile(POOL_WIDTH), tile(GLA_WIDTH), tile(GLA_WIDTH),
            pl.BlockSpec((1, TM, GLA_WIDTH), lambda b, i: (b, i, r_blk)),
            pl.BlockSpec((1, GLA_WIDTH), const),
            pl.BlockSpec(w_out.shape, const),
            pl.BlockSpec((1, 1, SUBLANES, d), lambda b, i: (b, jnp.where(i >= n_lat_tiles, 0, 1), 0, 0)),
            pl.BlockSpec((1, d), const),
            pl.BlockSpec(w_router_t.shape, const),
            pl.BlockSpec((n_exp, 1), const),
            pl.BlockSpec((TM, TM), const),
        ],
        out_specs=[tile(d), pl.BlockSpec((TM * d // LANES, LANES), lambda b, i: (b * n_tiles + i, 0)),
                   per_tok, per_tok, per_tok,
                   pl.BlockSpec((1, SUBLANES, n_exp), lambda b, i: (b * n_tiles + i, 0, 0))],
        out_shape=[
            jax.ShapeDtypeStruct((bsz, n_tiles * TM, d), F32),
            jax.ShapeDtypeStruct((t_tok * d // LANES, LANES), F32),
            jax.ShapeDtypeStruct((TOP_K, t_tok), jnp.int32),
            jax.ShapeDtypeStruct((TOP_K, t_tok), F32),
            jax.ShapeDtypeStruct((TOP_K, t_tok), F32),
            jax.ShapeDtypeStruct((bsz * n_tiles, SUBLANES, n_exp), F32),
        ],
        compiler_params=_params("parallel", "parallel"),
        name="outproj_router",
    )(x_lat, x_ctx, oa, ob, o_f, o_b, gg, g_gla4, w_out, mod, g_ffn, w_router_t, b_router, tri)


def _slot_layout(hist):
    n_tiles, n_exp = hist.shape
    counts = hist.sum(0)
    tile_base = jnp.cumsum(hist, axis=0) - hist
    padded = (counts + MOE_BLOCK - 1) // MOE_BLOCK * MOE_BLOCK
    pad_end = jnp.cumsum(padded)
    pad_start = pad_end - padded
    base = (pad_start[None, :] + tile_base).astype(F32).reshape(n_tiles, n_exp, 1)
    n_blocks = -(-(n_tiles * TM * TOP_K) // MOE_BLOCK) + n_exp
    n_used = (pad_end[-1] // MOE_BLOCK).astype(jnp.int32)
    fill = jnp.concatenate([jnp.maximum(pad_end // MOE_BLOCK - 1, 0).astype(jnp.int32),
                            jnp.minimum(n_used + jnp.arange(n_exp, dtype=jnp.int32), n_blocks - 1)])
    first_blk = (pad_start // MOE_BLOCK).astype(jnp.int32)
    blk_count = (padded // MOE_BLOCK).astype(jnp.int32)
    return base, first_blk, blk_count, fill, n_used.reshape(1), n_blocks


def _pos_kernel(e_ref, rk_ref, base_ref, posx_ref, posy_ref, half_ref, *, pieces):
    eid = lax.broadcasted_iota(jnp.int32, (base_ref.shape[1], TM), 0)
    for j in range(base_ref.shape[0]):
        cols = slice(j * TM, (j + 1) * TM)
        base = base_ref[j]
        rows = [jnp.sum(jnp.where(eid == e_ref[k:k + 1, cols], base, 0.0), axis=0, keepdims=True)
                for k in range(TOP_K)]
        slot = (_stack_rows(rows) + rk_ref[:, cols]).astype(jnp.int32)
        posx_ref[j] = slot * pieces
        posy_ref[j] = (slot // SLOTS_PER_Y_TILE) * pieces
        half_ref[:, cols] = slot % SLOTS_PER_Y_TILE


def _slot_positions(e_t, rank_t, base, pieces):
    n_tiles, n_exp, _ = base.shape
    per_step = max(p for p in range(1, DISPATCH_TILES + 1) if n_tiles % p == 0)
    per_tok = pl.BlockSpec((TOP_K, per_step * TM), lambda i: (0, i))
    flat = pl.BlockSpec((per_step, TOP_K, TM), lambda i: (i, 0, 0))
    pos_x, pos_y, half = pl.pallas_call(
        functools.partial(_pos_kernel, pieces=pieces),
        grid=(n_tiles // per_step,),
        in_specs=[per_tok, per_tok, pl.BlockSpec((per_step, n_exp, 1), lambda i: (i, 0, 0))],
        out_specs=[flat, flat, per_tok],
        out_shape=[jax.ShapeDtypeStruct((n_tiles, TOP_K, TM), jnp.int32)] * 2
        + [jax.ShapeDtypeStruct(e_t.shape, jnp.int32)],
        compiler_params=_params("parallel"),
        name="moe_positions",
    )(e_t, rank_t, base)
    return pos_x.reshape(-1), pos_y.reshape(-1), half


def _dispatch_kernel(fill_ref, pos_ref, h_ref, xs_hbm, zbuf, sem, zsem):
    @pl.when(pl.program_id(0) == 0)
    def _():
        zbuf[...] = jnp.zeros_like(zbuf)
        n_fill = fill_ref.shape[0]

        def is_new(j):
            return (j == 0) | (fill_ref[j] != fill_ref[jnp.maximum(j - 1, 0)])

        def start(j, c):
            @pl.when(is_new(j))
            def _():
                row0 = pl.multiple_of(fill_ref[j] * zbuf.shape[0], zbuf.shape[0])
                pltpu.make_async_copy(zbuf, xs_hbm.at[pl.ds(row0, zbuf.shape[0]), :], zsem).start()
            return c
        lax.fori_loop(0, n_fill, start, 0)

        def wait(j, c):
            @pl.when(is_new(j))
            def _():
                pltpu.make_async_copy(zbuf, xs_hbm.at[pl.ds(0, zbuf.shape[0]), :], zsem).wait()
            return c
        lax.fori_loop(0, n_fill, wait, 0)

    tiles = pos_ref.shape[0] // (TOP_K * TM)
    pieces = h_ref.shape[0] // (tiles * TM)

    for j in range(tiles):
        def body(t, c):
            src = h_ref.at[pl.ds(pl.multiple_of((j * TM + t) * pieces, pieces), pieces), :]
            for k in range(TOP_K):
                dst = xs_hbm.at[pl.ds(pl.multiple_of(pos_ref[(j * TOP_K + k) * TM + t], pieces), pieces), :]
                pltpu.make_async_copy(src, dst, sem).start(priority=k % 2)
            return c
        lax.fori_loop(0, TM, body, 0, unroll=2)
    for _ in range(TOP_K):
        pltpu.make_async_copy(h_ref, xs_hbm.at[pl.ds(0, h_ref.shape[0]), :], sem).wait()


def _dispatch(h2t, pos, fill, n_slots, pieces):
    n_tiles = h2t.shape[0] // (TM * pieces)
    per_step = max(p for p in range(1, DISPATCH_TILES + 1) if n_tiles % p == 0)
    grid_spec = pltpu.PrefetchScalarGridSpec(
        num_scalar_prefetch=1,
        grid=(n_tiles // per_step,),
        in_specs=[
            pl.BlockSpec((per_step * TOP_K * TM,), lambda i, fill: (i,), memory_space=pltpu.SMEM),
            pl.BlockSpec((per_step * TM * pieces, LANES), lambda i, fill: (i, 0)),
        ],
        out_specs=pl.BlockSpec(memory_space=pl.ANY),
        scratch_shapes=[pltpu.VMEM((MOE_BLOCK * pieces, LANES), F32), pltpu.SemaphoreType.DMA,
                        pltpu.SemaphoreType.DMA],
    )
    return pl.pallas_call(
        _dispatch_kernel,
        grid_spec=grid_spec,
        out_shape=jax.ShapeDtypeStruct((n_slots * pieces, LANES), F32),
        compiler_params=_params("arbitrary"),
        name="moe_dispatch",
    )(fill, pos, h2t)


def _expert_kernel(first_ref, cnt_ref, nu_ref, wi_ref, wd_ref, xs_hbm, ys_hbm, xbuf, ybuf, wi_bf, wd_bf, xsem, ysem,
                   *, n_blocks):
    e = pl.program_id(0)
    n_used = nu_ref[0]
    rows = xbuf.shape[1]
    pieces = rows // MOE_BLOCK
    d_exp = wd_ref.shape[2]

    def x_copy(g, s):
        return pltpu.make_async_copy(xs_hbm.at[pl.ds(pl.multiple_of(g * rows, rows), rows), :], xbuf.at[s], xsem.at[s])

    y_rows = ybuf.shape[1]

    def y_copy(g, s):
        return pltpu.make_async_copy(ybuf.at[s], ys_hbm.at[pl.ds(pl.multiple_of(g * y_rows, y_rows), y_rows), :],
                                     ysem.at[s])

    n_xbuf = xbuf.shape[0]
    n_ybuf = ybuf.shape[0]

    @pl.when(e == 0)
    def _():
        for g in range(n_xbuf - 1):
            @pl.when(g < n_used)
            def _():
                x_copy(g, g).start()

    @pl.when(cnt_ref[e] > 0)
    def _():
        wi_bf[...] = wi_ref[0, 0].astype(BF16)
        wd_bf[...] = wd_ref[0, 0].astype(BF16)

    def block(j, c):
        g = first_ref[e] + j
        s = g % n_ybuf
        x_copy(g, g % n_xbuf).wait()

        @pl.when(g + n_xbuf - 1 < n_used)
        def _():
            x_copy(g + n_xbuf - 1, (g + n_xbuf - 1) % n_xbuf).start()

        @pl.when(g >= n_ybuf)
        def _():
            y_copy(g - n_ybuf, s).wait()

        hh = _dot(_load_row_tiles(xbuf.at[g % n_xbuf], MOE_BLOCK, pieces).astype(BF16), wi_bf[...])
        a = (_silu(hh[:, :d_exp]) * hh[:, d_exp:]).astype(BF16)
        _store_row_tiles(ybuf.at[s], _pack_bf16_pairs(_dot(a, wd_bf[...])))
        y_copy(g, s).start(priority=1)
        return c
    lax.fori_loop(0, cnt_ref[e], block, 0)

    @pl.when(e == pl.num_programs(0) - 1)
    def _():
        for back in range(1, n_ybuf + 1):
            @pl.when(n_used >= back)
            def _():
                y_copy(n_used - back, (n_used - back) % n_ybuf).wait()

        ybuf[0] = jnp.zeros(ybuf.shape[1:], ybuf.dtype)

        def start(g, c):
            y_copy(g, 0).start()
            return c
        lax.fori_loop(n_used, n_blocks, start, 0)

        def wait(g, c):
            y_copy(g, 0).wait()
            return c
        lax.fori_loop(n_used, n_blocks, wait, 0)


def _moe_experts(xs, first_blk, blk_count, n_used, n_blocks, w_e_in, w_e_down, layer):
    _, n_exp, d, f2 = w_e_in.shape
    rows = xs.shape[0] // n_blocks
    grid_spec = pltpu.PrefetchScalarGridSpec(
        num_scalar_prefetch=3,
        grid=(n_exp,),
        in_specs=[
            pl.BlockSpec((1, 1, d, f2), lambda e, first, cnt, nu: (layer, e, 0, 0)),
            pl.BlockSpec((1, 1, f2 // 2, d), lambda e, first, cnt, nu: (layer, e, 0, 0)),
            pl.BlockSpec(memory_space=pl.ANY),
        ],
        out_specs=pl.BlockSpec(memory_space=pl.ANY),
        scratch_shapes=[
            pltpu.VMEM((EXPERT_X_BUFFERS, rows, LANES), F32),
            pltpu.VMEM((EXPERT_Y_BUFFERS, rows // SLOTS_PER_Y_TILE, LANES), jnp.uint32),
            pltpu.VMEM((d, f2), BF16),
            pltpu.VMEM((f2 // 2, d), BF16),
            pltpu.SemaphoreType.DMA((EXPERT_X_BUFFERS,)),
            pltpu.SemaphoreType.DMA((EXPERT_Y_BUFFERS,)),
        ],
    )
    return pl.pallas_call(
        functools.partial(_expert_kernel, n_blocks=n_blocks),
        grid_spec=grid_spec,
        out_shape=jax.ShapeDtypeStruct((xs.shape[0] // SLOTS_PER_Y_TILE, LANES), jnp.uint32),
        compiler_params=_params("arbitrary"),
        name="moe_experts",
    )(first_blk, blk_count, n_used, w_e_in, w_e_down, xs)


def _combine_kernel(pos0_ref, posn_ref, x_ref, h_ref, w_ref, half_ref, wsi_ref, wsd_ref, mod_ref, gfin_ref, ys_hbm,
                    o_ref, ybuf, sem, *, final, n_steps):
    step = pl.program_id(0) * pl.num_programs(1) + pl.program_id(1)
    slot = step % 2
    d_exp = wsd_ref.shape[0]
    pieces = h_ref.shape[0] // TMC

    def gather(pos_ref, st, s):
        off = (st % (TM // TMC)) * TMC
        def body(t, c):
            for k in range(TOP_K):
                src = ys_hbm.at[pl.ds(pl.multiple_of(pos_ref[k * TM + off + t], pieces), pieces), :]
                dst = ybuf.at[s, k, pl.ds(pl.multiple_of(t * pieces, pieces), pieces), :]
                pltpu.make_async_copy(src, dst, sem.at[s]).start(priority=k % 2)
            return c
        lax.fori_loop(0, TMC, body, 0, unroll=2)

    @pl.when(step == 0)
    def _():
        gather(pos0_ref, 0, 0)

    @pl.when(step + 1 < n_steps)
    def _():
        gather(posn_ref, step + 1, 1 - slot)

    hh = _dot(_load_row_tiles(h_ref, TMC, pieces).astype(BF16), wsi_ref[...])
    a = (_silu(hh[:, :d_exp]) * hh[:, d_exp:]).astype(BF16)
    y = _dot(a, wsd_ref[...])
    for k in range(TOP_K):
        pltpu.make_async_copy(ys_hbm.at[pl.ds(0, TMC * pieces), :], ybuf.at[slot, k], sem.at[slot]).wait()
    w = w_ref[...]
    half = half_ref[...]
    for k in range(TOP_K):
        both = _load_row_tiles(ybuf.at[slot, k], TMC, pieces)
        mine = jnp.where(half[:, k:k + 1] > 0, both[:, both.shape[1] // 2:], both[:, :both.shape[1] // 2])
        y = y + _unpack_bf16_pairs(mine) * w[:, k:k + 1]
    x = x_ref[0] + mod_ref[0, 0, 5:6, :] * y
    if final:
        x = _rmsnorm(x, gfin_ref[...])
    o_ref[0] = x


def _combine(x_mid, h2t, ys, pos, w_tok, half_tok, w_sh_in, w_sh_down, mod, g_final, n_lat_tiles, final):
    bsz, s, d = x_mid.shape
    nt = s // TMC
    n_steps = bsz * nt
    halves = TM // TMC
    lat_steps = n_lat_tiles * halves
    pieces = d // LANES
    const = lambda b, i: (0, 0)
    tile = pl.BlockSpec((1, TMC, d), lambda b, i: (b, i, 0))
    smem_blk = lambda f: pl.BlockSpec((TOP_K * TM,), f, memory_space=pltpu.SMEM)
    return pl.pallas_call(
        functools.partial(_combine_kernel, final=final, n_steps=n_steps),
        grid=(bsz, nt),
        in_specs=[
            smem_blk(lambda b, i: (0,)),
            smem_blk(lambda b, i: (jnp.minimum(b * nt + i + 1, n_steps - 1) // halves,)),
            tile,
            pl.BlockSpec((TMC * pieces, LANES), lambda b, i: (b * nt + i, 0)),
            pl.BlockSpec((TMC, TOP_K), lambda b, i: (b * nt + i, 0)),
            pl.BlockSpec((TMC, TOP_K), lambda b, i: (b * nt + i, 0)),
            pl.BlockSpec(w_sh_in.shape, const),
            pl.BlockSpec(w_sh_down.shape, const),
            pl.BlockSpec((1, 1, SUBLANES, d), lambda b, i: (b, jnp.where(i >= lat_steps, 0, 1), 0, 0)),
            pl.BlockSpec((1, d), const),
            pl.BlockSpec(memory_space=pl.ANY),
        ],
        out_specs=tile,
        out_shape=jax.ShapeDtypeStruct((bsz, s, d), F32),
        scratch_shapes=[pltpu.VMEM((2, TOP_K, TMC * pieces, LANES), jnp.uint32), pltpu.SemaphoreType.DMA((2,))],
        compiler_params=_params("arbitrary", "arbitrary"),
        name="moe_combine",
    )(pos, pos, x_mid, h2t, w_tok, half_tok, w_sh_in, w_sh_down, mod, g_final, ys)


def _block_diag(w):
    g, a, b = w.shape
    out = jnp.zeros((g * a, g * b), w.dtype)
    for j in range(g):
        out = out.at[j * a:(j + 1) * a, j * b:(j + 1) * b].set(w[j])
    return out


def kernel(x, c, ctx, c_ctx, w_mod, b_mod, g_mix, w_in, rpb, w_pool, pool_scale, w_gate_f, b_gate_f, w_gate_b, b_gate_b, g_gla, w_out, g_ffn, w_router, b_router, w_e_in, w_e_down, w_sh_in, w_sh_down, g_final):
    bsz, seq, d = x.shape
    n_ctx = ctx.shape[1]
    depth = w_mod.shape[0]
    n_exp = w_router.shape[2]
    assert n_ctx == TM and seq % TM == 0 and seq % GRID_W == 0
    n_lat = seq // TM
    s = seq + n_ctx

    n_rows = -(-(bsz + 1) // SUBLANES) * SUBLANES
    cc = jnp.zeros((n_rows, d), F32).at[:bsz].set(c).at[bsz].set(c_ctx)
    mod_all = _modulation(cc, w_mod, b_mod).reshape(depth, n_rows, 6, d)
    cos, sin = _rope_tables(seq, s)

    x_lat, x_ctx, ctx_blk = x, ctx, 0
    for layer in range(depth):
        last = layer == depth - 1
        n_tiles = n_lat if last else n_lat + 1
        m = mod_all[layer]
        mod = jnp.stack([jnp.broadcast_to(m[bsz], (bsz, 6, d)), m[:bsz]], axis=1)
        mod = jnp.pad(mod, ((0, 0), (0, 0), (0, SUBLANES - 6), (0, 0)))
        wl = w_in[layer]
        w_main = wl[:, :MAIN_W].astype(BF16)
        w_low = jnp.pad(wl[:, MAIN_W:], ((0, 0), (0, LANES - 2 * GLA_GATE_RANK))).astype(BF16)
        w2 = jnp.zeros((LANES, 2 * GLA_QK), F32)
        w2 = w2.at[:GLA_GATE_RANK, :GLA_QK].set(w_gate_f[layer])
        w2 = w2.at[GLA_GATE_RANK:2 * GLA_GATE_RANK, GLA_QK:].set(w_gate_b[layer]).astype(BF16)
        b2 = jnp.concatenate([b_gate_f[layer], b_gate_b[layer]])[None, :]
        qkv, u, gg = _inproj(x_lat, x_ctx, ctx_blk, mod, g_mix[layer][None, :], w_main, w_low, w2, b2, n_lat)

        oa = _neighborhood_attention(qkv, _na_bias_table(rpb[layer]), n_lat, not last)
        ob = _multiscale_pool(u, _block_diag(w_pool[layer]).astype(BF16), pool_scale[layer][None, :], n_lat, n_tiles)
        o_f, o_b = _gla(gg, cos, sin, n_lat)

        x_mid, h2t, e_t, w_t, rank_t, hist = _outproj(
            x_lat, x_ctx, ctx_blk, oa, ob, o_f, o_b, gg, jnp.tile(g_gla[layer], GLA_HEADS)[None, :], w_out[layer].astype(BF16), mod,
            g_ffn[layer][None, :], w_router[layer].T.astype(BF16), b_router[layer].reshape(n_exp, 1), n_lat, n_tiles)
        pieces = d // LANES
        base, first_blk, blk_count, fill, n_used, n_blocks = _slot_layout(hist[:, 0, :].astype(jnp.int32))
        pos_x, pos_y, half = _slot_positions(e_t, rank_t, base, pieces)
        xs = _dispatch(h2t, pos_x, fill, n_blocks * MOE_BLOCK, pieces)
        ys = _moe_experts(xs, first_blk, blk_count, n_used, n_blocks, w_e_in, w_e_down, layer)
        xa = _combine(x_mid, h2t, ys, pos_y, w_t.T, half.T, w_sh_in[layer].astype(BF16),
                      w_sh_down[layer].astype(BF16), mod, g_final[None, :], n_lat, last)
        x_lat, x_ctx, ctx_blk = xa, xa, n_lat
    return xa
```

```python
import functools

import jax
import jax.numpy as jnp
import numpy as np
from jax import lax
from jax.experimental import pallas as pl
from jax.experimental.pallas import tpu as pltpu

GRID_W = 64
NORM_EPS = 1e-6
NA_HEADS = 8
NA_HEAD_DIM = 64
NA_WIDTH = NA_HEADS * NA_HEAD_DIM
NA_WIN_ROWS = 8
NA_WIN_COLS = 16
POOL_WINDOWS = (2, 4, 8, 16)
POOL_GROUP_DIM = 64
POOL_WIDTH = len(POOL_WINDOWS) * POOL_GROUP_DIM
POOL_REACH = max(POOL_WINDOWS) // 2
assert POOL_WINDOWS == tuple(2 ** (g + 1) for g in range(len(POOL_WINDOWS)))
GLA_HEADS = 4
GLA_DK = 32
GLA_DV = 64
GLA_QK = GLA_HEADS * GLA_DK
GLA_WIDTH = GLA_HEADS * GLA_DV
GLA_GATE_RANK = 16
GLA_TAU = 16.0
GLA_CHUNK = 64
ROPE_BASE = 10000.0
N_EXPERTS = 256
TOP_K = 8
N_GROUPS = 8
TOPK_GROUPS = 4
ROUTED_SCALE = 2.5

TM = 256
TMC = 256
DISPATCH_TILES = 8
EXPERT_X_BUFFERS = 6
EXPERT_Y_BUFFERS = 4
MOE_BLOCK = 256
LANES = 128
SUBLANES = 8
MXU_DIM = 256
MASK_VALUE = -1e30
V7X_VMEM_BYTES = 64 * 1024 * 1024
VMEM_LIMIT = V7X_VMEM_BYTES * 3 // 4

QKV_W = 3 * NA_WIDTH
G_W = 2 * GLA_QK + 2 * GLA_WIDTH + 2 * GLA_QK
MAIN_W = QKV_W + POOL_WIDTH + 2 * GLA_QK + 2 * GLA_WIDTH

BF16 = jnp.bfloat16
F32 = jnp.float32


def _params(*sem):
    return pltpu.CompilerParams(dimension_semantics=sem, vmem_limit_bytes=VMEM_LIMIT)


def _sigmoid(x):
    return 1.0 / (1.0 + jnp.exp(-x))


def _silu(x):
    return x * _sigmoid(x)


def _rmsnorm(x, g):
    return x * lax.rsqrt(jnp.mean(x * x, axis=-1, keepdims=True) + NORM_EPS) * g


def _dot(a, b):
    return jnp.dot(a, b, preferred_element_type=F32)


def _dot_nt(a, b):
    return lax.dot_general(a, b, (((1,), (1,)), ((), ())), preferred_element_type=F32)


def _dot_tn(a, b):
    return lax.dot_general(a, b, (((0,), (0,)), ((), ())), preferred_element_type=F32)


def _mod_kernel(c_ref, w_ref, b_ref, o_ref):
    a = _silu(c_ref[...]).astype(BF16)
    o_ref[0] = _dot(a, w_ref[0].astype(BF16)) + b_ref[0]


def _modulation(cc, w_mod, b_mod):
    depth, d, n = w_mod.shape
    r = cc.shape[0]
    tn = d
    return pl.pallas_call(
        _mod_kernel,
        grid=(depth, n // tn),
        in_specs=[
            pl.BlockSpec((r, d), lambda l, j: (0, 0)),
            pl.BlockSpec((1, d, tn), lambda l, j: (l, 0, j)),
            pl.BlockSpec((1, 1, tn), lambda l, j: (l, 0, j)),
        ],
        out_specs=pl.BlockSpec((1, r, tn), lambda l, j: (l, 0, j)),
        out_shape=jax.ShapeDtypeStruct((depth, r, n), F32),
        compiler_params=_params("parallel", "parallel"),
        name="modulation",
    )(cc, w_mod, b_mod.reshape(depth, 1, n))


def _stream_tile(x_ref, xc_ref, n_lat_tiles):
    return jnp.where(pl.program_id(1) >= n_lat_tiles, xc_ref[0], x_ref[0])


def _stream_specs(d, n_lat_tiles, ctx_blk):
    return [pl.BlockSpec((1, TM, d), lambda b, i: (b, jnp.minimum(i, n_lat_tiles - 1), 0)),
            pl.BlockSpec((1, TM, d), lambda b, i: (b, ctx_blk, 0))]


def _inproj_kernel(x_ref, xc_ref, mod_ref, g_ref, wm_ref, wl_ref, w2_ref, b2_ref, qkv_ref, u_ref, gg_ref, *, n_lat_tiles):
    x = _stream_tile(x_ref, xc_ref, n_lat_tiles)
    h = _rmsnorm(x, g_ref[...]) * (1.0 + mod_ref[0, 0, 1:2, :]) + mod_ref[0, 0, 0:1, :]
    hb = h.astype(BF16)
    q = _dot(hb, wm_ref[:, 0:NA_WIDTH]) * (NA_HEAD_DIM ** -0.5)
    qkv_ref[0, :, 0:NA_WIDTH] = q.astype(BF16)
    qkv_ref[0, :, NA_WIDTH:2 * NA_WIDTH] = _dot(hb, wm_ref[:, NA_WIDTH:2 * NA_WIDTH]).astype(BF16)
    qkv_ref[0, :, 2 * NA_WIDTH:QKV_W] = _dot(hb, wm_ref[:, 2 * NA_WIDTH:QKV_W]).astype(BF16)
    u_ref[0] = _dot(hb, wm_ref[:, QKV_W:QKV_W + POOL_WIDTH])
    c0 = QKV_W + POOL_WIDTH
    gg_ref[0, :, 0:GLA_QK] = _dot(hb, wm_ref[:, c0:c0 + GLA_QK]) * (GLA_DK ** -0.5)
    gg_ref[0, :, GLA_QK:2 * GLA_QK + 2 * GLA_WIDTH] = _dot(hb, wm_ref[:, c0 + GLA_QK:MAIN_W])
    a_low = _dot(hb, wl_ref[...]).astype(BF16)
    lg = _dot(a_low, w2_ref[...]) + b2_ref[...]
    log_sig = jnp.minimum(lg, 0.0) - jnp.log1p(jnp.exp(-jnp.abs(lg)))
    gg_ref[0, :, 2 * GLA_QK + 2 * GLA_WIDTH:G_W] = log_sig / GLA_TAU


def _inproj(x_lat, x_ctx, ctx_blk, mod, g, w_main, w_low, w2, b2, n_lat_tiles):
    bsz, _, d = x_lat.shape
    nt = n_lat_tiles + 1
    s = nt * TM
    const = lambda b, i: (0, 0)
    return pl.pallas_call(
        functools.partial(_inproj_kernel, n_lat_tiles=n_lat_tiles),
        grid=(bsz, nt),
        in_specs=_stream_specs(d, n_lat_tiles, ctx_blk) + [
            pl.BlockSpec((1, 1, SUBLANES, d), lambda b, i: (b, jnp.where(i >= n_lat_tiles, 0, 1), 0, 0)),
            pl.BlockSpec((1, d), const),
            pl.BlockSpec(w_main.shape, const),
            pl.BlockSpec(w_low.shape, const),
            pl.BlockSpec(w2.shape, const),
            pl.BlockSpec(b2.shape, const),
        ],
        out_specs=[
            pl.BlockSpec((1, TM, QKV_W), lambda b, i: (b, i, 0)),
            pl.BlockSpec((1, TM, POOL_WIDTH), lambda b, i: (b, i, 0)),
            pl.BlockSpec((1, TM, G_W), lambda b, i: (b, i, 0)),
        ],
        out_shape=[
            jax.ShapeDtypeStruct((bsz, s, QKV_W), BF16),
            jax.ShapeDtypeStruct((bsz, s, POOL_WIDTH), F32),
            jax.ShapeDtypeStruct((bsz, s, G_W), F32),
        ],
        compiler_params=_params("parallel", "parallel"),
        name="inproj",
    )(x_lat, x_ctx, mod, g, w_main, w_low, w2, b2)


HEADS_PER_GROUP = MXU_DIM // NA_HEAD_DIM
NA_GROUPS = NA_HEADS // HEADS_PER_GROUP
NA_LOCAL_KEYS = NA_WIN_ROWS * GRID_W
NA_ROWS_PER_STEP = 4


def _stack_heads(x, width):
    lane = lax.broadcasted_iota(jnp.int32, x.shape, 1) // width
    n_heads = x.shape[1] // width
    return jnp.concatenate([jnp.where(lane == h, x, jnp.zeros_like(x)) for h in range(n_heads)], axis=0)


def _unstack_heads(o, width):
    n_heads = o.shape[1] // width
    r = o.shape[0] // n_heads
    lane = lax.broadcasted_iota(jnp.int32, (r, o.shape[1]), 1) // width
    acc = jnp.zeros((r, o.shape[1]), o.dtype)
    for h in range(n_heads):
        acc = jnp.where(lane == h, o[h * r:(h + 1) * r, :], acc)
    return acc


def _na_kernel(q_ref, k_ref, v_ref, kc_ref, vc_ref, *rest, n_rows):
    bias_refs, o_ref = rest[:-1], rest[-1]
    rows_per_step = len(bias_refs)
    step = pl.program_id(1)

    def attend(g, j, local):
        r = step * rows_per_step + j
        rows = slice(j * GRID_W, (j + 1) * GRID_W)
        cols = slice(g * MXU_DIM, (g + 1) * MXU_DIM)
        qs = _stack_heads(q_ref[0, rows, cols], NA_HEAD_DIM)
        kc = kc_ref[0, :, cols]
        vc = vc_ref[0, :, cols]
        s_ctx = _dot_nt(qs, kc)
        m = jnp.max(s_ctx, axis=-1, keepdims=True)
        if local:
            start = pl.multiple_of(jnp.clip(r - NA_WIN_ROWS // 2, 0, n_rows - NA_WIN_ROWS) * GRID_W, GRID_W)
            kw = k_ref[0, pl.ds(start, NA_LOCAL_KEYS), cols]
            vw = v_ref[0, pl.ds(start, NA_LOCAL_KEYS), cols]
            s_loc = _dot_nt(qs, kw) + bias_refs[j][0, g]
            m = jnp.maximum(m, jnp.max(s_loc, axis=-1, keepdims=True))
            p_loc = jnp.exp(s_loc - m)
        p_ctx = jnp.exp(s_ctx - m)
        den = jnp.sum(p_ctx, axis=-1, keepdims=True)
        o = _dot(p_ctx.astype(BF16), vc)
        if local:
            den = den + jnp.sum(p_loc, axis=-1, keepdims=True)
            o = o + _dot(p_loc.astype(BF16), vw)
        o_ref[0, rows, cols] = _unstack_heads(o / den, NA_HEAD_DIM).astype(o_ref.dtype)

    @pl.when(step * rows_per_step < n_rows)
    def _():
        for j in range(rows_per_step):
            for g in range(NA_GROUPS):
                attend(g, j, True)

    @pl.when(step * rows_per_step >= n_rows)
    def _():
        for j in range(rows_per_step):
            for g in range(NA_GROUPS):
                attend(g, j, False)


def _na_bias_table(rpb):
    o = jnp.arange(NA_WIN_ROWS)
    dr = o[None, :] - o[:, None] + NA_WIN_ROWS - 1
    col = jnp.arange(GRID_W)
    dc = jnp.clip(col[None, :] - col[:, None], -(NA_WIN_COLS - 1), NA_WIN_COLS - 1) + NA_WIN_COLS - 1
    col_start = jnp.clip(col - NA_WIN_COLS // 2, 0, GRID_W - NA_WIN_COLS)
    col_mask = (col[None, :] >= col_start[:, None]) & (col[None, :] < col_start[:, None] + NA_WIN_COLS)
    onehot = (dc[None] == jnp.arange(2 * NA_WIN_COLS - 1)[:, None, None]).astype(F32)
    t = jnp.einsum('howc,cqk->howqk', rpb[:, dr].astype(F32), onehot,
                   precision=lax.Precision.HIGHEST)
    t = jnp.where(col_mask[None, None, None], t, MASK_VALUE)
    t = t.transpose(1, 0, 3, 2, 4)
    return t.reshape(NA_WIN_ROWS, NA_GROUPS, HEADS_PER_GROUP * GRID_W, NA_LOCAL_KEYS)


def _neighborhood_attention(qkv, bias_tab, n_lat_tiles, with_ctx_queries):
    bsz, s, _ = qkv.shape
    seq = n_lat_tiles * TM
    n_rows = seq // GRID_W
    assert n_rows >= NA_WIN_ROWS
    n_ctx_rows = (s - seq) // GRID_W
    rq = NA_ROWS_PER_STEP
    assert n_rows % rq == 0 and n_ctx_rows % rq == 0
    nq = (n_rows + (n_ctx_rows if with_ctx_queries else 0)) // rq
    ctx_blk = seq // (s - seq)

    def bias_spec(j):
        def idx(b, i):
            rr = jnp.minimum(i * rq + j, n_rows - 1)
            return (rr - jnp.clip(rr - NA_WIN_ROWS // 2, 0, n_rows - NA_WIN_ROWS), 0, 0, 0)
        return pl.BlockSpec((1,) + bias_tab.shape[1:], idx)

    return pl.pallas_call(
        functools.partial(_na_kernel, n_rows=n_rows),
        grid=(bsz, nq),
        in_specs=[
            pl.BlockSpec((1, rq * GRID_W, NA_WIDTH), lambda b, i: (b, i, 0)),
            pl.BlockSpec((1, seq, NA_WIDTH), lambda b, i: (b, 0, 1)),
            pl.BlockSpec((1, seq, NA_WIDTH), lambda b, i: (b, 0, 2)),
            pl.BlockSpec((1, s - seq, NA_WIDTH), lambda b, i: (b, ctx_blk, 1)),
            pl.BlockSpec((1, s - seq, NA_WIDTH), lambda b, i: (b, ctx_blk, 2)),
        ] + [bias_spec(j) for j in range(rq)],
        out_specs=pl.BlockSpec((1, rq * GRID_W, NA_WIDTH), lambda b, i: (b, i, 0)),
        out_shape=jax.ShapeDtypeStruct((bsz, nq * rq * GRID_W, NA_WIDTH), BF16),
        compiler_params=_params("parallel", "arbitrary"),
        name="neighborhood_attention",
    )(qkv, qkv, qkv, qkv, qkv, *([bias_tab] * rq))


def _pool_kernel(up_ref, u_ref, un_ref, w_ref, sc_ref, o_ref, buf, s0, s1, s2, *, n_lat_tiles, seq, ctx_len):
    i = pl.program_id(1)
    hw = POOL_REACH
    n = TM + 2 * hw
    is_ctx = i >= n_lat_tiles
    base = jnp.where(is_ctx, 0, i * TM)
    lseq = jnp.where(is_ctx, ctx_len, seq)
    buf[0:hw] = up_ref[0]
    buf[hw:hw + TM] = u_ref[0]
    buf[hw + TM:n] = un_ref[0]
    p = base - hw + lax.broadcasted_iota(jnp.int32, (n, POOL_WIDTH), 0)
    buf[...] = jnp.where((p >= 0) & (p < lseq), buf[...], 0.0)
    s0[0:n - 1] = buf[0:n - 1] + buf[1:n]
    s1[0:n - 3] = s0[0:n - 3] + s0[2:n - 1]
    s2[0:n - 7] = s1[0:n - 7] + s1[4:n - 3]
    shape = (TM, POOL_WIDTH)
    t = base + lax.broadcasted_iota(jnp.int32, shape, 0)
    grp = lax.broadcasted_iota(jnp.int32, shape, 1) // POOL_GROUP_DIM
    acc = jnp.where(grp == 0, s0[hw - 1:hw - 1 + TM],
                    jnp.where(grp == 1, s1[hw - 2:hw - 2 + TM],
                              jnp.where(grp == 2, s2[hw - 4:hw - 4 + TM], s2[0:TM] + s2[hw:hw + TM])))
    win = jnp.left_shift(2, grp)
    back = win // 2
    fwd = win - back - 1
    lo = jnp.clip(t - back, 0, lseq - 1)
    hi = jnp.clip(t + fwd, 0, lseq - 1)
    mean = acc / (hi - lo + 1).astype(F32)
    diff = (mean - u_ref[0]).astype(BF16)
    o_ref[0] = (_dot(diff, w_ref[...]) * sc_ref[...]).astype(o_ref.dtype)


def _multiscale_pool(u, w_bd, scale, n_lat_tiles, n_tiles):
    bsz, s, c = u.shape
    hw = POOL_REACH
    per = TM // hw
    last = s // hw - 1
    return pl.pallas_call(
        functools.partial(_pool_kernel, n_lat_tiles=n_lat_tiles, seq=n_lat_tiles * TM, ctx_len=s - n_lat_tiles * TM),
        grid=(bsz, n_tiles),
        in_specs=[
            pl.BlockSpec((1, hw, c), lambda b, i: (b, jnp.maximum(i * per - 1, 0), 0)),
            pl.BlockSpec((1, TM, c), lambda b, i: (b, i, 0)),
            pl.BlockSpec((1, hw, c), lambda b, i: (b, jnp.minimum((i + 1) * per, last), 0)),
            pl.BlockSpec((c, c), lambda b, i: (0, 0)),
            pl.BlockSpec((1, c), lambda b, i: (0, 0)),
        ],
        out_specs=pl.BlockSpec((1, TM, c), lambda b, i: (b, i, 0)),
        out_shape=jax.ShapeDtypeStruct((bsz, n_tiles * TM, c), BF16),
        scratch_shapes=[pltpu.VMEM((TM + 2 * hw, c), F32)] * 4,
        compiler_params=_params("parallel", "parallel"),
        name="multiscale_pool",
    )(u, u, u, w_bd, scale)


def _gla_tile(qk_ref, v_ref, la_ref, cs_ref, sn_ref, st_ref, o_ref, reverse):
    ch = GLA_CHUNK
    lane = lax.broadcasted_iota(jnp.int32, (ch, GLA_QK), 1)
    row = lax.broadcasted_iota(jnp.int32, (ch, GLA_QK), 0)
    rr = lax.broadcasted_iota(jnp.int32, (GLA_HEADS * ch, ch), 0) % ch
    cc = lax.broadcasted_iota(jnp.int32, (GLA_HEADS * ch, ch), 1)
    causal = (rr <= cc) if reverse else (rr >= cc)
    st_shape = (GLA_WIDTH, GLA_QK)
    head_blk = (lax.broadcasted_iota(jnp.int32, st_shape, 0) // GLA_DV
                == lax.broadcasted_iota(jnp.int32, st_shape, 1) // GLA_DK)
    quarter = GLA_DK // 4
    first_half = (lane % (2 * quarter)) < quarter

    def rope(x, cs, sn):
        partner = jnp.where(first_half, pltpu.roll(x, GLA_QK - quarter, 1), pltpu.roll(x, quarter, 1))
        return x * cs + partner * sn

    chunks = range(TM // ch)
    for c in (reversed(chunks) if reverse else chunks):
        sl = slice(c * ch, (c + 1) * ch)
        cs = cs_ref[sl, :]
        sn = sn_ref[sl, :]
        q = rope(qk_ref[0, sl, 0:GLA_QK], cs, sn)
        k = rope(qk_ref[0, sl, GLA_QK:2 * GLA_QK], cs, sn)
        b = la_ref[0, sl, :]
        d = 1
        while d < ch:
            if reverse:
                b = b + jnp.where(row < ch - d, pltpu.roll(b, ch - d, 0), 0.0)
            else:
                b = b + jnp.where(row >= d, pltpu.roll(b, d, 0), 0.0)
            d *= 2
        b_last = b[0:1, :] if reverse else b[ch - 1:ch, :]
        q_in = q * jnp.exp(b)
        k_in = (k * jnp.exp(-b)).astype(BF16)
        k_st = (k * jnp.exp(b_last - b)).astype(BF16)
        att = _dot_nt(_stack_heads(q_in, GLA_DK).astype(BF16), k_in)
        att = jnp.where(causal, att, 0.0).astype(BF16)
        vb = v_ref[0, sl, :].astype(BF16)
        o = _unstack_heads(_dot(att, vb), GLA_DV)
        st = st_ref[...]
        o_ref[0, sl, :] = o + _dot_nt(q_in.astype(BF16), st.astype(BF16))
        st_ref[...] = st * jnp.exp(b_last) + jnp.where(head_blk, _dot_tn(vb, k_st), 0.0)


def _gla_kernel(qkf, vf, laf, csf, snf, qkb, vb, lab, csb, snb, of_ref, ob_ref, st_f, st_b):
    @pl.when(pl.program_id(1) == 0)
    def _():
        st_f[...] = jnp.zeros_like(st_f)
        st_b[...] = jnp.zeros_like(st_b)

    _gla_tile(qkf, vf, laf, csf, snf, st_f, of_ref, False)
    _gla_tile(qkb, vb, lab, csb, snb, st_b, ob_ref, True)


def _rope_tables(seq, s):
    quarter = GLA_DK // 4
    t = np.arange(s)
    pos_row = np.where(t < seq, t // GRID_W, 0).astype(np.float32)
    pos_col = np.where(t < seq, t % GRID_W, 0).astype(np.float32)
    d = np.arange(GLA_QK) % GLA_DK
    inv_freq = ROPE_BASE ** (-jnp.arange(quarter, dtype=F32) / quarter)
    freq = inv_freq[d % quarter]
    pos = jnp.where((d < GLA_DK // 2)[None, :], pos_row[:, None], pos_col[:, None])
    ang = pos * freq[None, :]
    sign = np.where((d % (2 * quarter)) < quarter, -1.0, 1.0).astype(np.float32)
    return jnp.cos(ang), jnp.sin(ang) * sign[None, :]


def _gla(gg, cos, sin, n_lat_tiles):
    bsz, s, _ = gg.shape
    nt = s // TM
    assert nt == n_lat_tiles + 1
    fwd = lambda i: (i + n_lat_tiles) % nt
    bwd = lambda i: jnp.where(i == 0, n_lat_tiles, n_lat_tiles - i)
    la_f_blk = (2 * GLA_QK + 2 * GLA_WIDTH) // GLA_QK
    la_b_blk = la_f_blk + 1
    v_blk = 2 * GLA_QK // GLA_WIDTH

    def specs(order, la_blk):
        return [
            pl.BlockSpec((1, TM, 2 * GLA_QK), lambda b, i: (b, order(i), 0)),
            pl.BlockSpec((1, TM, GLA_WIDTH), lambda b, i: (b, order(i), v_blk)),
            pl.BlockSpec((1, TM, GLA_QK), lambda b, i: (b, order(i), la_blk)),
            pl.BlockSpec((TM, GLA_QK), lambda b, i: (order(i), 0)),
            pl.BlockSpec((TM, GLA_QK), lambda b, i: (order(i), 0)),
        ]

    return pl.pallas_call(
        _gla_kernel,
        grid=(bsz, nt),
        in_specs=specs(fwd, la_f_blk) + specs(bwd, la_b_blk),
        out_specs=[
            pl.BlockSpec((1, TM, GLA_WIDTH), lambda b, i: (b, fwd(i), 0)),
            pl.BlockSpec((1, TM, GLA_WIDTH), lambda b, i: (b, bwd(i), 0)),
        ],
        out_shape=[jax.ShapeDtypeStruct((bsz, s, GLA_WIDTH), F32)] * 2,
        scratch_shapes=[pltpu.VMEM((GLA_WIDTH, GLA_QK), F32)] * 2,
        compiler_params=_params("arbitrary", "arbitrary"),
        name="gla_bidir",
    )(gg, gg, gg, cos, sin, gg, gg, gg, cos, sin)


def _store_row_tiles(ref, val):
    rows, width = val.shape
    pieces = width // LANES
    for s in range(pieces):
        ref[pl.ds(s, rows, stride=pieces), :] = val[:, s * LANES:(s + 1) * LANES]


def _load_row_tiles(ref, rows, pieces):
    return jnp.concatenate([ref[pl.ds(s, rows, stride=pieces), :] for s in range(pieces)], axis=1)


SLOTS_PER_Y_TILE = 2


def _pack_bf16_pairs(y):
    half = y.shape[1] // 2
    bits = lax.bitcast_convert_type(y.astype(jnp.bfloat16).astype(F32), jnp.uint32)
    return (bits[:, :half] >> 16) | (bits[:, half:] & jnp.uint32(0xFFFF0000))


def _unpack_bf16_pairs(words):
    lo = lax.bitcast_convert_type(words << 16, F32)
    hi = lax.bitcast_convert_type(words & jnp.uint32(0xFFFF0000), F32)
    return jnp.concatenate([lo, hi], axis=1)


def _row_index(shape):
    return lax.broadcasted_iota(jnp.int32, shape, 0).astype(F32)


def _stack_rows(rows):
    shape = (len(rows), rows[0].shape[1])
    rid = lax.broadcasted_iota(jnp.int32, shape, 0)
    out = jnp.zeros(shape, rows[0].dtype)
    for k, r in enumerate(rows):
        out = jnp.where(rid == k, r, out)
    return out


def _select_experts(scores, biased):
    n_exp, n_tok = biased.shape
    per = n_exp // N_GROUPS
    neg = -jnp.inf
    sub = _row_index((per, n_tok))
    grp_rows = []
    for g in range(N_GROUPS):
        blk = biased[g * per:(g + 1) * per, :]
        m1 = jnp.max(blk, axis=0, keepdims=True)
        i1 = jnp.min(jnp.where(blk == m1, sub, float(per)), axis=0, keepdims=True)
        m2 = jnp.max(jnp.where(sub == i1, neg, blk), axis=0, keepdims=True)
        grp_rows.append(m1 + m2)
    cur = _stack_rows(grp_rows)
    gid = _row_index(cur.shape)
    picked = jnp.zeros(cur.shape, F32)
    for _ in range(TOPK_GROUPS):
        gm = jnp.max(cur, axis=0, keepdims=True)
        gi = jnp.min(jnp.where(cur == gm, gid, float(N_GROUPS)), axis=0, keepdims=True)
        hit = gid == gi
        picked = jnp.where(hit, 1.0, picked)
        cur = jnp.where(hit, neg, cur)
    cur = jnp.concatenate(
        [jnp.where(picked[g:g + 1, :] > 0.0, biased[g * per:(g + 1) * per, :], neg) for g in range(N_GROUPS)], axis=0)
    eid = _row_index(cur.shape)
    ids, vals = [], []
    chosen = jnp.zeros(cur.shape, F32)
    for _ in range(TOP_K):
        m = jnp.max(cur, axis=0, keepdims=True)
        idx = jnp.min(jnp.where(cur == m, eid, float(n_exp)), axis=0, keepdims=True)
        hit = eid == idx
        ids.append(idx)
        vals.append(jnp.sum(jnp.where(hit, scores, 0.0), axis=0, keepdims=True))
        chosen = jnp.where(hit, 1.0, chosen)
        cur = jnp.where(hit, neg, cur)
    return ids, vals, chosen


def _outproj_kernel(x_ref, xc_ref, oa_ref, ob_ref, of_ref, obk_ref, r_ref, gg_ref, wo_ref, mod_ref, gf_ref, wr_ref, br_ref,
                    tri_ref, xo_ref, h_ref, e_ref, w_ref, rk_ref, hist_ref, *, n_lat_tiles):
    o = of_ref[0] + obk_ref[0]
    head = lax.broadcasted_iota(jnp.int32, o.shape, 1) // GLA_DV
    o2 = o * o
    rs = jnp.zeros_like(o)
    for h in range(GLA_HEADS):
        ssq = jnp.sum(jnp.where(head == h, o2, 0.0), axis=-1, keepdims=True)
        rs = jnp.where(head == h, lax.rsqrt(ssq / GLA_DV + NORM_EPS), rs)
    oc = (o * rs * gg_ref[...] * _silu(r_ref[0])).astype(BF16)
    c1 = NA_WIDTH + POOL_WIDTH
    acc = _dot(oa_ref[0], wo_ref[0:NA_WIDTH]) + _dot(ob_ref[0], wo_ref[NA_WIDTH:c1]) + _dot(oc, wo_ref[c1:])
    x = _stream_tile(x_ref, xc_ref, n_lat_tiles) + mod_ref[0, 0, 2:3, :] * acc
    xo_ref[0] = x
    h = _rmsnorm(x, gf_ref[...]) * (1.0 + mod_ref[0, 0, 4:5, :]) + mod_ref[0, 0, 3:4, :]
    _store_row_tiles(h_ref, h)

    scores = _sigmoid(_dot_nt(wr_ref[...], h.astype(BF16)))
    ids, vals, chosen = _select_experts(scores, scores + br_ref[...])
    total = vals[0]
    for v in vals[1:]:
        total = total + v
    w_ref[...] = _stack_rows([v / total * ROUTED_SCALE for v in vals])
    e_ref[...] = _stack_rows(ids).astype(jnp.int32)
    chosen_b = chosen.astype(BF16)
    before = _dot(chosen_b, tri_ref[...])
    eid = _row_index(before.shape)
    rk_ref[...] = _stack_rows([jnp.sum(jnp.where(eid == idx, before, 0.0), axis=0, keepdims=True) for idx in ids])
    hist_ref[0] = _dot_nt(jnp.ones((SUBLANES, chosen.shape[1]), BF16), chosen_b)


def _outproj(x_lat, x_ctx, ctx_blk, oa, ob, o_f, o_b, gg, g_gla4, w_out, mod, g_ffn, w_router_t, b_router,
             n_lat_tiles, n_tiles):
    bsz, _, d = x_lat.shape
    n_exp = w_router_t.shape[0]
    r_blk = (2 * GLA_QK + GLA_WIDTH) // GLA_WIDTH
    tile = lambda w: pl.BlockSpec((1, TM, w), lambda b, i: (b, i, 0))
    const = lambda b, i: (0, 0)
    per_tok = pl.BlockSpec((TOP_K, TM), lambda b, i: (0, b * n_tiles + i))
    t_tok = bsz * n_tiles * TM
    tri = (jnp.arange(TM)[:, None] < jnp.arange(TM)[None, :]).astype(BF16)
    return pl.pallas_call(
        functools.partial(_outproj_kernel, n_lat_tiles=n_lat_tiles),
        grid=(bsz, n_tiles),
        in_specs=[
            *_stream_specs(d, n_lat_tiles, ctx_blk),
            tile(NA_WIDTH), tile(POOL_WIDTH), tile(GLA_WIDTH), tile(GLA_WIDTH),
            pl.BlockSpec((1, TM, GLA_WIDTH), lambda b, i: (b, i, r_blk)),
            pl.BlockSpec((1, GLA_WIDTH), const),
            pl.BlockSpec(w_out.shape, const),
            pl.BlockSpec((1, 1, SUBLANES, d), lambda b, i: (b, jnp.where(i >= n_lat_tiles, 0, 1), 0, 0)),
            pl.BlockSpec((1, d), const),
            pl.BlockSpec(w_router_t.shape, const),
            pl.BlockSpec((n_exp, 1), const),
            pl.BlockSpec((TM, TM), const),
        ],
        out_specs=[tile(d), pl.BlockSpec((TM * d // LANES, LANES), lambda b, i: (b * n_tiles + i, 0)),
                   per_tok, per_tok, per_tok,
                   pl.BlockSpec((1, SUBLANES, n_exp), lambda b, i: (b * n_tiles + i, 0, 0))],
        out_shape=[
            jax.ShapeDtypeStruct((bsz, n_tiles * TM, d), F32),
            jax.ShapeDtypeStruct((t_tok * d // LANES, LANES), F32),
            jax.ShapeDtypeStruct((TOP_K, t_tok), jnp.int32),
            jax.ShapeDtypeStruct((TOP_K, t_tok), F32),
            jax.ShapeDtypeStruct((TOP_K, t_tok), F32),
            jax.ShapeDtypeStruct((bsz * n_tiles, SUBLANES, n_exp), F32),
        ],
        compiler_params=_params("parallel", "parallel"),
        name="outproj_router",
    )(x_lat, x_ctx, oa, ob, o_f, o_b, gg, g_gla4, w_out, mod, g_ffn, w_router_t, b_router, tri)


def _slot_layout(hist):
    n_tiles, n_exp = hist.shape
    counts = hist.sum(0)
    tile_base = jnp.cumsum(hist, axis=0) - hist
    padded = (counts + MOE_BLOCK - 1) // MOE_BLOCK * MOE_BLOCK
    pad_end = jnp.cumsum(padded)
    pad_start = pad_end - padded
    base = (pad_start[None, :] + tile_base).astype(F32).reshape(n_tiles, n_exp, 1)
    n_blocks = -(-(n_tiles * TM * TOP_K) // MOE_BLOCK) + n_exp
    n_used = (pad_end[-1] // MOE_BLOCK).astype(jnp.int32)
    first_blk = (pad_start // MOE_BLOCK).astype(jnp.int32)
    blk_count = (padded // MOE_BLOCK).astype(jnp.int32)
    pad_row = (pad_start + counts).astype(jnp.int32)
    pad_cnt = (padded - counts).astype(jnp.int32)
    return base, first_blk, blk_count, pad_row, pad_cnt, n_used.reshape(1), n_blocks


def _pos_kernel(e_ref, rk_ref, base_ref, posx_ref, posy_ref, half_ref, *, pieces):
    eid = lax.broadcasted_iota(jnp.int32, (base_ref.shape[1], TM), 0)
    for j in range(base_ref.shape[0]):
        cols = slice(j * TM, (j + 1) * TM)
        base = base_ref[j]
        rows = [jnp.sum(jnp.where(eid == e_ref[k:k + 1, cols], base, 0.0), axis=0, keepdims=True)
                for k in range(TOP_K)]
        slot = (_stack_rows(rows) + rk_ref[:, cols]).astype(jnp.int32)
        posx_ref[j] = slot * pieces
        posy_ref[j] = (slot // SLOTS_PER_Y_TILE) * pieces
        half_ref[:, cols] = slot % SLOTS_PER_Y_TILE


def _slot_positions(e_t, rank_t, base, pieces):
    n_tiles, n_exp, _ = base.shape
    per_step = max(p for p in range(1, DISPATCH_TILES + 1) if n_tiles % p == 0)
    per_tok = pl.BlockSpec((TOP_K, per_step * TM), lambda i: (0, i))
    flat = pl.BlockSpec((per_step, TOP_K, TM), lambda i: (i, 0, 0))
    pos_x, pos_y, half = pl.pallas_call(
        functools.partial(_pos_kernel, pieces=pieces),
        grid=(n_tiles // per_step,),
        in_specs=[per_tok, per_tok, pl.BlockSpec((per_step, n_exp, 1), lambda i: (i, 0, 0))],
        out_specs=[flat, flat, per_tok],
        out_shape=[jax.ShapeDtypeStruct((n_tiles, TOP_K, TM), jnp.int32)] * 2
        + [jax.ShapeDtypeStruct(e_t.shape, jnp.int32)],
        compiler_params=_params("parallel"),
        name="moe_positions",
    )(e_t, rank_t, base)
    return pos_x.reshape(-1), pos_y.reshape(-1), half


def _dispatch_kernel(pad_row_ref, pad_cnt_ref, nu_ref, pos_ref, h_ref, xs_hbm, zbuf, sem, zsem, *, n_blocks):
    step = pl.program_id(0)
    blk_rows = zbuf.shape[0]
    row_pieces = blk_rows // MOE_BLOCK

    def pad_copies(e, act):
        row = pad_row_ref[e]
        cnt = pad_cnt_ref[e]
        bit = MOE_BLOCK // 2
        while bit:
            @pl.when((cnt & bit) != 0)
            def _(row=row, bit=bit):
                dst = xs_hbm.at[pl.ds(pl.multiple_of(row * row_pieces, row_pieces), bit * row_pieces), :]
                act(pltpu.make_async_copy(zbuf.at[pl.ds(0, bit * row_pieces), :], dst, zsem))
            row = row + (cnt & bit)
            bit //= 2

    def tail_copy(g):
        return pltpu.make_async_copy(zbuf, xs_hbm.at[pl.ds(pl.multiple_of(g * blk_rows, blk_rows), blk_rows), :], zsem)

    def zero_rows(act):
        def per_expert(e, c):
            pad_copies(e, act)
            return c
        lax.fori_loop(0, pad_row_ref.shape[0], per_expert, 0)

        def per_block(g, c):
            act(tail_copy(g))
            return c
        lax.fori_loop(nu_ref[0], n_blocks, per_block, 0)

    @pl.when(step == 0)
    def _():
        zbuf[...] = jnp.zeros_like(zbuf)
        zero_rows(lambda cp: cp.start())

    tiles = pos_ref.shape[0] // (TOP_K * TM)
    pieces = h_ref.shape[0] // (tiles * TM)

    for j in range(tiles):
        def body(t, c):
            src = h_ref.at[pl.ds(pl.multiple_of((j * TM + t) * pieces, pieces), pieces), :]
            for k in range(TOP_K):
                dst = xs_hbm.at[pl.ds(pl.multiple_of(pos_ref[(j * TOP_K + k) * TM + t], pieces), pieces), :]
                pltpu.make_async_copy(src, dst, sem).start(priority=k % 2)
            return c
        lax.fori_loop(0, TM, body, 0, unroll=2)
    for _ in range(TOP_K):
        pltpu.make_async_copy(h_ref, xs_hbm.at[pl.ds(0, h_ref.shape[0]), :], sem).wait()

    @pl.when(step == pl.num_programs(0) - 1)
    def _():
        zero_rows(lambda cp: cp.wait())


def _dispatch(h2t, pos, pad_row, pad_cnt, n_used, n_blocks, pieces):
    n_tiles = h2t.shape[0] // (TM * pieces)
    per_step = max(p for p in range(1, DISPATCH_TILES + 1) if n_tiles % p == 0)
    grid_spec = pltpu.PrefetchScalarGridSpec(
        num_scalar_prefetch=3,
        grid=(n_tiles // per_step,),
        in_specs=[
            pl.BlockSpec((per_step * TOP_K * TM,), lambda i, *_: (i,), memory_space=pltpu.SMEM),
            pl.BlockSpec((per_step * TM * pieces, LANES), lambda i, *_: (i, 0)),
        ],
        out_specs=pl.BlockSpec(memory_space=pl.ANY),
        scratch_shapes=[pltpu.VMEM((MOE_BLOCK * pieces, LANES), F32), pltpu.SemaphoreType.DMA,
                        pltpu.SemaphoreType.DMA],
    )
    return pl.pallas_call(
        functools.partial(_dispatch_kernel, n_blocks=n_blocks),
        grid_spec=grid_spec,
        out_shape=jax.ShapeDtypeStruct((n_blocks * MOE_BLOCK * pieces, LANES), F32),
        compiler_params=_params("arbitrary"),
        name="moe_dispatch",
    )(pad_row, pad_cnt, n_used, pos, h2t)


def _expert_kernel(first_ref, cnt_ref, nu_ref, wi_ref, wd_ref, xs_hbm, ys_hbm, xbuf, ybuf, wi_bf, wd_bf, xsem, ysem,
                   *, n_blocks):
    e = pl.program_id(0)
    n_used = nu_ref[0]
    rows = xbuf.shape[1]
    pieces = rows // MOE_BLOCK
    d_exp = wd_ref.shape[2]

    def x_copy(g, s):
        return pltpu.make_async_copy(xs_hbm.at[pl.ds(pl.multiple_of(g * rows, rows), rows), :], xbuf.at[s], xsem.at[s])

    y_rows = ybuf.shape[1]

    def y_copy(g, s):
        return pltpu.make_async_copy(ybuf.at[s], ys_hbm.at[pl.ds(pl.multiple_of(g * y_rows, y_rows), y_rows), :],
                                     ysem.at[s])

    n_xbuf = xbuf.shape[0]
    n_ybuf = ybuf.shape[0]

    @pl.when(e == 0)
    def _():
        for g in range(n_xbuf - 1):
            @pl.when(g < n_used)
            def _():
                x_copy(g, g).start()

    @pl.when(cnt_ref[e] > 0)
    def _():
        wi_bf[...] = wi_ref[0, 0].astype(BF16)
        wd_bf[...] = wd_ref[0, 0].astype(BF16)

    def block(j, c):
        g = first_ref[e] + j
        s = g % n_ybuf
        x_copy(g, g % n_xbuf).wait()

        @pl.when(g + n_xbuf - 1 < n_used)
        def _():
            x_copy(g + n_xbuf - 1, (g + n_xbuf - 1) % n_xbuf).start()

        @pl.when(g >= n_ybuf)
        def _():
            y_copy(g - n_ybuf, s).wait()

        hh = _dot(_load_row_tiles(xbuf.at[g % n_xbuf], MOE_BLOCK, pieces).astype(BF16), wi_bf[...])
        a = (_silu(hh[:, :d_exp]) * hh[:, d_exp:]).astype(BF16)
        _store_row_tiles(ybuf.at[s], _pack_bf16_pairs(_dot(a, wd_bf[...])))
        y_copy(g, s).start(priority=1)
        return c
    lax.fori_loop(0, cnt_ref[e], block, 0)

    @pl.when(e == pl.num_programs(0) - 1)
    def _():
        for back in range(1, n_ybuf + 1):
            @pl.when(n_used >= back)
            def _():
                y_copy(n_used - back, (n_used - back) % n_ybuf).wait()

        ybuf[0] = jnp.zeros(ybuf.shape[1:], ybuf.dtype)

        def start(g, c):
            y_copy(g, 0).start()
            return c
        lax.fori_loop(n_used, n_blocks, start, 0)

        def wait(g, c):
            y_copy(g, 0).wait()
            return c
        lax.fori_loop(n_used, n_blocks, wait, 0)


def _moe_experts(xs, first_blk, blk_count, n_used, n_blocks, w_e_in, w_e_down, layer):
    _, n_exp, d, f2 = w_e_in.shape
    rows = xs.shape[0] // n_blocks
    grid_spec = pltpu.PrefetchScalarGridSpec(
        num_scalar_prefetch=3,
        grid=(n_exp,),
        in_specs=[
            pl.BlockSpec((1, 1, d, f2), lambda e, first, cnt, nu: (layer, e, 0, 0)),
            pl.BlockSpec((1, 1, f2 // 2, d), lambda e, first, cnt, nu: (layer, e, 0, 0)),
            pl.BlockSpec(memory_space=pl.ANY),
        ],
        out_specs=pl.BlockSpec(memory_space=pl.ANY),
        scratch_shapes=[
            pltpu.VMEM((EXPERT_X_BUFFERS, rows, LANES), F32),
            pltpu.VMEM((EXPERT_Y_BUFFERS, rows // SLOTS_PER_Y_TILE, LANES), jnp.uint32),
            pltpu.VMEM((d, f2), BF16),
            pltpu.VMEM((f2 // 2, d), BF16),
            pltpu.SemaphoreType.DMA((EXPERT_X_BUFFERS,)),
            pltpu.SemaphoreType.DMA((EXPERT_Y_BUFFERS,)),
        ],
    )
    return pl.pallas_call(
        functools.partial(_expert_kernel, n_blocks=n_blocks),
        grid_spec=grid_spec,
        out_shape=jax.ShapeDtypeStruct((xs.shape[0] // SLOTS_PER_Y_TILE, LANES), jnp.uint32),
        compiler_params=_params("arbitrary"),
        name="moe_experts",
    )(first_blk, blk_count, n_used, w_e_in, w_e_down, xs)


def _combine_kernel(pos0_ref, posn_ref, x_ref, h_ref, w_ref, half_ref, wsi_ref, wsd_ref, mod_ref, gfin_ref, ys_hbm,
                    o_ref, ybuf, sem, *, final, n_steps):
    step = pl.program_id(0) * pl.num_programs(1) + pl.program_id(1)
    slot = step % 2
    d_exp = wsd_ref.shape[0]
    pieces = h_ref.shape[0] // TMC

    def gather(pos_ref, st, s):
        off = (st % (TM // TMC)) * TMC
        def body(t, c):
            for k in range(TOP_K):
                src = ys_hbm.at[pl.ds(pl.multiple_of(pos_ref[k * TM + off + t], pieces), pieces), :]
                dst = ybuf.at[s, k, pl.ds(pl.multiple_of(t * pieces, pieces), pieces), :]
                pltpu.make_async_copy(src, dst, sem.at[s]).start(priority=k % 2)
            return c
        lax.fori_loop(0, TMC, body, 0, unroll=2)

    @pl.when(step == 0)
    def _():
        gather(pos0_ref, 0, 0)

    @pl.when(step + 1 < n_steps)
    def _():
        gather(posn_ref, step + 1, 1 - slot)

    hh = _dot(_load_row_tiles(h_ref, TMC, pieces).astype(BF16), wsi_ref[...])
    a = (_silu(hh[:, :d_exp]) * hh[:, d_exp:]).astype(BF16)
    y = _dot(a, wsd_ref[...])
    for k in range(TOP_K):
        pltpu.make_async_copy(ys_hbm.at[pl.ds(0, TMC * pieces), :], ybuf.at[slot, k], sem.at[slot]).wait()
    w = w_ref[...]
    half = half_ref[...]
    for k in range(TOP_K):
        both = _load_row_tiles(ybuf.at[slot, k], TMC, pieces)
        mine = jnp.where(half[:, k:k + 1] > 0, both[:, both.shape[1] // 2:], both[:, :both.shape[1] // 2])
        y = y + _unpack_bf16_pairs(mine) * w[:, k:k + 1]
    x = x_ref[0] + mod_ref[0, 0, 5:6, :] * y
    if final:
        x = _rmsnorm(x, gfin_ref[...])
    o_ref[0] = x


def _combine(x_mid, h2t, ys, pos, w_tok, half_tok, w_sh_in, w_sh_down, mod, g_final, n_lat_tiles, final):
    bsz, s, d = x_mid.shape
    nt = s // TMC
    n_steps = bsz * nt
    halves = TM // TMC
    lat_steps = n_lat_tiles * halves
    pieces = d // LANES
    const = lambda b, i: (0, 0)
    tile = pl.BlockSpec((1, TMC, d), lambda b, i: (b, i, 0))
    smem_blk = lambda f: pl.BlockSpec((TOP_K * TM,), f, memory_space=pltpu.SMEM)
    return pl.pallas_call(
        functools.partial(_combine_kernel, final=final, n_steps=n_steps),
        grid=(bsz, nt),
        in_specs=[
            smem_blk(lambda b, i: (0,)),
            smem_blk(lambda b, i: (jnp.minimum(b * nt + i + 1, n_steps - 1) // halves,)),
            tile,
            pl.BlockSpec((TMC * pieces, LANES), lambda b, i: (b * nt + i, 0)),
            pl.BlockSpec((TMC, TOP_K), lambda b, i: (b * nt + i, 0)),
            pl.BlockSpec((TMC, TOP_K), lambda b, i: (b * nt + i, 0)),
            pl.BlockSpec(w_sh_in.shape, const),
            pl.BlockSpec(w_sh_down.shape, const),
            pl.BlockSpec((1, 1, SUBLANES, d), lambda b, i: (b, jnp.where(i >= lat_steps, 0, 1), 0, 0)),
            pl.BlockSpec((1, d), const),
            pl.BlockSpec(memory_space=pl.ANY),
        ],
        out_specs=tile,
        out_shape=jax.ShapeDtypeStruct((bsz, s, d), F32),
        scratch_shapes=[pltpu.VMEM((2, TOP_K, TMC * pieces, LANES), jnp.uint32), pltpu.SemaphoreType.DMA((2,))],
        compiler_params=_params("arbitrary", "arbitrary"),
        name="moe_combine",
    )(pos, pos, x_mid, h2t, w_tok, half_tok, w_sh_in, w_sh_down, mod, g_final, ys)


def _block_diag(w):
    g, a, b = w.shape
    out = jnp.zeros((g * a, g * b), w.dtype)
    for j in range(g):
        out = out.at[j * a:(j + 1) * a, j * b:(j + 1) * b].set(w[j])
    return out


def kernel(x, c, ctx, c_ctx, w_mod, b_mod, g_mix, w_in, rpb, w_pool, pool_scale, w_gate_f, b_gate_f, w_gate_b, b_gate_b, g_gla, w_out, g_ffn, w_router, b_router, w_e_in, w_e_down, w_sh_in, w_sh_down, g_final):
    bsz, seq, d = x.shape
    n_ctx = ctx.shape[1]
    depth = w_mod.shape[0]
    n_exp = w_router.shape[2]
    assert n_ctx == TM and seq % TM == 0 and seq % GRID_W == 0
    n_lat = seq // TM
    s = seq + n_ctx

    n_rows = -(-(bsz + 1) // SUBLANES) * SUBLANES
    cc = jnp.zeros((n_rows, d), F32).at[:bsz].set(c).at[bsz].set(c_ctx)
    mod_all = _modulation(cc, w_mod, b_mod).reshape(depth, n_rows, 6, d)
    cos, sin = _rope_tables(seq, s)

    x_lat, x_ctx, ctx_blk = x, ctx, 0
    for layer in range(depth):
        last = layer == depth - 1
        n_tiles = n_lat if last else n_lat + 1
        m = mod_all[layer]
        mod = jnp.stack([jnp.broadcast_to(m[bsz], (bsz, 6, d)), m[:bsz]], axis=1)
        mod = jnp.pad(mod, ((0, 0), (0, 0), (0, SUBLANES - 6), (0, 0)))
        wl = w_in[layer]
        w_main = wl[:, :MAIN_W].astype(BF16)
        w_low = jnp.pad(wl[:, MAIN_W:], ((0, 0), (0, LANES - 2 * GLA_GATE_RANK))).astype(BF16)
        w2 = jnp.zeros((LANES, 2 * GLA_QK), F32)
        w2 = w2.at[:GLA_GATE_RANK, :GLA_QK].set(w_gate_f[layer])
        w2 = w2.at[GLA_GATE_RANK:2 * GLA_GATE_RANK, GLA_QK:].set(w_gate_b[layer]).astype(BF16)
        b2 = jnp.concatenate([b_gate_f[layer], b_gate_b[layer]])[None, :]
        qkv, u, gg = _inproj(x_lat, x_ctx, ctx_blk, mod, g_mix[layer][None, :], w_main, w_low, w2, b2, n_lat)

        oa = _neighborhood_attention(qkv, _na_bias_table(rpb[layer]), n_lat, not last)
        ob = _multiscale_pool(u, _block_diag(w_pool[layer]).astype(BF16), pool_scale[layer][None, :], n_lat, n_tiles)
        o_f, o_b = _gla(gg, cos, sin, n_lat)

        x_mid, h2t, e_t, w_t, rank_t, hist = _outproj(
            x_lat, x_ctx, ctx_blk, oa, ob, o_f, o_b, gg, jnp.tile(g_gla[layer], GLA_HEADS)[None, :], w_out[layer].astype(BF16), mod,
            g_ffn[layer][None, :], w_router[layer].T.astype(BF16), b_router[layer].reshape(n_exp, 1), n_lat, n_tiles)
        pieces = d // LANES
        base, first_blk, blk_count, pad_row, pad_cnt, n_used, n_blocks = _slot_layout(
            hist[:, 0, :].astype(jnp.int32))
        pos_x, pos_y, half = _slot_positions(e_t, rank_t, base, pieces)
        xs = _dispatch(h2t, pos_x, pad_row, pad_cnt, n_used, n_blocks, pieces)
        ys = _moe_experts(xs, first_blk, blk_count, n_used, n_blocks, w_e_in, w_e_down, layer)
        xa = _combine(x_mid, h2t, ys, pos_y, w_t.T, half.T, w_sh_in[layer].astype(BF16),
                      w_sh_down[layer].astype(BF16), mod, g_final[None, :], n_lat, last)
        x_lat, x_ctx, ctx_blk = xa, xa, n_lat
    return xa
```

```python
import functools

import jax
import jax.numpy as jnp
import numpy as np
from jax import lax
from jax.experimental import pallas as pl
from jax.experimental.pallas import tpu as pltpu

GRID_W = 64
NORM_EPS = 1e-6
NA_HEADS = 8
NA_HEAD_DIM = 64
NA_WIDTH = NA_HEADS * NA_HEAD_DIM
NA_WIN_ROWS = 8
NA_WIN_COLS = 16
POOL_WINDOWS = (2, 4, 8, 16)
POOL_GROUP_DIM = 64
POOL_WIDTH = len(POOL_WINDOWS) * POOL_GROUP_DIM
POOL_REACH = max(POOL_WINDOWS) // 2
assert POOL_WINDOWS == tuple(2 ** (g + 1) for g in range(len(POOL_WINDOWS)))
GLA_HEADS = 4
GLA_DK = 32
GLA_DV = 64
GLA_QK = GLA_HEADS * GLA_DK
GLA_WIDTH = GLA_HEADS * GLA_DV
GLA_GATE_RANK = 16
GLA_TAU = 16.0
GLA_CHUNK = 64
GLA_BATCH = 2
ROPE_BASE = 10000.0
N_EXPERTS = 256
TOP_K = 8
N_GROUPS = 8
TOPK_GROUPS = 4
ROUTED_SCALE = 2.5

TM = 256
TMC = 256
DISPATCH_TILES = 8
EXPERT_X_BUFFERS = 6
EXPERT_Y_BUFFERS = 4
MOE_BLOCK = 256
LANES = 128
SUBLANES = 8
MXU_DIM = 256
MASK_VALUE = -1e30
V7X_VMEM_BYTES = 64 * 1024 * 1024
VMEM_LIMIT = V7X_VMEM_BYTES * 3 // 4

QKV_W = 3 * NA_WIDTH
G_W = 2 * GLA_QK + 2 * GLA_WIDTH + 2 * GLA_QK
MAIN_W = QKV_W + POOL_WIDTH + 2 * GLA_QK + 2 * GLA_WIDTH

BF16 = jnp.bfloat16
F32 = jnp.float32


def _params(*sem):
    return pltpu.CompilerParams(dimension_semantics=sem, vmem_limit_bytes=VMEM_LIMIT)


def _sigmoid(x):
    return 1.0 / (1.0 + jnp.exp(-x))


def _silu(x):
    return x * _sigmoid(x)


def _rmsnorm(x, g):
    return x * lax.rsqrt(jnp.mean(x * x, axis=-1, keepdims=True) + NORM_EPS) * g


def _dot(a, b):
    return jnp.dot(a, b, preferred_element_type=F32)


def _dot_nt(a, b):
    return lax.dot_general(a, b, (((1,), (1,)), ((), ())), preferred_element_type=F32)


def _dot_tn(a, b):
    return lax.dot_general(a, b, (((0,), (0,)), ((), ())), preferred_element_type=F32)


def _mod_kernel(c_ref, w_ref, b_ref, o_ref):
    a = _silu(c_ref[...]).astype(BF16)
    o_ref[0] = _dot(a, w_ref[0].astype(BF16)) + b_ref[0]


def _modulation(cc, w_mod, b_mod):
    depth, d, n = w_mod.shape
    r = cc.shape[0]
    tn = d
    return pl.pallas_call(
        _mod_kernel,
        grid=(depth, n // tn),
        in_specs=[
            pl.BlockSpec((r, d), lambda l, j: (0, 0)),
            pl.BlockSpec((1, d, tn), lambda l, j: (l, 0, j)),
            pl.BlockSpec((1, 1, tn), lambda l, j: (l, 0, j)),
        ],
        out_specs=pl.BlockSpec((1, r, tn), lambda l, j: (l, 0, j)),
        out_shape=jax.ShapeDtypeStruct((depth, r, n), F32),
        compiler_params=_params("parallel", "parallel"),
        name="modulation",
    )(cc, w_mod, b_mod.reshape(depth, 1, n))


def _stream_tile(x_ref, xc_ref, n_lat_tiles):
    return jnp.where(pl.program_id(1) >= n_lat_tiles, xc_ref[0], x_ref[0])


def _stream_specs(d, n_lat_tiles, ctx_blk):
    return [pl.BlockSpec((1, TM, d), lambda b, i: (b, jnp.minimum(i, n_lat_tiles - 1), 0)),
            pl.BlockSpec((1, TM, d), lambda b, i: (b, ctx_blk, 0))]


def _inproj_kernel(x_ref, xc_ref, mod_ref, g_ref, wm_ref, wl_ref, w2_ref, b2_ref, qkv_ref, u_ref, gg_ref, *, n_lat_tiles):
    x = _stream_tile(x_ref, xc_ref, n_lat_tiles)
    h = _rmsnorm(x, g_ref[...]) * (1.0 + mod_ref[0, 0, 1:2, :]) + mod_ref[0, 0, 0:1, :]
    hb = h.astype(BF16)
    q = _dot(hb, wm_ref[:, 0:NA_WIDTH]) * (NA_HEAD_DIM ** -0.5)
    qkv_ref[0, :, 0:NA_WIDTH] = q.astype(BF16)
    qkv_ref[0, :, NA_WIDTH:2 * NA_WIDTH] = _dot(hb, wm_ref[:, NA_WIDTH:2 * NA_WIDTH]).astype(BF16)
    qkv_ref[0, :, 2 * NA_WIDTH:QKV_W] = _dot(hb, wm_ref[:, 2 * NA_WIDTH:QKV_W]).astype(BF16)
    u_ref[0] = _dot(hb, wm_ref[:, QKV_W:QKV_W + POOL_WIDTH])
    c0 = QKV_W + POOL_WIDTH
    gg_ref[0, :, 0:GLA_QK] = _dot(hb, wm_ref[:, c0:c0 + GLA_QK]) * (GLA_DK ** -0.5)
    gg_ref[0, :, GLA_QK:2 * GLA_QK + 2 * GLA_WIDTH] = _dot(hb, wm_ref[:, c0 + GLA_QK:MAIN_W])
    a_low = _dot(hb, wl_ref[...]).astype(BF16)
    lg = _dot(a_low, w2_ref[...]) + b2_ref[...]
    log_sig = jnp.minimum(lg, 0.0) - jnp.log1p(jnp.exp(-jnp.abs(lg)))
    gg_ref[0, :, 2 * GLA_QK + 2 * GLA_WIDTH:G_W] = log_sig / GLA_TAU


def _inproj(x_lat, x_ctx, ctx_blk, mod, g, w_main, w_low, w2, b2, n_lat_tiles):
    bsz, _, d = x_lat.shape
    nt = n_lat_tiles + 1
    s = nt * TM
    const = lambda b, i: (0, 0)
    return pl.pallas_call(
        functools.partial(_inproj_kernel, n_lat_tiles=n_lat_tiles),
        grid=(bsz, nt),
        in_specs=_stream_specs(d, n_lat_tiles, ctx_blk) + [
            pl.BlockSpec((1, 1, SUBLANES, d), lambda b, i: (b, jnp.where(i >= n_lat_tiles, 0, 1), 0, 0)),
            pl.BlockSpec((1, d), const),
            pl.BlockSpec(w_main.shape, const),
            pl.BlockSpec(w_low.shape, const),
            pl.BlockSpec(w2.shape, const),
            pl.BlockSpec(b2.shape, const),
        ],
        out_specs=[
            pl.BlockSpec((1, TM, QKV_W), lambda b, i: (b, i, 0)),
            pl.BlockSpec((1, TM, POOL_WIDTH), lambda b, i: (b, i, 0)),
            pl.BlockSpec((1, TM, G_W), lambda b, i: (b, i, 0)),
        ],
        out_shape=[
            jax.ShapeDtypeStruct((bsz, s, QKV_W), BF16),
            jax.ShapeDtypeStruct((bsz, s, POOL_WIDTH), F32),
            jax.ShapeDtypeStruct((bsz, s, G_W), F32),
        ],
        compiler_params=_params("parallel", "parallel"),
        name="inproj",
    )(x_lat, x_ctx, mod, g, w_main, w_low, w2, b2)


HEADS_PER_GROUP = MXU_DIM // NA_HEAD_DIM
NA_GROUPS = NA_HEADS // HEADS_PER_GROUP
NA_LOCAL_KEYS = NA_WIN_ROWS * GRID_W
NA_ROWS_PER_STEP = 4


def _stack_heads(x, width):
    lane = lax.broadcasted_iota(jnp.int32, x.shape, 1) // width
    n_heads = x.shape[1] // width
    return jnp.concatenate([jnp.where(lane == h, x, jnp.zeros_like(x)) for h in range(n_heads)], axis=0)


def _unstack_heads(o, width):
    n_heads = o.shape[1] // width
    r = o.shape[0] // n_heads
    lane = lax.broadcasted_iota(jnp.int32, (r, o.shape[1]), 1) // width
    acc = jnp.zeros((r, o.shape[1]), o.dtype)
    for h in range(n_heads):
        acc = jnp.where(lane == h, o[h * r:(h + 1) * r, :], acc)
    return acc


def _na_kernel(q_ref, k_ref, v_ref, kc_ref, vc_ref, *rest, n_rows):
    bias_refs, o_ref = rest[:-1], rest[-1]
    rows_per_step = len(bias_refs)
    step = pl.program_id(1)

    def attend(g, j, local):
        r = step * rows_per_step + j
        rows = slice(j * GRID_W, (j + 1) * GRID_W)
        cols = slice(g * MXU_DIM, (g + 1) * MXU_DIM)
        qs = _stack_heads(q_ref[0, rows, cols], NA_HEAD_DIM)
        kc = kc_ref[0, :, cols]
        vc = vc_ref[0, :, cols]
        s_ctx = _dot_nt(qs, kc)
        m = jnp.max(s_ctx, axis=-1, keepdims=True)
        if local:
            start = pl.multiple_of(jnp.clip(r - NA_WIN_ROWS // 2, 0, n_rows - NA_WIN_ROWS) * GRID_W, GRID_W)
            kw = k_ref[0, pl.ds(start, NA_LOCAL_KEYS), cols]
            vw = v_ref[0, pl.ds(start, NA_LOCAL_KEYS), cols]
            s_loc = _dot_nt(qs, kw) + bias_refs[j][0, g]
            m = jnp.maximum(m, jnp.max(s_loc, axis=-1, keepdims=True))
            p_loc = jnp.exp(s_loc - m)
        p_ctx = jnp.exp(s_ctx - m)
        den = jnp.sum(p_ctx, axis=-1, keepdims=True)
        o = _dot(p_ctx.astype(BF16), vc)
        if local:
            den = den + jnp.sum(p_loc, axis=-1, keepdims=True)
            o = o + _dot(p_loc.astype(BF16), vw)
        o_ref[0, rows, cols] = _unstack_heads(o / den, NA_HEAD_DIM).astype(o_ref.dtype)

    @pl.when(step * rows_per_step < n_rows)
    def _():
        for j in range(rows_per_step):
            for g in range(NA_GROUPS):
                attend(g, j, True)

    @pl.when(step * rows_per_step >= n_rows)
    def _():
        for j in range(rows_per_step):
            for g in range(NA_GROUPS):
                attend(g, j, False)


def _na_bias_table(rpb):
    o = jnp.arange(NA_WIN_ROWS)
    dr = o[None, :] - o[:, None] + NA_WIN_ROWS - 1
    col = jnp.arange(GRID_W)
    dc = jnp.clip(col[None, :] - col[:, None], -(NA_WIN_COLS - 1), NA_WIN_COLS - 1) + NA_WIN_COLS - 1
    col_start = jnp.clip(col - NA_WIN_COLS // 2, 0, GRID_W - NA_WIN_COLS)
    col_mask = (col[None, :] >= col_start[:, None]) & (col[None, :] < col_start[:, None] + NA_WIN_COLS)
    onehot = (dc[None] == jnp.arange(2 * NA_WIN_COLS - 1)[:, None, None]).astype(F32)
    t = jnp.einsum('howc,cqk->howqk', rpb[:, dr].astype(F32), onehot,
                   precision=lax.Precision.HIGHEST)
    t = jnp.where(col_mask[None, None, None], t, MASK_VALUE)
    t = t.transpose(1, 0, 3, 2, 4)
    return t.reshape(NA_WIN_ROWS, NA_GROUPS, HEADS_PER_GROUP * GRID_W, NA_LOCAL_KEYS)


def _neighborhood_attention(qkv, bias_tab, n_lat_tiles, with_ctx_queries):
    bsz, s, _ = qkv.shape
    seq = n_lat_tiles * TM
    n_rows = seq // GRID_W
    assert n_rows >= NA_WIN_ROWS
    n_ctx_rows = (s - seq) // GRID_W
    rq = NA_ROWS_PER_STEP
    assert n_rows % rq == 0 and n_ctx_rows % rq == 0
    nq = (n_rows + (n_ctx_rows if with_ctx_queries else 0)) // rq
    ctx_blk = seq // (s - seq)

    def bias_spec(j):
        def idx(b, i):
            rr = jnp.minimum(i * rq + j, n_rows - 1)
            return (rr - jnp.clip(rr - NA_WIN_ROWS // 2, 0, n_rows - NA_WIN_ROWS), 0, 0, 0)
        return pl.BlockSpec((1,) + bias_tab.shape[1:], idx)

    return pl.pallas_call(
        functools.partial(_na_kernel, n_rows=n_rows),
        grid=(bsz, nq),
        in_specs=[
            pl.BlockSpec((1, rq * GRID_W, NA_WIDTH), lambda b, i: (b, i, 0)),
            pl.BlockSpec((1, seq, NA_WIDTH), lambda b, i: (b, 0, 1)),
            pl.BlockSpec((1, seq, NA_WIDTH), lambda b, i: (b, 0, 2)),
            pl.BlockSpec((1, s - seq, NA_WIDTH), lambda b, i: (b, ctx_blk, 1)),
            pl.BlockSpec((1, s - seq, NA_WIDTH), lambda b, i: (b, ctx_blk, 2)),
        ] + [bias_spec(j) for j in range(rq)],
        out_specs=pl.BlockSpec((1, rq * GRID_W, NA_WIDTH), lambda b, i: (b, i, 0)),
        out_shape=jax.ShapeDtypeStruct((bsz, nq * rq * GRID_W, NA_WIDTH), BF16),
        compiler_params=_params("parallel", "arbitrary"),
        name="neighborhood_attention",
    )(qkv, qkv, qkv, qkv, qkv, *([bias_tab] * rq))


def _pool_kernel(up_ref, u_ref, un_ref, w_ref, sc_ref, o_ref, buf, s0, s1, s2, *, n_lat_tiles, seq, ctx_len):
    i = pl.program_id(1)
    hw = POOL_REACH
    n = TM + 2 * hw
    is_ctx = i >= n_lat_tiles
    base = jnp.where(is_ctx, 0, i * TM)
    lseq = jnp.where(is_ctx, ctx_len, seq)
    buf[0:hw] = up_ref[0]
    buf[hw:hw + TM] = u_ref[0]
    buf[hw + TM:n] = un_ref[0]
    p = base - hw + lax.broadcasted_iota(jnp.int32, (n, POOL_WIDTH), 0)
    buf[...] = jnp.where((p >= 0) & (p < lseq), buf[...], 0.0)
    s0[0:n - 1] = buf[0:n - 1] + buf[1:n]
    s1[0:n - 3] = s0[0:n - 3] + s0[2:n - 1]
    s2[0:n - 7] = s1[0:n - 7] + s1[4:n - 3]
    shape = (TM, POOL_WIDTH)
    t = base + lax.broadcasted_iota(jnp.int32, shape, 0)
    grp = lax.broadcasted_iota(jnp.int32, shape, 1) // POOL_GROUP_DIM
    acc = jnp.where(grp == 0, s0[hw - 1:hw - 1 + TM],
                    jnp.where(grp == 1, s1[hw - 2:hw - 2 + TM],
                              jnp.where(grp == 2, s2[hw - 4:hw - 4 + TM], s2[0:TM] + s2[hw:hw + TM])))
    win = jnp.left_shift(2, grp)
    back = win // 2
    fwd = win - back - 1
    lo = jnp.clip(t - back, 0, lseq - 1)
    hi = jnp.clip(t + fwd, 0, lseq - 1)
    mean = acc / (hi - lo + 1).astype(F32)
    diff = (mean - u_ref[0]).astype(BF16)
    o_ref[0] = (_dot(diff, w_ref[...]) * sc_ref[...]).astype(o_ref.dtype)


def _multiscale_pool(u, w_bd, scale, n_lat_tiles, n_tiles):
    bsz, s, c = u.shape
    hw = POOL_REACH
    per = TM // hw
    last = s // hw - 1
    return pl.pallas_call(
        functools.partial(_pool_kernel, n_lat_tiles=n_lat_tiles, seq=n_lat_tiles * TM, ctx_len=s - n_lat_tiles * TM),
        grid=(bsz, n_tiles),
        in_specs=[
            pl.BlockSpec((1, hw, c), lambda b, i: (b, jnp.maximum(i * per - 1, 0), 0)),
            pl.BlockSpec((1, TM, c), lambda b, i: (b, i, 0)),
            pl.BlockSpec((1, hw, c), lambda b, i: (b, jnp.minimum((i + 1) * per, last), 0)),
            pl.BlockSpec((c, c), lambda b, i: (0, 0)),
            pl.BlockSpec((1, c), lambda b, i: (0, 0)),
        ],
        out_specs=pl.BlockSpec((1, TM, c), lambda b, i: (b, i, 0)),
        out_shape=jax.ShapeDtypeStruct((bsz, n_tiles * TM, c), BF16),
        scratch_shapes=[pltpu.VMEM((TM + 2 * hw, c), F32)] * 4,
        compiler_params=_params("parallel", "parallel"),
        name="multiscale_pool",
    )(u, u, u, w_bd, scale)


def _gla_tile(qk_ref, v_ref, la_ref, cs_ref, sn_ref, st_ref, o_ref, reverse, g):
    ch = GLA_CHUNK
    lane = lax.broadcasted_iota(jnp.int32, (ch, GLA_QK), 1)
    row = lax.broadcasted_iota(jnp.int32, (ch, GLA_QK), 0)
    rr = lax.broadcasted_iota(jnp.int32, (GLA_HEADS * ch, ch), 0) % ch
    cc = lax.broadcasted_iota(jnp.int32, (GLA_HEADS * ch, ch), 1)
    causal = (rr <= cc) if reverse else (rr >= cc)
    st_shape = (GLA_WIDTH, GLA_QK)
    head_blk = (lax.broadcasted_iota(jnp.int32, st_shape, 0) // GLA_DV
                == lax.broadcasted_iota(jnp.int32, st_shape, 1) // GLA_DK)
    quarter = GLA_DK // 4
    first_half = (lane % (2 * quarter)) < quarter

    def rope(x, cs, sn):
        partner = jnp.where(first_half, pltpu.roll(x, GLA_QK - quarter, 1), pltpu.roll(x, quarter, 1))
        return x * cs + partner * sn

    chunks = range(TM // ch)
    for c in (reversed(chunks) if reverse else chunks):
        sl = slice(c * ch, (c + 1) * ch)
        cs = cs_ref[sl, :]
        sn = sn_ref[sl, :]
        q = rope(qk_ref[g, sl, 0:GLA_QK], cs, sn)
        k = rope(qk_ref[g, sl, GLA_QK:2 * GLA_QK], cs, sn)
        b = la_ref[g, sl, :]
        d = 1
        while d < ch:
            if reverse:
                b = b + jnp.where(row < ch - d, pltpu.roll(b, ch - d, 0), 0.0)
            else:
                b = b + jnp.where(row >= d, pltpu.roll(b, d, 0), 0.0)
            d *= 2
        b_last = b[0:1, :] if reverse else b[ch - 1:ch, :]
        q_in = q * jnp.exp(b)
        k_in = (k * jnp.exp(-b)).astype(BF16)
        k_st = (k * jnp.exp(b_last - b)).astype(BF16)
        att = _dot_nt(_stack_heads(q_in, GLA_DK).astype(BF16), k_in)
        att = jnp.where(causal, att, 0.0).astype(BF16)
        vb = v_ref[g, sl, :].astype(BF16)
        o = _unstack_heads(_dot(att, vb), GLA_DV)
        st = st_ref[g]
        o_ref[g, sl, :] = o + _dot_nt(q_in.astype(BF16), st.astype(BF16))
        st_ref[g] = st * jnp.exp(b_last) + jnp.where(head_blk, _dot_tn(vb, k_st), 0.0)


def _gla_kernel(qkf, vf, laf, csf, snf, qkb, vb, lab, csb, snb, of_ref, ob_ref, st_f, st_b):
    @pl.when(pl.program_id(1) == 0)
    def _():
        st_f[...] = jnp.zeros_like(st_f)
        st_b[...] = jnp.zeros_like(st_b)

    for g in range(st_f.shape[0]):
        _gla_tile(qkf, vf, laf, csf, snf, st_f, of_ref, False, g)
        _gla_tile(qkb, vb, lab, csb, snb, st_b, ob_ref, True, g)


def _rope_tables(seq, s):
    quarter = GLA_DK // 4
    t = np.arange(s)
    pos_row = np.where(t < seq, t // GRID_W, 0).astype(np.float32)
    pos_col = np.where(t < seq, t % GRID_W, 0).astype(np.float32)
    d = np.arange(GLA_QK) % GLA_DK
    inv_freq = ROPE_BASE ** (-jnp.arange(quarter, dtype=F32) / quarter)
    freq = inv_freq[d % quarter]
    pos = jnp.where((d < GLA_DK // 2)[None, :], pos_row[:, None], pos_col[:, None])
    ang = pos * freq[None, :]
    sign = np.where((d % (2 * quarter)) < quarter, -1.0, 1.0).astype(np.float32)
    return jnp.cos(ang), jnp.sin(ang) * sign[None, :]


def _gla(gg, cos, sin, n_lat_tiles):
    bsz, s, _ = gg.shape
    nt = s // TM
    assert nt == n_lat_tiles + 1
    gb = GLA_BATCH if bsz % GLA_BATCH == 0 else 1
    fwd = lambda i: (i + n_lat_tiles) % nt
    bwd = lambda i: jnp.where(i == 0, n_lat_tiles, n_lat_tiles - i)
    la_f_blk = (2 * GLA_QK + 2 * GLA_WIDTH) // GLA_QK
    la_b_blk = la_f_blk + 1
    v_blk = 2 * GLA_QK // GLA_WIDTH

    def specs(order, la_blk):
        return [
            pl.BlockSpec((gb, TM, 2 * GLA_QK), lambda b, i: (b, order(i), 0)),
            pl.BlockSpec((gb, TM, GLA_WIDTH), lambda b, i: (b, order(i), v_blk)),
            pl.BlockSpec((gb, TM, GLA_QK), lambda b, i: (b, order(i), la_blk)),
            pl.BlockSpec((TM, GLA_QK), lambda b, i: (order(i), 0)),
            pl.BlockSpec((TM, GLA_QK), lambda b, i: (order(i), 0)),
        ]

    return pl.pallas_call(
        _gla_kernel,
        grid=(bsz // gb, nt),
        in_specs=specs(fwd, la_f_blk) + specs(bwd, la_b_blk),
        out_specs=[
            pl.BlockSpec((gb, TM, GLA_WIDTH), lambda b, i: (b, fwd(i), 0)),
            pl.BlockSpec((gb, TM, GLA_WIDTH), lambda b, i: (b, bwd(i), 0)),
        ],
        out_shape=[jax.ShapeDtypeStruct((bsz, s, GLA_WIDTH), F32)] * 2,
        scratch_shapes=[pltpu.VMEM((gb, GLA_WIDTH, GLA_QK), F32)] * 2,
        compiler_params=_params("arbitrary", "arbitrary"),
        name="gla_bidir",
    )(gg, gg, gg, cos, sin, gg, gg, gg, cos, sin)


def _store_row_tiles(ref, val):
    rows, width = val.shape
    pieces = width // LANES
    for s in range(pieces):
        ref[pl.ds(s, rows, stride=pieces), :] = val[:, s * LANES:(s + 1) * LANES]


def _load_row_tiles(ref, rows, pieces):
    return jnp.concatenate([ref[pl.ds(s, rows, stride=pieces), :] for s in range(pieces)], axis=1)


SLOTS_PER_Y_TILE = 2


def _pack_bf16_pairs(y):
    half = y.shape[1] // 2
    bits = lax.bitcast_convert_type(y.astype(jnp.bfloat16).astype(F32), jnp.uint32)
    return (bits[:, :half] >> 16) | (bits[:, half:] & jnp.uint32(0xFFFF0000))


def _unpack_bf16_pairs(words):
    lo = lax.bitcast_convert_type(words << 16, F32)
    hi = lax.bitcast_convert_type(words & jnp.uint32(0xFFFF0000), F32)
    return jnp.concatenate([lo, hi], axis=1)


def _row_index(shape):
    return lax.broadcasted_iota(jnp.int32, shape, 0).astype(F32)


def _stack_rows(rows):
    shape = (len(rows), rows[0].shape[1])
    rid = lax.broadcasted_iota(jnp.int32, shape, 0)
    out = jnp.zeros(shape, rows[0].dtype)
    for k, r in enumerate(rows):
        out = jnp.where(rid == k, r, out)
    return out


def _select_experts(scores, biased):
    n_exp, n_tok = biased.shape
    per = n_exp // N_GROUPS
    neg = -jnp.inf
    sub = _row_index((per, n_tok))
    grp_rows = []
    for g in range(N_GROUPS):
        blk = biased[g * per:(g + 1) * per, :]
        m1 = jnp.max(blk, axis=0, keepdims=True)
        i1 = jnp.min(jnp.where(blk == m1, sub, float(per)), axis=0, keepdims=True)
        m2 = jnp.max(jnp.where(sub == i1, neg, blk), axis=0, keepdims=True)
        grp_rows.append(m1 + m2)
    cur = _stack_rows(grp_rows)
    gid = _row_index(cur.shape)
    picked = jnp.zeros(cur.shape, F32)
    for _ in range(TOPK_GROUPS):
        gm = jnp.max(cur, axis=0, keepdims=True)
        gi = jnp.min(jnp.where(cur == gm, gid, float(N_GROUPS)), axis=0, keepdims=True)
        hit = gid == gi
        picked = jnp.where(hit, 1.0, picked)
        cur = jnp.where(hit, neg, cur)
    cur = jnp.concatenate(
        [jnp.where(picked[g:g + 1, :] > 0.0, biased[g * per:(g + 1) * per, :], neg) for g in range(N_GROUPS)], axis=0)
    eid = _row_index(cur.shape)
    ids, vals = [], []
    chosen = jnp.zeros(cur.shape, F32)
    for _ in range(TOP_K):
        m = jnp.max(cur, axis=0, keepdims=True)
        idx = jnp.min(jnp.where(cur == m, eid, float(n_exp)), axis=0, keepdims=True)
        hit = eid == idx
        ids.append(idx)
        vals.append(jnp.sum(jnp.where(hit, scores, 0.0), axis=0, keepdims=True))
        chosen = jnp.where(hit, 1.0, chosen)
        cur = jnp.where(hit, neg, cur)
    return ids, vals, chosen


def _outproj_kernel(x_ref, xc_ref, oa_ref, ob_ref, of_ref, obk_ref, r_ref, gg_ref, wo_ref, mod_ref, gf_ref, wr_ref, br_ref,
                    tri_ref, xo_ref, h_ref, e_ref, w_ref, rk_ref, hist_ref, *, n_lat_tiles):
    o = of_ref[0] + obk_ref[0]
    head = lax.broadcasted_iota(jnp.int32, o.shape, 1) // GLA_DV
    o2 = o * o
    rs = jnp.zeros_like(o)
    for h in range(GLA_HEADS):
        ssq = jnp.sum(jnp.where(head == h, o2, 0.0), axis=-1, keepdims=True)
        rs = jnp.where(head == h, lax.rsqrt(ssq / GLA_DV + NORM_EPS), rs)
    oc = (o * rs * gg_ref[...] * _silu(r_ref[0])).astype(BF16)
    c1 = NA_WIDTH + POOL_WIDTH
    acc = _dot(oa_ref[0], wo_ref[0:NA_WIDTH]) + _dot(ob_ref[0], wo_ref[NA_WIDTH:c1]) + _dot(oc, wo_ref[c1:])
    x = _stream_tile(x_ref, xc_ref, n_lat_tiles) + mod_ref[0, 0, 2:3, :] * acc
    xo_ref[0] = x
    h = _rmsnorm(x, gf_ref[...]) * (1.0 + mod_ref[0, 0, 4:5, :]) + mod_ref[0, 0, 3:4, :]
    _store_row_tiles(h_ref, h)

    scores = _sigmoid(_dot_nt(wr_ref[...], h.astype(BF16)))
    ids, vals, chosen = _select_experts(scores, scores + br_ref[...])
    total = vals[0]
    for v in vals[1:]:
        total = total + v
    w_ref[...] = _stack_rows([v / total * ROUTED_SCALE for v in vals])
    e_ref[...] = _stack_rows(ids).astype(jnp.int32)
    chosen_b = chosen.astype(BF16)
    before = _dot(chosen_b, tri_ref[...])
    eid = _row_index(before.shape)
    rk_ref[...] = _stack_rows([jnp.sum(jnp.where(eid == idx, before, 0.0), axis=0, keepdims=True) for idx in ids])
    hist_ref[0] = _dot_nt(jnp.ones((SUBLANES, chosen.shape[1]), BF16), chosen_b)


def _outproj(x_lat, x_ctx, ctx_blk, oa, ob, o_f, o_b, gg, g_gla4, w_out, mod, g_ffn, w_router_t, b_router,
             n_lat_tiles, n_tiles):
    bsz, _, d = x_lat.shape
    n_exp = w_router_t.shape[0]
    r_blk = (2 * GLA_QK + GLA_WIDTH) // GLA_WIDTH
    tile = lambda w: pl.BlockSpec((1, TM, w), lambda b, i: (b, i, 0))
    const = lambda b, i: (0, 0)
    per_tok = pl.BlockSpec((TOP_K, TM), lambda b, i: (0, b * n_tiles + i))
    t_tok = bsz * n_tiles * TM
    tri = (jnp.arange(TM)[:, None] < jnp.arange(TM)[None, :]).astype(BF16)
    return pl.pallas_call(
        functools.partial(_outproj_kernel, n_lat_tiles=n_lat_tiles),
        grid=(bsz, n_tiles),
        in_specs=[
            *_stream_specs(d, n_lat_tiles, ctx_blk),
            tile(NA_WIDTH), tile(POOL_WIDTH), tile(GLA_WIDTH), tile(GLA_WIDTH),
            pl.BlockSpec((1, TM, GLA_WIDTH), lambda b, i: (b, i, r_blk)),
            pl.BlockSpec((1, GLA_WIDTH), const),
            pl.BlockSpec(w_out.shape, const),
            pl.BlockSpec((1, 1, SUBLANES, d), lambda b, i: (b, jnp.where(i >= n_lat_tiles, 0, 1), 0, 0)),
            pl.BlockSpec((1, d), const),
            pl.BlockSpec(w_router_t.shape, const),
            pl.BlockSpec((n_exp, 1), const),
            pl.BlockSpec((TM, TM), const),
        ],
        out_specs=[tile(d), pl.BlockSpec((TM * d // LANES, LANES), lambda b, i: (b * n_tiles + i, 0)),
                   per_tok, per_tok, per_tok,
                   pl.BlockSpec((1, SUBLANES, n_exp), lambda b, i: (b * n_tiles + i, 0, 0))],
        out_shape=[
            jax.ShapeDtypeStruct((bsz, n_tiles * TM, d), F32),
            jax.ShapeDtypeStruct((t_tok * d // LANES, LANES), F32),
            jax.ShapeDtypeStruct((TOP_K, t_tok), jnp.int32),
            jax.ShapeDtypeStruct((TOP_K, t_tok), F32),
            jax.ShapeDtypeStruct((TOP_K, t_tok), F32),
            jax.ShapeDtypeStruct((bsz * n_tiles, SUBLANES, n_exp), F32),
        ],
        compiler_params=_params("parallel", "parallel"),
        name="outproj_router",
    )(x_lat, x_ctx, oa, ob, o_f, o_b, gg, g_gla4, w_out, mod, g_ffn, w_router_t, b_router, tri)


def _slot_layout(hist):
    n_tiles, n_exp = hist.shape
    counts = hist.sum(0)
    tile_base = jnp.cumsum(hist, axis=0) - hist
    padded = (counts + MOE_BLOCK - 1) // MOE_BLOCK * MOE_BLOCK
    pad_end = jnp.cumsum(padded)
    pad_start = pad_end - padded
    base = (pad_start[None, :] + tile_base).astype(F32).reshape(n_tiles, n_exp, 1)
    n_blocks = -(-(n_tiles * TM * TOP_K) // MOE_BLOCK) + n_exp
    n_used = (pad_end[-1] // MOE_BLOCK).astype(jnp.int32)
    first_blk = (pad_start // MOE_BLOCK).astype(jnp.int32)
    blk_count = (padded // MOE_BLOCK).astype(jnp.int32)
    pad_row = (pad_start + counts).astype(jnp.int32)
    pad_cnt = (padded - counts).astype(jnp.int32)
    return base, first_blk, blk_count, pad_row, pad_cnt, n_used.reshape(1), n_blocks


def _pos_kernel(e_ref, rk_ref, base_ref, posx_ref, posy_ref, half_ref, *, pieces):
    eid = lax.broadcasted_iota(jnp.int32, (base_ref.shape[1], TM), 0)
    for j in range(base_ref.shape[0]):
        cols = slice(j * TM, (j + 1) * TM)
        base = base_ref[j]
        rows = [jnp.sum(jnp.where(eid == e_ref[k:k + 1, cols], base, 0.0), axis=0, keepdims=True)
                for k in range(TOP_K)]
        slot = (_stack_rows(rows) + rk_ref[:, cols]).astype(jnp.int32)
        posx_ref[j] = slot * pieces
        posy_ref[j] = (slot // SLOTS_PER_Y_TILE) * pieces
        half_ref[:, cols] = slot % SLOTS_PER_Y_TILE


def _slot_positions(e_t, rank_t, base, pieces):
    n_tiles, n_exp, _ = base.shape
    per_step = max(p for p in range(1, DISPATCH_TILES + 1) if n_tiles % p == 0)
    per_tok = pl.BlockSpec((TOP_K, per_step * TM), lambda i: (0, i))
    flat = pl.BlockSpec((per_step, TOP_K, TM), lambda i: (i, 0, 0))
    pos_x, pos_y, half = pl.pallas_call(
        functools.partial(_pos_kernel, pieces=pieces),
        grid=(n_tiles // per_step,),
        in_specs=[per_tok, per_tok, pl.BlockSpec((per_step, n_exp, 1), lambda i: (i, 0, 0))],
        out_specs=[flat, flat, per_tok],
        out_shape=[jax.ShapeDtypeStruct((n_tiles, TOP_K, TM), jnp.int32)] * 2
        + [jax.ShapeDtypeStruct(e_t.shape, jnp.int32)],
        compiler_params=_params("parallel"),
        name="moe_positions",
    )(e_t, rank_t, base)
    return pos_x.reshape(-1), pos_y.reshape(-1), half


def _dispatch_kernel(pad_row_ref, pad_cnt_ref, nu_ref, pos_ref, h_ref, xs_hbm, zbuf, sem, zsem, *, n_blocks):
    step = pl.program_id(0)
    blk_rows = zbuf.shape[0]
    row_pieces = blk_rows // MOE_BLOCK

    def pad_copies(e, act):
        row = pad_row_ref[e]
        cnt = pad_cnt_ref[e]
        bit = MOE_BLOCK // 2
        while bit:
            @pl.when((cnt & bit) != 0)
            def _(row=row, bit=bit):
                dst = xs_hbm.at[pl.ds(pl.multiple_of(row * row_pieces, row_pieces), bit * row_pieces), :]
                act(pltpu.make_async_copy(zbuf.at[pl.ds(0, bit * row_pieces), :], dst, zsem))
            row = row + (cnt & bit)
            bit //= 2

    def tail_copy(g):
        return pltpu.make_async_copy(zbuf, xs_hbm.at[pl.ds(pl.multiple_of(g * blk_rows, blk_rows), blk_rows), :], zsem)

    def zero_rows(act):
        def per_expert(e, c):
            pad_copies(e, act)
            return c
        lax.fori_loop(0, pad_row_ref.shape[0], per_expert, 0)

        def per_block(g, c):
            act(tail_copy(g))
            return c
        lax.fori_loop(nu_ref[0], n_blocks, per_block, 0)

    @pl.when(step == 0)
    def _():
        zbuf[...] = jnp.zeros_like(zbuf)
        zero_rows(lambda cp: cp.start())

    tiles = pos_ref.shape[0] // (TOP_K * TM)
    pieces = h_ref.shape[0] // (tiles * TM)

    for j in range(tiles):
        def body(t, c):
            src = h_ref.at[pl.ds(pl.multiple_of((j * TM + t) * pieces, pieces), pieces), :]
            for k in range(TOP_K):
                dst = xs_hbm.at[pl.ds(pl.multiple_of(pos_ref[(j * TOP_K + k) * TM + t], pieces), pieces), :]
                pltpu.make_async_copy(src, dst, sem).start(priority=k % 2)
            return c
        lax.fori_loop(0, TM, body, 0, unroll=2)
    for _ in range(TOP_K):
        pltpu.make_async_copy(h_ref, xs_hbm.at[pl.ds(0, h_ref.shape[0]), :], sem).wait()

    @pl.when(step == pl.num_programs(0) - 1)
    def _():
        zero_rows(lambda cp: cp.wait())


def _dispatch(h2t, pos, pad_row, pad_cnt, n_used, n_blocks, pieces):
    n_tiles = h2t.shape[0] // (TM * pieces)
    per_step = max(p for p in range(1, DISPATCH_TILES + 1) if n_tiles % p == 0)
    grid_spec = pltpu.PrefetchScalarGridSpec(
        num_scalar_prefetch=3,
        grid=(n_tiles // per_step,),
        in_specs=[
            pl.BlockSpec((per_step * TOP_K * TM,), lambda i, *_: (i,), memory_space=pltpu.SMEM),
            pl.BlockSpec((per_step * TM * pieces, LANES), lambda i, *_: (i, 0)),
        ],
        out_specs=pl.BlockSpec(memory_space=pl.ANY),
        scratch_shapes=[pltpu.VMEM((MOE_BLOCK * pieces, LANES), F32), pltpu.SemaphoreType.DMA,
                        pltpu.SemaphoreType.DMA],
    )
    return pl.pallas_call(
        functools.partial(_dispatch_kernel, n_blocks=n_blocks),
        grid_spec=grid_spec,
        out_shape=jax.ShapeDtypeStruct((n_blocks * MOE_BLOCK * pieces, LANES), F32),
        compiler_params=_params("arbitrary"),
        name="moe_dispatch",
    )(pad_row, pad_cnt, n_used, pos, h2t)


def _expert_kernel(first_ref, cnt_ref, nu_ref, wi_ref, wd_ref, xs_hbm, ys_hbm, xbuf, ybuf, wi_bf, wd_bf, xsem, ysem,
                   *, n_blocks):
    e = pl.program_id(0)
    n_used = nu_ref[0]
    rows = xbuf.shape[1]
    pieces = rows // MOE_BLOCK
    d_exp = wd_ref.shape[2]

    def x_copy(g, s):
        return pltpu.make_async_copy(xs_hbm.at[pl.ds(pl.multiple_of(g * rows, rows), rows), :], xbuf.at[s], xsem.at[s])

    y_rows = ybuf.shape[1]

    def y_copy(g, s):
        return pltpu.make_async_copy(ybuf.at[s], ys_hbm.at[pl.ds(pl.multiple_of(g * y_rows, y_rows), y_rows), :],
                                     ysem.at[s])

    n_xbuf = xbuf.shape[0]
    n_ybuf = ybuf.shape[0]

    @pl.when(e == 0)
    def _():
        for g in range(n_xbuf - 1):
            @pl.when(g < n_used)
            def _():
                x_copy(g, g).start()

    @pl.when(cnt_ref[e] > 0)
    def _():
        wi_bf[...] = wi_ref[0, 0].astype(BF16)
        wd_bf[...] = wd_ref[0, 0].astype(BF16)

    def block(j, c):
        g = first_ref[e] + j
        s = g % n_ybuf
        x_copy(g, g % n_xbuf).wait()

        @pl.when(g + n_xbuf - 1 < n_used)
        def _():
            x_copy(g + n_xbuf - 1, (g + n_xbuf - 1) % n_xbuf).start()

        @pl.when(g >= n_ybuf)
        def _():
            y_copy(g - n_ybuf, s).wait()

        hh = _dot(_load_row_tiles(xbuf.at[g % n_xbuf], MOE_BLOCK, pieces).astype(BF16), wi_bf[...])
        a = (_silu(hh[:, :d_exp]) * hh[:, d_exp:]).astype(BF16)
        _store_row_tiles(ybuf.at[s], _pack_bf16_pairs(_dot(a, wd_bf[...])))
        y_copy(g, s).start(priority=1)
        return c
    lax.fori_loop(0, cnt_ref[e], block, 0)

    @pl.when(e == pl.num_programs(0) - 1)
    def _():
        for back in range(1, n_ybuf + 1):
            @pl.when(n_used >= back)
            def _():
                y_copy(n_used - back, (n_used - back) % n_ybuf).wait()

        ybuf[0] = jnp.zeros(ybuf.shape[1:], ybuf.dtype)

        def start(g, c):
            y_copy(g, 0).start()
            return c
        lax.fori_loop(n_used, n_blocks, start, 0)

        def wait(g, c):
            y_copy(g, 0).wait()
            return c
        lax.fori_loop(n_used, n_blocks, wait, 0)


def _moe_experts(xs, first_blk, blk_count, n_used, n_blocks, w_e_in, w_e_down, layer):
    _, n_exp, d, f2 = w_e_in.shape
    rows = xs.shape[0] // n_blocks
    grid_spec = pltpu.PrefetchScalarGridSpec(
        num_scalar_prefetch=3,
        grid=(n_exp,),
        in_specs=[
            pl.BlockSpec((1, 1, d, f2), lambda e, first, cnt, nu: (layer, e, 0, 0)),
            pl.BlockSpec((1, 1, f2 // 2, d), lambda e, first, cnt, nu: (layer, e, 0, 0)),
            pl.BlockSpec(memory_space=pl.ANY),
        ],
        out_specs=pl.BlockSpec(memory_space=pl.ANY),
        scratch_shapes=[
            pltpu.VMEM((EXPERT_X_BUFFERS, rows, LANES), F32),
            pltpu.VMEM((EXPERT_Y_BUFFERS, rows // SLOTS_PER_Y_TILE, LANES), jnp.uint32),
            pltpu.VMEM((d, f2), BF16),
            pltpu.VMEM((f2 // 2, d), BF16),
            pltpu.SemaphoreType.DMA((EXPERT_X_BUFFERS,)),
            pltpu.SemaphoreType.DMA((EXPERT_Y_BUFFERS,)),
        ],
    )
    return pl.pallas_call(
        functools.partial(_expert_kernel, n_blocks=n_blocks),
        grid_spec=grid_spec,
        out_shape=jax.ShapeDtypeStruct((xs.shape[0] // SLOTS_PER_Y_TILE, LANES), jnp.uint32),
        compiler_params=_params("arbitrary"),
        name="moe_experts",
    )(first_blk, blk_count, n_used, w_e_in, w_e_down, xs)


def _combine_kernel(pos0_ref, posn_ref, x_ref, h_ref, w_ref, half_ref, wsi_ref, wsd_ref, mod_ref, gfin_ref, ys_hbm,
                    o_ref, ybuf, sem, *, final, n_steps):
    step = pl.program_id(0) * pl.num_programs(1) + pl.program_id(1)
    slot = step % 2
    d_exp = wsd_ref.shape[0]
    pieces = h_ref.shape[0] // TMC

    def gather(pos_ref, st, s):
        off = (st % (TM // TMC)) * TMC
        def body(t, c):
            for k in range(TOP_K):
                src = ys_hbm.at[pl.ds(pl.multiple_of(pos_ref[k * TM + off + t], pieces), pieces), :]
                dst = ybuf.at[s, k, pl.ds(pl.multiple_of(t * pieces, pieces), pieces), :]
                pltpu.make_async_copy(src, dst, sem.at[s]).start(priority=k % 2)
            return c
        lax.fori_loop(0, TMC, body, 0, unroll=2)

    @pl.when(step == 0)
    def _():
        gather(pos0_ref, 0, 0)

    @pl.when(step + 1 < n_steps)
    def _():
        gather(posn_ref, step + 1, 1 - slot)

    hh = _dot(_load_row_tiles(h_ref, TMC, pieces).astype(BF16), wsi_ref[...])
    a = (_silu(hh[:, :d_exp]) * hh[:, d_exp:]).astype(BF16)
    y = _dot(a, wsd_ref[...])
    for k in range(TOP_K):
        pltpu.make_async_copy(ys_hbm.at[pl.ds(0, TMC * pieces), :], ybuf.at[slot, k], sem.at[slot]).wait()
    w = w_ref[...]
    half = half_ref[...]
    for k in range(TOP_K):
        both = _load_row_tiles(ybuf.at[slot, k], TMC, pieces)
        mine = jnp.where(half[:, k:k + 1] > 0, both[:, both.shape[1] // 2:], both[:, :both.shape[1] // 2])
        y = y + _unpack_bf16_pairs(mine) * w[:, k:k + 1]
    x = x_ref[0] + mod_ref[0, 0, 5:6, :] * y
    if final:
        x = _rmsnorm(x, gfin_ref[...])
    o_ref[0] = x


def _combine(x_mid, h2t, ys, pos, w_tok, half_tok, w_sh_in, w_sh_down, mod, g_final, n_lat_tiles, final):
    bsz, s, d = x_mid.shape
    nt = s // TMC
    n_steps = bsz * nt
    halves = TM // TMC
    lat_steps = n_lat_tiles * halves
    pieces = d // LANES
    const = lambda b, i: (0, 0)
    tile = pl.BlockSpec((1, TMC, d), lambda b, i: (b, i, 0))
    smem_blk = lambda f: pl.BlockSpec((TOP_K * TM,), f, memory_space=pltpu.SMEM)
    return pl.pallas_call(
        functools.partial(_combine_kernel, final=final, n_steps=n_steps),
        grid=(bsz, nt),
        in_specs=[
            smem_blk(lambda b, i: (0,)),
            smem_blk(lambda b, i: (jnp.minimum(b * nt + i + 1, n_steps - 1) // halves,)),
            tile,
            pl.BlockSpec((TMC * pieces, LANES), lambda b, i: (b * nt + i, 0)),
            pl.BlockSpec((TMC, TOP_K), lambda b, i: (b * nt + i, 0)),
            pl.BlockSpec((TMC, TOP_K), lambda b, i: (b * nt + i, 0)),
            pl.BlockSpec(w_sh_in.shape, const),
            pl.BlockSpec(w_sh_down.shape, const),
            pl.BlockSpec((1, 1, SUBLANES, d), lambda b, i: (b, jnp.where(i >= lat_steps, 0, 1), 0, 0)),
            pl.BlockSpec((1, d), const),
            pl.BlockSpec(memory_space=pl.ANY),
        ],
        out_specs=tile,
        out_shape=jax.ShapeDtypeStruct((bsz, s, d), F32),
        scratch_shapes=[pltpu.VMEM((2, TOP_K, TMC * pieces, LANES), jnp.uint32), pltpu.SemaphoreType.DMA((2,))],
        compiler_params=_params("arbitrary", "arbitrary"),
        name="moe_combine",
    )(pos, pos, x_mid, h2t, w_tok, half_tok, w_sh_in, w_sh_down, mod, g_final, ys)


def _block_diag(w):
    g, a, b = w.shape
    out = jnp.zeros((g * a, g * b), w.dtype)
    for j in range(g):
        out = out.at[j * a:(j + 1) * a, j * b:(j + 1) * b].set(w[j])
    return out


def kernel(x, c, ctx, c_ctx, w_mod, b_mod, g_mix, w_in, rpb, w_pool, pool_scale, w_gate_f, b_gate_f, w_gate_b, b_gate_b, g_gla, w_out, g_ffn, w_router, b_router, w_e_in, w_e_down, w_sh_in, w_sh_down, g_final):
    bsz, seq, d = x.shape
    n_ctx = ctx.shape[1]
    depth = w_mod.shape[0]
    n_exp = w_router.shape[2]
    assert n_ctx == TM and seq % TM == 0 and seq % GRID_W == 0
    n_lat = seq // TM
    s = seq + n_ctx

    n_rows = -(-(bsz + 1) // SUBLANES) * SUBLANES
    cc = jnp.zeros((n_rows, d), F32).at[:bsz].set(c).at[bsz].set(c_ctx)
    mod_all = _modulation(cc, w_mod, b_mod).reshape(depth, n_rows, 6, d)
    cos, sin = _rope_tables(seq, s)

    x_lat, x_ctx, ctx_blk = x, ctx, 0
    for layer in range(depth):
        last = layer == depth - 1
        n_tiles = n_lat if last else n_lat + 1
        m = mod_all[layer]
        mod = jnp.stack([jnp.broadcast_to(m[bsz], (bsz, 6, d)), m[:bsz]], axis=1)
        mod = jnp.pad(mod, ((0, 0), (0, 0), (0, SUBLANES - 6), (0, 0)))
        wl = w_in[layer]
        w_main = wl[:, :MAIN_W].astype(BF16)
        w_low = jnp.pad(wl[:, MAIN_W:], ((0, 0), (0, LANES - 2 * GLA_GATE_RANK))).astype(BF16)
        w2 = jnp.zeros((LANES, 2 * GLA_QK), F32)
        w2 = w2.at[:GLA_GATE_RANK, :GLA_QK].set(w_gate_f[layer])
        w2 = w2.at[GLA_GATE_RANK:2 * GLA_GATE_RANK, GLA_QK:].set(w_gate_b[layer]).astype(BF16)
        b2 = jnp.concatenate([b_gate_f[layer], b_gate_b[layer]])[None, :]
        qkv, u, gg = _inproj(x_lat, x_ctx, ctx_blk, mod, g_mix[layer][None, :], w_main, w_low, w2, b2, n_lat)

        oa = _neighborhood_attention(qkv, _na_bias_table(rpb[layer]), n_lat, not last)
        ob = _multiscale_pool(u, _block_diag(w_pool[layer]).astype(BF16), pool_scale[layer][None, :], n_lat, n_tiles)
        o_f, o_b = _gla(gg, cos, sin, n_lat)

        x_mid, h2t, e_t, w_t, rank_t, hist = _outproj(
            x_lat, x_ctx, ctx_blk, oa, ob, o_f, o_b, gg, jnp.tile(g_gla[layer], GLA_HEADS)[None, :], w_out[layer].astype(BF16), mod,
            g_ffn[layer][None, :], w_router[layer].T.astype(BF16), b_router[layer].reshape(n_exp, 1), n_lat, n_tiles)
        pieces = d // LANES
        base, first_blk, blk_count, pad_row, pad_cnt, n_used, n_blocks = _slot_layout(
            hist[:, 0, :].astype(jnp.int32))
        pos_x, pos_y, half = _slot_positions(e_t, rank_t, base, pieces)
        xs = _dispatch(h2t, pos_x, pad_row, pad_cnt, n_used, n_blocks, pieces)
        ys = _moe_experts(xs, first_blk, blk_count, n_used, n_blocks, w_e_in, w_e_down, layer)
        xa = _combine(x_mid, h2t, ys, pos_y, w_t.T, half.T, w_sh_in[layer].astype(BF16),
                      w_sh_down[layer].astype(BF16), mod, g_final[None, :], n_lat, last)
        x_lat, x_ctx, ctx_blk = xa, xa, n_lat
    return xa
```

```python
import functools

import jax
import jax.numpy as jnp
import numpy as np
from jax import lax
from jax.experimental import pallas as pl
from jax.experimental.pallas import tpu as pltpu

GRID_W = 64
NORM_EPS = 1e-6
NA_HEADS = 8
NA_HEAD_DIM = 64
NA_WIDTH = NA_HEADS * NA_HEAD_DIM
NA_WIN_ROWS = 8
NA_WIN_COLS = 16
POOL_WINDOWS = (2, 4, 8, 16)
POOL_GROUP_DIM = 64
POOL_WIDTH = len(POOL_WINDOWS) * POOL_GROUP_DIM
POOL_REACH = max(POOL_WINDOWS) // 2
assert POOL_WINDOWS == tuple(2 ** (g + 1) for g in range(len(POOL_WINDOWS)))
GLA_HEADS = 4
GLA_DK = 32
GLA_DV = 64
GLA_QK = GLA_HEADS * GLA_DK
GLA_WIDTH = GLA_HEADS * GLA_DV
GLA_GATE_RANK = 16
GLA_TAU = 16.0
GLA_CHUNK = 64
GLA_BATCH = 4
ROPE_BASE = 10000.0
N_EXPERTS = 256
TOP_K = 8
N_GROUPS = 8
TOPK_GROUPS = 4
ROUTED_SCALE = 2.5

TM = 256
TMC = 256
DISPATCH_TILES = 8
EXPERT_X_BUFFERS = 6
EXPERT_Y_BUFFERS = 4
MOE_BLOCK = 256
LANES = 128
SUBLANES = 8
MXU_DIM = 256
MASK_VALUE = -1e30
V7X_VMEM_BYTES = 64 * 1024 * 1024
VMEM_LIMIT = V7X_VMEM_BYTES * 3 // 4

QKV_W = 3 * NA_WIDTH
G_W = 2 * GLA_QK + 2 * GLA_WIDTH + 2 * GLA_QK
MAIN_W = QKV_W + POOL_WIDTH + 2 * GLA_QK + 2 * GLA_WIDTH

BF16 = jnp.bfloat16
F32 = jnp.float32


def _params(*sem):
    return pltpu.CompilerParams(dimension_semantics=sem, vmem_limit_bytes=VMEM_LIMIT)


def _sigmoid(x):
    return 1.0 / (1.0 + jnp.exp(-x))


def _silu(x):
    return x * _sigmoid(x)


def _rmsnorm(x, g):
    return x * lax.rsqrt(jnp.mean(x * x, axis=-1, keepdims=True) + NORM_EPS) * g


def _dot(a, b):
    return jnp.dot(a, b, preferred_element_type=F32)


def _dot_nt(a, b):
    return lax.dot_general(a, b, (((1,), (1,)), ((), ())), preferred_element_type=F32)


def _dot_tn(a, b):
    return lax.dot_general(a, b, (((0,), (0,)), ((), ())), preferred_element_type=F32)


def _mod_kernel(c_ref, w_ref, b_ref, o_ref):
    a = _silu(c_ref[...]).astype(BF16)
    o_ref[0] = _dot(a, w_ref[0].astype(BF16)) + b_ref[0]


def _modulation(cc, w_mod, b_mod):
    depth, d, n = w_mod.shape
    r = cc.shape[0]
    tn = d
    return pl.pallas_call(
        _mod_kernel,
        grid=(depth, n // tn),
        in_specs=[
            pl.BlockSpec((r, d), lambda l, j: (0, 0)),
            pl.BlockSpec((1, d, tn), lambda l, j: (l, 0, j)),
            pl.BlockSpec((1, 1, tn), lambda l, j: (l, 0, j)),
        ],
        out_specs=pl.BlockSpec((1, r, tn), lambda l, j: (l, 0, j)),
        out_shape=jax.ShapeDtypeStruct((depth, r, n), F32),
        compiler_params=_params("parallel", "parallel"),
        name="modulation",
    )(cc, w_mod, b_mod.reshape(depth, 1, n))


def _stream_tile(x_ref, xc_ref, n_lat_tiles):
    return jnp.where(pl.program_id(1) >= n_lat_tiles, xc_ref[0], x_ref[0])


def _stream_specs(d, n_lat_tiles, ctx_blk):
    return [pl.BlockSpec((1, TM, d), lambda b, i: (b, jnp.minimum(i, n_lat_tiles - 1), 0)),
            pl.BlockSpec((1, TM, d), lambda b, i: (b, ctx_blk, 0))]


def _inproj_kernel(x_ref, xc_ref, mod_ref, g_ref, wm_ref, wl_ref, w2_ref, b2_ref, qkv_ref, u_ref, gg_ref, *, n_lat_tiles):
    x = _stream_tile(x_ref, xc_ref, n_lat_tiles)
    h = _rmsnorm(x, g_ref[...]) * (1.0 + mod_ref[0, 0, 1:2, :]) + mod_ref[0, 0, 0:1, :]
    hb = h.astype(BF16)
    q = _dot(hb, wm_ref[:, 0:NA_WIDTH]) * (NA_HEAD_DIM ** -0.5)
    qkv_ref[0, :, 0:NA_WIDTH] = q.astype(BF16)
    qkv_ref[0, :, NA_WIDTH:2 * NA_WIDTH] = _dot(hb, wm_ref[:, NA_WIDTH:2 * NA_WIDTH]).astype(BF16)
    qkv_ref[0, :, 2 * NA_WIDTH:QKV_W] = _dot(hb, wm_ref[:, 2 * NA_WIDTH:QKV_W]).astype(BF16)
    u_ref[0] = _dot(hb, wm_ref[:, QKV_W:QKV_W + POOL_WIDTH])
    c0 = QKV_W + POOL_WIDTH
    gg_ref[0, :, 0:GLA_QK] = _dot(hb, wm_ref[:, c0:c0 + GLA_QK]) * (GLA_DK ** -0.5)
    gg_ref[0, :, GLA_QK:2 * GLA_QK + 2 * GLA_WIDTH] = _dot(hb, wm_ref[:, c0 + GLA_QK:MAIN_W])
    a_low = _dot(hb, wl_ref[...]).astype(BF16)
    lg = _dot(a_low, w2_ref[...]) + b2_ref[...]
    log_sig = jnp.minimum(lg, 0.0) - jnp.log1p(jnp.exp(-jnp.abs(lg)))
    gg_ref[0, :, 2 * GLA_QK + 2 * GLA_WIDTH:G_W] = log_sig / GLA_TAU


def _inproj(x_lat, x_ctx, ctx_blk, mod, g, w_main, w_low, w2, b2, n_lat_tiles):
    bsz, _, d = x_lat.shape
    nt = n_lat_tiles + 1
    s = nt * TM
    const = lambda b, i: (0, 0)
    return pl.pallas_call(
        functools.partial(_inproj_kernel, n_lat_tiles=n_lat_tiles),
        grid=(bsz, nt),
        in_specs=_stream_specs(d, n_lat_tiles, ctx_blk) + [
            pl.BlockSpec((1, 1, SUBLANES, d), lambda b, i: (b, jnp.where(i >= n_lat_tiles, 0, 1), 0, 0)),
            pl.BlockSpec((1, d), const),
            pl.BlockSpec(w_main.shape, const),
            pl.BlockSpec(w_low.shape, const),
            pl.BlockSpec(w2.shape, const),
            pl.BlockSpec(b2.shape, const),
        ],
        out_specs=[
            pl.BlockSpec((1, TM, QKV_W), lambda b, i: (b, i, 0)),
            pl.BlockSpec((1, TM, POOL_WIDTH), lambda b, i: (b, i, 0)),
            pl.BlockSpec((1, TM, G_W), lambda b, i: (b, i, 0)),
        ],
        out_shape=[
            jax.ShapeDtypeStruct((bsz, s, QKV_W), BF16),
            jax.ShapeDtypeStruct((bsz, s, POOL_WIDTH), F32),
            jax.ShapeDtypeStruct((bsz, s, G_W), F32),
        ],
        compiler_params=_params("parallel", "parallel"),
        name="inproj",
    )(x_lat, x_ctx, mod, g, w_main, w_low, w2, b2)


HEADS_PER_GROUP = MXU_DIM // NA_HEAD_DIM
NA_GROUPS = NA_HEADS // HEADS_PER_GROUP
NA_LOCAL_KEYS = NA_WIN_ROWS * GRID_W
NA_ROWS_PER_STEP = 4


def _stack_heads(x, width):
    lane = lax.broadcasted_iota(jnp.int32, x.shape, 1) // width
    n_heads = x.shape[1] // width
    return jnp.concatenate([jnp.where(lane == h, x, jnp.zeros_like(x)) for h in range(n_heads)], axis=0)


def _unstack_heads(o, width):
    n_heads = o.shape[1] // width
    r = o.shape[0] // n_heads
    lane = lax.broadcasted_iota(jnp.int32, (r, o.shape[1]), 1) // width
    acc = jnp.zeros((r, o.shape[1]), o.dtype)
    for h in range(n_heads):
        acc = jnp.where(lane == h, o[h * r:(h + 1) * r, :], acc)
    return acc


def _na_kernel(q_ref, k_ref, v_ref, kc_ref, vc_ref, *rest, n_rows):
    bias_refs, o_ref = rest[:-1], rest[-1]
    rows_per_step = len(bias_refs)
    step = pl.program_id(1)

    def attend(g, j, local):
        r = step * rows_per_step + j
        rows = slice(j * GRID_W, (j + 1) * GRID_W)
        cols = slice(g * MXU_DIM, (g + 1) * MXU_DIM)
        qs = _stack_heads(q_ref[0, rows, cols], NA_HEAD_DIM)
        kc = kc_ref[0, :, cols]
        vc = vc_ref[0, :, cols]
        s_ctx = _dot_nt(qs, kc)
        m = jnp.max(s_ctx, axis=-1, keepdims=True)
        if local:
            start = pl.multiple_of(jnp.clip(r - NA_WIN_ROWS // 2, 0, n_rows - NA_WIN_ROWS) * GRID_W, GRID_W)
            kw = k_ref[0, pl.ds(start, NA_LOCAL_KEYS), cols]
            vw = v_ref[0, pl.ds(start, NA_LOCAL_KEYS), cols]
            s_loc = _dot_nt(qs, kw) + bias_refs[j][0, g]
            m = jnp.maximum(m, jnp.max(s_loc, axis=-1, keepdims=True))
            p_loc = jnp.exp(s_loc - m)
        p_ctx = jnp.exp(s_ctx - m)
        den = jnp.sum(p_ctx, axis=-1, keepdims=True)
        o = _dot(p_ctx.astype(BF16), vc)
        if local:
            den = den + jnp.sum(p_loc, axis=-1, keepdims=True)
            o = o + _dot(p_loc.astype(BF16), vw)
        o_ref[0, rows, cols] = _unstack_heads(o / den, NA_HEAD_DIM).astype(o_ref.dtype)

    @pl.when(step * rows_per_step < n_rows)
    def _():
        for j in range(rows_per_step):
            for g in range(NA_GROUPS):
                attend(g, j, True)

    @pl.when(step * rows_per_step >= n_rows)
    def _():
        for j in range(rows_per_step):
            for g in range(NA_GROUPS):
                attend(g, j, False)


def _na_bias_table(rpb):
    o = jnp.arange(NA_WIN_ROWS)
    dr = o[None, :] - o[:, None] + NA_WIN_ROWS - 1
    col = jnp.arange(GRID_W)
    dc = jnp.clip(col[None, :] - col[:, None], -(NA_WIN_COLS - 1), NA_WIN_COLS - 1) + NA_WIN_COLS - 1
    col_start = jnp.clip(col - NA_WIN_COLS // 2, 0, GRID_W - NA_WIN_COLS)
    col_mask = (col[None, :] >= col_start[:, None]) & (col[None, :] < col_start[:, None] + NA_WIN_COLS)
    onehot = (dc[None] == jnp.arange(2 * NA_WIN_COLS - 1)[:, None, None]).astype(F32)
    t = jnp.einsum('howc,cqk->howqk', rpb[:, dr].astype(F32), onehot,
                   precision=lax.Precision.HIGHEST)
    t = jnp.where(col_mask[None, None, None], t, MASK_VALUE)
    t = t.transpose(1, 0, 3, 2, 4)
    return t.reshape(NA_WIN_ROWS, NA_GROUPS, HEADS_PER_GROUP * GRID_W, NA_LOCAL_KEYS)


def _neighborhood_attention(qkv, bias_tab, n_lat_tiles, with_ctx_queries):
    bsz, s, _ = qkv.shape
    seq = n_lat_tiles * TM
    n_rows = seq // GRID_W
    assert n_rows >= NA_WIN_ROWS
    n_ctx_rows = (s - seq) // GRID_W
    rq = NA_ROWS_PER_STEP
    assert n_rows % rq == 0 and n_ctx_rows % rq == 0
    nq = (n_rows + (n_ctx_rows if with_ctx_queries else 0)) // rq
    ctx_blk = seq // (s - seq)

    def bias_spec(j):
        def idx(b, i):
            rr = jnp.minimum(i * rq + j, n_rows - 1)
            return (rr - jnp.clip(rr - NA_WIN_ROWS // 2, 0, n_rows - NA_WIN_ROWS), 0, 0, 0)
        return pl.BlockSpec((1,) + bias_tab.shape[1:], idx)

    return pl.pallas_call(
        functools.partial(_na_kernel, n_rows=n_rows),
        grid=(bsz, nq),
        in_specs=[
            pl.BlockSpec((1, rq * GRID_W, NA_WIDTH), lambda b, i: (b, i, 0)),
            pl.BlockSpec((1, seq, NA_WIDTH), lambda b, i: (b, 0, 1)),
            pl.BlockSpec((1, seq, NA_WIDTH), lambda b, i: (b, 0, 2)),
            pl.BlockSpec((1, s - seq, NA_WIDTH), lambda b, i: (b, ctx_blk, 1)),
            pl.BlockSpec((1, s - seq, NA_WIDTH), lambda b, i: (b, ctx_blk, 2)),
        ] + [bias_spec(j) for j in range(rq)],
        out_specs=pl.BlockSpec((1, rq * GRID_W, NA_WIDTH), lambda b, i: (b, i, 0)),
        out_shape=jax.ShapeDtypeStruct((bsz, nq * rq * GRID_W, NA_WIDTH), BF16),
        compiler_params=_params("parallel", "arbitrary"),
        name="neighborhood_attention",
    )(qkv, qkv, qkv, qkv, qkv, *([bias_tab] * rq))


def _pool_kernel(up_ref, u_ref, un_ref, w_ref, sc_ref, o_ref, buf, s0, s1, s2, *, n_lat_tiles, seq, ctx_len):
    i = pl.program_id(1)
    hw = POOL_REACH
    n = TM + 2 * hw
    is_ctx = i >= n_lat_tiles
    base = jnp.where(is_ctx, 0, i * TM)
    lseq = jnp.where(is_ctx, ctx_len, seq)
    buf[0:hw] = up_ref[0]
    buf[hw:hw + TM] = u_ref[0]
    buf[hw + TM:n] = un_ref[0]
    p = base - hw + lax.broadcasted_iota(jnp.int32, (n, POOL_WIDTH), 0)
    buf[...] = jnp.where((p >= 0) & (p < lseq), buf[...], 0.0)
    s0[0:n - 1] = buf[0:n - 1] + buf[1:n]
    s1[0:n - 3] = s0[0:n - 3] + s0[2:n - 1]
    s2[0:n - 7] = s1[0:n - 7] + s1[4:n - 3]
    shape = (TM, POOL_WIDTH)
    t = base + lax.broadcasted_iota(jnp.int32, shape, 0)
    grp = lax.broadcasted_iota(jnp.int32, shape, 1) // POOL_GROUP_DIM
    acc = jnp.where(grp == 0, s0[hw - 1:hw - 1 + TM],
                    jnp.where(grp == 1, s1[hw - 2:hw - 2 + TM],
                              jnp.where(grp == 2, s2[hw - 4:hw - 4 + TM], s2[0:TM] + s2[hw:hw + TM])))
    win = jnp.left_shift(2, grp)
    back = win // 2
    fwd = win - back - 1
    lo = jnp.clip(t - back, 0, lseq - 1)
    hi = jnp.clip(t + fwd, 0, lseq - 1)
    mean = acc / (hi - lo + 1).astype(F32)
    diff = (mean - u_ref[0]).astype(BF16)
    o_ref[0] = (_dot(diff, w_ref[...]) * sc_ref[...]).astype(o_ref.dtype)


def _multiscale_pool(u, w_bd, scale, n_lat_tiles, n_tiles):
    bsz, s, c = u.shape
    hw = POOL_REACH
    per = TM // hw
    last = s // hw - 1
    return pl.pallas_call(
        functools.partial(_pool_kernel, n_lat_tiles=n_lat_tiles, seq=n_lat_tiles * TM, ctx_len=s - n_lat_tiles * TM),
        grid=(bsz, n_tiles),
        in_specs=[
            pl.BlockSpec((1, hw, c), lambda b, i: (b, jnp.maximum(i * per - 1, 0), 0)),
            pl.BlockSpec((1, TM, c), lambda b, i: (b, i, 0)),
            pl.BlockSpec((1, hw, c), lambda b, i: (b, jnp.minimum((i + 1) * per, last), 0)),
            pl.BlockSpec((c, c), lambda b, i: (0, 0)),
            pl.BlockSpec((1, c), lambda b, i: (0, 0)),
        ],
        out_specs=pl.BlockSpec((1, TM, c), lambda b, i: (b, i, 0)),
        out_shape=jax.ShapeDtypeStruct((bsz, n_tiles * TM, c), BF16),
        scratch_shapes=[pltpu.VMEM((TM + 2 * hw, c), F32)] * 4,
        compiler_params=_params("parallel", "parallel"),
        name="multiscale_pool",
    )(u, u, u, w_bd, scale)


def _gla_tile(qk_ref, v_ref, la_ref, cs_ref, sn_ref, st_ref, o_ref, reverse, g):
    ch = GLA_CHUNK
    lane = lax.broadcasted_iota(jnp.int32, (ch, GLA_QK), 1)
    row = lax.broadcasted_iota(jnp.int32, (ch, GLA_QK), 0)
    rr = lax.broadcasted_iota(jnp.int32, (GLA_HEADS * ch, ch), 0) % ch
    cc = lax.broadcasted_iota(jnp.int32, (GLA_HEADS * ch, ch), 1)
    causal = (rr <= cc) if reverse else (rr >= cc)
    st_shape = (GLA_WIDTH, GLA_QK)
    head_blk = (lax.broadcasted_iota(jnp.int32, st_shape, 0) // GLA_DV
                == lax.broadcasted_iota(jnp.int32, st_shape, 1) // GLA_DK)
    quarter = GLA_DK // 4
    first_half = (lane % (2 * quarter)) < quarter

    def rope(x, cs, sn):
        partner = jnp.where(first_half, pltpu.roll(x, GLA_QK - quarter, 1), pltpu.roll(x, quarter, 1))
        return x * cs + partner * sn

    chunks = range(TM // ch)
    for c in (reversed(chunks) if reverse else chunks):
        sl = slice(c * ch, (c + 1) * ch)
        cs = cs_ref[sl, :]
        sn = sn_ref[sl, :]
        q = rope(qk_ref[g, sl, 0:GLA_QK], cs, sn)
        k = rope(qk_ref[g, sl, GLA_QK:2 * GLA_QK], cs, sn)
        b = la_ref[g, sl, :]
        d = 1
        while d < ch:
            if reverse:
                b = b + jnp.where(row < ch - d, pltpu.roll(b, ch - d, 0), 0.0)
            else:
                b = b + jnp.where(row >= d, pltpu.roll(b, d, 0), 0.0)
            d *= 2
        b_last = b[0:1, :] if reverse else b[ch - 1:ch, :]
        q_in = q * jnp.exp(b)
        k_in = (k * jnp.exp(-b)).astype(BF16)
        k_st = (k * jnp.exp(b_last - b)).astype(BF16)
        att = _dot_nt(_stack_heads(q_in, GLA_DK).astype(BF16), k_in)
        att = jnp.where(causal, att, 0.0).astype(BF16)
        vb = v_ref[g, sl, :].astype(BF16)
        o = _unstack_heads(_dot(att, vb), GLA_DV)
        st = st_ref[g]
        o_ref[g, sl, :] = o + _dot_nt(q_in.astype(BF16), st.astype(BF16))
        st_ref[g] = st * jnp.exp(b_last) + jnp.where(head_blk, _dot_tn(vb, k_st), 0.0)


def _gla_kernel(qkf, vf, laf, csf, snf, qkb, vb, lab, csb, snb, of_ref, ob_ref, st_f, st_b):
    @pl.when(pl.program_id(1) == 0)
    def _():
        st_f[...] = jnp.zeros_like(st_f)
        st_b[...] = jnp.zeros_like(st_b)

    for g in range(st_f.shape[0]):
        _gla_tile(qkf, vf, laf, csf, snf, st_f, of_ref, False, g)
        _gla_tile(qkb, vb, lab, csb, snb, st_b, ob_ref, True, g)


def _rope_tables(seq, s):
    quarter = GLA_DK // 4
    t = np.arange(s)
    pos_row = np.where(t < seq, t // GRID_W, 0).astype(np.float32)
    pos_col = np.where(t < seq, t % GRID_W, 0).astype(np.float32)
    d = np.arange(GLA_QK) % GLA_DK
    inv_freq = ROPE_BASE ** (-jnp.arange(quarter, dtype=F32) / quarter)
    freq = inv_freq[d % quarter]
    pos = jnp.where((d < GLA_DK // 2)[None, :], pos_row[:, None], pos_col[:, None])
    ang = pos * freq[None, :]
    sign = np.where((d % (2 * quarter)) < quarter, -1.0, 1.0).astype(np.float32)
    return jnp.cos(ang), jnp.sin(ang) * sign[None, :]


def _gla(gg, cos, sin, n_lat_tiles):
    bsz, s, _ = gg.shape
    nt = s // TM
    assert nt == n_lat_tiles + 1
    gb = GLA_BATCH if bsz % GLA_BATCH == 0 else 1
    fwd = lambda i: (i + n_lat_tiles) % nt
    bwd = lambda i: jnp.where(i == 0, n_lat_tiles, n_lat_tiles - i)
    la_f_blk = (2 * GLA_QK + 2 * GLA_WIDTH) // GLA_QK
    la_b_blk = la_f_blk + 1
    v_blk = 2 * GLA_QK // GLA_WIDTH

    def specs(order, la_blk):
        return [
            pl.BlockSpec((gb, TM, 2 * GLA_QK), lambda b, i: (b, order(i), 0)),
            pl.BlockSpec((gb, TM, GLA_WIDTH), lambda b, i: (b, order(i), v_blk)),
            pl.BlockSpec((gb, TM, GLA_QK), lambda b, i: (b, order(i), la_blk)),
            pl.BlockSpec((TM, GLA_QK), lambda b, i: (order(i), 0)),
            pl.BlockSpec((TM, GLA_QK), lambda b, i: (order(i), 0)),
        ]

    return pl.pallas_call(
        _gla_kernel,
        grid=(bsz // gb, nt),
        in_specs=specs(fwd, la_f_blk) + specs(bwd, la_b_blk),
        out_specs=[
            pl.BlockSpec((gb, TM, GLA_WIDTH), lambda b, i: (b, fwd(i), 0)),
            pl.BlockSpec((gb, TM, GLA_WIDTH), lambda b, i: (b, bwd(i), 0)),
        ],
        out_shape=[jax.ShapeDtypeStruct((bsz, s, GLA_WIDTH), F32)] * 2,
        scratch_shapes=[pltpu.VMEM((gb, GLA_WIDTH, GLA_QK), F32)] * 2,
        compiler_params=_params("arbitrary", "arbitrary"),
        name="gla_bidir",
    )(gg, gg, gg, cos, sin, gg, gg, gg, cos, sin)


def _store_row_tiles(ref, val):
    rows, width = val.shape
    pieces = width // LANES
    for s in range(pieces):
        ref[pl.ds(s, rows, stride=pieces), :] = val[:, s * LANES:(s + 1) * LANES]


def _load_row_tiles(ref, rows, pieces):
    return jnp.concatenate([ref[pl.ds(s, rows, stride=pieces), :] for s in range(pieces)], axis=1)


SLOTS_PER_Y_TILE = 2


def _pack_bf16_pairs(y):
    half = y.shape[1] // 2
    bits = lax.bitcast_convert_type(y.astype(jnp.bfloat16).astype(F32), jnp.uint32)
    return (bits[:, :half] >> 16) | (bits[:, half:] & jnp.uint32(0xFFFF0000))


def _unpack_bf16_pairs(words):
    lo = lax.bitcast_convert_type(words << 16, F32)
    hi = lax.bitcast_convert_type(words & jnp.uint32(0xFFFF0000), F32)
    return jnp.concatenate([lo, hi], axis=1)


def _row_index(shape):
    return lax.broadcasted_iota(jnp.int32, shape, 0).astype(F32)


def _stack_rows(rows):
    shape = (len(rows), rows[0].shape[1])
    rid = lax.broadcasted_iota(jnp.int32, shape, 0)
    out = jnp.zeros(shape, rows[0].dtype)
    for k, r in enumerate(rows):
        out = jnp.where(rid == k, r, out)
    return out


def _select_experts(scores, biased):
    n_exp, n_tok = biased.shape
    per = n_exp // N_GROUPS
    neg = -jnp.inf
    sub = _row_index((per, n_tok))
    grp_rows = []
    for g in range(N_GROUPS):
        blk = biased[g * per:(g + 1) * per, :]
        m1 = jnp.max(blk, axis=0, keepdims=True)
        i1 = jnp.min(jnp.where(blk == m1, sub, float(per)), axis=0, keepdims=True)
        m2 = jnp.max(jnp.where(sub == i1, neg, blk), axis=0, keepdims=True)
        grp_rows.append(m1 + m2)
    cur = _stack_rows(grp_rows)
    gid = _row_index(cur.shape)
    picked = jnp.zeros(cur.shape, F32)
    for _ in range(TOPK_GROUPS):
        gm = jnp.max(cur, axis=0, keepdims=True)
        gi = jnp.min(jnp.where(cur == gm, gid, float(N_GROUPS)), axis=0, keepdims=True)
        hit = gid == gi
        picked = jnp.where(hit, 1.0, picked)
        cur = jnp.where(hit, neg, cur)
    cur = jnp.concatenate(
        [jnp.where(picked[g:g + 1, :] > 0.0, biased[g * per:(g + 1) * per, :], neg) for g in range(N_GROUPS)], axis=0)
    eid = _row_index(cur.shape)
    ids, vals = [], []
    chosen = jnp.zeros(cur.shape, F32)
    for _ in range(TOP_K):
        m = jnp.max(cur, axis=0, keepdims=True)
        idx = jnp.min(jnp.where(cur == m, eid, float(n_exp)), axis=0, keepdims=True)
        hit = eid == idx
        ids.append(idx)
        vals.append(jnp.sum(jnp.where(hit, scores, 0.0), axis=0, keepdims=True))
        chosen = jnp.where(hit, 1.0, chosen)
        cur = jnp.where(hit, neg, cur)
    return ids, vals, chosen


def _outproj_kernel(x_ref, xc_ref, oa_ref, ob_ref, of_ref, obk_ref, r_ref, gg_ref, wo_ref, mod_ref, gf_ref, wr_ref, br_ref,
                    tri_ref, xo_ref, h_ref, e_ref, w_ref, rk_ref, hist_ref, *, n_lat_tiles):
    o = of_ref[0] + obk_ref[0]
    head = lax.broadcasted_iota(jnp.int32, o.shape, 1) // GLA_DV
    o2 = o * o
    rs = jnp.zeros_like(o)
    for h in range(GLA_HEADS):
        ssq = jnp.sum(jnp.where(head == h, o2, 0.0), axis=-1, keepdims=True)
        rs = jnp.where(head == h, lax.rsqrt(ssq / GLA_DV + NORM_EPS), rs)
    oc = (o * rs * gg_ref[...] * _silu(r_ref[0])).astype(BF16)
    c1 = NA_WIDTH + POOL_WIDTH
    acc = _dot(oa_ref[0], wo_ref[0:NA_WIDTH]) + _dot(ob_ref[0], wo_ref[NA_WIDTH:c1]) + _dot(oc, wo_ref[c1:])
    x = _stream_tile(x_ref, xc_ref, n_lat_tiles) + mod_ref[0, 0, 2:3, :] * acc
    xo_ref[0] = x
    h = _rmsnorm(x, gf_ref[...]) * (1.0 + mod_ref[0, 0, 4:5, :]) + mod_ref[0, 0, 3:4, :]
    _store_row_tiles(h_ref, h)

    scores = _sigmoid(_dot_nt(wr_ref[...], h.astype(BF16)))
    ids, vals, chosen = _select_experts(scores, scores + br_ref[...])
    total = vals[0]
    for v in vals[1:]:
        total = total + v
    w_ref[...] = _stack_rows([v / total * ROUTED_SCALE for v in vals])
    e_ref[...] = _stack_rows(ids).astype(jnp.int32)
    chosen_b = chosen.astype(BF16)
    before = _dot(chosen_b, tri_ref[...])
    eid = _row_index(before.shape)
    rk_ref[...] = _stack_rows([jnp.sum(jnp.where(eid == idx, before, 0.0), axis=0, keepdims=True) for idx in ids])
    hist_ref[0] = _dot_nt(jnp.ones((SUBLANES, chosen.shape[1]), BF16), chosen_b)


def _outproj(x_lat, x_ctx, ctx_blk, oa, ob, o_f, o_b, gg, g_gla4, w_out, mod, g_ffn, w_router_t, b_router,
             n_lat_tiles, n_tiles):
    bsz, _, d = x_lat.shape
    n_exp = w_router_t.shape[0]
    r_blk = (2 * GLA_QK + GLA_WIDTH) // GLA_WIDTH
    tile = lambda w: pl.BlockSpec((1, TM, w), lambda b, i: (b, i, 0))
    const = lambda b, i: (0, 0)
    per_tok = pl.BlockSpec((TOP_K, TM), lambda b, i: (0, b * n_tiles + i))
    t_tok = bsz * n_tiles * TM
    tri = (jnp.arange(TM)[:, None] < jnp.arange(TM)[None, :]).astype(BF16)
    return pl.pallas_call(
        functools.partial(_outproj_kernel, n_lat_tiles=n_lat_tiles),
        grid=(bsz, n_tiles),
        in_specs=[
            *_stream_specs(d, n_lat_tiles, ctx_blk),
            tile(NA_WIDTH), tile(POOL_WIDTH), tile(GLA_WIDTH), tile(GLA_WIDTH),
            pl.BlockSpec((1, TM, GLA_WIDTH), lambda b, i: (b, i, r_blk)),
            pl.BlockSpec((1, GLA_WIDTH), const),
            pl.BlockSpec(w_out.shape, const),
            pl.BlockSpec((1, 1, SUBLANES, d), lambda b, i: (b, jnp.where(i >= n_lat_tiles, 0, 1), 0, 0)),
            pl.BlockSpec((1, d), const),
            pl.BlockSpec(w_router_t.shape, const),
            pl.BlockSpec((n_exp, 1), const),
            pl.BlockSpec((TM, TM), const),
        ],
        out_specs=[tile(d), pl.BlockSpec((TM * d // LANES, LANES), lambda b, i: (b * n_tiles + i, 0)),
                   per_tok, per_tok, per_tok,
                   pl.BlockSpec((1, SUBLANES, n_exp), lambda b, i: (b * n_tiles + i, 0, 0))],
        out_shape=[
            jax.ShapeDtypeStruct((bsz, n_tiles * TM, d), F32),
            jax.ShapeDtypeStruct((t_tok * d // LANES, LANES), F32),
            jax.ShapeDtypeStruct((TOP_K, t_tok), jnp.int32),
            jax.ShapeDtypeStruct((TOP_K, t_tok), F32),
            jax.ShapeDtypeStruct((TOP_K, t_tok), F32),
            jax.ShapeDtypeStruct((bsz * n_tiles, SUBLANES, n_exp), F32),
        ],
        compiler_params=_params("parallel", "parallel"),
        name="outproj_router",
    )(x_lat, x_ctx, oa, ob, o_f, o_b, gg, g_gla4, w_out, mod, g_ffn, w_router_t, b_router, tri)


def _slot_layout(hist):
    n_tiles, n_exp = hist.shape
    counts = hist.sum(0)
    tile_base = jnp.cumsum(hist, axis=0) - hist
    padded = (counts + MOE_BLOCK - 1) // MOE_BLOCK * MOE_BLOCK
    pad_end = jnp.cumsum(padded)
    pad_start = pad_end - padded
    base = (pad_start[None, :] + tile_base).astype(F32).reshape(n_tiles, n_exp, 1)
    n_blocks = -(-(n_tiles * TM * TOP_K) // MOE_BLOCK) + n_exp
    n_used = (pad_end[-1] // MOE_BLOCK).astype(jnp.int32)
    first_blk = (pad_start // MOE_BLOCK).astype(jnp.int32)
    blk_count = (padded // MOE_BLOCK).astype(jnp.int32)
    pad_row = (pad_start + counts).astype(jnp.int32)
    pad_cnt = (padded - counts).astype(jnp.int32)
    return base, first_blk, blk_count, pad_row, pad_cnt, n_used.reshape(1), n_blocks


def _pos_kernel(e_ref, rk_ref, base_ref, posx_ref, posy_ref, half_ref, *, pieces):
    eid = lax.broadcasted_iota(jnp.int32, (base_ref.shape[1], TM), 0)
    for j in range(base_ref.shape[0]):
        cols = slice(j * TM, (j + 1) * TM)
        base = base_ref[j]
        rows = [jnp.sum(jnp.where(eid == e_ref[k:k + 1, cols], base, 0.0), axis=0, keepdims=True)
                for k in range(TOP_K)]
        slot = (_stack_rows(rows) + rk_ref[:, cols]).astype(jnp.int32)
        posx_ref[j] = slot * pieces
        posy_ref[j] = (slot // SLOTS_PER_Y_TILE) * pieces
        half_ref[:, cols] = slot % SLOTS_PER_Y_TILE


def _slot_positions(e_t, rank_t, base, pieces):
    n_tiles, n_exp, _ = base.shape
    per_step = max(p for p in range(1, DISPATCH_TILES + 1) if n_tiles % p == 0)
    per_tok = pl.BlockSpec((TOP_K, per_step * TM), lambda i: (0, i))
    flat = pl.BlockSpec((per_step, TOP_K, TM), lambda i: (i, 0, 0))
    pos_x, pos_y, half = pl.pallas_call(
        functools.partial(_pos_kernel, pieces=pieces),
        grid=(n_tiles // per_step,),
        in_specs=[per_tok, per_tok, pl.BlockSpec((per_step, n_exp, 1), lambda i: (i, 0, 0))],
        out_specs=[flat, flat, per_tok],
        out_shape=[jax.ShapeDtypeStruct((n_tiles, TOP_K, TM), jnp.int32)] * 2
        + [jax.ShapeDtypeStruct(e_t.shape, jnp.int32)],
        compiler_params=_params("parallel"),
        name="moe_positions",
    )(e_t, rank_t, base)
    return pos_x.reshape(-1), pos_y.reshape(-1), half


def _dispatch_kernel(pad_row_ref, pad_cnt_ref, nu_ref, pos_ref, h_ref, xs_hbm, zbuf, sem, zsem, *, n_blocks):
    step = pl.program_id(0)
    blk_rows = zbuf.shape[0]
    row_pieces = blk_rows // MOE_BLOCK

    def pad_copies(e, act):
        row = pad_row_ref[e]
        cnt = pad_cnt_ref[e]
        bit = MOE_BLOCK // 2
        while bit:
            @pl.when((cnt & bit) != 0)
            def _(row=row, bit=bit):
                dst = xs_hbm.at[pl.ds(pl.multiple_of(row * row_pieces, row_pieces), bit * row_pieces), :]
                act(pltpu.make_async_copy(zbuf.at[pl.ds(0, bit * row_pieces), :], dst, zsem))
            row = row + (cnt & bit)
            bit //= 2

    def tail_copy(g):
        return pltpu.make_async_copy(zbuf, xs_hbm.at[pl.ds(pl.multiple_of(g * blk_rows, blk_rows), blk_rows), :], zsem)

    def zero_rows(act):
        def per_expert(e, c):
            pad_copies(e, act)
            return c
        lax.fori_loop(0, pad_row_ref.shape[0], per_expert, 0)

        def per_block(g, c):
            act(tail_copy(g))
            return c
        lax.fori_loop(nu_ref[0], n_blocks, per_block, 0)

    @pl.when(step == 0)
    def _():
        zbuf[...] = jnp.zeros_like(zbuf)
        zero_rows(lambda cp: cp.start())

    tiles = pos_ref.shape[0] // (TOP_K * TM)
    pieces = h_ref.shape[0] // (tiles * TM)

    for j in range(tiles):
        def body(t, c):
            src = h_ref.at[pl.ds(pl.multiple_of((j * TM + t) * pieces, pieces), pieces), :]
            for k in range(TOP_K):
                dst = xs_hbm.at[pl.ds(pl.multiple_of(pos_ref[(j * TOP_K + k) * TM + t], pieces), pieces), :]
                pltpu.make_async_copy(src, dst, sem).start(priority=k % 2)
            return c
        lax.fori_loop(0, TM, body, 0, unroll=2)
    for _ in range(TOP_K):
        pltpu.make_async_copy(h_ref, xs_hbm.at[pl.ds(0, h_ref.shape[0]), :], sem).wait()

    @pl.when(step == pl.num_programs(0) - 1)
    def _():
        zero_rows(lambda cp: cp.wait())


def _dispatch(h2t, pos, pad_row, pad_cnt, n_used, n_blocks, pieces):
    n_tiles = h2t.shape[0] // (TM * pieces)
    per_step = max(p for p in range(1, DISPATCH_TILES + 1) if n_tiles % p == 0)
    grid_spec = pltpu.PrefetchScalarGridSpec(
        num_scalar_prefetch=3,
        grid=(n_tiles // per_step,),
        in_specs=[
            pl.BlockSpec((per_step * TOP_K * TM,), lambda i, *_: (i,), memory_space=pltpu.SMEM),
            pl.BlockSpec((per_step * TM * pieces, LANES), lambda i, *_: (i, 0)),
        ],
        out_specs=pl.BlockSpec(memory_space=pl.ANY),
        scratch_shapes=[pltpu.VMEM((MOE_BLOCK * pieces, LANES), F32), pltpu.SemaphoreType.DMA,
                        pltpu.SemaphoreType.DMA],
    )
    return pl.pallas_call(
        functools.partial(_dispatch_kernel, n_blocks=n_blocks),
        grid_spec=grid_spec,
        out_shape=jax.ShapeDtypeStruct((n_blocks * MOE_BLOCK * pieces, LANES), F32),
        compiler_params=_params("arbitrary"),
        name="moe_dispatch",
    )(pad_row, pad_cnt, n_used, pos, h2t)


def _expert_kernel(first_ref, cnt_ref, nu_ref, wi_ref, wd_ref, xs_hbm, ys_hbm, xbuf, ybuf, wi_bf, wd_bf, xsem, ysem,
                   *, n_blocks):
    e = pl.program_id(0)
    n_used = nu_ref[0]
    rows = xbuf.shape[1]
    pieces = rows // MOE_BLOCK
    d_exp = wd_ref.shape[2]

    def x_copy(g, s):
        return pltpu.make_async_copy(xs_hbm.at[pl.ds(pl.multiple_of(g * rows, rows), rows), :], xbuf.at[s], xsem.at[s])

    y_rows = ybuf.shape[1]

    def y_copy(g, s):
        return pltpu.make_async_copy(ybuf.at[s], ys_hbm.at[pl.ds(pl.multiple_of(g * y_rows, y_rows), y_rows), :],
                                     ysem.at[s])

    n_xbuf = xbuf.shape[0]
    n_ybuf = ybuf.shape[0]

    @pl.when(e == 0)
    def _():
        for g in range(n_xbuf - 1):
            @pl.when(g < n_used)
            def _():
                x_copy(g, g).start()

    @pl.when(cnt_ref[e] > 0)
    def _():
        wi_bf[...] = wi_ref[0, 0].astype(BF16)
        wd_bf[...] = wd_ref[0, 0].astype(BF16)

    def block(j, c):
        g = first_ref[e] + j
        s = g % n_ybuf
        x_copy(g, g % n_xbuf).wait()

        @pl.when(g + n_xbuf - 1 < n_used)
        def _():
            x_copy(g + n_xbuf - 1, (g + n_xbuf - 1) % n_xbuf).start()

        @pl.when(g >= n_ybuf)
        def _():
            y_copy(g - n_ybuf, s).wait()

        hh = _dot(_load_row_tiles(xbuf.at[g % n_xbuf], MOE_BLOCK, pieces).astype(BF16), wi_bf[...])
        a = (_silu(hh[:, :d_exp]) * hh[:, d_exp:]).astype(BF16)
        _store_row_tiles(ybuf.at[s], _pack_bf16_pairs(_dot(a, wd_bf[...])))
        y_copy(g, s).start(priority=1)
        return c
    lax.fori_loop(0, cnt_ref[e], block, 0)

    @pl.when(e == pl.num_programs(0) - 1)
    def _():
        for back in range(1, n_ybuf + 1):
            @pl.when(n_used >= back)
            def _():
                y_copy(n_used - back, (n_used - back) % n_ybuf).wait()

        ybuf[0] = jnp.zeros(ybuf.shape[1:], ybuf.dtype)

        def start(g, c):
            y_copy(g, 0).start()
            return c
        lax.fori_loop(n_used, n_blocks, start, 0)

        def wait(g, c):
            y_copy(g, 0).wait()
            return c
        lax.fori_loop(n_used, n_blocks, wait, 0)


def _moe_experts(xs, first_blk, blk_count, n_used, n_blocks, w_e_in, w_e_down, layer):
    _, n_exp, d, f2 = w_e_in.shape
    rows = xs.shape[0] // n_blocks
    grid_spec = pltpu.PrefetchScalarGridSpec(
        num_scalar_prefetch=3,
        grid=(n_exp,),
        in_specs=[
            pl.BlockSpec((1, 1, d, f2), lambda e, first, cnt, nu: (layer, e, 0, 0)),
            pl.BlockSpec((1, 1, f2 // 2, d), lambda e, first, cnt, nu: (layer, e, 0, 0)),
            pl.BlockSpec(memory_space=pl.ANY),
        ],
        out_specs=pl.BlockSpec(memory_space=pl.ANY),
        scratch_shapes=[
            pltpu.VMEM((EXPERT_X_BUFFERS, rows, LANES), F32),
            pltpu.VMEM((EXPERT_Y_BUFFERS, rows // SLOTS_PER_Y_TILE, LANES), jnp.uint32),
            pltpu.VMEM((d, f2), BF16),
            pltpu.VMEM((f2 // 2, d), BF16),
            pltpu.SemaphoreType.DMA((EXPERT_X_BUFFERS,)),
            pltpu.SemaphoreType.DMA((EXPERT_Y_BUFFERS,)),
        ],
    )
    return pl.pallas_call(
        functools.partial(_expert_kernel, n_blocks=n_blocks),
        grid_spec=grid_spec,
        out_shape=jax.ShapeDtypeStruct((xs.shape[0] // SLOTS_PER_Y_TILE, LANES), jnp.uint32),
        compiler_params=_params("arbitrary"),
        name="moe_experts",
    )(first_blk, blk_count, n_used, w_e_in, w_e_down, xs)


def _combine_kernel(pos0_ref, posn_ref, x_ref, h_ref, w_ref, half_ref, wsi_ref, wsd_ref, mod_ref, gfin_ref, ys_hbm,
                    o_ref, ybuf, sem, *, final, n_steps):
    step = pl.program_id(0) * pl.num_programs(1) + pl.program_id(1)
    slot = step % 2
    d_exp = wsd_ref.shape[0]
    pieces = h_ref.shape[0] // TMC

    def gather(pos_ref, st, s):
        off = (st % (TM // TMC)) * TMC
        def body(t, c):
            for k in range(TOP_K):
                src = ys_hbm.at[pl.ds(pl.multiple_of(pos_ref[k * TM + off + t], pieces), pieces), :]
                dst = ybuf.at[s, k, pl.ds(pl.multiple_of(t * pieces, pieces), pieces), :]
                pltpu.make_async_copy(src, dst, sem.at[s]).start(priority=k % 2)
            return c
        lax.fori_loop(0, TMC, body, 0, unroll=2)

    @pl.when(step == 0)
    def _():
        gather(pos0_ref, 0, 0)

    @pl.when(step + 1 < n_steps)
    def _():
        gather(posn_ref, step + 1, 1 - slot)

    hh = _dot(_load_row_tiles(h_ref, TMC, pieces).astype(BF16), wsi_ref[...])
    a = (_silu(hh[:, :d_exp]) * hh[:, d_exp:]).astype(BF16)
    y = _dot(a, wsd_ref[...])
    for k in range(TOP_K):
        pltpu.make_async_copy(ys_hbm.at[pl.ds(0, TMC * pieces), :], ybuf.at[slot, k], sem.at[slot]).wait()
    w = w_ref[...]
    half = half_ref[...]
    for k in range(TOP_K):
        both = _load_row_tiles(ybuf.at[slot, k], TMC, pieces)
        mine = jnp.where(half[:, k:k + 1] > 0, both[:, both.shape[1] // 2:], both[:, :both.shape[1] // 2])
        y = y + _unpack_bf16_pairs(mine) * w[:, k:k + 1]
    x = x_ref[0] + mod_ref[0, 0, 5:6, :] * y
    if final:
        x = _rmsnorm(x, gfin_ref[...])
    o_ref[0] = x


def _combine(x_mid, h2t, ys, pos, w_tok, half_tok, w_sh_in, w_sh_down, mod, g_final, n_lat_tiles, final):
    bsz, s, d = x_mid.shape
    nt = s // TMC
    n_steps = bsz * nt
    halves = TM // TMC
    lat_steps = n_lat_tiles * halves
    pieces = d // LANES
    const = lambda b, i: (0, 0)
    tile = pl.BlockSpec((1, TMC, d), lambda b, i: (b, i, 0))
    smem_blk = lambda f: pl.BlockSpec((TOP_K * TM,), f, memory_space=pltpu.SMEM)
    return pl.pallas_call(
        functools.partial(_combine_kernel, final=final, n_steps=n_steps),
        grid=(bsz, nt),
        in_specs=[
            smem_blk(lambda b, i: (0,)),
            smem_blk(lambda b, i: (jnp.minimum(b * nt + i + 1, n_steps - 1) // halves,)),
            tile,
            pl.BlockSpec((TMC * pieces, LANES), lambda b, i: (b * nt + i, 0)),
            pl.BlockSpec((TMC, TOP_K), lambda b, i: (b * nt + i, 0)),
            pl.BlockSpec((TMC, TOP_K), lambda b, i: (b * nt + i, 0)),
            pl.BlockSpec(w_sh_in.shape, const),
            pl.BlockSpec(w_sh_down.shape, const),
            pl.BlockSpec((1, 1, SUBLANES, d), lambda b, i: (b, jnp.where(i >= lat_steps, 0, 1), 0, 0)),
            pl.BlockSpec((1, d), const),
            pl.BlockSpec(memory_space=pl.ANY),
        ],
        out_specs=tile,
        out_shape=jax.ShapeDtypeStruct((bsz, s, d), F32),
        scratch_shapes=[pltpu.VMEM((2, TOP_K, TMC * pieces, LANES), jnp.uint32), pltpu.SemaphoreType.DMA((2,))],
        compiler_params=_params("arbitrary", "arbitrary"),
        name="moe_combine",
    )(pos, pos, x_mid, h2t, w_tok, half_tok, w_sh_in, w_sh_down, mod, g_final, ys)


def _block_diag(w):
    g, a, b = w.shape
    out = jnp.zeros((g * a, g * b), w.dtype)
    for j in range(g):
        out = out.at[j * a:(j + 1) * a, j * b:(j + 1) * b].set(w[j])
    return out


def kernel(x, c, ctx, c_ctx, w_mod, b_mod, g_mix, w_in, rpb, w_pool, pool_scale, w_gate_f, b_gate_f, w_gate_b, b_gate_b, g_gla, w_out, g_ffn, w_router, b_router, w_e_in, w_e_down, w_sh_in, w_sh_down, g_final):
    bsz, seq, d = x.shape
    n_ctx = ctx.shape[1]
    depth = w_mod.shape[0]
    n_exp = w_router.shape[2]
    assert n_ctx == TM and seq % TM == 0 and seq % GRID_W == 0
    n_lat = seq // TM
    s = seq + n_ctx

    n_rows = -(-(bsz + 1) // SUBLANES) * SUBLANES
    cc = jnp.zeros((n_rows, d), F32).at[:bsz].set(c).at[bsz].set(c_ctx)
    mod_all = _modulation(cc, w_mod, b_mod).reshape(depth, n_rows, 6, d)
    cos, sin = _rope_tables(seq, s)

    x_lat, x_ctx, ctx_blk = x, ctx, 0
    for layer in range(depth):
        last = layer == depth - 1
        n_tiles = n_lat if last else n_lat + 1
        m = mod_all[layer]
        mod = jnp.stack([jnp.broadcast_to(m[bsz], (bsz, 6, d)), m[:bsz]], axis=1)
        mod = jnp.pad(mod, ((0, 0), (0, 0), (0, SUBLANES - 6), (0, 0)))
        wl = w_in[layer]
        w_main = wl[:, :MAIN_W].astype(BF16)
        w_low = jnp.pad(wl[:, MAIN_W:], ((0, 0), (0, LANES - 2 * GLA_GATE_RANK))).astype(BF16)
        w2 = jnp.zeros((LANES, 2 * GLA_QK), F32)
        w2 = w2.at[:GLA_GATE_RANK, :GLA_QK].set(w_gate_f[layer])
        w2 = w2.at[GLA_GATE_RANK:2 * GLA_GATE_RANK, GLA_QK:].set(w_gate_b[layer]).astype(BF16)
        b2 = jnp.concatenate([b_gate_f[layer], b_gate_b[layer]])[None, :]
        qkv, u, gg = _inproj(x_lat, x_ctx, ctx_blk, mod, g_mix[layer][None, :], w_main, w_low, w2, b2, n_lat)

        oa = _neighborhood_attention(qkv, _na_bias_table(rpb[layer]), n_lat, not last)
        ob = _multiscale_pool(u, _block_diag(w_pool[layer]).astype(BF16), pool_scale[layer][None, :], n_lat, n_tiles)
        o_f, o_b = _gla(gg, cos, sin, n_lat)

        x_mid, h2t, e_t, w_t, rank_t, hist = _outproj(
            x_lat, x_ctx, ctx_blk, oa, ob, o_f, o_b, gg, jnp.tile(g_gla[layer], GLA_HEADS)[None, :], w_out[layer].astype(BF16), mod,
            g_ffn[layer][None, :], w_router[layer].T.astype(BF16), b_router[layer].reshape(n_exp, 1), n_lat, n_tiles)
        pieces = d // LANES
        base, first_blk, blk_count, pad_row, pad_cnt, n_used, n_blocks = _slot_layout(
            hist[:, 0, :].astype(jnp.int32))
        pos_x, pos_y, half = _slot_positions(e_t, rank_t, base, pieces)
        xs = _dispatch(h2t, pos_x, pad_row, pad_cnt, n_used, n_blocks, pieces)
        ys = _moe_experts(xs, first_blk, blk_count, n_used, n_blocks, w_e_in, w_e_down, layer)
        xa = _combine(x_mid, h2t, ys, pos_y, w_t.T, half.T, w_sh_in[layer].astype(BF16),
                      w_sh_down[layer].astype(BF16), mod, g_final[None, :], n_lat, last)
        x_lat, x_ctx, ctx_blk = xa, xa, n_lat
    return xa
```
